```python
import jax, jax.numpy as jnp
from jax import lax
import numpy as np

D_MODEL = 2048
BATCH = 4
SEQ = 4096
DEPTH = 1

MEM_LEN = 256
FOX_HEADS = 8
FOX_HEAD_DIM = 128
FOX_WIDTH = FOX_HEADS * FOX_HEAD_DIM
Q_BLOCK = 128
SG_GROUPS = 8
SG_GROUP_DIM = 128
SG_WIDTH = SG_GROUPS * SG_GROUP_DIM
SG_CHUNK = 128
X_HEADS = 4
X_HEAD_DIM = 128
X_WIDTH = X_HEADS * X_HEAD_DIM
N_EXPERTS = 32
TOP_K = 4
D_FF_EXPERT = D_MODEL
SWIGLU_LIMIT = 7.0
SWIGLU_ALPHA = 1.702
MOE_BLOCK = 256
RMS_EPS = 1e-6
LN_EPS = 1e-5
SPLITS = (FOX_WIDTH, 2 * FOX_WIDTH, 3 * FOX_WIDTH, 3 * FOX_WIDTH + FOX_HEADS,
          3 * FOX_WIDTH + FOX_HEADS + 2 * SG_WIDTH)
IN_COLS = 3 * FOX_WIDTH + FOX_HEADS + 2 * SG_WIDTH + 2 * D_MODEL

kernel_name = 'hybrid_fox_sgu_xattn_moe'


def rmsnorm(x, g):
    xf = x.astype(jnp.float32)
    y = xf * lax.rsqrt(jnp.mean(xf * xf, axis=-1, keepdims=True) + RMS_EPS)
    return (y * g.astype(jnp.float32)).astype(x.dtype)


def forgetting_attention(q, k, v, logf):
    B, S, H, Dh = q.shape
    nb = S // Q_BLOCK
    c = jnp.cumsum(logf, axis=1).transpose(0, 2, 1)
    qh = q.transpose(0, 2, 1, 3)
    kh = k.transpose(0, 2, 1, 3)
    vh = v.transpose(0, 2, 1, 3)
    q_blocks = qh.reshape(B, H, nb, Q_BLOCK, Dh).transpose(2, 0, 1, 3, 4)
    c_blocks = c.reshape(B, H, nb, Q_BLOCK).transpose(2, 0, 1, 3)
    starts = jnp.arange(nb, dtype=jnp.int32) * Q_BLOCK
    k_pos = jnp.arange(S, dtype=jnp.int32)
    scale = Dh ** -0.5

    def one_block(args):
        qb, cb, start = args
        s = jnp.einsum('bhqd,bhkd->bhqk', qb, kh, preferred_element_type=jnp.float32) * scale
        s = s + cb[..., None] - c[:, :, None, :]
        q_pos = start + jnp.arange(Q_BLOCK, dtype=jnp.int32)
        s = jnp.where(k_pos[None, :] <= q_pos[:, None], s, -jnp.inf)
        p = jax.nn.softmax(s, axis=-1)
        return jnp.einsum('bhqk,bhkd->bhqd', p.astype(vh.dtype), vh)

    out = lax.map(one_block, (q_blocks, c_blocks, starts))
    return out.transpose(1, 0, 3, 2, 4).reshape(B, S, H * Dh)


def spatial_gating(z, ln_g, ln_b, w_s, b_s):
    B, S, _ = z.shape
    u, v = jnp.split(z, 2, axis=-1)
    vf = v.astype(jnp.float32)
    mu = jnp.mean(vf, axis=-1, keepdims=True)
    var = jnp.mean(jnp.square(vf - mu), axis=-1, keepdims=True)
    v = ((vf - mu) * lax.rsqrt(var + LN_EPS) * ln_g.astype(jnp.float32)
         + ln_b.astype(jnp.float32)).astype(z.dtype)
    nc = S // SG_CHUNK
    v = v.reshape(B, nc, SG_CHUNK, SG_GROUPS, SG_GROUP_DIM)
    mask = jnp.tril(jnp.ones((SG_CHUNK, SG_CHUNK), dtype=bool))
    w = jnp.where(mask[None], w_s, jnp.zeros_like(w_s))
    mixed = jnp.einsum('gts,bnsgc->bntgc', w, v) + b_s.T[None, None, :, :, None]
    return u * mixed.reshape(B, S, SG_WIDTH)


def memory_cross_attention(h, hm, w_xq, w_xkv, w_xo):
    B, S, _ = h.shape
    M = hm.shape[1]
    q = (h @ w_xq).reshape(B, S, X_HEADS, X_HEAD_DIM)
    k, v = jnp.split(hm @ w_xkv, 2, axis=-1)
    k = k.reshape(B, M, X_HEADS, X_HEAD_DIM)
    v = v.reshape(B, M, X_HEADS, X_HEAD_DIM)
    s = jnp.einsum('bshd,bmhd->bhsm', q, k, preferred_element_type=jnp.float32) * (X_HEAD_DIM ** -0.5)
    p = jax.nn.softmax(s, axis=-1)
    o = jnp.einsum('bhsm,bmhd->bshd', p.astype(v.dtype), v).reshape(B, S, X_WIDTH)
    return o @ w_xo


def moe(h, w_router, b_router, w_gate_up, b_gate_up, w_down, b_down):
    B, S, D = h.shape
    N = B * S
    xt = h.reshape(N, D)
    logits = jnp.dot(xt, w_router, preferred_element_type=jnp.float32) + b_router.astype(jnp.float32)
    top_vals, top_idx = lax.top_k(logits, TOP_K)
    top_w = jax.nn.softmax(top_vals, axis=-1)
    A = N * TOP_K
    e_flat = top_idx.reshape(A)
    tok_flat = jnp.arange(A, dtype=jnp.int32) // TOP_K
    w_flat = top_w.reshape(A)
    order = jnp.argsort(e_flat)
    e_sorted = e_flat[order]
    tok_sorted = tok_flat[order]
    w_sorted = w_flat[order]
    counts = jnp.bincount(e_flat, length=N_EXPERTS)
    padded = ((counts + MOE_BLOCK - 1) // MOE_BLOCK) * MOE_BLOCK
    starts = jnp.cumsum(counts) - counts
    padded_ends = jnp.cumsum(padded)
    padded_starts = padded_ends - padded
    rank = jnp.arange(A, dtype=jnp.int32) - starts[e_sorted]
    dest = padded_starts[e_sorted] + rank
    n_blocks = -(-A // MOE_BLOCK) + N_EXPERTS
    P = n_blocks * MOE_BLOCK
    tok_buf = jnp.full((P,), N, jnp.int32).at[dest].set(tok_sorted)
    gate_buf = jnp.zeros((P,), jnp.float32).at[dest].set(w_sorted)
    block_start = jnp.arange(n_blocks, dtype=jnp.int32) * MOE_BLOCK
    block_expert = jnp.minimum(jnp.searchsorted(padded_ends, block_start, side='right'), N_EXPERTS - 1)
    x_pad = jnp.concatenate([xt, jnp.zeros((1, D), xt.dtype)], axis=0)

    def expert_block(args):
        tok, e = args
        xb = x_pad[tok]
        gu = xb @ w_gate_up[e] + b_gate_up[e]
        gate, up = jnp.split(gu, 2, axis=-1)
        gate = jnp.minimum(gate, SWIGLU_LIMIT)
        up = jnp.clip(up, -SWIGLU_LIMIT, SWIGLU_LIMIT)
        act = (up + 1) * gate * jax.nn.sigmoid(SWIGLU_ALPHA * gate)
        return act @ w_down[e] + b_down[e]

    y_buf = lax.map(expert_block, (tok_buf.reshape(n_blocks, MOE_BLOCK), block_expert))
    y_buf = y_buf.reshape(P, D)
    y_buf = y_buf * gate_buf[:, None].astype(y_buf.dtype)
    out = jnp.zeros((N + 1, D), y_buf.dtype).at[tok_buf].add(y_buf)[:N]
    return out.reshape(B, S, D)


def setup_inputs(seed: int = 0) -> dict:
    key = jax.random.key(seed)
    ks = jax.random.split(key, 28)
    f32 = jnp.float32
    nrm = lambda k, shape, s: jax.random.normal(k, shape, f32) * s
    D = D_MODEL
    return {
        'x': nrm(ks[0], (BATCH, SEQ, D), 1.0),
        'mem': nrm(ks[1], (BATCH, MEM_LEN, D), 1.0),
        'norm_mix_g': 1.0 + nrm(ks[2], (D,), 0.02),
        'w_in': nrm(ks[3], (D, IN_COLS), D ** -0.5),
        'b_forget': 3.0 + nrm(ks[4], (FOX_HEADS,), 0.1),
        'sg_ln_g': 1.0 + nrm(ks[5], (SG_WIDTH,), 0.02),
        'sg_ln_b': nrm(ks[6], (SG_WIDTH,), 0.02),
        'w_spatial': nrm(ks[7], (SG_GROUPS, SG_CHUNK, SG_CHUNK), SG_CHUNK ** -0.5),
        'b_spatial': 1.0 + nrm(ks[8], (SG_GROUPS, SG_CHUNK), 0.02),
        'w_branch_a': nrm(ks[9], (FOX_WIDTH, D), FOX_WIDTH ** -0.5),
        'w_branch_b': nrm(ks[10], (SG_WIDTH, D), SG_WIDTH ** -0.5),
        'w_out': nrm(ks[11], (D, D), D ** -0.5),
        'norm_x_g': 1.0 + nrm(ks[12], (D,), 0.02),
        'norm_mem_g': 1.0 + nrm(ks[13], (D,), 0.02),
        'w_xq': nrm(ks[14], (D, X_WIDTH), D ** -0.5),
        'w_xkv': nrm(ks[15], (D, 2 * X_WIDTH), D ** -0.5),
        'w_xo': nrm(ks[16], (X_WIDTH, D), X_WIDTH ** -0.5),
        'norm_ffn_g': 1.0 + nrm(ks[17], (D,), 0.02),
        'w_router': nrm(ks[18], (D, N_EXPERTS), D ** -0.5),
        'b_router': nrm(ks[19], (N_EXPERTS,), 0.01),
        'w_gate_up': nrm(ks[20], (N_EXPERTS, D, 2 * D_FF_EXPERT), D ** -0.5),
        'b_gate_up': nrm(ks[21], (N_EXPERTS, 2 * D_FF_EXPERT), 0.01),
        'w_down': nrm(ks[22], (N_EXPERTS, D_FF_EXPERT, D), D_FF_EXPERT ** -0.5),
        'b_down': nrm(ks[23], (N_EXPERTS, D), 0.01),
        'norm_final_g': 1.0 + nrm(ks[24], (D,), 0.02),
    }


def reference(x, mem, norm_mix_g, w_in, b_forget, sg_ln_g, sg_ln_b, w_spatial, b_spatial,
              w_branch_a, w_branch_b, w_out, norm_x_g, norm_mem_g, w_xq, w_xkv, w_xo,
              norm_ffn_g, w_router, b_router, w_gate_up, b_gate_up, w_down, b_down, norm_final_g):
    B, S, D = x.shape
    for _ in range(DEPTH):
        h = rmsnorm(x, norm_mix_g)
        proj = h @ w_in
        q, k, v, f_logit, z, gates = jnp.split(proj, SPLITS, axis=-1)
        logf = jax.nn.log_sigmoid(f_logit.astype(jnp.float32) + b_forget.astype(jnp.float32))
        y_a = forgetting_attention(q.reshape(B, S, FOX_HEADS, FOX_HEAD_DIM),
                                   k.reshape(B, S, FOX_HEADS, FOX_HEAD_DIM),
                                   v.reshape(B, S, FOX_HEADS, FOX_HEAD_DIM), logf) @ w_branch_a
        y_b = spatial_gating(jax.nn.gelu(z), sg_ln_g, sg_ln_b, w_spatial, b_spatial) @ w_branch_b
        g_a, g_b = jnp.split(gates, 2, axis=-1)
        merged = jax.nn.sigmoid(g_a) * y_a + jax.nn.sigmoid(g_b) * y_b
        x = x + merged @ w_out
        x = x + memory_cross_attention(rmsnorm(x, norm_x_g), rmsnorm(mem, norm_mem_g), w_xq, w_xkv, w_xo)
        x = x + moe(rmsnorm(x, norm_ffn_g), w_router, b_router, w_gate_up, b_gate_up, w_down, b_down)
    return rmsnorm(x, norm_final_g)
```

```python
import functools
from typing import NamedTuple

import jax
import jax.numpy as jnp
from jax import lax
from jax.experimental import pallas as pl
from jax.experimental.pallas import tpu as pltpu

F32 = jnp.float32
BF16 = jnp.bfloat16
U32 = jnp.uint32
I32 = jnp.int32

LANES = 128
VMEM_LIMIT = 56 * 1024 * 1024

RMS_EPS = 1e-6
LN_EPS = 1e-5
SWIGLU_LIMIT = 7.0
SWIGLU_ALPHA = 1.702
GELU_C = 0.7978845608028654


class Cfg(NamedTuple):
    B: int
    S: int
    D: int
    MEM: int
    FH: int
    SG: int
    SGC: int
    XH: int
    E: int
    K: int
    DFF: int
    tm_in: int
    tn_in: int
    tq: int
    tm_mix: int
    tm_x: int
    tm_e: int
    ts_e: int
    tf_e: int
    tm_c: int
    tc_cs: int


def _cparams(sem):
    return pltpu.CompilerParams(dimension_semantics=sem, vmem_limit_bytes=VMEM_LIMIT)


def _rms(x, g):
    ms = jnp.mean(x * x, axis=-1, keepdims=True)
    return x * lax.rsqrt(ms + RMS_EPS) * g


def _pack_bf16_pair(a, b):
    def rne(v):
        bits = lax.bitcast_convert_type(v, U32)
        return bits + jnp.uint32(0x7FFF) + ((bits >> 16) & jnp.uint32(1))
    return (rne(a) >> 16) | (rne(b) & jnp.uint32(0xFFFF0000))


def _unpack_lo(p):
    return lax.bitcast_convert_type(p << 16, F32)


def _unpack_hi(p):
    return lax.bitcast_convert_type(p & jnp.uint32(0xFFFF0000), F32)


def _in_proj_kernel(x_ref, g_ref, w_ref, wf_ref, bf_ref, o_ref, f_ref, h_ref, *, nj_gate, nj_z, nj_q, qscale):
    j = pl.program_id(1)

    @pl.when(j == 0)
    def _():
        hb = _rms(x_ref[...], g_ref[...]).astype(BF16)
        h_ref[...] = hb
        f = jnp.dot(hb, wf_ref[...], preferred_element_type=F32) + bf_ref[...]
        f_ref[...] = jnp.minimum(f, 0.0) - jnp.log1p(jnp.exp(-jnp.abs(f)))

    acc = jnp.dot(h_ref[...], w_ref[...], preferred_element_type=F32)

    @pl.when(j < nj_gate)
    def _():
        o_ref[...] = jax.nn.sigmoid(acc).astype(o_ref.dtype)

    @pl.when((j >= nj_gate) & (j < nj_gate + nj_z))
    def _():
        inner = GELU_C * (acc + 0.044715 * (acc * acc * acc))
        o_ref[...] = (0.5 * acc * (1.0 + jnp.tanh(inner))).astype(o_ref.dtype)

    @pl.when((j >= nj_gate + nj_z) & (j < nj_gate + nj_z + nj_q))
    def _():
        o_ref[...] = (acc * qscale).astype(o_ref.dtype)

    @pl.when(j >= nj_gate + nj_z + nj_q)
    def _():
        o_ref[...] = acc.astype(o_ref.dtype)


def _in_proj(c, x2d, g, w_main, w_f, b_f, qscale):
    n, d = x2d.shape
    nc = w_main.shape[1]
    tm, tn = c.tm_in, c.tn_in
    fw = c.FH * LANES
    kern = functools.partial(_in_proj_kernel, nj_gate=2 * d // tn, nj_z=2 * c.SG * LANES // tn,
                             nj_q=fw // tn, qscale=qscale)
    return pl.pallas_call(
        kern,
        grid=(n // tm, nc // tn),
        in_specs=[
            pl.BlockSpec((tm, d), lambda i, j: (i, 0)),
            pl.BlockSpec((1, d), lambda i, j: (0, 0)),
            pl.BlockSpec((d, tn), lambda i, j: (0, j)),
            pl.BlockSpec((d, LANES), lambda i, j: (0, 0)),
            pl.BlockSpec((1, LANES), lambda i, j: (0, 0)),
        ],
        out_specs=[
            pl.BlockSpec((tm, tn), lambda i, j: (i, j)),
            pl.BlockSpec((tm, LANES), lambda i, j: (i, 0)),
        ],
        out_shape=[jax.ShapeDtypeStruct((n, nc), BF16), jax.ShapeDtypeStruct((n, LANES), F32)],
        scratch_shapes=[pltpu.VMEM((tm, d), BF16)],
        compiler_params=_cparams(("parallel", "arbitrary")),
        name="in_proj",
    )(x2d, g, w_main, w_f, b_f)


def _cumsum_kernel(f_ref, o_ref, *, tc):
    rows, s = f_ref.shape[1], f_ref.shape[2]
    r = lax.broadcasted_iota(I32, (tc, tc), 0)
    col = lax.broadcasted_iota(I32, (tc, tc), 1)
    upper = (r <= col).astype(F32)
    carry = jnp.zeros((rows, 1), F32)
    for i in range(s // tc):
        blk = f_ref[0, :, i * tc:(i + 1) * tc]
        cs = jnp.dot(blk, upper, preferred_element_type=F32, precision=lax.Precision.HIGHEST) + carry
        o_ref[0, :, i * tc:(i + 1) * tc] = cs
        carry = cs[:, tc - 1:tc]


def _cumsum(c, logf_bhs):
    b, h, s = logf_bhs.shape
    return pl.pallas_call(
        functools.partial(_cumsum_kernel, tc=c.tc_cs),
        grid=(b,),
        in_specs=[pl.BlockSpec((1, h, s), lambda i: (i, 0, 0))],
        out_specs=pl.BlockSpec((1, h, s), lambda i: (i, 0, 0)),
        out_shape=jax.ShapeDtypeStruct((b, h, s), F32),
        compiler_params=_cparams(("parallel",)),
        name="cumsum",
    )(logf_bhs)


def _fox_kernel(q_ref, k_ref, v_ref, crow_ref, ccol_ref, o_ref, m_ref, l_ref, acc_ref, *, t):
    qi = pl.program_id(2)
    q = q_ref[...]
    cq = ccol_ref[0]
    m_ref[...] = jnp.full(m_ref.shape, -jnp.inf, F32)
    l_ref[...] = jnp.zeros(l_ref.shape, F32)
    acc_ref[...] = jnp.zeros(acc_ref.shape, F32)

    def chunk(kc, masked):
        ks = pl.multiple_of(kc * t, t)
        k = k_ref[pl.ds(ks, t), :]
        v = v_ref[pl.ds(ks, t), :]
        s = lax.dot_general(q, k, (((1,), (1,)), ((), ())), preferred_element_type=F32)
        s = s + cq - crow_ref[0, :, pl.ds(ks, t)]
        if masked:
            row = lax.broadcasted_iota(I32, (t, t), 0)
            col = lax.broadcasted_iota(I32, (t, t), 1)
            s = jnp.where(col <= row, s, -jnp.inf)
        m_old = m_ref[...]
        m_new = jnp.maximum(m_old, jnp.max(s, axis=-1, keepdims=True))
        alpha = jnp.exp(m_old - m_new)
        p = jnp.exp(s - m_new)
        l_ref[...] = alpha * l_ref[...] + jnp.sum(p, axis=-1, keepdims=True)
        acc_ref[...] = alpha * acc_ref[...] + jnp.dot(p.astype(BF16), v, preferred_element_type=F32)
        m_ref[...] = m_new

    def body(kc, carry):
        chunk(kc, False)
        return carry

    lax.fori_loop(0, qi, body, 0)
    chunk(qi, True)
    o_ref[...] = (acc_ref[...] / l_ref[...]).astype(o_ref.dtype)


def _fox(c, proj, crow, ccol, qcol0):
    n = c.B * c.S
    t = c.tq
    nq = c.S // t
    h = c.FH
    return pl.pallas_call(
        functools.partial(_fox_kernel, t=t),
        grid=(c.B, h, nq),
        in_specs=[
            pl.BlockSpec((t, LANES), lambda b, hh, qi: (b * nq + qi, qcol0 + hh)),
            pl.BlockSpec((c.S, LANES), lambda b, hh, qi: (b, qcol0 + h + hh)),
            pl.BlockSpec((c.S, LANES), lambda b, hh, qi: (b, qcol0 + 2 * h + hh)),
            pl.BlockSpec((1, 1, c.S), lambda b, hh, qi: (b * h + hh, 0, 0)),
            pl.BlockSpec((1, t, 1), lambda b, hh, qi: (b * h + hh, qi, 0)),
        ],
        out_specs=pl.BlockSpec((t, LANES), lambda b, hh, qi: (b * nq + qi, hh)),
        out_shape=jax.ShapeDtypeStruct((n, h * LANES), BF16),
        scratch_shapes=[pltpu.VMEM((t, 1), F32), pltpu.VMEM((t, 1), F32), pltpu.VMEM((t, LANES), F32)],
        compiler_params=_cparams(("parallel", "parallel", "arbitrary")),
        name="fox",
    )(proj, proj, proj, crow, ccol)


def _mix_kernel(x_ref, a_ref, u_ref, v_ref, ga_ref, gb_ref, lng_ref, lnb_ref, ws_ref, bs_ref,
                wa_ref, wb_ref, wo_ref, o_ref, sg_ref, *, sgc, groups):
    tm = x_ref.shape[0]
    v = v_ref[...].astype(F32)
    mu = jnp.mean(v, axis=-1, keepdims=True)
    vc = v - mu
    var = jnp.mean(vc * vc, axis=-1, keepdims=True)
    vn = (vc * lax.rsqrt(var + LN_EPS) * lng_ref[...] + lnb_ref[...]).astype(BF16)
    row = lax.broadcasted_iota(I32, (sgc, sgc), 0)
    col = lax.broadcasted_iota(I32, (sgc, sgc), 1)
    for g in range(groups):
        w = jnp.where(col <= row, ws_ref[g], 0.0).astype(BF16)
        bias = bs_ref[g]
        for ci in range(tm // sgc):
            rs = slice(ci * sgc, (ci + 1) * sgc)
            cs = slice(g * LANES, (g + 1) * LANES)
            mixed = jnp.dot(w, vn[rs, cs], preferred_element_type=F32) + bias
            sg_ref[rs, cs] = (u_ref[rs, cs].astype(F32) * mixed).astype(BF16)
    ya = jnp.dot(a_ref[...], wa_ref[...], preferred_element_type=F32)
    yb = jnp.dot(sg_ref[...], wb_ref[...], preferred_element_type=F32)
    merged = (ga_ref[...].astype(F32) * ya + gb_ref[...].astype(F32) * yb).astype(BF16)
    o_ref[...] = x_ref[...] + jnp.dot(merged, wo_ref[...], preferred_element_type=F32)


def _const_spec(shape):
    nd = len(shape)
    return pl.BlockSpec(shape, lambda i: (0,) * nd, pipeline_mode=pl.Buffered(1))


def _mix(c, x2d, attn, proj, ln_g, ln_b, w_s, b_s_full, w_a, w_b, w_o):
    n, d = x2d.shape
    tm = c.tm_mix
    fw = c.FH * LANES
    sw = c.SG * LANES
    ucol = 2 * d // sw
    return pl.pallas_call(
        functools.partial(_mix_kernel, sgc=c.SGC, groups=c.SG),
        grid=(n // tm,),
        in_specs=[
            pl.BlockSpec((tm, d), lambda i: (i, 0)),
            pl.BlockSpec((tm, fw), lambda i: (i, 0)),
            pl.BlockSpec((tm, sw), lambda i: (i, ucol)),
            pl.BlockSpec((tm, sw), lambda i: (i, ucol + 1)),
            pl.BlockSpec((tm, d), lambda i: (i, 0)),
            pl.BlockSpec((tm, d), lambda i: (i, 1)),
            _const_spec((1, sw)),
            _const_spec((1, sw)),
            _const_spec((c.SG, c.SGC, c.SGC)),
            _const_spec((c.SG, c.SGC, LANES)),
            _const_spec((fw, d)),
            _const_spec((sw, d)),
            _const_spec((d, d)),
        ],
        out_specs=pl.BlockSpec((tm, d), lambda i: (i, 0)),
        out_shape=jax.ShapeDtypeStruct((n, d), F32),
        scratch_shapes=[pltpu.VMEM((tm, sw), BF16)],
        compiler_params=_cparams(("parallel",)),
        name="mix",
    )(x2d, attn, proj, proj, proj, proj, ln_g, ln_b, w_s, b_s_full, w_a, w_b, w_o)


def _mem_kv_kernel(m_ref, g_ref, w_ref, o_ref):
    hm = _rms(m_ref[...], g_ref[...]).astype(BF16)
    o_ref[...] = jnp.dot(hm, w_ref[...], preferred_element_type=F32).astype(o_ref.dtype)


def _mem_kv(c, mem2d, g, w_xkv):
    n, d = mem2d.shape
    nc = w_xkv.shape[1]
    tm = c.MEM
    return pl.pallas_call(
        _mem_kv_kernel,
        grid=(n // tm,),
        in_specs=[pl.BlockSpec((tm, d), lambda i: (i, 0)), _const_spec((1, d)), _const_spec((d, nc))],
        out_specs=pl.BlockSpec((tm, nc), lambda i: (i, 0)),
        out_shape=jax.ShapeDtypeStruct((n, nc), BF16),
        compiler_params=_cparams(("parallel",)),
        name="mem_kv",
    )(mem2d, g, w_xkv)


def _xattn_kernel(x1_ref, gx_ref, wq_ref, kv_ref, wo_ref, gf_ref, wr_ref, br_ref,
                  x2_ref, hfp_ref, idx_ref, gate_ref, rank_ref, cnt_ref, carry_ref,
                  *, heads, n_exp, topk, qscale):
    i = pl.program_id(0)
    tm, d = x1_ref.shape
    xw = heads * LANES
    x1 = x1_ref[...]
    hx = _rms(x1, gx_ref[...]).astype(BF16)
    q = (jnp.dot(hx, wq_ref[...], preferred_element_type=F32) * qscale).astype(BF16)
    outs = []
    for h in range(heads):
        k = kv_ref[:, h * LANES:(h + 1) * LANES]
        v = kv_ref[:, xw + h * LANES:xw + (h + 1) * LANES]
        s = lax.dot_general(q[:, h * LANES:(h + 1) * LANES], k, (((1,), (1,)), ((), ())),
                            preferred_element_type=F32)
        s = s - jnp.max(s, axis=-1, keepdims=True)
        p = jnp.exp(s)
        p = p / jnp.sum(p, axis=-1, keepdims=True)
        outs.append(jnp.dot(p.astype(BF16), v, preferred_element_type=F32).astype(BF16))
    o = jnp.concatenate(outs, axis=-1)
    x2 = x1 + jnp.dot(o, wo_ref[...], preferred_element_type=F32)
    x2_ref[...] = x2

    hf = _rms(x2, gf_ref[...])
    hfp_ref[...] = _pack_bf16_pair(hf[:, :d // 2], hf[:, d // 2:])

    logits = jnp.dot(hf, wr_ref[...], preferred_element_type=F32, precision=lax.Precision.HIGHEST) + br_ref[...]
    lane = lax.broadcasted_iota(I32, (tm, LANES), 1)
    lg = jnp.where(lane < n_exp, logits, -jnp.inf)
    vals, idxs = [], []
    for _ in range(topk):
        m = jnp.max(lg, axis=-1, keepdims=True)
        ix = jnp.min(jnp.where(lg == m, lane, LANES), axis=-1, keepdims=True)
        vals.append(m)
        idxs.append(ix)
        lg = jnp.where(lane == ix, -jnp.inf, lg)
    es = [jnp.exp(vv - vals[0]) for vv in vals]
    denom = es[0]
    for e in es[1:]:
        denom = denom + e

    @pl.when(i == 0)
    def _():
        carry_ref[...] = jnp.zeros(carry_ref.shape, F32)

    onehot = jnp.zeros((tm, LANES), F32)
    for ix in idxs:
        onehot = onehot + (lane == ix).astype(F32)
    r = lax.broadcasted_iota(I32, (tm, tm), 0)
    cc = lax.broadcasted_iota(I32, (tm, tm), 1)
    strict = (cc < r).astype(BF16)
    rank_full = jnp.dot(strict, onehot.astype(BF16), preferred_element_type=F32) + carry_ref[...]
    idx_out = jnp.zeros((tm, LANES), I32)
    gate_out = jnp.zeros((tm, LANES), F32)
    rank_out = jnp.zeros((tm, LANES), F32)
    for kk in range(topk):
        rk = jnp.sum(jnp.where(lane == idxs[kk], rank_full, 0.0), axis=-1, keepdims=True)
        idx_out = jnp.where(lane == kk, idxs[kk], idx_out)
        gate_out = jnp.where(lane == kk, es[kk] / denom, gate_out)
        rank_out = jnp.where(lane == kk, rk, rank_out)
    idx_ref[...] = idx_out
    gate_ref[...] = gate_out
    rank_ref[...] = rank_out.astype(I32)
    carry_ref[...] = carry_ref[...] + jnp.sum(onehot, axis=0, keepdims=True)
    cnt_ref[...] = carry_ref[...]


def _xattn(c, x1, gx, w_xq, kv, w_xo, gf, w_r, b_r, qscale):
    n, d = x1.shape
    tm = c.tm_x
    xw = c.XH * LANES
    per_b = c.S // tm
    kern = functools.partial(_xattn_kernel, heads=c.XH, n_exp=c.E, topk=c.K, qscale=qscale)
    return pl.pallas_call(
        kern,
        grid=(n // tm,),
        in_specs=[
            pl.BlockSpec((tm, d), lambda i: (i, 0)),
            _const_spec((1, d)),
            _const_spec((d, xw)),
            pl.BlockSpec((c.MEM, 2 * xw), lambda i: (i // per_b, 0)),
            _const_spec((xw, d)),
            _const_spec((1, d)),
            _const_spec((d, LANES)),
            _const_spec((1, LANES)),
        ],
        out_specs=[
            pl.BlockSpec((tm, d), lambda i: (i, 0)),
            pl.BlockSpec((tm, d // 2), lambda i: (i, 0)),
            pl.BlockSpec((tm, LANES), lambda i: (i, 0)),
            pl.BlockSpec((tm, LANES), lambda i: (i, 0)),
            pl.BlockSpec((tm, LANES), lambda i: (i, 0)),
            pl.BlockSpec((1, LANES), lambda i: (0, 0)),
        ],
        out_shape=[
            jax.ShapeDtypeStruct((n, d), F32),
            jax.ShapeDtypeStruct((n, d // 2), U32),
            jax.ShapeDtypeStruct((n, LANES), I32),
            jax.ShapeDtypeStruct((n, LANES), F32),
            jax.ShapeDtypeStruct((n, LANES), I32),
            jax.ShapeDtypeStruct((1, LANES), F32),
        ],
        scratch_shapes=[pltpu.VMEM((1, LANES), F32)],
        compiler_params=_cparams(("arbitrary",)),
        name="xattn_router",
    )(x1, gx, w_xq, kv, w_xo, gf, w_r, b_r)


def _dispatch_kernel(dest_ref, zrow_ref, hfp_ref, xs_ref, zbuf_ref, sem, zsem, *, n_exp, topk, unroll):
    i = pl.program_id(0)
    tm = hfp_ref.shape[0]
    ts = zbuf_ref.shape[0]

    def zero_copy(e):
        return pltpu.make_async_copy(zbuf_ref, xs_ref.at[pl.ds(pl.multiple_of(zrow_ref[e], ts), ts)], zsem)

    @pl.when(i == 0)
    def _():
        zbuf_ref[...] = jnp.zeros(zbuf_ref.shape, zbuf_ref.dtype)
        for e in range(n_exp):
            @pl.when(zrow_ref[e] >= 0)
            def _():
                zero_copy(e).start()
        for e in range(n_exp):
            @pl.when(zrow_ref[e] >= 0)
            def _():
                zero_copy(e).wait()

    def row_copy(t, a):
        return pltpu.make_async_copy(hfp_ref.at[pl.ds(t, 1)], xs_ref.at[pl.ds(dest_ref[a], 1)], sem)

    def issue(tb, carry):
        for u in range(unroll):
            t = tb * unroll + u
            for kk in range(topk):
                row_copy(t, t * topk + kk).start()
        return carry

    lax.fori_loop(0, tm // unroll, issue, 0)

    def drain(tb, carry):
        for u in range(unroll * topk):
            row_copy(0, 0).wait()
        return carry

    lax.fori_loop(0, tm // unroll, drain, 0)


def _dispatch(c, dest_flat, zrow, hfp, p_rows):
    n, dh = hfp.shape
    tm = c.tm_c
    kern = functools.partial(_dispatch_kernel, n_exp=c.E, topk=c.K, unroll=8)
    return pl.pallas_call(
        kern,
        grid=(n // tm,),
        in_specs=[
            pl.BlockSpec((tm * c.K,), lambda i: (i,), memory_space=pltpu.SMEM),
            pl.BlockSpec(memory_space=pltpu.SMEM),
            pl.BlockSpec((tm, dh), lambda i: (i, 0)),
        ],
        out_specs=pl.BlockSpec(memory_space=pl.ANY),
        out_shape=jax.ShapeDtypeStruct((p_rows, dh), U32),
        scratch_shapes=[pltpu.VMEM((c.ts_e, dh), U32), pltpu.SemaphoreType.DMA(()), pltpu.SemaphoreType.DMA(())],
        compiler_params=_cparams(("arbitrary",)),
        name="dispatch",
    )(dest_flat, zrow, hfp)


def _expert_kernel(te_ref, ns_ref, x_ref, wg_ref, wu_ref, bg_ref, bu_ref, wd_ref, bd_ref, y_ref,
                   xb_ref, wgb_ref, wub_ref, wdb_ref, acc_ref, *, ts):
    t = pl.program_id(0)
    f = pl.program_id(1)
    nf = pl.num_programs(1)
    nsub = ns_ref[t]
    dh = x_ref.shape[1]

    @pl.when(nsub > 0)
    def _():
        @pl.when(f == 0)
        def _():
            acc_ref[...] = jnp.broadcast_to(bd_ref[0], acc_ref.shape)

            def unpack(s, carry):
                r = pl.multiple_of(s * ts, ts)
                p = x_ref[pl.ds(r, ts), :]
                xb_ref[pl.ds(r, ts), :dh] = _unpack_lo(p).astype(BF16)
                xb_ref[pl.ds(r, ts), dh:] = _unpack_hi(p).astype(BF16)
                return carry

            lax.fori_loop(0, nsub, unpack, 0)

        wgb_ref[...] = wg_ref[0].astype(BF16)
        wub_ref[...] = wu_ref[0].astype(BF16)
        wdb_ref[...] = wd_ref[0].astype(BF16)
        bg = bg_ref[0]
        bu = bu_ref[0]

        def sub(s, carry):
            r = pl.multiple_of(s * ts, ts)
            xs = xb_ref[pl.ds(r, ts), :]
            gate = jnp.dot(xs, wgb_ref[...], preferred_element_type=F32) + bg
            up = jnp.dot(xs, wub_ref[...], preferred_element_type=F32) + bu
            gate = jnp.minimum(gate, SWIGLU_LIMIT)
            up = jnp.clip(up, -SWIGLU_LIMIT, SWIGLU_LIMIT)
            act = (up + 1.0) * gate * jax.nn.sigmoid(SWIGLU_ALPHA * gate)
            acc_ref[pl.ds(r, ts), :] += jnp.dot(act.astype(BF16), wdb_ref[...], preferred_element_type=F32)
            return carry

        lax.fori_loop(0, nsub, sub, 0)

        @pl.when(f == nf - 1)
        def _():
            y_ref[...] = _pack_bf16_pair(acc_ref[:, :dh], acc_ref[:, dh:])


def _experts(c, tile_expert, tile_nsub, tile_blk, xs, w_gu, b_gu3, w_d, b_d3):
    p_rows, dh = xs.shape
    d = 2 * dh
    tm, tf = c.tm_e, c.tf_e
    nf = c.DFF // tf
    n_tiles = p_rows // tm

    def fblk(t, f, ns):
        return jnp.where(ns[t] > 0, f, nf - 1)

    grid_spec = pltpu.PrefetchScalarGridSpec(
        num_scalar_prefetch=3,
        grid=(n_tiles, nf),
        in_specs=[
            pl.BlockSpec((tm, dh), lambda t, f, te, ns, tb: (tb[t], 0)),
            pl.BlockSpec((1, d, tf), lambda t, f, te, ns, tb: (te[t], 0, fblk(t, f, ns))),
            pl.BlockSpec((1, d, tf), lambda t, f, te, ns, tb: (te[t], 0, nf + fblk(t, f, ns))),
            pl.BlockSpec((1, 1, tf), lambda t, f, te, ns, tb: (te[t], 0, fblk(t, f, ns))),
            pl.BlockSpec((1, 1, tf), lambda t, f, te, ns, tb: (te[t], 0, nf + fblk(t, f, ns))),
            pl.BlockSpec((1, tf, d), lambda t, f, te, ns, tb: (te[t], fblk(t, f, ns), 0)),
            pl.BlockSpec((1, 1, d), lambda t, f, te, ns, tb: (te[t], 0, 0)),
        ],
        out_specs=pl.BlockSpec((tm, dh), lambda t, f, te, ns, tb: (tb[t], 0)),
        scratch_shapes=[
            pltpu.VMEM((tm, d), BF16),
            pltpu.VMEM((d, tf), BF16),
            pltpu.VMEM((d, tf), BF16),
            pltpu.VMEM((tf, d), BF16),
            pltpu.VMEM((tm, d), F32),
        ],
    )

    def kern(te_ref, ns_ref, tb_ref, *rest):
        _expert_kernel(te_ref, ns_ref, *rest, ts=c.ts_e)

    return pl.pallas_call(
        kern,
        grid_spec=grid_spec,
        out_shape=jax.ShapeDtypeStruct((p_rows, dh), U32),
        compiler_params=_cparams(("arbitrary", "arbitrary")),
        name="experts",
    )(tile_expert, tile_nsub, tile_blk, xs, w_gu, w_gu, b_gu3, b_gu3, w_d, b_d3)


def _combine_kernel(dest_ref, x2_ref, gate_ref, g_ref, ys_ref, o_ref, buf_ref, sem, *, topk, unroll):
    tm, d = x2_ref.shape
    dh = d // 2

    def row_copy(t, kk, a):
        return pltpu.make_async_copy(ys_ref.at[pl.ds(dest_ref[a], 1)], buf_ref.at[kk, pl.ds(t, 1)], sem)

    def issue(tb, carry):
        for u in range(unroll):
            t = tb * unroll + u
            for kk in range(topk):
                row_copy(t, kk, t * topk + kk).start()
        return carry

    lax.fori_loop(0, tm // unroll, issue, 0)

    def drain(tb, carry):
        for u in range(unroll * topk):
            row_copy(0, 0, 0).wait()
        return carry

    lax.fori_loop(0, tm // unroll, drain, 0)

    lo = x2_ref[:, :dh]
    hi = x2_ref[:, dh:]
    gates = gate_ref[...]
    for kk in range(topk):
        wk = gates[:, kk:kk + 1]
        p = buf_ref[kk]
        lo = lo + wk * _unpack_lo(p)
        hi = hi + wk * _unpack_hi(p)
    ms = (jnp.sum(lo * lo, axis=-1, keepdims=True) + jnp.sum(hi * hi, axis=-1, keepdims=True)) / d
    inv = lax.rsqrt(ms + RMS_EPS)
    o_ref[:, :dh] = lo * inv * g_ref[:, :dh]
    o_ref[:, dh:] = hi * inv * g_ref[:, dh:]


def _combine(c, dest_flat, x2, gates, g, ys):
    n, d = x2.shape
    tm = c.tm_c
    kern = functools.partial(_combine_kernel, topk=c.K, unroll=8)
    return pl.pallas_call(
        kern,
        grid=(n // tm,),
        in_specs=[
            pl.BlockSpec((tm * c.K,), lambda i: (i,), memory_space=pltpu.SMEM),
            pl.BlockSpec((tm, d), lambda i: (i, 0)),
            pl.BlockSpec((tm, LANES), lambda i: (i, 0)),
            pl.BlockSpec((1, d), lambda i: (0, 0)),
            pl.BlockSpec(memory_space=pl.ANY),
        ],
        out_specs=pl.BlockSpec((tm, d), lambda i: (i, 0)),
        out_shape=jax.ShapeDtypeStruct((n, d), F32),
        scratch_shapes=[pltpu.VMEM((c.K, tm, d // 2), U32), pltpu.SemaphoreType.DMA(())],
        compiler_params=_cparams(("arbitrary",)),
        name="combine",
    )(dest_flat, x2, gates, g, ys)


def _plan(c, idx, rank, counts):
    tm, ts = c.tm_e, c.ts_e
    counts = counts.astype(I32)
    tiles_per_e = (counts + tm - 1) // tm
    tile_end = jnp.cumsum(tiles_per_e)
    tile_start = tile_end - tiles_per_e
    row_start = tile_start * tm
    dest = row_start[idx] + rank
    n_tiles = (c.B * c.S * c.K) // tm + c.E
    t = jnp.arange(n_tiles, dtype=I32)
    n_used = tile_end[-1]
    tc = jnp.minimum(t, n_used - 1)
    te = jnp.minimum(jnp.searchsorted(tile_end, tc, side="right"), c.E - 1).astype(I32)
    rows_left = counts[te] - (tc - tile_start[te]) * tm
    nsub = jnp.clip((rows_left + ts - 1) // ts, 0, tm // ts)
    nsub = jnp.where(t < n_used, nsub, 0).astype(I32)
    padded_rows = ((counts + ts - 1) // ts) * ts
    zrow = jnp.where(counts > 0, row_start + padded_rows - ts, -1).astype(I32)
    return dest.reshape(-1).astype(I32), te, nsub, tc.astype(I32), zrow


def _forward(c, x, mem, norm_mix_g, w_in, b_forget, sg_ln_g, sg_ln_b, w_spatial, b_spatial,
             w_branch_a, w_branch_b, w_out, norm_x_g, norm_mem_g, w_xq, w_xkv, w_xo,
             norm_ffn_g, w_router, b_router, w_gate_up, b_gate_up, w_down, b_down, norm_final_g):
    B, S, D = x.shape
    n = B * S
    fw = c.FH * LANES
    sw = c.SG * LANES
    x2d = x.reshape(n, D)

    o_f = 3 * fw
    o_z = o_f + c.FH
    o_g = o_z + 2 * sw
    w_main = jnp.concatenate([w_in[:, o_g:], w_in[:, o_z:o_g], w_in[:, :o_f]], axis=1).astype(BF16)
    w_f = jnp.pad(w_in[:, o_f:o_z], ((0, 0), (0, LANES - c.FH))).astype(BF16)
    b_f = jnp.pad(b_forget.astype(F32), (0, LANES - c.FH)).reshape(1, LANES)
    proj, logf = _in_proj(c, x2d, norm_mix_g.reshape(1, D), w_main, w_f, b_f, LANES ** -0.5)

    logf_bhs = logf[:, :c.FH].reshape(B, S, c.FH).transpose(0, 2, 1)
    csum = _cumsum(c, logf_bhs)
    crow = csum.reshape(B * c.FH, 1, S)
    ccol = csum.reshape(B * c.FH, S, 1)
    qcol0 = (2 * D + 2 * sw) // LANES
    attn = _fox(c, proj, crow, ccol, qcol0)

    b_s_full = jnp.broadcast_to(b_spatial.astype(F32)[:, :, None], (c.SG, c.SGC, LANES))
    x1 = _mix(c, x2d, attn, proj, sg_ln_g.reshape(1, sw), sg_ln_b.reshape(1, sw), w_spatial, b_s_full,
              w_branch_a.astype(BF16), w_branch_b.astype(BF16), w_out.astype(BF16))

    kv = _mem_kv(c, mem.reshape(B * c.MEM, D), norm_mem_g.reshape(1, D), w_xkv.astype(BF16))
    w_r = jnp.pad(w_router.astype(F32), ((0, 0), (0, LANES - c.E)))
    b_r = jnp.pad(b_router.astype(F32), (0, LANES - c.E)).reshape(1, LANES)
    x2, hfp, idx, gates, rank, counts = _xattn(
        c, x1, norm_x_g.reshape(1, D), w_xq.astype(BF16), kv, w_xo.astype(BF16),
        norm_ffn_g.reshape(1, D), w_r, b_r, LANES ** -0.5)

    dest, te, nsub, tblk, zrow = _plan(c, idx[:, :c.K], rank[:, :c.K], counts[0, :c.E])
    p_rows = (n * c.K // c.tm_e + c.E) * c.tm_e
    xs = _dispatch(c, dest, zrow, hfp, p_rows)
    ys = _experts(c, te, nsub, tblk, xs, w_gate_up, b_gate_up.reshape(c.E, 1, 2 * c.DFF),
                  w_down, b_down.reshape(c.E, 1, D))
    out = _combine(c, dest, x2, gates, norm_final_g.reshape(1, D), ys)
    return out.reshape(B, S, D)


_CFG = Cfg(B=4, S=4096, D=2048, MEM=256, FH=8, SG=8, SGC=128, XH=4, E=32, K=4, DFF=2048,
           tm_in=1024, tn_in=512, tq=512, tm_mix=256, tm_x=256, tm_e=1024, ts_e=256, tf_e=256,
           tm_c=256, tc_cs=512)


@jax.jit
def kernel(x, mem, norm_mix_g, w_in, b_forget, sg_ln_g, sg_ln_b, w_spatial, b_spatial, w_branch_a, w_branch_b,
           w_out, norm_x_g, norm_mem_g, w_xq, w_xkv, w_xo, norm_ffn_g, w_router, b_router, w_gate_up,
           b_gate_up, w_down, b_down, norm_final_g):
    return _forward(_CFG, x, mem, norm_mix_g, w_in, b_forget, sg_ln_g, sg_ln_b, w_spatial, b_spatial,
                    w_branch_a, w_branch_b, w_out, norm_x_g, norm_mem_g, w_xq, w_xkv, w_xo,
                    norm_ffn_g, w_router, b_router, w_gate_up, b_gate_up, w_down, b_down, norm_final_g)
```

```python
import functools
from typing import NamedTuple

import jax
import jax.numpy as jnp
from jax import lax
from jax.experimental import pallas as pl
from jax.experimental.pallas import tpu as pltpu

F32 = jnp.float32
BF16 = jnp.bfloat16
U32 = jnp.uint32
I32 = jnp.int32

LANES = 128
VMEM_LIMIT = 56 * 1024 * 1024

RMS_EPS = 1e-6
LN_EPS = 1e-5
SWIGLU_LIMIT = 7.0
SWIGLU_ALPHA = 1.702
GELU_C = 0.7978845608028654
LOG2E = 1.4426950408889634


class Cfg(NamedTuple):
    B: int
    S: int
    D: int
    MEM: int
    FH: int
    SG: int
    SGC: int
    XH: int
    E: int
    K: int
    DFF: int
    tm_in: int
    tn_in: int
    tq: int
    tm_mix: int
    tm_x: int
    tm_e: int
    ts_e: int
    tf_e: int
    tm_c: int
    tc_cs: int
    fox_parts: int


def _cparams(sem):
    return pltpu.CompilerParams(dimension_semantics=sem, vmem_limit_bytes=VMEM_LIMIT)


def _rms(x, g):
    ms = jnp.mean(x * x, axis=-1, keepdims=True)
    return x * lax.rsqrt(ms + RMS_EPS) * g


def _pack_bf16_pair(a, b):
    def rne(v):
        bits = lax.bitcast_convert_type(v, U32)
        return bits + jnp.uint32(0x7FFF) + ((bits >> 16) & jnp.uint32(1))
    return (rne(a) >> 16) | (rne(b) & jnp.uint32(0xFFFF0000))


def _unpack_lo(p):
    return lax.bitcast_convert_type(p << 16, F32)


def _unpack_hi(p):
    return lax.bitcast_convert_type(p & jnp.uint32(0xFFFF0000), F32)


def _in_proj_kernel(x_ref, g_ref, w_ref, wf_ref, bf_ref, o_ref, f_ref, h_ref, *, nj_gate, nj_z, nj_q, qscale):
    j = pl.program_id(1)

    @pl.when(j == 0)
    def _():
        hb = _rms(x_ref[...], g_ref[...]).astype(BF16)
        h_ref[...] = hb
        f = jnp.dot(hb, wf_ref[...], preferred_element_type=F32) + bf_ref[...]
        f_ref[...] = jnp.minimum(f, 0.0) - jnp.log1p(jnp.exp(-jnp.abs(f)))

    acc = jnp.dot(h_ref[...], w_ref[...], preferred_element_type=F32)

    is_gate = j < nj_gate
    is_z = (j >= nj_gate) & (j < nj_gate + nj_z)
    is_q = (j >= nj_gate + nj_z) & (j < nj_gate + nj_z + nj_q)
    a1 = jnp.where(is_gate, 0.5, jnp.where(is_z, GELU_C, 0.0)).astype(F32)
    a3 = jnp.where(is_z, GELU_C * 0.044715, 0.0).astype(F32)
    b0 = jnp.where(is_gate, 0.5, 0.0).astype(F32)
    b1 = jnp.where(is_gate, 0.0, jnp.where(is_z, 0.5, jnp.where(is_q, qscale, 1.0))).astype(F32)
    th = jnp.tanh(acc * (a1 + a3 * (acc * acc)))
    o_ref[...] = ((b0 + b1 * acc) * (1.0 + th)).astype(o_ref.dtype)


def _in_proj(c, x2d, g, w_main, w_f, b_f, qscale):
    n, d = x2d.shape
    nc = w_main.shape[1]
    tm, tn = c.tm_in, c.tn_in
    fw = c.FH * LANES
    kern = functools.partial(_in_proj_kernel, nj_gate=2 * d // tn, nj_z=2 * c.SG * LANES // tn,
                             nj_q=fw // tn, qscale=qscale)
    return pl.pallas_call(
        kern,
        grid=(n // tm, nc // tn),
        in_specs=[
            pl.BlockSpec((tm, d), lambda i, j: (i, 0)),
            pl.BlockSpec((1, d), lambda i, j: (0, 0)),
            pl.BlockSpec((d, tn), lambda i, j: (0, j)),
            pl.BlockSpec((d, LANES), lambda i, j: (0, 0)),
            pl.BlockSpec((1, LANES), lambda i, j: (0, 0)),
        ],
        out_specs=[
            pl.BlockSpec((tm, tn), lambda i, j: (i, j)),
            pl.BlockSpec((tm, LANES), lambda i, j: (i, 0)),
        ],
        out_shape=[jax.ShapeDtypeStruct((n, nc), BF16), jax.ShapeDtypeStruct((n, LANES), F32)],
        scratch_shapes=[pltpu.VMEM((tm, d), BF16)],
        compiler_params=_cparams(("parallel", "arbitrary")),
        name="in_proj",
    )(x2d, g, w_main, w_f, b_f)


def _cumsum_kernel(f_ref, o_ref, *, tc):
    rows, s = f_ref.shape[1], f_ref.shape[2]
    r = lax.broadcasted_iota(I32, (tc, tc), 0)
    col = lax.broadcasted_iota(I32, (tc, tc), 1)
    upper = (r <= col).astype(F32)
    carry = jnp.zeros((rows, 1), F32)
    for i in range(s // tc):
        blk = f_ref[0, :, i * tc:(i + 1) * tc]
        cs = jnp.dot(blk, upper, preferred_element_type=F32, precision=lax.Precision.HIGHEST) + carry
        o_ref[0, :, i * tc:(i + 1) * tc] = cs * LOG2E
        carry = cs[:, tc - 1:tc]


def _cumsum(c, logf_bhs):
    b, h, s = logf_bhs.shape
    return pl.pallas_call(
        functools.partial(_cumsum_kernel, tc=c.tc_cs),
        grid=(b,),
        in_specs=[pl.BlockSpec((1, h, s), lambda i: (i, 0, 0))],
        out_specs=pl.BlockSpec((1, h, s), lambda i: (i, 0, 0)),
        out_shape=jax.ShapeDtypeStruct((b, h, s), F32),
        compiler_params=_cparams(("parallel",)),
        name="cumsum",
    )(logf_bhs)


def _fox_kernel(q_ref, k_ref, v_ref, crow_ref, ccol_ref, o_ref, m_ref, l_ref, acc_ref, cq_ref, s_ref,
                *, t, parts):
    qi = pl.program_id(2)
    hr = t // parts
    m_ref[...] = jnp.full(m_ref.shape, -jnp.inf, F32)
    l_ref[...] = jnp.zeros(l_ref.shape, F32)
    acc_ref[...] = jnp.zeros(acc_ref.shape, F32)
    cq_ref[...] = jnp.broadcast_to(ccol_ref[0], cq_ref.shape)

    def scores(ks):
        k = k_ref[pl.ds(ks, t), :]
        return lax.dot_general(q_ref[...], k, (((1,), (1,)), ((), ())), preferred_element_type=F32)

    def block(part, ks, width, masked):
        rows = slice(part * hr, (part + 1) * hr)
        v = v_ref[pl.ds(ks, width), :]
        s = s_ref[rows, :width]
        cq = cq_ref[rows, :]
        crow = crow_ref[0, :, pl.ds(ks, width)]
        nj = width // LANES
        sj = [s[:, j * LANES:(j + 1) * LANES] + cq - crow[:, j * LANES:(j + 1) * LANES] for j in range(nj)]
        if masked:
            row = lax.broadcasted_iota(I32, (hr, LANES), 0) + part * hr
            col = lax.broadcasted_iota(I32, (hr, LANES), 1)
            sj = [jnp.where(col + j * LANES <= row, sj[j], -jnp.inf) for j in range(nj)]
        mx = sj[0]
        for j in range(1, nj):
            mx = jnp.maximum(mx, sj[j])
        m_prev = m_ref[rows, :]
        m_next = jnp.maximum(m_prev, jnp.max(mx, axis=-1, keepdims=True))
        alpha = jnp.exp2(m_prev - m_next)
        pj = [jnp.exp2(sj[j] - m_next) for j in range(nj)]
        psum = pj[0]
        for j in range(1, nj):
            psum = psum + pj[j]
        p = jnp.concatenate([x.astype(BF16) for x in pj], axis=-1)
        l_ref[rows, :] = alpha * l_ref[rows, :] + psum
        acc_ref[rows, :] = alpha * acc_ref[rows, :] + jnp.dot(p, v, preferred_element_type=F32)
        m_ref[rows, :] = m_next

    s_ref[...] = scores(0)

    def body(kc, carry):
        ks = pl.multiple_of(kc * t, t)
        s_next = scores(pl.multiple_of(ks + t, t))
        for part in range(parts):
            block(part, ks, t, False)
        s_ref[...] = s_next
        return carry

    lax.fori_loop(0, qi, body, 0)
    kd = pl.multiple_of(qi * t, t)
    for part in range(parts):
        block(part, kd, (part + 1) * hr, True)
    l = jnp.sum(l_ref[...], axis=-1, keepdims=True)
    o_ref[...] = (acc_ref[...] / l).astype(o_ref.dtype)


def _fox(c, proj, crow, ccol, qcol0):
    n = c.B * c.S
    t = c.tq
    nq = c.S // t
    h = c.FH
    return pl.pallas_call(
        functools.partial(_fox_kernel, t=t, parts=c.fox_parts),
        grid=(c.B, h, nq),
        in_specs=[
            pl.BlockSpec((t, LANES), lambda b, hh, qi: (b * nq + qi, qcol0 + hh)),
            pl.BlockSpec((c.S, LANES), lambda b, hh, qi: (b, qcol0 + h + hh)),
            pl.BlockSpec((c.S, LANES), lambda b, hh, qi: (b, qcol0 + 2 * h + hh)),
            pl.BlockSpec((1, 1, c.S), lambda b, hh, qi: (b * h + hh, 0, 0)),
            pl.BlockSpec((1, t, 1), lambda b, hh, qi: (b * h + hh, qi, 0)),
        ],
        out_specs=pl.BlockSpec((t, LANES), lambda b, hh, qi: (b * nq + qi, hh)),
        out_shape=jax.ShapeDtypeStruct((n, h * LANES), BF16),
        scratch_shapes=[pltpu.VMEM((t, LANES), F32)] * 4 + [pltpu.VMEM((t, t), F32)],
        compiler_params=_cparams(("parallel", "parallel", "arbitrary")),
        name="fox",
    )(proj, proj, proj, crow, ccol)


def _mix_kernel(x_ref, a_ref, u_ref, v_ref, ga_ref, gb_ref, lng_ref, lnb_ref, ws_ref, bs_ref,
                wa_ref, wb_ref, wo_ref, o_ref, sg_ref, *, sgc, groups):
    tm = x_ref.shape[0]
    v = v_ref[...].astype(F32)
    mu = jnp.mean(v, axis=-1, keepdims=True)
    vc = v - mu
    var = jnp.mean(vc * vc, axis=-1, keepdims=True)
    vn = (vc * lax.rsqrt(var + LN_EPS) * lng_ref[...] + lnb_ref[...]).astype(BF16)
    row = lax.broadcasted_iota(I32, (sgc, sgc), 0)
    col = lax.broadcasted_iota(I32, (sgc, sgc), 1)
    for g in range(groups):
        w = jnp.where(col <= row, ws_ref[g], 0.0).astype(BF16)
        bias = bs_ref[g]
        for ci in range(tm // sgc):
            rs = slice(ci * sgc, (ci + 1) * sgc)
            cs = slice(g * LANES, (g + 1) * LANES)
            mixed = jnp.dot(w, vn[rs, cs], preferred_element_type=F32) + bias
            sg_ref[rs, cs] = (u_ref[rs, cs].astype(F32) * mixed).astype(BF16)
    ya = jnp.dot(a_ref[...], wa_ref[...], preferred_element_type=F32)
    yb = jnp.dot(sg_ref[...], wb_ref[...], preferred_element_type=F32)
    merged = (ga_ref[...].astype(F32) * ya + gb_ref[...].astype(F32) * yb).astype(BF16)
    o_ref[...] = x_ref[...] + jnp.dot(merged, wo_ref[...], preferred_element_type=F32)


def _const_spec(shape):
    nd = len(shape)
    return pl.BlockSpec(shape, lambda i: (0,) * nd, pipeline_mode=pl.Buffered(1))


def _mix(c, x2d, attn, proj, ln_g, ln_b, w_s, b_s_full, w_a, w_b, w_o):
    n, d = x2d.shape
    tm = c.tm_mix
    fw = c.FH * LANES
    sw = c.SG * LANES
    ucol = 2 * d // sw
    return pl.pallas_call(
        functools.partial(_mix_kernel, sgc=c.SGC, groups=c.SG),
        grid=(n // tm,),
        in_specs=[
            pl.BlockSpec((tm, d), lambda i: (i, 0)),
            pl.BlockSpec((tm, fw), lambda i: (i, 0)),
            pl.BlockSpec((tm, sw), lambda i: (i, ucol)),
            pl.BlockSpec((tm, sw), lambda i: (i, ucol + 1)),
            pl.BlockSpec((tm, d), lambda i: (i, 0)),
            pl.BlockSpec((tm, d), lambda i: (i, 1)),
            _const_spec((1, sw)),
            _const_spec((1, sw)),
            _const_spec((c.SG, c.SGC, c.SGC)),
            _const_spec((c.SG, c.SGC, LANES)),
            _const_spec((fw, d)),
            _const_spec((sw, d)),
            _const_spec((d, d)),
        ],
        out_specs=pl.BlockSpec((tm, d), lambda i: (i, 0)),
        out_shape=jax.ShapeDtypeStruct((n, d), F32),
        scratch_shapes=[pltpu.VMEM((tm, sw), BF16)],
        compiler_params=_cparams(("parallel",)),
        name="mix",
    )(x2d, attn, proj, proj, proj, proj, ln_g, ln_b, w_s, b_s_full, w_a, w_b, w_o)


def _mem_kv_kernel(m_ref, g_ref, w_ref, o_ref):
    hm = _rms(m_ref[...], g_ref[...]).astype(BF16)
    o_ref[...] = jnp.dot(hm, w_ref[...], preferred_element_type=F32).astype(o_ref.dtype)


def _mem_kv(c, mem2d, g, w_xkv):
    n, d = mem2d.shape
    nc = w_xkv.shape[1]
    tm = c.MEM
    return pl.pallas_call(
        _mem_kv_kernel,
        grid=(n // tm,),
        in_specs=[pl.BlockSpec((tm, d), lambda i: (i, 0)), _const_spec((1, d)), _const_spec((d, nc))],
        out_specs=pl.BlockSpec((tm, nc), lambda i: (i, 0)),
        out_shape=jax.ShapeDtypeStruct((n, nc), BF16),
        compiler_params=_cparams(("parallel",)),
        name="mem_kv",
    )(mem2d, g, w_xkv)


def _xattn_kernel(x1_ref, gx_ref, wq_ref, kv_ref, wo_ref, gf_ref, wr_ref, br_ref,
                  x2_ref, hfp_ref, idx_ref, gate_ref, rank_ref, cnt_ref, carry_ref,
                  *, heads, n_exp, topk, qscale):
    i = pl.program_id(0)
    tm, d = x1_ref.shape
    xw = heads * LANES
    x1 = x1_ref[...]
    hx = _rms(x1, gx_ref[...]).astype(BF16)
    q = (jnp.dot(hx, wq_ref[...], preferred_element_type=F32) * qscale).astype(BF16)
    outs = []
    for h in range(heads):
        k = kv_ref[:, h * LANES:(h + 1) * LANES]
        v = kv_ref[:, xw + h * LANES:xw + (h + 1) * LANES]
        s = lax.dot_general(q[:, h * LANES:(h + 1) * LANES], k, (((1,), (1,)), ((), ())),
                            preferred_element_type=F32)
        s = s - jnp.max(s, axis=-1, keepdims=True)
        p = jnp.exp(s)
        p = p / jnp.sum(p, axis=-1, keepdims=True)
        outs.append(jnp.dot(p.astype(BF16), v, preferred_element_type=F32).astype(BF16))
    o = jnp.concatenate(outs, axis=-1)
    x2 = x1 + jnp.dot(o, wo_ref[...], preferred_element_type=F32)
    x2_ref[...] = x2

    hf = _rms(x2, gf_ref[...])
    hfp_ref[...] = _pack_bf16_pair(hf[:, :d // 2], hf[:, d // 2:])

    h_hi = hf.astype(BF16)
    h_lo = (hf - h_hi.astype(F32)).astype(BF16)
    l_hi = jnp.dot(h_hi, wr_ref[...], preferred_element_type=F32)
    l_lo = jnp.dot(h_lo, wr_ref[:, :LANES], preferred_element_type=F32)
    logits = l_hi[:, :LANES] + l_hi[:, LANES:] + l_lo + br_ref[...]
    lane = lax.broadcasted_iota(I32, (tm, LANES), 1)
    lg = jnp.where(lane < n_exp, logits, -jnp.inf)
    vals, idxs = [], []
    for _ in range(topk):
        m = jnp.max(lg, axis=-1, keepdims=True)
        ix = jnp.min(jnp.where(lg == m, lane, LANES), axis=-1, keepdims=True)
        vals.append(m)
        idxs.append(ix)
        lg = jnp.where(lane == ix, -jnp.inf, lg)
    es = [jnp.exp(vv - vals[0]) for vv in vals]
    denom = es[0]
    for e in es[1:]:
        denom = denom + e

    @pl.when(i == 0)
    def _():
        carry_ref[...] = jnp.zeros(carry_ref.shape, F32)

    onehot = jnp.zeros((tm, LANES), F32)
    for ix in idxs:
        onehot = onehot + (lane == ix).astype(F32)
    r = lax.broadcasted_iota(I32, (tm, tm), 0)
    cc = lax.broadcasted_iota(I32, (tm, tm), 1)
    strict = (cc < r).astype(BF16)
    rank_full = jnp.dot(strict, onehot.astype(BF16), preferred_element_type=F32) + carry_ref[...]
    idx_out = jnp.zeros((tm, LANES), I32)
    gate_out = jnp.zeros((tm, LANES), F32)
    rank_out = jnp.zeros((tm, LANES), F32)
    for kk in range(topk):
        rk = jnp.sum(jnp.where(lane == idxs[kk], rank_full, 0.0), axis=-1, keepdims=True)
        idx_out = jnp.where(lane == kk, idxs[kk], idx_out)
        gate_out = jnp.where(lane == kk, es[kk] / denom, gate_out)
        rank_out = jnp.where(lane == kk, rk, rank_out)
    idx_ref[...] = idx_out
    gate_ref[...] = gate_out
    rank_ref[...] = rank_out.astype(I32)
    carry_ref[...] = carry_ref[...] + jnp.sum(onehot, axis=0, keepdims=True)
    cnt_ref[...] = carry_ref[...]


def _xattn(c, x1, gx, w_xq, kv, w_xo, gf, w_r, b_r, qscale):
    n, d = x1.shape
    tm = c.tm_x
    xw = c.XH * LANES
    per_b = c.S // tm
    kern = functools.partial(_xattn_kernel, heads=c.XH, n_exp=c.E, topk=c.K, qscale=qscale)
    return pl.pallas_call(
        kern,
        grid=(n // tm,),
        in_specs=[
            pl.BlockSpec((tm, d), lambda i: (i, 0)),
            _const_spec((1, d)),
            _const_spec((d, xw)),
            pl.BlockSpec((c.MEM, 2 * xw), lambda i: (i // per_b, 0)),
            _const_spec((xw, d)),
            _const_spec((1, d)),
            _const_spec((d, 2 * LANES)),
            _const_spec((1, LANES)),
        ],
        out_specs=[
            pl.BlockSpec((tm, d), lambda i: (i, 0)),
            pl.BlockSpec((tm, d // 2), lambda i: (i, 0)),
            pl.BlockSpec((tm, LANES), lambda i: (i, 0)),
            pl.BlockSpec((tm, LANES), lambda i: (i, 0)),
            pl.BlockSpec((tm, LANES), lambda i: (i, 0)),
            pl.BlockSpec((1, LANES), lambda i: (0, 0)),
        ],
        out_shape=[
            jax.ShapeDtypeStruct((n, d), F32),
            jax.ShapeDtypeStruct((n, d // 2), U32),
            jax.ShapeDtypeStruct((n, LANES), I32),
            jax.ShapeDtypeStruct((n, LANES), F32),
            jax.ShapeDtypeStruct((n, LANES), I32),
            jax.ShapeDtypeStruct((1, LANES), F32),
        ],
        scratch_shapes=[pltpu.VMEM((1, LANES), F32)],
        compiler_params=_cparams(("arbitrary",)),
        name="xattn_router",
    )(x1, gx, w_xq, kv, w_xo, gf, w_r, b_r)


def _dispatch_kernel(dest_ref, zrow_ref, hfp_ref, xs_ref, zbuf_ref, sem, zsem, *, n_exp, topk, unroll):
    i = pl.program_id(0)
    tm = hfp_ref.shape[0]
    ts = zbuf_ref.shape[0]

    def zero_copy(e):
        return pltpu.make_async_copy(zbuf_ref, xs_ref.at[pl.ds(pl.multiple_of(zrow_ref[e], ts), ts)], zsem)

    @pl.when(i == 0)
    def _():
        zbuf_ref[...] = jnp.zeros(zbuf_ref.shape, zbuf_ref.dtype)
        for e in range(n_exp):
            @pl.when(zrow_ref[e] >= 0)
            def _():
                zero_copy(e).start()
        for e in range(n_exp):
            @pl.when(zrow_ref[e] >= 0)
            def _():
                zero_copy(e).wait()

    def row_copy(t, a):
        return pltpu.make_async_copy(hfp_ref.at[pl.ds(t, 1)], xs_ref.at[pl.ds(dest_ref[a], 1)], sem)

    def issue(tb, carry):
        for u in range(unroll):
            t = tb * unroll + u
            for kk in range(topk):
                row_copy(t, t * topk + kk).start()
        return carry

    lax.fori_loop(0, tm // unroll, issue, 0)

    def drain(tb, carry):
        for u in range(unroll * topk):
            row_copy(0, 0).wait()
        return carry

    lax.fori_loop(0, tm // unroll, drain, 0)


def _dispatch(c, dest_flat, zrow, hfp, p_rows):
    n, dh = hfp.shape
    tm = c.tm_c
    kern = functools.partial(_dispatch_kernel, n_exp=c.E, topk=c.K, unroll=8)
    return pl.pallas_call(
        kern,
        grid=(n // tm,),
        in_specs=[
            pl.BlockSpec((tm * c.K,), lambda i: (i,), memory_space=pltpu.SMEM),
            pl.BlockSpec(memory_space=pltpu.SMEM),
            pl.BlockSpec((tm, dh), lambda i: (i, 0)),
        ],
        out_specs=pl.BlockSpec(memory_space=pl.ANY),
        out_shape=jax.ShapeDtypeStruct((p_rows, dh), U32),
        scratch_shapes=[pltpu.VMEM((c.ts_e, dh), U32), pltpu.SemaphoreType.DMA(()), pltpu.SemaphoreType.DMA(())],
        compiler_params=_cparams(("arbitrary",)),
        name="dispatch",
    )(dest_flat, zrow, hfp)


def _expert_kernel(te_ref, ns_ref, x_ref, wg_ref, wu_ref, bg_ref, bu_ref, wd_ref, bd_ref, y_ref,
                   xb_ref, wgb_ref, wub_ref, wdb_ref, acc_ref, *, ts):
    t = pl.program_id(0)
    f = pl.program_id(1)
    nf = pl.num_programs(1)
    nsub = ns_ref[t]
    dh = x_ref.shape[1]

    @pl.when(nsub > 0)
    def _():
        @pl.when(f == 0)
        def _():
            acc_ref[...] = jnp.broadcast_to(bd_ref[0], acc_ref.shape)

            def unpack(s, carry):
                r = pl.multiple_of(s * ts, ts)
                p = x_ref[pl.ds(r, ts), :]
                xb_ref[pl.ds(r, ts), :dh] = _unpack_lo(p).astype(BF16)
                xb_ref[pl.ds(r, ts), dh:] = _unpack_hi(p).astype(BF16)
                return carry

            lax.fori_loop(0, nsub, unpack, 0)

        wgb_ref[...] = wg_ref[0].astype(BF16)
        wub_ref[...] = wu_ref[0].astype(BF16)
        wdb_ref[...] = wd_ref[0].astype(BF16)
        bg = bg_ref[0]
        bu = bu_ref[0]

        def sub(s):
            r = s * ts if isinstance(s, int) else pl.multiple_of(s * ts, ts)
            xs = xb_ref[pl.ds(r, ts), :]
            gate = jnp.dot(xs, wgb_ref[...], preferred_element_type=F32) + bg
            up = jnp.dot(xs, wub_ref[...], preferred_element_type=F32) + bu
            gate = jnp.minimum(gate, SWIGLU_LIMIT)
            up = jnp.clip(up, -SWIGLU_LIMIT, SWIGLU_LIMIT)
            act = (up + 1.0) * gate * jax.nn.sigmoid(SWIGLU_ALPHA * gate)
            acc_ref[pl.ds(r, ts), :] += jnp.dot(act.astype(BF16), wdb_ref[...], preferred_element_type=F32)

        sub(0)

        def pair(i, carry):
            sub(2 * i + 1)
            sub(2 * i + 2)
            return carry

        lax.fori_loop(0, (nsub - 1) // 2, pair, 0)

        @pl.when((nsub - 1) % 2 == 1)
        def _():
            sub(nsub - 1)

        @pl.when(f == nf - 1)
        def _():
            y_ref[...] = _pack_bf16_pair(acc_ref[:, :dh], acc_ref[:, dh:])


def _experts(c, tile_expert, tile_nsub, tile_blk, xs, w_gu, b_gu3, w_d, b_d3):
    p_rows, dh = xs.shape
    d = 2 * dh
    tm, tf = c.tm_e, c.tf_e
    nf = c.DFF // tf
    n_tiles = p_rows // tm

    def fblk(t, f, ns):
        return jnp.where(ns[t] > 0, f, nf - 1)

    grid_spec = pltpu.PrefetchScalarGridSpec(
        num_scalar_prefetch=3,
        grid=(n_tiles, nf),
        in_specs=[
            pl.BlockSpec((tm, dh), lambda t, f, te, ns, tb: (tb[t], 0)),
            pl.BlockSpec((1, d, tf), lambda t, f, te, ns, tb: (te[t], 0, fblk(t, f, ns))),
            pl.BlockSpec((1, d, tf), lambda t, f, te, ns, tb: (te[t], 0, nf + fblk(t, f, ns))),
            pl.BlockSpec((1, 1, tf), lambda t, f, te, ns, tb: (te[t], 0, fblk(t, f, ns))),
            pl.BlockSpec((1, 1, tf), lambda t, f, te, ns, tb: (te[t], 0, nf + fblk(t, f, ns))),
            pl.BlockSpec((1, tf, d), lambda t, f, te, ns, tb: (te[t], fblk(t, f, ns), 0)),
            pl.BlockSpec((1, 1, d), lambda t, f, te, ns, tb: (te[t], 0, 0)),
        ],
        out_specs=pl.BlockSpec((tm, dh), lambda t, f, te, ns, tb: (tb[t], 0)),
        scratch_shapes=[
            pltpu.VMEM((tm, d), BF16),
            pltpu.VMEM((d, tf), BF16),
            pltpu.VMEM((d, tf), BF16),
            pltpu.VMEM((tf, d), BF16),
            pltpu.VMEM((tm, d), F32),
        ],
    )

    def kern(te_ref, ns_ref, tb_ref, *rest):
        _expert_kernel(te_ref, ns_ref, *rest, ts=c.ts_e)

    return pl.pallas_call(
        kern,
        grid_spec=grid_spec,
        out_shape=jax.ShapeDtypeStruct((p_rows, dh), U32),
        compiler_params=_cparams(("arbitrary", "arbitrary")),
        name="experts",
    )(tile_expert, tile_nsub, tile_blk, xs, w_gu, w_gu, b_gu3, b_gu3, w_d, b_d3)


def _combine_kernel(dest_ref, x2_ref, gate_ref, g_ref, ys_ref, o_ref, buf_ref, sem, *, topk, unroll):
    tm, d = x2_ref.shape
    dh = d // 2

    def row_copy(t, kk, a):
        return pltpu.make_async_copy(ys_ref.at[pl.ds(dest_ref[a], 1)], buf_ref.at[kk, pl.ds(t, 1)], sem)

    def issue(tb, carry):
        for u in range(unroll):
            t = tb * unroll + u
            for kk in range(topk):
                row_copy(t, kk, t * topk + kk).start()
        return carry

    lax.fori_loop(0, tm // unroll, issue, 0)

    def drain(tb, carry):
        for u in range(unroll * topk):
            row_copy(0, 0, 0).wait()
        return carry

    lax.fori_loop(0, tm // unroll, drain, 0)

    lo = x2_ref[:, :dh]
    hi = x2_ref[:, dh:]
    gates = gate_ref[...]
    for kk in range(topk):
        wk = gates[:, kk:kk + 1]
        p = buf_ref[kk]
        lo = lo + wk * _unpack_lo(p)
        hi = hi + wk * _unpack_hi(p)
    ms = (jnp.sum(lo * lo, axis=-1, keepdims=True) + jnp.sum(hi * hi, axis=-1, keepdims=True)) / d
    inv = lax.rsqrt(ms + RMS_EPS)
    o_ref[:, :dh] = lo * inv * g_ref[:, :dh]
    o_ref[:, dh:] = hi * inv * g_ref[:, dh:]


def _combine(c, dest_flat, x2, gates, g, ys):
    n, d = x2.shape
    tm = c.tm_c
    kern = functools.partial(_combine_kernel, topk=c.K, unroll=8)
    return pl.pallas_call(
        kern,
        grid=(n // tm,),
        in_specs=[
            pl.BlockSpec((tm * c.K,), lambda i: (i,), memory_space=pltpu.SMEM),
            pl.BlockSpec((tm, d), lambda i: (i, 0)),
            pl.BlockSpec((tm, LANES), lambda i: (i, 0)),
            pl.BlockSpec((1, d), lambda i: (0, 0)),
            pl.BlockSpec(memory_space=pl.ANY),
        ],
        out_specs=pl.BlockSpec((tm, d), lambda i: (i, 0)),
        out_shape=jax.ShapeDtypeStruct((n, d), F32),
        scratch_shapes=[pltpu.VMEM((c.K, tm, d // 2), U32), pltpu.SemaphoreType.DMA(())],
        compiler_params=_cparams(("arbitrary",)),
        name="combine",
    )(dest_flat, x2, gates, g, ys)


def _plan(c, idx, rank, counts):
    tm, ts = c.tm_e, c.ts_e
    counts = counts.astype(I32)
    subs_e = (counts + ts - 1) // ts
    tiles_per_e = (counts + tm - 1) // tm
    tiles_safe = jnp.maximum(tiles_per_e, 1)
    fl = jnp.maximum(subs_e // tiles_safe, 1)
    rem = subs_e - (subs_e // tiles_safe) * tiles_safe
    hi_total = rem * (fl + 1)
    tile_end = jnp.cumsum(tiles_per_e)
    tile_start = tile_end - tiles_per_e

    def locate(e, s):
        in_hi = s < hi_total[e]
        s2 = s - hi_total[e]
        j = jnp.where(in_hi, s // (fl[e] + 1), rem[e] + s2 // fl[e])
        w = jnp.where(in_hi, s % (fl[e] + 1), s2 % fl[e])
        return j, w

    j, w = locate(idx, rank // ts)
    dest = (tile_start[idx] + j) * tm + w * ts + rank % ts
    n_tiles = (c.B * c.S * c.K) // tm + c.E
    t = jnp.arange(n_tiles, dtype=I32)
    n_used = tile_end[-1]
    tc = jnp.minimum(t, n_used - 1)
    te = jnp.minimum(jnp.searchsorted(tile_end, tc, side="right"), c.E - 1).astype(I32)
    nsub = jnp.where(tc - tile_start[te] < rem[te], fl[te] + 1, fl[te])
    nsub = jnp.where(t < n_used, nsub, 0).astype(I32)
    e_all = jnp.arange(c.E, dtype=I32)
    jl, wl = locate(e_all, jnp.maximum(subs_e - 1, 0))
    zrow = jnp.where(counts > 0, (tile_start + jl) * tm + wl * ts, -1).astype(I32)
    return dest.reshape(-1).astype(I32), te, nsub, tc.astype(I32), zrow


def _forward(c, x, mem, norm_mix_g, w_in, b_forget, sg_ln_g, sg_ln_b, w_spatial, b_spatial,
             w_branch_a, w_branch_b, w_out, norm_x_g, norm_mem_g, w_xq, w_xkv, w_xo,
             norm_ffn_g, w_router, b_router, w_gate_up, b_gate_up, w_down, b_down, norm_final_g):
    B, S, D = x.shape
    n = B * S
    fw = c.FH * LANES
    sw = c.SG * LANES
    x2d = x.reshape(n, D)

    o_f = 3 * fw
    o_z = o_f + c.FH
    o_g = o_z + 2 * sw
    w_main = jnp.concatenate([w_in[:, o_g:], w_in[:, o_z:o_g], w_in[:, :o_f]], axis=1).astype(BF16)
    w_f = jnp.pad(w_in[:, o_f:o_z], ((0, 0), (0, LANES - c.FH))).astype(BF16)
    b_f = jnp.pad(b_forget.astype(F32), (0, LANES - c.FH)).reshape(1, LANES)
    proj, logf = _in_proj(c, x2d, norm_mix_g.reshape(1, D), w_main, w_f, b_f, LOG2E * LANES ** -0.5)

    logf_bhs = logf[:, :c.FH].reshape(B, S, c.FH).transpose(0, 2, 1)
    csum = _cumsum(c, logf_bhs)
    crow = csum.reshape(B * c.FH, 1, S)
    ccol = csum.reshape(B * c.FH, S, 1)
    qcol0 = (2 * D + 2 * sw) // LANES
    attn = _fox(c, proj, crow, ccol, qcol0)

    b_s_full = jnp.broadcast_to(b_spatial.astype(F32)[:, :, None], (c.SG, c.SGC, LANES))
    x1 = _mix(c, x2d, attn, proj, sg_ln_g.reshape(1, sw), sg_ln_b.reshape(1, sw), w_spatial, b_s_full,
              w_branch_a.astype(BF16), w_branch_b.astype(BF16), w_out.astype(BF16))

    kv = _mem_kv(c, mem.reshape(B * c.MEM, D), norm_mem_g.reshape(1, D), w_xkv.astype(BF16))
    w_r32 = jnp.pad(w_router.astype(F32), ((0, 0), (0, LANES - c.E)))
    w_r_hi = w_r32.astype(BF16)
    w_r = jnp.concatenate([w_r_hi, (w_r32 - w_r_hi.astype(F32)).astype(BF16)], axis=1)
    b_r = jnp.pad(b_router.astype(F32), (0, LANES - c.E)).reshape(1, LANES)
    x2, hfp, idx, gates, rank, counts = _xattn(
        c, x1, norm_x_g.reshape(1, D), w_xq.astype(BF16), kv, w_xo.astype(BF16),
        norm_ffn_g.reshape(1, D), w_r, b_r, LANES ** -0.5)

    dest, te, nsub, tblk, zrow = _plan(c, idx[:, :c.K], rank[:, :c.K], counts[0, :c.E])
    p_rows = (n * c.K // c.tm_e + c.E) * c.tm_e
    xs = _dispatch(c, dest, zrow, hfp, p_rows)
    ys = _experts(c, te, nsub, tblk, xs, w_gate_up, b_gate_up.reshape(c.E, 1, 2 * c.DFF),
                  w_down, b_down.reshape(c.E, 1, D))
    out = _combine(c, dest, x2, gates, norm_final_g.reshape(1, D), ys)
    return out.reshape(B, S, D)


_CFG = Cfg(B=4, S=4096, D=2048, MEM=256, FH=8, SG=8, SGC=128, XH=4, E=32, K=4, DFF=2048,
           tm_in=1024, tn_in=512, tq=512, tm_mix=256, tm_x=512, tm_e=1024, ts_e=256, tf_e=256,
           tm_c=256, tc_cs=512, fox_parts=2)


@jax.jit
def kernel(x, mem, norm_mix_g, w_in, b_forget, sg_ln_g, sg_ln_b, w_spatial, b_spatial, w_branch_a, w_branch_b,
           w_out, norm_x_g, norm_mem_g, w_xq, w_xkv, w_xo, norm_ffn_g, w_router, b_router, w_gate_up,
           b_gate_up, w_down, b_down, norm_final_g):
    return _forward(_CFG, x, mem, norm_mix_g, w_in, b_forget, sg_ln_g, sg_ln_b, w_spatial, b_spatial,
                    w_branch_a, w_branch_b, w_out, norm_x_g, norm_mem_g, w_xq, w_xkv, w_xo,
                    norm_ffn_g, w_router, b_router, w_gate_up, b_gate_up, w_down, b_down, norm_final_g)
```

```python
import functools
from typing import NamedTuple

import jax
import jax.numpy as jnp
from jax import lax
from jax.experimental import pallas as pl
from jax.experimental.pallas import tpu as pltpu

F32 = jnp.float32
BF16 = jnp.bfloat16
U32 = jnp.uint32
I32 = jnp.int32

LANES = 128
VMEM_LIMIT = 56 * 1024 * 1024

RMS_EPS = 1e-6
LN_EPS = 1e-5
SWIGLU_LIMIT = 7.0
SWIGLU_ALPHA = 1.702
GELU_C = 0.7978845608028654
LOG2E = 1.4426950408889634


class Cfg(NamedTuple):
    B: int
    S: int
    D: int
    MEM: int
    FH: int
    SG: int
    SGC: int
    XH: int
    E: int
    K: int
    DFF: int
    tm_in: int
    tn_in: int
    tq: int
    tm_mix: int
    tm_x: int
    tm_e: int
    ts_e: int
    tf_e: int
    tm_c: int
    tc_cs: int
    fox_parts: int


def _cparams(sem):
    return pltpu.CompilerParams(dimension_semantics=sem, vmem_limit_bytes=VMEM_LIMIT)


def _rms(x, g):
    ms = jnp.mean(x * x, axis=-1, keepdims=True)
    return x * lax.rsqrt(ms + RMS_EPS) * g


def _pack_bf16_pair(a, b):
    def rne(v):
        bits = lax.bitcast_convert_type(v, U32)
        return bits + jnp.uint32(0x7FFF) + ((bits >> 16) & jnp.uint32(1))
    return (rne(a) >> 16) | (rne(b) & jnp.uint32(0xFFFF0000))


def _unpack_lo(p):
    return lax.bitcast_convert_type(p << 16, F32)


def _unpack_hi(p):
    return lax.bitcast_convert_type(p & jnp.uint32(0xFFFF0000), F32)


def _in_proj_kernel(x_ref, g_ref, w_ref, wf_ref, bf_ref, o_ref, f_ref, h_ref, *, nj_gate, nj_z, nj_q, qscale):
    j = pl.program_id(1)

    @pl.when(j == 0)
    def _():
        hb = _rms(x_ref[...], g_ref[...]).astype(BF16)
        h_ref[...] = hb
        f = jnp.dot(hb, wf_ref[...], preferred_element_type=F32) + bf_ref[...]
        f_ref[...] = jnp.minimum(f, 0.0) - jnp.log1p(jnp.exp(-jnp.abs(f)))

    acc = jnp.dot(h_ref[...], w_ref[...], preferred_element_type=F32)

    is_gate = j < nj_gate
    is_z = (j >= nj_gate) & (j < nj_gate + nj_z)
    is_q = (j >= nj_gate + nj_z) & (j < nj_gate + nj_z + nj_q)
    a1 = jnp.where(is_gate, 0.5, jnp.where(is_z, GELU_C, 0.0)).astype(F32)
    a3 = jnp.where(is_z, GELU_C * 0.044715, 0.0).astype(F32)
    b0 = jnp.where(is_gate, 0.5, 0.0).astype(F32)
    b1 = jnp.where(is_gate, 0.0, jnp.where(is_z, 0.5, jnp.where(is_q, qscale, 1.0))).astype(F32)
    th = jnp.tanh(acc * (a1 + a3 * (acc * acc)))
    o_ref[...] = ((b0 + b1 * acc) * (1.0 + th)).astype(o_ref.dtype)


def _in_proj(c, x2d, g, w_main, w_f, b_f, qscale):
    n, d = x2d.shape
    nc = w_main.shape[1]
    tm, tn = c.tm_in, c.tn_in
    fw = c.FH * LANES
    kern = functools.partial(_in_proj_kernel, nj_gate=2 * d // tn, nj_z=2 * c.SG * LANES // tn,
                             nj_q=fw // tn, qscale=qscale)
    return pl.pallas_call(
        kern,
        grid=(n // tm, nc // tn),
        in_specs=[
            pl.BlockSpec((tm, d), lambda i, j: (i, 0)),
            pl.BlockSpec((1, d), lambda i, j: (0, 0)),
            pl.BlockSpec((d, tn), lambda i, j: (0, j)),
            pl.BlockSpec((d, LANES), lambda i, j: (0, 0)),
            pl.BlockSpec((1, LANES), lambda i, j: (0, 0)),
        ],
        out_specs=[
            pl.BlockSpec((tm, tn), lambda i, j: (i, j)),
            pl.BlockSpec((tm, LANES), lambda i, j: (i, 0)),
        ],
        out_shape=[jax.ShapeDtypeStruct((n, nc), BF16), jax.ShapeDtypeStruct((n, LANES), F32)],
        scratch_shapes=[pltpu.VMEM((tm, d), BF16)],
        compiler_params=_cparams(("parallel", "arbitrary")),
        name="in_proj",
    )(x2d, g, w_main, w_f, b_f)


def _cumsum_kernel(f_ref, o_ref, *, tc):
    rows, s = f_ref.shape[1], f_ref.shape[2]
    r = lax.broadcasted_iota(I32, (tc, tc), 0)
    col = lax.broadcasted_iota(I32, (tc, tc), 1)
    upper = (r <= col).astype(F32)
    carry = jnp.zeros((rows, 1), F32)
    for i in range(s // tc):
        blk = f_ref[0, :, i * tc:(i + 1) * tc]
        cs = jnp.dot(blk, upper, preferred_element_type=F32, precision=lax.Precision.HIGHEST) + carry
        o_ref[0, :, i * tc:(i + 1) * tc] = cs * LOG2E
        carry = cs[:, tc - 1:tc]


def _cumsum(c, logf_bhs):
    b, h, s = logf_bhs.shape
    return pl.pallas_call(
        functools.partial(_cumsum_kernel, tc=c.tc_cs),
        grid=(b,),
        in_specs=[pl.BlockSpec((1, h, s), lambda i: (i, 0, 0))],
        out_specs=pl.BlockSpec((1, h, s), lambda i: (i, 0, 0)),
        out_shape=jax.ShapeDtypeStruct((b, h, s), F32),
        compiler_params=_cparams(("parallel",)),
        name="cumsum",
    )(logf_bhs)


def _fox_kernel(q_ref, k_ref, v_ref, crow_ref, ccol_ref, o_ref, m_ref, l_ref, acc_ref, cq_ref, s_ref,
                *, t, parts):
    qi = pl.program_id(2)
    hr = t // parts
    m_ref[...] = jnp.full(m_ref.shape, -jnp.inf, F32)
    l_ref[...] = jnp.zeros(l_ref.shape, F32)
    acc_ref[...] = jnp.zeros(acc_ref.shape, F32)
    cq_ref[...] = jnp.broadcast_to(ccol_ref[0], cq_ref.shape)

    def scores(ks):
        k = k_ref[pl.ds(ks, t), :]
        return lax.dot_general(q_ref[...], k, (((1,), (1,)), ((), ())), preferred_element_type=F32)

    def block(part, ks, width, masked):
        rows = slice(part * hr, (part + 1) * hr)
        v = v_ref[pl.ds(ks, width), :]
        s = s_ref[rows, :width]
        cq = cq_ref[rows, :]
        crow = crow_ref[0, :, pl.ds(ks, width)]
        nj = width // LANES
        sj = [s[:, j * LANES:(j + 1) * LANES] + cq - crow[:, j * LANES:(j + 1) * LANES] for j in range(nj)]
        if masked:
            row = lax.broadcasted_iota(I32, (hr, LANES), 0) + part * hr
            col = lax.broadcasted_iota(I32, (hr, LANES), 1)
            sj = [jnp.where(col + j * LANES <= row, sj[j], -jnp.inf) for j in range(nj)]
        mx = sj[0]
        for j in range(1, nj):
            mx = jnp.maximum(mx, sj[j])
        m_prev = m_ref[rows, :]
        m_next = jnp.maximum(m_prev, jnp.max(mx, axis=-1, keepdims=True))
        alpha = jnp.exp2(m_prev - m_next)
        pj = [jnp.exp2(sj[j] - m_next) for j in range(nj)]
        psum = pj[0]
        for j in range(1, nj):
            psum = psum + pj[j]
        p = jnp.concatenate([x.astype(BF16) for x in pj], axis=-1)
        l_ref[rows, :] = alpha * l_ref[rows, :] + psum
        acc_ref[rows, :] = alpha * acc_ref[rows, :] + jnp.dot(p, v, preferred_element_type=F32)
        m_ref[rows, :] = m_next

    s_ref[...] = scores(0)

    def body(kc, carry):
        ks = pl.multiple_of(kc * t, t)
        s_next = scores(pl.multiple_of(ks + t, t))
        for part in range(parts):
            block(part, ks, t, False)
        s_ref[...] = s_next
        return carry

    lax.fori_loop(0, qi, body, 0)
    kd = pl.multiple_of(qi * t, t)
    for part in range(parts):
        block(part, kd, (part + 1) * hr, True)
    l = jnp.sum(l_ref[...], axis=-1, keepdims=True)
    o_ref[...] = (acc_ref[...] / l).astype(o_ref.dtype)


def _fox(c, proj, crow, ccol, qcol0):
    n = c.B * c.S
    t = c.tq
    nq = c.S // t
    h = c.FH
    return pl.pallas_call(
        functools.partial(_fox_kernel, t=t, parts=c.fox_parts),
        grid=(c.B, h, nq),
        in_specs=[
            pl.BlockSpec((t, LANES), lambda b, hh, qi: (b * nq + qi, qcol0 + hh)),
            pl.BlockSpec((c.S, LANES), lambda b, hh, qi: (b, qcol0 + h + hh)),
            pl.BlockSpec((c.S, LANES), lambda b, hh, qi: (b, qcol0 + 2 * h + hh)),
            pl.BlockSpec((1, 1, c.S), lambda b, hh, qi: (b * h + hh, 0, 0)),
            pl.BlockSpec((1, t, 1), lambda b, hh, qi: (b * h + hh, qi, 0)),
        ],
        out_specs=pl.BlockSpec((t, LANES), lambda b, hh, qi: (b * nq + qi, hh)),
        out_shape=jax.ShapeDtypeStruct((n, h * LANES), BF16),
        scratch_shapes=[pltpu.VMEM((t, LANES), F32)] * 4 + [pltpu.VMEM((t, t), F32)],
        compiler_params=_cparams(("parallel", "parallel", "arbitrary")),
        name="fox",
    )(proj, proj, proj, crow, ccol)


def _mix_kernel(x_ref, a_ref, u_ref, v_ref, ga_ref, gb_ref, lng_ref, lnb_ref, ws_ref, bs_ref,
                wa_ref, wb_ref, wo_ref, o_ref, sg_ref, *, sgc, groups):
    tm = x_ref.shape[0]
    v = v_ref[...].astype(F32)
    mu = jnp.mean(v, axis=-1, keepdims=True)
    vc = v - mu
    var = jnp.mean(vc * vc, axis=-1, keepdims=True)
    vn = (vc * lax.rsqrt(var + LN_EPS) * lng_ref[...] + lnb_ref[...]).astype(BF16)
    row = lax.broadcasted_iota(I32, (sgc, sgc), 0)
    col = lax.broadcasted_iota(I32, (sgc, sgc), 1)
    for g in range(groups):
        w = jnp.where(col <= row, ws_ref[g], 0.0).astype(BF16)
        bias = bs_ref[g]
        for ci in range(tm // sgc):
            rs = slice(ci * sgc, (ci + 1) * sgc)
            cs = slice(g * LANES, (g + 1) * LANES)
            mixed = jnp.dot(w, vn[rs, cs], preferred_element_type=F32) + bias
            sg_ref[rs, cs] = (u_ref[rs, cs].astype(F32) * mixed).astype(BF16)
    ya = jnp.dot(a_ref[...], wa_ref[...], preferred_element_type=F32)
    yb = jnp.dot(sg_ref[...], wb_ref[...], preferred_element_type=F32)
    merged = (ga_ref[...].astype(F32) * ya + gb_ref[...].astype(F32) * yb).astype(BF16)
    o_ref[...] = x_ref[...] + jnp.dot(merged, wo_ref[...], preferred_element_type=F32)


def _const_spec(shape):
    nd = len(shape)
    return pl.BlockSpec(shape, lambda i: (0,) * nd, pipeline_mode=pl.Buffered(1))


def _mix(c, x2d, attn, proj, ln_g, ln_b, w_s, b_s_full, w_a, w_b, w_o):
    n, d = x2d.shape
    tm = c.tm_mix
    fw = c.FH * LANES
    sw = c.SG * LANES
    ucol = 2 * d // sw
    return pl.pallas_call(
        functools.partial(_mix_kernel, sgc=c.SGC, groups=c.SG),
        grid=(n // tm,),
        in_specs=[
            pl.BlockSpec((tm, d), lambda i: (i, 0)),
            pl.BlockSpec((tm, fw), lambda i: (i, 0)),
            pl.BlockSpec((tm, sw), lambda i: (i, ucol)),
            pl.BlockSpec((tm, sw), lambda i: (i, ucol + 1)),
            pl.BlockSpec((tm, d), lambda i: (i, 0)),
            pl.BlockSpec((tm, d), lambda i: (i, 1)),
            _const_spec((1, sw)),
            _const_spec((1, sw)),
            _const_spec((c.SG, c.SGC, c.SGC)),
            _const_spec((c.SG, c.SGC, LANES)),
            _const_spec((fw, d)),
            _const_spec((sw, d)),
            _const_spec((d, d)),
        ],
        out_specs=pl.BlockSpec((tm, d), lambda i: (i, 0)),
        out_shape=jax.ShapeDtypeStruct((n, d), F32),
        scratch_shapes=[pltpu.VMEM((tm, sw), BF16)],
        compiler_params=_cparams(("parallel",)),
        name="mix",
    )(x2d, attn, proj, proj, proj, proj, ln_g, ln_b, w_s, b_s_full, w_a, w_b, w_o)


def _mem_kv_kernel(m_ref, g_ref, w_ref, o_ref):
    hm = _rms(m_ref[...], g_ref[...]).astype(BF16)
    o_ref[...] = jnp.dot(hm, w_ref[...], preferred_element_type=F32).astype(o_ref.dtype)


def _mem_kv(c, mem2d, g, w_xkv):
    n, d = mem2d.shape
    nc = w_xkv.shape[1]
    tm = c.MEM
    return pl.pallas_call(
        _mem_kv_kernel,
        grid=(n // tm,),
        in_specs=[pl.BlockSpec((tm, d), lambda i: (i, 0)), _const_spec((1, d)), _const_spec((d, nc))],
        out_specs=pl.BlockSpec((tm, nc), lambda i: (i, 0)),
        out_shape=jax.ShapeDtypeStruct((n, nc), BF16),
        compiler_params=_cparams(("parallel",)),
        name="mem_kv",
    )(mem2d, g, w_xkv)


def _xattn_kernel(x1_ref, gx_ref, wq_ref, kv_ref, wo_ref, gf_ref, wr_ref, br_ref,
                  x2_ref, hfp_ref, idx_ref, gate_ref, rank_ref, cnt_ref, carry_ref,
                  *, heads, n_exp, topk, qscale):
    i = pl.program_id(0)
    tm, d = x1_ref.shape
    xw = heads * LANES
    x1 = x1_ref[...]
    hx = _rms(x1, gx_ref[...]).astype(BF16)
    q = (jnp.dot(hx, wq_ref[...], preferred_element_type=F32) * qscale).astype(BF16)
    outs = []
    for h in range(heads):
        k = kv_ref[:, h * LANES:(h + 1) * LANES]
        v = kv_ref[:, xw + h * LANES:xw + (h + 1) * LANES]
        s = lax.dot_general(q[:, h * LANES:(h + 1) * LANES], k, (((1,), (1,)), ((), ())),
                            preferred_element_type=F32)
        s = s - jnp.max(s, axis=-1, keepdims=True)
        p = jnp.exp(s)
        p = p / jnp.sum(p, axis=-1, keepdims=True)
        outs.append(jnp.dot(p.astype(BF16), v, preferred_element_type=F32).astype(BF16))
    o = jnp.concatenate(outs, axis=-1)
    x2 = x1 + jnp.dot(o, wo_ref[...], preferred_element_type=F32)
    x2_ref[...] = x2

    hf = _rms(x2, gf_ref[...])
    hfp_ref[...] = _pack_bf16_pair(hf[:, :d // 2], hf[:, d // 2:])

    h_hi = hf.astype(BF16)
    h_lo = (hf - h_hi.astype(F32)).astype(BF16)
    l_hi = jnp.dot(h_hi, wr_ref[...], preferred_element_type=F32)
    l_lo = jnp.dot(h_lo, wr_ref[:, :LANES], preferred_element_type=F32)
    logits = l_hi[:, :LANES] + l_hi[:, LANES:] + l_lo + br_ref[...]
    lane = lax.broadcasted_iota(I32, (tm, LANES), 1)
    lg = jnp.where(lane < n_exp, logits, -jnp.inf)
    vals, idxs = [], []
    for _ in range(topk):
        m = jnp.max(lg, axis=-1, keepdims=True)
        ix = jnp.min(jnp.where(lg == m, lane, LANES), axis=-1, keepdims=True)
        vals.append(m)
        idxs.append(ix)
        lg = jnp.where(lane == ix, -jnp.inf, lg)
    es = [jnp.exp(vv - vals[0]) for vv in vals]
    denom = es[0]
    for e in es[1:]:
        denom = denom + e

    @pl.when(i == 0)
    def _():
        carry_ref[...] = jnp.zeros(carry_ref.shape, F32)

    onehot = jnp.zeros((tm, LANES), F32)
    for ix in idxs:
        onehot = onehot + (lane == ix).astype(F32)
    r = lax.broadcasted_iota(I32, (tm, tm), 0)
    cc = lax.broadcasted_iota(I32, (tm, tm), 1)
    strict = (cc < r).astype(BF16)
    rank_full = jnp.dot(strict, onehot.astype(BF16), preferred_element_type=F32) + carry_ref[...]
    idx_out = jnp.zeros((tm, LANES), I32)
    gate_out = jnp.zeros((tm, LANES), F32)
    rank_out = jnp.zeros((tm, LANES), F32)
    for kk in range(topk):
        rk = jnp.sum(jnp.where(lane == idxs[kk], rank_full, 0.0), axis=-1, keepdims=True)
        idx_out = jnp.where(lane == kk, idxs[kk], idx_out)
        gate_out = jnp.where(lane == kk, es[kk] / denom, gate_out)
        rank_out = jnp.where(lane == kk, rk, rank_out)
    idx_ref[...] = idx_out
    gate_ref[...] = gate_out
    rank_ref[...] = rank_out.astype(I32)
    carry_ref[...] = carry_ref[...] + jnp.sum(onehot, axis=0, keepdims=True)
    cnt_ref[...] = carry_ref[...]


def _xattn(c, x1, gx, w_xq, kv, w_xo, gf, w_r, b_r, qscale):
    n, d = x1.shape
    tm = c.tm_x
    xw = c.XH * LANES
    per_b = c.S // tm
    kern = functools.partial(_xattn_kernel, heads=c.XH, n_exp=c.E, topk=c.K, qscale=qscale)
    return pl.pallas_call(
        kern,
        grid=(n // tm,),
        in_specs=[
            pl.BlockSpec((tm, d), lambda i: (i, 0)),
            _const_spec((1, d)),
            _const_spec((d, xw)),
            pl.BlockSpec((c.MEM, 2 * xw), lambda i: (i // per_b, 0)),
            _const_spec((xw, d)),
            _const_spec((1, d)),
            _const_spec((d, 2 * LANES)),
            _const_spec((1, LANES)),
        ],
        out_specs=[
            pl.BlockSpec((tm, d), lambda i: (i, 0)),
            pl.BlockSpec((tm, d // 2), lambda i: (i, 0)),
            pl.BlockSpec((tm, LANES), lambda i: (i, 0)),
            pl.BlockSpec((tm, LANES), lambda i: (i, 0)),
            pl.BlockSpec((tm, LANES), lambda i: (i, 0)),
            pl.BlockSpec((1, LANES), lambda i: (0, 0)),
        ],
        out_shape=[
            jax.ShapeDtypeStruct((n, d), F32),
            jax.ShapeDtypeStruct((n, d // 2), U32),
            jax.ShapeDtypeStruct((n, LANES), I32),
            jax.ShapeDtypeStruct((n, LANES), F32),
            jax.ShapeDtypeStruct((n, LANES), I32),
            jax.ShapeDtypeStruct((1, LANES), F32),
        ],
        scratch_shapes=[pltpu.VMEM((1, LANES), F32)],
        compiler_params=_cparams(("arbitrary",)),
        name="xattn_router",
    )(x1, gx, w_xq, kv, w_xo, gf, w_r, b_r)


def _dispatch_kernel(dest_ref, zrow_ref, hfp_ref, xs_ref, zbuf_ref, sem, zsem, *, n_exp, topk, unroll):
    i = pl.program_id(0)
    tm = hfp_ref.shape[0]
    ts = zbuf_ref.shape[0]

    def zero_copy(e):
        return pltpu.make_async_copy(zbuf_ref, xs_ref.at[pl.ds(pl.multiple_of(zrow_ref[e], ts), ts)], zsem)

    @pl.when(i == 0)
    def _():
        zbuf_ref[...] = jnp.zeros(zbuf_ref.shape, zbuf_ref.dtype)
        for e in range(n_exp):
            @pl.when(zrow_ref[e] >= 0)
            def _():
                zero_copy(e).start()
        for e in range(n_exp):
            @pl.when(zrow_ref[e] >= 0)
            def _():
                zero_copy(e).wait()

    def row_copy(t, a):
        return pltpu.make_async_copy(hfp_ref.at[pl.ds(t, 1)], xs_ref.at[pl.ds(dest_ref[a], 1)], sem)

    def issue(tb, carry):
        for u in range(unroll):
            t = tb * unroll + u
            for kk in range(topk):
                row_copy(t, t * topk + kk).start(priority=kk % 2)
        return carry

    lax.fori_loop(0, tm // unroll, issue, 0)

    def drain(tb, carry):
        for u in range(unroll * topk):
            row_copy(0, 0).wait()
        return carry

    lax.fori_loop(0, tm // unroll, drain, 0)


def _dispatch(c, dest_flat, zrow, hfp, p_rows):
    n, dh = hfp.shape
    tm = c.tm_c
    kern = functools.partial(_dispatch_kernel, n_exp=c.E, topk=c.K, unroll=8)
    return pl.pallas_call(
        kern,
        grid=(n // tm,),
        in_specs=[
            pl.BlockSpec((tm * c.K,), lambda i: (i,), memory_space=pltpu.SMEM),
            pl.BlockSpec(memory_space=pltpu.SMEM),
            pl.BlockSpec((tm, dh), lambda i: (i, 0)),
        ],
        out_specs=pl.BlockSpec(memory_space=pl.ANY),
        out_shape=jax.ShapeDtypeStruct((p_rows, dh), U32),
        scratch_shapes=[pltpu.VMEM((c.ts_e, dh), U32), pltpu.SemaphoreType.DMA(()), pltpu.SemaphoreType.DMA(())],
        compiler_params=_cparams(("arbitrary",)),
        name="dispatch",
    )(dest_flat, zrow, hfp)


def _expert_kernel(te_ref, ns_ref, x_ref, wg_ref, wu_ref, bg_ref, bu_ref, wd_ref, bd_ref, y_ref,
                   xb_ref, wgb_ref, wub_ref, wdb_ref, acc_ref, *, ts):
    t = pl.program_id(0)
    f = pl.program_id(1)
    nf = pl.num_programs(1)
    nsub = ns_ref[t]
    dh = x_ref.shape[1]

    @pl.when(nsub > 0)
    def _():
        @pl.when(f == 0)
        def _():
            acc_ref[...] = jnp.broadcast_to(bd_ref[0], acc_ref.shape)

            def unpack(s, carry):
                r = pl.multiple_of(s * ts, ts)
                p = x_ref[pl.ds(r, ts), :]
                xb_ref[pl.ds(r, ts), :dh] = _unpack_lo(p).astype(BF16)
                xb_ref[pl.ds(r, ts), dh:] = _unpack_hi(p).astype(BF16)
                return carry

            lax.fori_loop(0, nsub, unpack, 0)

        wgb_ref[...] = wg_ref[0].astype(BF16)
        wub_ref[...] = wu_ref[0].astype(BF16)
        wdb_ref[...] = wd_ref[0].astype(BF16)
        bg = bg_ref[0]
        bu = bu_ref[0]

        def sub(s):
            r = s * ts if isinstance(s, int) else pl.multiple_of(s * ts, ts)
            xs = xb_ref[pl.ds(r, ts), :]
            gate = jnp.dot(xs, wgb_ref[...], preferred_element_type=F32) + bg
            up = jnp.dot(xs, wub_ref[...], preferred_element_type=F32) + bu
            gate = jnp.minimum(gate, SWIGLU_LIMIT)
            up = jnp.clip(up, -SWIGLU_LIMIT, SWIGLU_LIMIT)
            act = (up + 1.0) * gate * jax.nn.sigmoid(SWIGLU_ALPHA * gate)
            acc_ref[pl.ds(r, ts), :] += jnp.dot(act.astype(BF16), wdb_ref[...], preferred_element_type=F32)

        sub(0)

        def pair(i, carry):
            sub(2 * i + 1)
            sub(2 * i + 2)
            return carry

        lax.fori_loop(0, (nsub - 1) // 2, pair, 0)

        @pl.when((nsub - 1) % 2 == 1)
        def _():
            sub(nsub - 1)

        @pl.when(f == nf - 1)
        def _():
            y_ref[...] = _pack_bf16_pair(acc_ref[:, :dh], acc_ref[:, dh:])


def _experts(c, tile_expert, tile_nsub, tile_blk, xs, w_gu, b_gu3, w_d, b_d3):
    p_rows, dh = xs.shape
    d = 2 * dh
    tm, tf = c.tm_e, c.tf_e
    nf = c.DFF // tf
    n_tiles = p_rows // tm

    def fblk(t, f, ns):
        return jnp.where(ns[t] > 0, f, nf - 1)

    grid_spec = pltpu.PrefetchScalarGridSpec(
        num_scalar_prefetch=3,
        grid=(n_tiles, nf),
        in_specs=[
            pl.BlockSpec((tm, dh), lambda t, f, te, ns, tb: (tb[t], 0)),
            pl.BlockSpec((1, d, tf), lambda t, f, te, ns, tb: (te[t], 0, fblk(t, f, ns))),
            pl.BlockSpec((1, d, tf), lambda t, f, te, ns, tb: (te[t], 0, nf + fblk(t, f, ns))),
            pl.BlockSpec((1, 1, tf), lambda t, f, te, ns, tb: (te[t], 0, fblk(t, f, ns))),
            pl.BlockSpec((1, 1, tf), lambda t, f, te, ns, tb: (te[t], 0, nf + fblk(t, f, ns))),
            pl.BlockSpec((1, tf, d), lambda t, f, te, ns, tb: (te[t], fblk(t, f, ns), 0)),
            pl.BlockSpec((1, 1, d), lambda t, f, te, ns, tb: (te[t], 0, 0)),
        ],
        out_specs=pl.BlockSpec((tm, dh), lambda t, f, te, ns, tb: (tb[t], 0)),
        scratch_shapes=[
            pltpu.VMEM((tm, d), BF16),
            pltpu.VMEM((d, tf), BF16),
            pltpu.VMEM((d, tf), BF16),
            pltpu.VMEM((tf, d), BF16),
            pltpu.VMEM((tm, d), F32),
        ],
    )

    def kern(te_ref, ns_ref, tb_ref, *rest):
        _expert_kernel(te_ref, ns_ref, *rest, ts=c.ts_e)

    return pl.pallas_call(
        kern,
        grid_spec=grid_spec,
        out_shape=jax.ShapeDtypeStruct((p_rows, dh), U32),
        compiler_params=_cparams(("arbitrary", "arbitrary")),
        name="experts",
    )(tile_expert, tile_nsub, tile_blk, xs, w_gu, w_gu, b_gu3, b_gu3, w_d, b_d3)


def _combine_kernel(dest_ref, x2_ref, gate_ref, g_ref, ys_ref, o_ref, buf_ref, sem, *, topk, unroll):
    tm, d = x2_ref.shape
    dh = d // 2

    def row_copy(t, kk, a):
        return pltpu.make_async_copy(ys_ref.at[pl.ds(dest_ref[a], 1)], buf_ref.at[kk, pl.ds(t, 1)], sem)

    def issue(tb, carry):
        for u in range(unroll):
            t = tb * unroll + u
            for kk in range(topk):
                row_copy(t, kk, t * topk + kk).start(priority=kk % 2)
        return carry

    lax.fori_loop(0, tm // unroll, issue, 0)

    def drain(tb, carry):
        for u in range(unroll * topk):
            row_copy(0, 0, 0).wait()
        return carry

    lax.fori_loop(0, tm // unroll, drain, 0)

    lo = x2_ref[:, :dh]
    hi = x2_ref[:, dh:]
    gates = gate_ref[...]
    for kk in range(topk):
        wk = gates[:, kk:kk + 1]
        p = buf_ref[kk]
        lo = lo + wk * _unpack_lo(p)
        hi = hi + wk * _unpack_hi(p)
    ms = (jnp.sum(lo * lo, axis=-1, keepdims=True) + jnp.sum(hi * hi, axis=-1, keepdims=True)) / d
    inv = lax.rsqrt(ms + RMS_EPS)
    o_ref[:, :dh] = lo * inv * g_ref[:, :dh]
    o_ref[:, dh:] = hi * inv * g_ref[:, dh:]


def _combine(c, dest_flat, x2, gates, g, ys):
    n, d = x2.shape
    tm = c.tm_c
    kern = functools.partial(_combine_kernel, topk=c.K, unroll=8)
    return pl.pallas_call(
        kern,
        grid=(n // tm,),
        in_specs=[
            pl.BlockSpec((tm * c.K,), lambda i: (i,), memory_space=pltpu.SMEM),
            pl.BlockSpec((tm, d), lambda i: (i, 0)),
            pl.BlockSpec((tm, LANES), lambda i: (i, 0)),
            pl.BlockSpec((1, d), lambda i: (0, 0)),
            pl.BlockSpec(memory_space=pl.ANY),
        ],
        out_specs=pl.BlockSpec((tm, d), lambda i: (i, 0)),
        out_shape=jax.ShapeDtypeStruct((n, d), F32),
        scratch_shapes=[pltpu.VMEM((c.K, tm, d // 2), U32), pltpu.SemaphoreType.DMA(())],
        compiler_params=_cparams(("arbitrary",)),
        name="combine",
    )(dest_flat, x2, gates, g, ys)


def _plan(c, idx, rank, counts):
    tm, ts = c.tm_e, c.ts_e
    counts = counts.astype(I32)
    subs_e = (counts + ts - 1) // ts
    tiles_per_e = (counts + tm - 1) // tm
    tiles_safe = jnp.maximum(tiles_per_e, 1)
    fl = jnp.maximum(subs_e // tiles_safe, 1)
    rem = subs_e - (subs_e // tiles_safe) * tiles_safe
    hi_total = rem * (fl + 1)
    tile_end = jnp.cumsum(tiles_per_e)
    tile_start = tile_end - tiles_per_e

    def locate(s, hi_t, fl_t, rem_t):
        in_hi = s < hi_t
        s2 = s - hi_t
        j = jnp.where(in_hi, s // (fl_t + 1), rem_t + s2 // fl_t)
        w = jnp.where(in_hi, s % (fl_t + 1), s2 % fl_t)
        return j, w

    onehot = idx[..., None] == jnp.arange(c.E, dtype=I32)

    def look(table):
        return jnp.sum(jnp.where(onehot, table, 0), axis=-1)

    j, w = locate(rank // ts, look(hi_total), look(fl), look(rem))
    dest = (look(tile_start) + j) * tm + w * ts + rank % ts
    n_tiles = (c.B * c.S * c.K) // tm + c.E
    t = jnp.arange(n_tiles, dtype=I32)
    n_used = tile_end[-1]
    tc = jnp.minimum(t, n_used - 1)
    te = jnp.minimum(jnp.searchsorted(tile_end, tc, side="right"), c.E - 1).astype(I32)
    nsub = jnp.where(tc - tile_start[te] < rem[te], fl[te] + 1, fl[te])
    nsub = jnp.where(t < n_used, nsub, 0).astype(I32)
    jl, wl = locate(jnp.maximum(subs_e - 1, 0), hi_total, fl, rem)
    zrow = jnp.where(counts > 0, (tile_start + jl) * tm + wl * ts, -1).astype(I32)
    return dest.reshape(-1).astype(I32), te, nsub, tc.astype(I32), zrow


def _forward(c, x, mem, norm_mix_g, w_in, b_forget, sg_ln_g, sg_ln_b, w_spatial, b_spatial,
             w_branch_a, w_branch_b, w_out, norm_x_g, norm_mem_g, w_xq, w_xkv, w_xo,
             norm_ffn_g, w_router, b_router, w_gate_up, b_gate_up, w_down, b_down, norm_final_g):
    B, S, D = x.shape
    n = B * S
    fw = c.FH * LANES
    sw = c.SG * LANES
    x2d = x.reshape(n, D)

    o_f = 3 * fw
    o_z = o_f + c.FH
    o_g = o_z + 2 * sw
    w_main = jnp.concatenate([w_in[:, o_g:], w_in[:, o_z:o_g], w_in[:, :o_f]], axis=1).astype(BF16)
    w_f = jnp.pad(w_in[:, o_f:o_z], ((0, 0), (0, LANES - c.FH))).astype(BF16)
    b_f = jnp.pad(b_forget.astype(F32), (0, LANES - c.FH)).reshape(1, LANES)
    proj, logf = _in_proj(c, x2d, norm_mix_g.reshape(1, D), w_main, w_f, b_f, LOG2E * LANES ** -0.5)

    logf_bhs = logf[:, :c.FH].reshape(B, S, c.FH).transpose(0, 2, 1)
    csum = _cumsum(c, logf_bhs)
    crow = csum.reshape(B * c.FH, 1, S)
    ccol = csum.reshape(B * c.FH, S, 1)
    qcol0 = (2 * D + 2 * sw) // LANES
    attn = _fox(c, proj, crow, ccol, qcol0)

    b_s_full = jnp.broadcast_to(b_spatial.astype(F32)[:, :, None], (c.SG, c.SGC, LANES))
    x1 = _mix(c, x2d, attn, proj, sg_ln_g.reshape(1, sw), sg_ln_b.reshape(1, sw), w_spatial, b_s_full,
              w_branch_a.astype(BF16), w_branch_b.astype(BF16), w_out.astype(BF16))

    kv = _mem_kv(c, mem.reshape(B * c.MEM, D), norm_mem_g.reshape(1, D), w_xkv.astype(BF16))
    w_r32 = jnp.pad(w_router.astype(F32), ((0, 0), (0, LANES - c.E)))
    w_r_hi = w_r32.astype(BF16)
    w_r = jnp.concatenate([w_r_hi, (w_r32 - w_r_hi.astype(F32)).astype(BF16)], axis=1)
    b_r = jnp.pad(b_router.astype(F32), (0, LANES - c.E)).reshape(1, LANES)
    x2, hfp, idx, gates, rank, counts = _xattn(
        c, x1, norm_x_g.reshape(1, D), w_xq.astype(BF16), kv, w_xo.astype(BF16),
        norm_ffn_g.reshape(1, D), w_r, b_r, LANES ** -0.5)

    dest, te, nsub, tblk, zrow = _plan(c, idx[:, :c.K], rank[:, :c.K], counts[0, :c.E])
    p_rows = (n * c.K // c.tm_e + c.E) * c.tm_e
    xs = _dispatch(c, dest, zrow, hfp, p_rows)
    ys = _experts(c, te, nsub, tblk, xs, w_gate_up, b_gate_up.reshape(c.E, 1, 2 * c.DFF),
                  w_down, b_down.reshape(c.E, 1, D))
    out = _combine(c, dest, x2, gates, norm_final_g.reshape(1, D), ys)
    return out.reshape(B, S, D)


_CFG = Cfg(B=4, S=4096, D=2048, MEM=256, FH=8, SG=8, SGC=128, XH=4, E=32, K=4, DFF=2048,
           tm_in=1024, tn_in=512, tq=512, tm_mix=256, tm_x=512, tm_e=1024, ts_e=256, tf_e=256,
           tm_c=256, tc_cs=512, fox_parts=2)


@jax.jit
def kernel(x, mem, norm_mix_g, w_in, b_forget, sg_ln_g, sg_ln_b, w_spatial, b_spatial, w_branch_a, w_branch_b,
           w_out, norm_x_g, norm_mem_g, w_xq, w_xkv, w_xo, norm_ffn_g, w_router, b_router, w_gate_up,
           b_gate_up, w_down, b_down, norm_final_g):
    return _forward(_CFG, x, mem, norm_mix_g, w_in, b_forget, sg_ln_g, sg_ln_b, w_spatial, b_spatial,
                    w_branch_a, w_branch_b, w_out, norm_x_g, norm_mem_g, w_xq, w_xkv, w_xo,
                    norm_ffn_g, w_router, b_router, w_gate_up, b_gate_up, w_down, b_down, norm_final_g)
```

```python
import functools
from typing import NamedTuple

import jax
import jax.numpy as jnp
from jax import lax
from jax.experimental import pallas as pl
from jax.experimental.pallas import tpu as pltpu

F32 = jnp.float32
BF16 = jnp.bfloat16
U32 = jnp.uint32
I32 = jnp.int32

LANES = 128
VMEM_LIMIT = 56 * 1024 * 1024

RMS_EPS = 1e-6
LN_EPS = 1e-5
SWIGLU_LIMIT = 7.0
SWIGLU_ALPHA = 1.702
GELU_C = 0.7978845608028654
LOG2E = 1.4426950408889634


class Cfg(NamedTuple):
    B: int
    S: int
    D: int
    MEM: int
    FH: int
    SG: int
    SGC: int
    XH: int
    E: int
    K: int
    DFF: int
    tm_in: int
    tn_in: int
    tq: int
    tm_mix: int
    tm_x: int
    tm_e: int
    ts_e: int
    fc_e: int
    nck_e: int
    tm_c: int
    tc_cs: int
    fox_parts: int


def _cparams(sem):
    return pltpu.CompilerParams(dimension_semantics=sem, vmem_limit_bytes=VMEM_LIMIT)


def _rms(x, g):
    ms = jnp.mean(x * x, axis=-1, keepdims=True)
    return x * lax.rsqrt(ms + RMS_EPS) * g


def _pack_bf16_pair(a, b):
    def rne(v):
        bits = lax.bitcast_convert_type(v, U32)
        return bits + jnp.uint32(0x7FFF) + ((bits >> 16) & jnp.uint32(1))
    return (rne(a) >> 16) | (rne(b) & jnp.uint32(0xFFFF0000))


def _unpack_lo(p):
    return lax.bitcast_convert_type(p << 16, F32)


def _unpack_hi(p):
    return lax.bitcast_convert_type(p & jnp.uint32(0xFFFF0000), F32)


def _in_proj_kernel(x_ref, g_ref, w_ref, wf_ref, bf_ref, o_ref, f_ref, h_ref, *, nj_gate, nj_z, nj_q, qscale):
    j = pl.program_id(1)

    @pl.when(j == 0)
    def _():
        hb = _rms(x_ref[...], g_ref[...]).astype(BF16)
        h_ref[...] = hb
        f = jnp.dot(hb, wf_ref[...], preferred_element_type=F32) + bf_ref[...]
        f_ref[...] = jnp.minimum(f, 0.0) - jnp.log1p(jnp.exp(-jnp.abs(f)))

    acc = jnp.dot(h_ref[...], w_ref[...], preferred_element_type=F32)

    is_gate = j < nj_gate
    is_z = (j >= nj_gate) & (j < nj_gate + nj_z)
    is_q = (j >= nj_gate + nj_z) & (j < nj_gate + nj_z + nj_q)
    a1 = jnp.where(is_gate, 0.5, jnp.where(is_z, GELU_C, 0.0)).astype(F32)
    a3 = jnp.where(is_z, GELU_C * 0.044715, 0.0).astype(F32)
    b0 = jnp.where(is_gate, 0.5, 0.0).astype(F32)
    b1 = jnp.where(is_gate, 0.0, jnp.where(is_z, 0.5, jnp.where(is_q, qscale, 1.0))).astype(F32)
    th = jnp.tanh(acc * (a1 + a3 * (acc * acc)))
    o_ref[...] = ((b0 + b1 * acc) * (1.0 + th)).astype(o_ref.dtype)


def _in_proj(c, x2d, g, w_main, w_f, b_f, qscale):
    n, d = x2d.shape
    nc = w_main.shape[1]
    tm, tn = c.tm_in, c.tn_in
    fw = c.FH * LANES
    kern = functools.partial(_in_proj_kernel, nj_gate=2 * d // tn, nj_z=2 * c.SG * LANES // tn,
                             nj_q=fw // tn, qscale=qscale)
    return pl.pallas_call(
        kern,
        grid=(n // tm, nc // tn),
        in_specs=[
            pl.BlockSpec((tm, d), lambda i, j: (i, 0)),
            pl.BlockSpec((1, d), lambda i, j: (0, 0)),
            pl.BlockSpec((d, tn), lambda i, j: (0, j)),
            pl.BlockSpec((d, LANES), lambda i, j: (0, 0)),
            pl.BlockSpec((1, LANES), lambda i, j: (0, 0)),
        ],
        out_specs=[
            pl.BlockSpec((tm, tn), lambda i, j: (i, j)),
            pl.BlockSpec((tm, LANES), lambda i, j: (i, 0)),
        ],
        out_shape=[jax.ShapeDtypeStruct((n, nc), BF16), jax.ShapeDtypeStruct((n, LANES), F32)],
        scratch_shapes=[pltpu.VMEM((tm, d), BF16)],
        compiler_params=_cparams(("parallel", "arbitrary")),
        name="in_proj",
    )(x2d, g, w_main, w_f, b_f)


def _cumsum_kernel(f_ref, o_ref, *, tc):
    rows, s = f_ref.shape[1], f_ref.shape[2]
    r = lax.broadcasted_iota(I32, (tc, tc), 0)
    col = lax.broadcasted_iota(I32, (tc, tc), 1)
    upper = (r <= col).astype(F32)
    carry = jnp.zeros((rows, 1), F32)
    for i in range(s // tc):
        blk = f_ref[0, :, i * tc:(i + 1) * tc]
        cs = jnp.dot(blk, upper, preferred_element_type=F32, precision=lax.Precision.HIGHEST) + carry
        o_ref[0, :, i * tc:(i + 1) * tc] = cs * LOG2E
        carry = cs[:, tc - 1:tc]


def _cumsum(c, logf_bhs):
    b, h, s = logf_bhs.shape
    return pl.pallas_call(
        functools.partial(_cumsum_kernel, tc=c.tc_cs),
        grid=(b,),
        in_specs=[pl.BlockSpec((1, h, s), lambda i: (i, 0, 0))],
        out_specs=pl.BlockSpec((1, h, s), lambda i: (i, 0, 0)),
        out_shape=jax.ShapeDtypeStruct((b, h, s), F32),
        compiler_params=_cparams(("parallel",)),
        name="cumsum",
    )(logf_bhs)


def _fox_kernel(q_ref, k_ref, v_ref, crow_ref, ccol_ref, o_ref, m_ref, l_ref, acc_ref, cq_ref, s_ref,
                *, t, parts):
    qi = pl.program_id(2)
    hr = t // parts
    m_ref[...] = jnp.full(m_ref.shape, -jnp.inf, F32)
    l_ref[...] = jnp.zeros(l_ref.shape, F32)
    acc_ref[...] = jnp.zeros(acc_ref.shape, F32)
    cq_ref[...] = jnp.broadcast_to(ccol_ref[0], cq_ref.shape)

    def scores(ks):
        k = k_ref[pl.ds(ks, t), :]
        return lax.dot_general(q_ref[...], k, (((1,), (1,)), ((), ())), preferred_element_type=F32)

    def block(part, ks, width, masked):
        rows = slice(part * hr, (part + 1) * hr)
        v = v_ref[pl.ds(ks, width), :]
        s = s_ref[rows, :width]
        cq = cq_ref[rows, :]
        crow = crow_ref[0, :, pl.ds(ks, width)]
        nj = width // LANES
        sj = [s[:, j * LANES:(j + 1) * LANES] + cq - crow[:, j * LANES:(j + 1) * LANES] for j in range(nj)]
        if masked:
            row = lax.broadcasted_iota(I32, (hr, LANES), 0) + part * hr
            col = lax.broadcasted_iota(I32, (hr, LANES), 1)
            sj = [jnp.where(col + j * LANES <= row, sj[j], -jnp.inf) for j in range(nj)]
        mx = sj[0]
        for j in range(1, nj):
            mx = jnp.maximum(mx, sj[j])
        m_prev = m_ref[rows, :]
        m_next = jnp.maximum(m_prev, jnp.max(mx, axis=-1, keepdims=True))
        alpha = jnp.exp2(m_prev - m_next)
        pj = [jnp.exp2(sj[j] - m_next) for j in range(nj)]
        psum = pj[0]
        for j in range(1, nj):
            psum = psum + pj[j]
        p = jnp.concatenate([x.astype(BF16) for x in pj], axis=-1)
        l_ref[rows, :] = alpha * l_ref[rows, :] + psum
        acc_ref[rows, :] = alpha * acc_ref[rows, :] + jnp.dot(p, v, preferred_element_type=F32)
        m_ref[rows, :] = m_next

    s_ref[...] = scores(0)

    def body(kc, carry):
        ks = pl.multiple_of(kc * t, t)
        s_next = scores(pl.multiple_of(ks + t, t))
        for part in range(parts):
            block(part, ks, t, False)
        s_ref[...] = s_next
        return carry

    lax.fori_loop(0, qi, body, 0)
    kd = pl.multiple_of(qi * t, t)
    for part in range(parts):
        block(part, kd, (part + 1) * hr, True)
    l = jnp.sum(l_ref[...], axis=-1, keepdims=True)
    o_ref[...] = (acc_ref[...] / l).astype(o_ref.dtype)


def _fox(c, proj, crow, ccol, qcol0):
    n = c.B * c.S
    t = c.tq
    nq = c.S // t
    h = c.FH
    return pl.pallas_call(
        functools.partial(_fox_kernel, t=t, parts=c.fox_parts),
        grid=(c.B, h, nq),
        in_specs=[
            pl.BlockSpec((t, LANES), lambda b, hh, qi: (b * nq + qi, qcol0 + hh)),
            pl.BlockSpec((c.S, LANES), lambda b, hh, qi: (b, qcol0 + h + hh)),
            pl.BlockSpec((c.S, LANES), lambda b, hh, qi: (b, qcol0 + 2 * h + hh)),
            pl.BlockSpec((1, 1, c.S), lambda b, hh, qi: (b * h + hh, 0, 0)),
            pl.BlockSpec((1, t, 1), lambda b, hh, qi: (b * h + hh, qi, 0)),
        ],
        out_specs=pl.BlockSpec((t, LANES), lambda b, hh, qi: (b * nq + qi, hh)),
        out_shape=jax.ShapeDtypeStruct((n, h * LANES), BF16),
        scratch_shapes=[pltpu.VMEM((t, LANES), F32)] * 4 + [pltpu.VMEM((t, t), F32)],
        compiler_params=_cparams(("parallel", "parallel", "arbitrary")),
        name="fox",
    )(proj, proj, proj, crow, ccol)


def _mix_kernel(x_ref, a_ref, u_ref, v_ref, ga_ref, gb_ref, lng_ref, lnb_ref, ws_ref, bs_ref,
                wa_ref, wb_ref, wo_ref, o_ref, sg_ref, *, sgc, groups):
    tm = x_ref.shape[0]
    v = v_ref[...].astype(F32)
    mu = jnp.mean(v, axis=-1, keepdims=True)
    vc = v - mu
    var = jnp.mean(vc * vc, axis=-1, keepdims=True)
    vn = (vc * lax.rsqrt(var + LN_EPS) * lng_ref[...] + lnb_ref[...]).astype(BF16)
    row = lax.broadcasted_iota(I32, (sgc, sgc), 0)
    col = lax.broadcasted_iota(I32, (sgc, sgc), 1)
    for g in range(groups):
        w = jnp.where(col <= row, ws_ref[g], 0.0).astype(BF16)
        bias = bs_ref[g]
        for ci in range(tm // sgc):
            rs = slice(ci * sgc, (ci + 1) * sgc)
            cs = slice(g * LANES, (g + 1) * LANES)
            mixed = jnp.dot(w, vn[rs, cs], preferred_element_type=F32) + bias
            sg_ref[rs, cs] = (u_ref[rs, cs].astype(F32) * mixed).astype(BF16)
    ya = jnp.dot(a_ref[...], wa_ref[...], preferred_element_type=F32)
    yb = jnp.dot(sg_ref[...], wb_ref[...], preferred_element_type=F32)
    merged = (ga_ref[...].astype(F32) * ya + gb_ref[...].astype(F32) * yb).astype(BF16)
    o_ref[...] = x_ref[...] + jnp.dot(merged, wo_ref[...], preferred_element_type=F32)


def _const_spec(shape):
    nd = len(shape)
    return pl.BlockSpec(shape, lambda i: (0,) * nd, pipeline_mode=pl.Buffered(1))


def _mix(c, x2d, attn, proj, ln_g, ln_b, w_s, b_s_full, w_a, w_b, w_o):
    n, d = x2d.shape
    tm = c.tm_mix
    fw = c.FH * LANES
    sw = c.SG * LANES
    ucol = 2 * d // sw
    return pl.pallas_call(
        functools.partial(_mix_kernel, sgc=c.SGC, groups=c.SG),
        grid=(n // tm,),
        in_specs=[
            pl.BlockSpec((tm, d), lambda i: (i, 0)),
            pl.BlockSpec((tm, fw), lambda i: (i, 0)),
            pl.BlockSpec((tm, sw), lambda i: (i, ucol)),
            pl.BlockSpec((tm, sw), lambda i: (i, ucol + 1)),
            pl.BlockSpec((tm, d), lambda i: (i, 0)),
            pl.BlockSpec((tm, d), lambda i: (i, 1)),
            _const_spec((1, sw)),
            _const_spec((1, sw)),
            _const_spec((c.SG, c.SGC, c.SGC)),
            _const_spec((c.SG, c.SGC, LANES)),
            _const_spec((fw, d)),
            _const_spec((sw, d)),
            _const_spec((d, d)),
        ],
        out_specs=pl.BlockSpec((tm, d), lambda i: (i, 0)),
        out_shape=jax.ShapeDtypeStruct((n, d), F32),
        scratch_shapes=[pltpu.VMEM((tm, sw), BF16)],
        compiler_params=_cparams(("parallel",)),
        name="mix",
    )(x2d, attn, proj, proj, proj, proj, ln_g, ln_b, w_s, b_s_full, w_a, w_b, w_o)


def _mem_kv_kernel(m_ref, g_ref, w_ref, o_ref):
    hm = _rms(m_ref[...], g_ref[...]).astype(BF16)
    o_ref[...] = jnp.dot(hm, w_ref[...], preferred_element_type=F32).astype(o_ref.dtype)


def _mem_kv(c, mem2d, g, w_xkv):
    n, d = mem2d.shape
    nc = w_xkv.shape[1]
    tm = c.MEM
    return pl.pallas_call(
        _mem_kv_kernel,
        grid=(n // tm,),
        in_specs=[pl.BlockSpec((tm, d), lambda i: (i, 0)), _const_spec((1, d)), _const_spec((d, nc))],
        out_specs=pl.BlockSpec((tm, nc), lambda i: (i, 0)),
        out_shape=jax.ShapeDtypeStruct((n, nc), BF16),
        compiler_params=_cparams(("parallel",)),
        name="mem_kv",
    )(mem2d, g, w_xkv)


def _xattn_kernel(x1_ref, gx_ref, wq_ref, kv_ref, wo_ref, gf_ref, wr_ref, br_ref,
                  x2_ref, hfp_ref, idx_ref, gate_ref, rank_ref, cnt_ref, carry_ref,
                  *, heads, n_exp, topk, qscale):
    i = pl.program_id(0)
    tm, d = x1_ref.shape
    xw = heads * LANES
    x1 = x1_ref[...]
    hx = _rms(x1, gx_ref[...]).astype(BF16)
    q = (jnp.dot(hx, wq_ref[...], preferred_element_type=F32) * qscale).astype(BF16)
    outs = []
    for h in range(heads):
        k = kv_ref[:, h * LANES:(h + 1) * LANES]
        v = kv_ref[:, xw + h * LANES:xw + (h + 1) * LANES]
        s = lax.dot_general(q[:, h * LANES:(h + 1) * LANES], k, (((1,), (1,)), ((), ())),
                            preferred_element_type=F32)
        s = s - jnp.max(s, axis=-1, keepdims=True)
        p = jnp.exp(s)
        p = p / jnp.sum(p, axis=-1, keepdims=True)
        outs.append(jnp.dot(p.astype(BF16), v, preferred_element_type=F32).astype(BF16))
    o = jnp.concatenate(outs, axis=-1)
    x2 = x1 + jnp.dot(o, wo_ref[...], preferred_element_type=F32)
    x2_ref[...] = x2

    hf = _rms(x2, gf_ref[...])
    hfp_ref[...] = _pack_bf16_pair(hf[:, :d // 2], hf[:, d // 2:])

    h_hi = hf.astype(BF16)
    h_lo = (hf - h_hi.astype(F32)).astype(BF16)
    l_hi = jnp.dot(h_hi, wr_ref[...], preferred_element_type=F32)
    l_lo = jnp.dot(h_lo, wr_ref[:, :LANES], preferred_element_type=F32)
    logits = l_hi[:, :LANES] + l_hi[:, LANES:] + l_lo + br_ref[...]
    lane = lax.broadcasted_iota(I32, (tm, LANES), 1)
    lg = jnp.where(lane < n_exp, logits, -jnp.inf)
    vals, idxs = [], []
    for _ in range(topk):
        m = jnp.max(lg, axis=-1, keepdims=True)
        ix = jnp.min(jnp.where(lg == m, lane, LANES), axis=-1, keepdims=True)
        vals.append(m)
        idxs.append(ix)
        lg = jnp.where(lane == ix, -jnp.inf, lg)
    es = [jnp.exp(vv - vals[0]) for vv in vals]
    denom = es[0]
    for e in es[1:]:
        denom = denom + e

    @pl.when(i == 0)
    def _():
        carry_ref[...] = jnp.zeros(carry_ref.shape, F32)

    onehot = jnp.zeros((tm, LANES), F32)
    for ix in idxs:
        onehot = onehot + (lane == ix).astype(F32)
    r = lax.broadcasted_iota(I32, (tm, tm), 0)
    cc = lax.broadcasted_iota(I32, (tm, tm), 1)
    strict = (cc < r).astype(BF16)
    rank_full = jnp.dot(strict, onehot.astype(BF16), preferred_element_type=F32) + carry_ref[...]
    idx_out = jnp.zeros((tm, LANES), I32)
    gate_out = jnp.zeros((tm, LANES), F32)
    rank_out = jnp.zeros((tm, LANES), F32)
    for kk in range(topk):
        rk = jnp.sum(jnp.where(lane == idxs[kk], rank_full, 0.0), axis=-1, keepdims=True)
        idx_out = jnp.where(lane == kk, idxs[kk], idx_out)
        gate_out = jnp.where(lane == kk, es[kk] / denom, gate_out)
        rank_out = jnp.where(lane == kk, rk, rank_out)
    idx_ref[...] = idx_out
    gate_ref[...] = gate_out
    rank_ref[...] = rank_out.astype(I32)
    carry_ref[...] = carry_ref[...] + jnp.sum(onehot, axis=0, keepdims=True)
    cnt_ref[...] = carry_ref[...]


def _xattn(c, x1, gx, w_xq, kv, w_xo, gf, w_r, b_r, qscale):
    n, d = x1.shape
    tm = c.tm_x
    xw = c.XH * LANES
    per_b = c.S // tm
    kern = functools.partial(_xattn_kernel, heads=c.XH, n_exp=c.E, topk=c.K, qscale=qscale)
    return pl.pallas_call(
        kern,
        grid=(n // tm,),
        in_specs=[
            pl.BlockSpec((tm, d), lambda i: (i, 0)),
            _const_spec((1, d)),
            _const_spec((d, xw)),
            pl.BlockSpec((c.MEM, 2 * xw), lambda i: (i // per_b, 0)),
            _const_spec((xw, d)),
            _const_spec((1, d)),
            _const_spec((d, 2 * LANES)),
            _const_spec((1, LANES)),
        ],
        out_specs=[
            pl.BlockSpec((tm, d), lambda i: (i, 0)),
            pl.BlockSpec((tm, d // 2), lambda i: (i, 0)),
            pl.BlockSpec((tm, LANES), lambda i: (i, 0)),
            pl.BlockSpec((tm, LANES), lambda i: (i, 0)),
            pl.BlockSpec((tm, LANES), lambda i: (i, 0)),
            pl.BlockSpec((1, LANES), lambda i: (0, 0)),
        ],
        out_shape=[
            jax.ShapeDtypeStruct((n, d), F32),
            jax.ShapeDtypeStruct((n, d // 2), U32),
            jax.ShapeDtypeStruct((n, LANES), I32),
            jax.ShapeDtypeStruct((n, LANES), F32),
            jax.ShapeDtypeStruct((n, LANES), I32),
            jax.ShapeDtypeStruct((1, LANES), F32),
        ],
        scratch_shapes=[pltpu.VMEM((1, LANES), F32)],
        compiler_params=_cparams(("arbitrary",)),
        name="xattn_router",
    )(x1, gx, w_xq, kv, w_xo, gf, w_r, b_r)


def _dispatch_kernel(dest_ref, zrow_ref, hfp_ref, xs_ref, zbuf_ref, sem, zsem, *, n_exp, topk, unroll):
    i = pl.program_id(0)
    tm = hfp_ref.shape[0]
    ts = zbuf_ref.shape[0]

    def zero_copy(e):
        return pltpu.make_async_copy(zbuf_ref, xs_ref.at[pl.ds(pl.multiple_of(zrow_ref[e], ts), ts)], zsem)

    @pl.when(i == 0)
    def _():
        zbuf_ref[...] = jnp.zeros(zbuf_ref.shape, zbuf_ref.dtype)
        for e in range(n_exp):
            @pl.when(zrow_ref[e] >= 0)
            def _():
                zero_copy(e).start()
        for e in range(n_exp):
            @pl.when(zrow_ref[e] >= 0)
            def _():
                zero_copy(e).wait()

    def row_copy(t, a):
        return pltpu.make_async_copy(hfp_ref.at[pl.ds(t, 1)], xs_ref.at[pl.ds(dest_ref[a], 1)], sem)

    def issue(tb, carry):
        for u in range(unroll):
            t = tb * unroll + u
            for kk in range(topk):
                row_copy(t, t * topk + kk).start(priority=kk % 2)
        return carry

    lax.fori_loop(0, tm // unroll, issue, 0)

    def drain(tb, carry):
        for u in range(unroll * topk):
            row_copy(0, 0).wait()
        return carry

    lax.fori_loop(0, tm // unroll, drain, 0)


def _dispatch(c, dest_flat, zrow, hfp, p_rows):
    n, dh = hfp.shape
    tm = c.tm_c
    kern = functools.partial(_dispatch_kernel, n_exp=c.E, topk=c.K, unroll=8)
    return pl.pallas_call(
        kern,
        grid=(n // tm,),
        in_specs=[
            pl.BlockSpec((tm * c.K,), lambda i: (i,), memory_space=pltpu.SMEM),
            pl.BlockSpec(memory_space=pltpu.SMEM),
            pl.BlockSpec((tm, dh), lambda i: (i, 0)),
        ],
        out_specs=pl.BlockSpec(memory_space=pl.ANY),
        out_shape=jax.ShapeDtypeStruct((p_rows, dh), U32),
        scratch_shapes=[pltpu.VMEM((c.ts_e, dh), U32), pltpu.SemaphoreType.DMA(()), pltpu.SemaphoreType.DMA(())],
        compiler_params=_cparams(("arbitrary",)),
        name="dispatch",
    )(dest_flat, zrow, hfp)


def _expert_kernel(te_ref, ns_ref, x_ref, wg_ref, wu_ref, bg_ref, bu_ref, wd_ref, bd_ref, y_ref,
                   xb_ref, wgb_ref, wub_ref, wdb_ref, acc_ref, *, ts):
    t = pl.program_id(0)
    f = pl.program_id(1)
    nf = pl.num_programs(1)
    nsub = ns_ref[t]
    dh = x_ref.shape[1]

    @pl.when(nsub > 0)
    def _():
        @pl.when(f == 0)
        def _():
            acc_ref[...] = jnp.broadcast_to(bd_ref[0], acc_ref.shape)

            def unpack(s, carry):
                r = pl.multiple_of(s * ts, ts)
                p = x_ref[pl.ds(r, ts), :]
                xb_ref[pl.ds(r, ts), :dh] = _unpack_lo(p).astype(BF16)
                xb_ref[pl.ds(r, ts), dh:] = _unpack_hi(p).astype(BF16)
                return carry

            lax.fori_loop(0, nsub, unpack, 0)

        wgb_ref[...] = wg_ref[0].astype(BF16)
        wub_ref[...] = wu_ref[0].astype(BF16)
        wdb_ref[...] = wd_ref[0].astype(BF16)
        bg = bg_ref[0]
        bu = bu_ref[0]

        def sub(s):
            r = s * ts if isinstance(s, int) else pl.multiple_of(s * ts, ts)
            xs = xb_ref[pl.ds(r, ts), :]
            gate = jnp.dot(xs, wgb_ref[...], preferred_element_type=F32) + bg
            up = jnp.dot(xs, wub_ref[...], preferred_element_type=F32) + bu
            gate = jnp.minimum(gate, SWIGLU_LIMIT)
            up = jnp.clip(up, -SWIGLU_LIMIT, SWIGLU_LIMIT)
            act = (up + 1.0) * gate * jax.nn.sigmoid(SWIGLU_ALPHA * gate)
            acc_ref[pl.ds(r, ts), :] += jnp.dot(act.astype(BF16), wdb_ref[...], preferred_element_type=F32)

        sub(0)

        def pair(i, carry):
            sub(2 * i + 1)
            sub(2 * i + 2)
            return carry

        lax.fori_loop(0, (nsub - 1) // 2, pair, 0)

        @pl.when((nsub - 1) % 2 == 1)
        def _():
            sub(nsub - 1)

        @pl.when(f == nf - 1)
        def _():
            y_ref[...] = _pack_bf16_pair(acc_ref[:, :dh], acc_ref[:, dh:])


def _experts(c, tile_expert, tile_nsub, tile_blk, xs, w_gu, b_gu3, w_d, b_d3):
    p_rows, dh = xs.shape
    d = 2 * dh
    tm, tf = c.tm_e, c.tf_e
    nf = c.DFF // tf
    n_tiles = p_rows // tm

    def fblk(t, f, ns):
        return jnp.where(ns[t] > 0, f, nf - 1)

    grid_spec = pltpu.PrefetchScalarGridSpec(
        num_scalar_prefetch=3,
        grid=(n_tiles, nf),
        in_specs=[
            pl.BlockSpec((tm, dh), lambda t, f, te, ns, tb: (tb[t], 0)),
            pl.BlockSpec((1, d, tf), lambda t, f, te, ns, tb: (te[t], 0, fblk(t, f, ns))),
            pl.BlockSpec((1, d, tf), lambda t, f, te, ns, tb: (te[t], 0, nf + fblk(t, f, ns))),
            pl.BlockSpec((1, 1, tf), lambda t, f, te, ns, tb: (te[t], 0, fblk(t, f, ns))),
            pl.BlockSpec((1, 1, tf), lambda t, f, te, ns, tb: (te[t], 0, nf + fblk(t, f, ns))),
            pl.BlockSpec((1, tf, d), lambda t, f, te, ns, tb: (te[t], fblk(t, f, ns), 0)),
            pl.BlockSpec((1, 1, d), lambda t, f, te, ns, tb: (te[t], 0, 0)),
        ],
        out_specs=pl.BlockSpec((tm, dh), lambda t, f, te, ns, tb: (tb[t], 0)),
        scratch_shapes=[
            pltpu.VMEM((tm, d), BF16),
            pltpu.VMEM((d, tf), BF16),
            pltpu.VMEM((d, tf), BF16),
            pltpu.VMEM((tf, d), BF16),
            pltpu.VMEM((tm, d), F32),
        ],
    )

    def kern(te_ref, ns_ref, tb_ref, *rest):
        _expert_kernel(te_ref, ns_ref, *rest, ts=c.ts_e)

    return pl.pallas_call(
        kern,
        grid_spec=grid_spec,
        out_shape=jax.ShapeDtypeStruct((p_rows, dh), U32),
        compiler_params=_cparams(("arbitrary", "arbitrary")),
        name="experts",
    )(tile_expert, tile_nsub, tile_blk, xs, w_gu, w_gu, b_gu3, b_gu3, w_d, b_d3)


def _moe_kernel(te_ref, ns_ref, cpi_ref, xs_ref, wgu_ref, wd_ref, bgu_ref, bd_ref, ys_ref,
                wgub, wdb, stage, act, xin, ybuf, wsem, xsem, ysem, *, ts, tm, nck, fc):
    i = pl.program_id(0)
    n = pl.num_programs(0)
    nsub = ns_ref[i]
    e = te_ref[i]
    cpi = cpi_ref[i]
    d, f2 = wgub.shape
    f = f2 // 2
    dh = d // 2
    ckr = d // nck
    ckd = f // nck

    def gu_copy(ee, c, slot):
        return pltpu.make_async_copy(wgu_ref.at[ee, pl.ds(pl.multiple_of(c * ckr, ckr), ckr), :],
                                     stage.at[slot], wsem.at[slot])

    def d_copy(ee, c, slot):
        return pltpu.make_async_copy(wd_ref.at[ee, pl.ds(pl.multiple_of(c * ckd, ckd), ckd), :],
                                     stage.at[slot, pl.ds(0, ckd), pl.ds(0, d)], wsem.at[slot])

    def gu_step(ee, c):
        slot = c % 2
        gu_copy(ee, c, slot).wait()
        wgub[pl.ds(pl.multiple_of(c * ckr, ckr), ckr), :] = stage[slot].astype(BF16)

        @pl.when(c + 2 < nck)
        def _():
            gu_copy(ee, c + 2, slot).start()

    def d_step(ee, c):
        slot = c % 2
        d_copy(ee, c, slot).wait()
        wdb[pl.ds(pl.multiple_of(c * ckd, ckd), ckd), :] = stage[slot, :ckd, :d].astype(BF16)

        @pl.when(c + 2 < nck)
        def _():
            d_copy(ee, c + 2, slot).start()

    def steps_after(s, step_fn, ee):
        def one(jj, carry):
            c = s * cpi + jj

            @pl.when(c < nck)
            def _():
                step_fn(ee, c)
            return carry
        lax.fori_loop(0, cpi, one, 0)

    def x_copy(s, slot):
        row = pl.multiple_of(i * tm + s * ts, ts)
        return pltpu.make_async_copy(xs_ref.at[pl.ds(row, ts), :], xin.at[slot], xsem.at[slot])

    def y_copy(s, slot):
        row = pl.multiple_of(i * tm + s * ts, ts)
        return pltpu.make_async_copy(ybuf.at[slot], ys_ref.at[pl.ds(row, ts), :], ysem.at[slot])

    @pl.when(nsub > 0)
    def _():
        @pl.when(i == 0)
        def _():
            gu_copy(e, 0, 0).start()
            gu_copy(e, 1, 1).start()

            def first(c, carry):
                gu_step(e, c)
                return carry
            lax.fori_loop(0, nck, first, 0)

        d_copy(e, 0, 0).start()
        d_copy(e, 1, 1).start()
        x_copy(0, 0).start()
        bgu = bgu_ref[e]

        def phase_a(s, carry):
            slot = s % 2
            x_copy(s, slot).wait()

            @pl.when(s + 1 < nsub)
            def _():
                x_copy(s + 1, 1 - slot).start()

            p = xin[slot]
            xb = jnp.concatenate([_unpack_lo(p).astype(BF16), _unpack_hi(p).astype(BF16)], axis=-1)
            for cc in range(f // fc):
                gate = jnp.dot(xb, wgub[:, cc * fc:(cc + 1) * fc], preferred_element_type=F32)
                gate = gate + bgu[:, cc * fc:(cc + 1) * fc]
                up = jnp.dot(xb, wgub[:, f + cc * fc:f + (cc + 1) * fc], preferred_element_type=F32)
                up = up + bgu[:, f + cc * fc:f + (cc + 1) * fc]
                gate = jnp.minimum(gate, SWIGLU_LIMIT)
                up = jnp.clip(up, -SWIGLU_LIMIT, SWIGLU_LIMIT)
                a = (up + 1.0) * gate * jax.nn.sigmoid(SWIGLU_ALPHA * gate)
                act[s, :, cc * fc:(cc + 1) * fc] = a.astype(BF16)
            steps_after(s, d_step, e)
            return carry

        lax.fori_loop(0, nsub, phase_a, 0)

        nxt = jnp.minimum(i + 1, n - 1)
        e_next = te_ref[nxt]
        load_next = (i + 1 < n) & (ns_ref[nxt] > 0) & (e_next != e)

        @pl.when(load_next)
        def _():
            gu_copy(e_next, 0, 0).start()
            gu_copy(e_next, 1, 1).start()

        bd = bd_ref[e]

        def phase_b(s, carry):
            slot = s % 2
            y = jnp.dot(act[s], wdb[...], preferred_element_type=F32) + bd

            @pl.when(s >= 2)
            def _():
                y_copy(0, slot).wait()

            ybuf[slot] = _pack_bf16_pair(y[:, :dh], y[:, dh:])
            y_copy(s, slot).start()

            @pl.when(load_next)
            def _():
                steps_after(s, gu_step, e_next)
            return carry

        lax.fori_loop(0, nsub, phase_b, 0)

        y_copy(0, (nsub - 1) % 2).wait()

        @pl.when(nsub >= 2)
        def _():
            y_copy(0, nsub % 2).wait()


def _moe(c, tile_expert, tile_nsub, tile_cpi, xs, w_gu, b_gu3, w_d, b_d3):
    p_rows, dh = xs.shape
    d = 2 * dh
    f = c.DFF
    tm, ts, nck = c.tm_e, c.ts_e, c.nck_e
    n_tiles = p_rows // tm
    grid_spec = pltpu.PrefetchScalarGridSpec(
        num_scalar_prefetch=3,
        grid=(n_tiles,),
        in_specs=[
            pl.BlockSpec(memory_space=pl.ANY),
            pl.BlockSpec(memory_space=pl.ANY),
            pl.BlockSpec(memory_space=pl.ANY),
            pl.BlockSpec((c.E, 1, 2 * f), lambda i, *_: (0, 0, 0), pipeline_mode=pl.Buffered(1)),
            pl.BlockSpec((c.E, 1, d), lambda i, *_: (0, 0, 0), pipeline_mode=pl.Buffered(1)),
        ],
        out_specs=pl.BlockSpec(memory_space=pl.ANY),
        scratch_shapes=[
            pltpu.VMEM((d, 2 * f), BF16),
            pltpu.VMEM((f, d), BF16),
            pltpu.VMEM((2, d // nck, 2 * f), F32),
            pltpu.VMEM((tm // ts, ts, f), BF16),
            pltpu.VMEM((2, ts, dh), U32),
            pltpu.VMEM((2, ts, dh), U32),
            pltpu.SemaphoreType.DMA((2,)),
            pltpu.SemaphoreType.DMA((2,)),
            pltpu.SemaphoreType.DMA((2,)),
        ],
    )
    kern = functools.partial(_moe_kernel, ts=ts, tm=tm, nck=nck, fc=c.fc_e)
    return pl.pallas_call(
        kern,
        grid_spec=grid_spec,
        out_shape=jax.ShapeDtypeStruct((p_rows, dh), U32),
        compiler_params=_cparams(("arbitrary",)),
        name="experts",
    )(tile_expert, tile_nsub, tile_cpi, xs, w_gu, w_d, b_gu3, b_d3)


def _combine_kernel(dest_ref, x2_ref, gate_ref, g_ref, ys_ref, o_ref, buf_ref, sem, *, topk, unroll):
    tm, d = x2_ref.shape
    dh = d // 2

    def row_copy(t, kk, a):
        return pltpu.make_async_copy(ys_ref.at[pl.ds(dest_ref[a], 1)], buf_ref.at[kk, pl.ds(t, 1)], sem)

    def issue(tb, carry):
        for u in range(unroll):
            t = tb * unroll + u
            for kk in range(topk):
                row_copy(t, kk, t * topk + kk).start(priority=kk % 2)
        return carry

    lax.fori_loop(0, tm // unroll, issue, 0)

    def drain(tb, carry):
        for u in range(unroll * topk):
            row_copy(0, 0, 0).wait()
        return carry

    lax.fori_loop(0, tm // unroll, drain, 0)

    lo = x2_ref[:, :dh]
    hi = x2_ref[:, dh:]
    gates = gate_ref[...]
    for kk in range(topk):
        wk = gates[:, kk:kk + 1]
        p = buf_ref[kk]
        lo = lo + wk * _unpack_lo(p)
        hi = hi + wk * _unpack_hi(p)
    ms = (jnp.sum(lo * lo, axis=-1, keepdims=True) + jnp.sum(hi * hi, axis=-1, keepdims=True)) / d
    inv = lax.rsqrt(ms + RMS_EPS)
    o_ref[:, :dh] = lo * inv * g_ref[:, :dh]
    o_ref[:, dh:] = hi * inv * g_ref[:, dh:]


def _combine(c, dest_flat, x2, gates, g, ys):
    n, d = x2.shape
    tm = c.tm_c
    kern = functools.partial(_combine_kernel, topk=c.K, unroll=8)
    return pl.pallas_call(
        kern,
        grid=(n // tm,),
        in_specs=[
            pl.BlockSpec((tm * c.K,), lambda i: (i,), memory_space=pltpu.SMEM),
            pl.BlockSpec((tm, d), lambda i: (i, 0)),
            pl.BlockSpec((tm, LANES), lambda i: (i, 0)),
            pl.BlockSpec((1, d), lambda i: (0, 0)),
            pl.BlockSpec(memory_space=pl.ANY),
        ],
        out_specs=pl.BlockSpec((tm, d), lambda i: (i, 0)),
        out_shape=jax.ShapeDtypeStruct((n, d), F32),
        scratch_shapes=[pltpu.VMEM((c.K, tm, d // 2), U32), pltpu.SemaphoreType.DMA(())],
        compiler_params=_cparams(("arbitrary",)),
        name="combine",
    )(dest_flat, x2, gates, g, ys)


def _plan(c, idx, rank, counts):
    tm, ts = c.tm_e, c.ts_e
    counts = counts.astype(I32)
    subs_e = (counts + ts - 1) // ts
    tiles_per_e = (counts + tm - 1) // tm
    tiles_safe = jnp.maximum(tiles_per_e, 1)
    fl = jnp.maximum(subs_e // tiles_safe, 1)
    rem = subs_e - (subs_e // tiles_safe) * tiles_safe
    hi_total = rem * (fl + 1)
    tile_end = jnp.cumsum(tiles_per_e)
    tile_start = tile_end - tiles_per_e

    def locate(s, hi_t, fl_t, rem_t):
        in_hi = s < hi_t
        s2 = s - hi_t
        j = jnp.where(in_hi, s // (fl_t + 1), rem_t + s2 // fl_t)
        w = jnp.where(in_hi, s % (fl_t + 1), s2 % fl_t)
        return j, w

    onehot = idx[..., None] == jnp.arange(c.E, dtype=I32)

    def look(table):
        return jnp.sum(jnp.where(onehot, table, 0), axis=-1)

    j, w = locate(rank // ts, look(hi_total), look(fl), look(rem))
    dest = (look(tile_start) + j) * tm + w * ts + rank % ts
    n_tiles = _n_tiles(c)
    t = jnp.arange(n_tiles, dtype=I32)
    n_used = tile_end[-1]
    tc = jnp.minimum(t, n_used - 1)
    te = jnp.minimum(jnp.searchsorted(tile_end, tc, side="right"), c.E - 1).astype(I32)
    nsub = jnp.where(tc - tile_start[te] < rem[te], fl[te] + 1, fl[te])
    nsub = jnp.where(t < n_used, nsub, 0).astype(I32)
    jl, wl = locate(jnp.maximum(subs_e - 1, 0), hi_total, fl, rem)
    zrow = jnp.where(counts > 0, (tile_start + jl) * tm + wl * ts, -1).astype(I32)
    cpi = (c.nck_e + jnp.maximum(nsub, 1) - 1) // jnp.maximum(nsub, 1)
    return dest.reshape(-1).astype(I32), te, nsub, cpi.astype(I32), zrow


def _n_tiles(c):
    return -(-(c.B * c.S * c.K) // c.tm_e) + c.E


def _forward(c, x, mem, norm_mix_g, w_in, b_forget, sg_ln_g, sg_ln_b, w_spatial, b_spatial,
             w_branch_a, w_branch_b, w_out, norm_x_g, norm_mem_g, w_xq, w_xkv, w_xo,
             norm_ffn_g, w_router, b_router, w_gate_up, b_gate_up, w_down, b_down, norm_final_g):
    B, S, D = x.shape
    n = B * S
    fw = c.FH * LANES
    sw = c.SG * LANES
    x2d = x.reshape(n, D)

    o_f = 3 * fw
    o_z = o_f + c.FH
    o_g = o_z + 2 * sw
    w_main = jnp.concatenate([w_in[:, o_g:], w_in[:, o_z:o_g], w_in[:, :o_f]], axis=1).astype(BF16)
    w_f = jnp.pad(w_in[:, o_f:o_z], ((0, 0), (0, LANES - c.FH))).astype(BF16)
    b_f = jnp.pad(b_forget.astype(F32), (0, LANES - c.FH)).reshape(1, LANES)
    proj, logf = _in_proj(c, x2d, norm_mix_g.reshape(1, D), w_main, w_f, b_f, LOG2E * LANES ** -0.5)

    logf_bhs = logf[:, :c.FH].reshape(B, S, c.FH).transpose(0, 2, 1)
    csum = _cumsum(c, logf_bhs)
    crow = csum.reshape(B * c.FH, 1, S)
    ccol = csum.reshape(B * c.FH, S, 1)
    qcol0 = (2 * D + 2 * sw) // LANES
    attn = _fox(c, proj, crow, ccol, qcol0)

    b_s_full = jnp.broadcast_to(b_spatial.astype(F32)[:, :, None], (c.SG, c.SGC, LANES))
    x1 = _mix(c, x2d, attn, proj, sg_ln_g.reshape(1, sw), sg_ln_b.reshape(1, sw), w_spatial, b_s_full,
              w_branch_a.astype(BF16), w_branch_b.astype(BF16), w_out.astype(BF16))

    kv = _mem_kv(c, mem.reshape(B * c.MEM, D), norm_mem_g.reshape(1, D), w_xkv.astype(BF16))
    w_r32 = jnp.pad(w_router.astype(F32), ((0, 0), (0, LANES - c.E)))
    w_r_hi = w_r32.astype(BF16)
    w_r = jnp.concatenate([w_r_hi, (w_r32 - w_r_hi.astype(F32)).astype(BF16)], axis=1)
    b_r = jnp.pad(b_router.astype(F32), (0, LANES - c.E)).reshape(1, LANES)
    x2, hfp, idx, gates, rank, counts = _xattn(
        c, x1, norm_x_g.reshape(1, D), w_xq.astype(BF16), kv, w_xo.astype(BF16),
        norm_ffn_g.reshape(1, D), w_r, b_r, LANES ** -0.5)

    dest, te, nsub, cpi, zrow = _plan(c, idx[:, :c.K], rank[:, :c.K], counts[0, :c.E])
    p_rows = _n_tiles(c) * c.tm_e
    xs = _dispatch(c, dest, zrow, hfp, p_rows)
    ys = _moe(c, te, nsub, cpi, xs, w_gate_up, b_gate_up.reshape(c.E, 1, 2 * c.DFF),
              w_down, b_down.reshape(c.E, 1, D))
    out = _combine(c, dest, x2, gates, norm_final_g.reshape(1, D), ys)
    return out.reshape(B, S, D)


_CFG = Cfg(B=4, S=4096, D=2048, MEM=256, FH=8, SG=8, SGC=128, XH=4, E=32, K=4, DFF=2048,
           tm_in=1024, tn_in=512, tq=512, tm_mix=256, tm_x=512, tm_e=2560, ts_e=256, fc_e=512, nck_e=16,
           tm_c=256, tc_cs=512, fox_parts=2)


@jax.jit
def kernel(x, mem, norm_mix_g, w_in, b_forget, sg_ln_g, sg_ln_b, w_spatial, b_spatial, w_branch_a, w_branch_b,
           w_out, norm_x_g, norm_mem_g, w_xq, w_xkv, w_xo, norm_ffn_g, w_router, b_router, w_gate_up,
           b_gate_up, w_down, b_down, norm_final_g):
    return _forward(_CFG, x, mem, norm_mix_g, w_in, b_forget, sg_ln_g, sg_ln_b, w_spatial, b_spatial,
                    w_branch_a, w_branch_b, w_out, norm_x_g, norm_mem_g, w_xq, w_xkv, w_xo,
                    norm_ffn_g, w_router, b_router, w_gate_up, b_gate_up, w_down, b_down, norm_final_g)
```

```python
import functools
from typing import NamedTuple

import jax
import jax.numpy as jnp
from jax import lax
from jax.experimental import pallas as pl
from jax.experimental.pallas import tpu as pltpu

F32 = jnp.float32
BF16 = jnp.bfloat16
U32 = jnp.uint32
I32 = jnp.int32

LANES = 128
VMEM_LIMIT = 56 * 1024 * 1024

RMS_EPS = 1e-6
LN_EPS = 1e-5
SWIGLU_LIMIT = 7.0
SWIGLU_ALPHA = 1.702
GELU_C = 0.7978845608028654
LOG2E = 1.4426950408889634


class Cfg(NamedTuple):
    B: int
    S: int
    D: int
    MEM: int
    FH: int
    SG: int
    SGC: int
    XH: int
    E: int
    K: int
    DFF: int
    tm_in: int
    tn_in: int
    tq: int
    tm_mix: int
    tm_x: int
    tm_e: int
    ts_e: int
    fc_e: int
    nck_e: int
    tm_c: int
    tc_cs: int
    fox_parts: int


def _cparams(sem):
    return pltpu.CompilerParams(dimension_semantics=sem, vmem_limit_bytes=VMEM_LIMIT)


def _rms(x, g):
    ms = jnp.mean(x * x, axis=-1, keepdims=True)
    return x * lax.rsqrt(ms + RMS_EPS) * g


def _pack_bf16_pair(a, b):
    def rne(v):
        bits = lax.bitcast_convert_type(v, U32)
        return bits + jnp.uint32(0x7FFF) + ((bits >> 16) & jnp.uint32(1))
    return (rne(a) >> 16) | (rne(b) & jnp.uint32(0xFFFF0000))


def _unpack_lo(p):
    return lax.bitcast_convert_type(p << 16, F32)


def _unpack_hi(p):
    return lax.bitcast_convert_type(p & jnp.uint32(0xFFFF0000), F32)


def _in_proj_kernel(x_ref, g_ref, w_ref, wf_ref, bf_ref, o_ref, f_ref, h_ref, *, nj_gate, nj_z, nj_q, qscale):
    j = pl.program_id(1)

    @pl.when(j == 0)
    def _():
        hb = _rms(x_ref[...], g_ref[...]).astype(BF16)
        h_ref[...] = hb
        f = jnp.dot(hb, wf_ref[...], preferred_element_type=F32) + bf_ref[...]
        f_ref[...] = jnp.minimum(f, 0.0) - jnp.log1p(jnp.exp(-jnp.abs(f)))

    acc = jnp.dot(h_ref[...], w_ref[...], preferred_element_type=F32)

    is_gate = j < nj_gate
    is_z = (j >= nj_gate) & (j < nj_gate + nj_z)
    is_q = (j >= nj_gate + nj_z) & (j < nj_gate + nj_z + nj_q)
    a1 = jnp.where(is_gate, 0.5, jnp.where(is_z, GELU_C, 0.0)).astype(F32)
    a3 = jnp.where(is_z, GELU_C * 0.044715, 0.0).astype(F32)
    b0 = jnp.where(is_gate, 0.5, 0.0).astype(F32)
    b1 = jnp.where(is_gate, 0.0, jnp.where(is_z, 0.5, jnp.where(is_q, qscale, 1.0))).astype(F32)
    th = jnp.tanh(acc * (a1 + a3 * (acc * acc)))
    o_ref[...] = ((b0 + b1 * acc) * (1.0 + th)).astype(o_ref.dtype)


def _in_proj(c, x2d, g, w_main, w_f, b_f, qscale):
    n, d = x2d.shape
    nc = w_main.shape[1]
    tm, tn = c.tm_in, c.tn_in
    fw = c.FH * LANES
    kern = functools.partial(_in_proj_kernel, nj_gate=2 * d // tn, nj_z=2 * c.SG * LANES // tn,
                             nj_q=fw // tn, qscale=qscale)
    return pl.pallas_call(
        kern,
        grid=(n // tm, nc // tn),
        in_specs=[
            pl.BlockSpec((tm, d), lambda i, j: (i, 0)),
            pl.BlockSpec((1, d), lambda i, j: (0, 0)),
            pl.BlockSpec((d, tn), lambda i, j: (0, j)),
            pl.BlockSpec((d, LANES), lambda i, j: (0, 0)),
            pl.BlockSpec((1, LANES), lambda i, j: (0, 0)),
        ],
        out_specs=[
            pl.BlockSpec((tm, tn), lambda i, j: (i, j)),
            pl.BlockSpec((tm, LANES), lambda i, j: (i, 0)),
        ],
        out_shape=[jax.ShapeDtypeStruct((n, nc), BF16), jax.ShapeDtypeStruct((n, LANES), F32)],
        scratch_shapes=[pltpu.VMEM((tm, d), BF16)],
        compiler_params=_cparams(("parallel", "arbitrary")),
        name="in_proj",
    )(x2d, g, w_main, w_f, b_f)


def _cumsum_kernel(f_ref, o_ref, *, tc):
    rows, s = f_ref.shape[1], f_ref.shape[2]
    r = lax.broadcasted_iota(I32, (tc, tc), 0)
    col = lax.broadcasted_iota(I32, (tc, tc), 1)
    upper = (r <= col).astype(F32)
    carry = jnp.zeros((rows, 1), F32)
    for i in range(s // tc):
        blk = f_ref[0, :, i * tc:(i + 1) * tc]
        cs = jnp.dot(blk, upper, preferred_element_type=F32, precision=lax.Precision.HIGHEST) + carry
        o_ref[0, :, i * tc:(i + 1) * tc] = cs * LOG2E
        carry = cs[:, tc - 1:tc]


def _cumsum(c, logf_bhs):
    b, h, s = logf_bhs.shape
    return pl.pallas_call(
        functools.partial(_cumsum_kernel, tc=c.tc_cs),
        grid=(b,),
        in_specs=[pl.BlockSpec((1, h, s), lambda i: (i, 0, 0))],
        out_specs=pl.BlockSpec((1, h, s), lambda i: (i, 0, 0)),
        out_shape=jax.ShapeDtypeStruct((b, h, s), F32),
        compiler_params=_cparams(("parallel",)),
        name="cumsum",
    )(logf_bhs)


def _fox_kernel(q_ref, k_ref, v_ref, crow_ref, ccol_ref, o_ref, m_ref, l_ref, acc_ref, cq_ref, s_ref,
                *, t, parts):
    qi = pl.program_id(2)
    hr = t // parts
    m_ref[...] = jnp.full(m_ref.shape, -jnp.inf, F32)
    l_ref[...] = jnp.zeros(l_ref.shape, F32)
    acc_ref[...] = jnp.zeros(acc_ref.shape, F32)
    cq_ref[...] = jnp.broadcast_to(ccol_ref[0], cq_ref.shape)

    def scores(ks):
        k = k_ref[pl.ds(ks, t), :]
        return lax.dot_general(q_ref[...], k, (((1,), (1,)), ((), ())), preferred_element_type=F32)

    def block(part, ks, width, masked):
        rows = slice(part * hr, (part + 1) * hr)
        v = v_ref[pl.ds(ks, width), :]
        s = s_ref[rows, :width]
        cq = cq_ref[rows, :]
        crow = crow_ref[0, :, pl.ds(ks, width)]
        nj = width // LANES
        sj = [s[:, j * LANES:(j + 1) * LANES] + cq - crow[:, j * LANES:(j + 1) * LANES] for j in range(nj)]
        if masked:
            row = lax.broadcasted_iota(I32, (hr, LANES), 0) + part * hr
            col = lax.broadcasted_iota(I32, (hr, LANES), 1)
            sj = [jnp.where(col + j * LANES <= row, sj[j], -jnp.inf) for j in range(nj)]
        mx = sj[0]
        for j in range(1, nj):
            mx = jnp.maximum(mx, sj[j])
        m_prev = m_ref[rows, :]
        m_next = jnp.maximum(m_prev, jnp.max(mx, axis=-1, keepdims=True))
        alpha = jnp.exp2(m_prev - m_next)
        pj = [jnp.exp2(sj[j] - m_next) for j in range(nj)]
        psum = pj[0]
        for j in range(1, nj):
            psum = psum + pj[j]
        p = jnp.concatenate([x.astype(BF16) for x in pj], axis=-1)
        l_ref[rows, :] = alpha * l_ref[rows, :] + psum
        acc_ref[rows, :] = alpha * acc_ref[rows, :] + jnp.dot(p, v, preferred_element_type=F32)
        m_ref[rows, :] = m_next

    s_ref[...] = scores(0)

    def body(kc, carry):
        ks = pl.multiple_of(kc * t, t)
        s_next = scores(pl.multiple_of(ks + t, t))
        for part in range(parts):
            block(part, ks, t, False)
        s_ref[...] = s_next
        return carry

    lax.fori_loop(0, qi, body, 0)
    kd = pl.multiple_of(qi * t, t)
    for part in range(parts):
        block(part, kd, (part + 1) * hr, True)
    l = jnp.sum(l_ref[...], axis=-1, keepdims=True)
    o_ref[...] = (acc_ref[...] / l).astype(o_ref.dtype)


def _fox(c, proj, crow, ccol, qcol0):
    n = c.B * c.S
    t = c.tq
    nq = c.S // t
    h = c.FH
    return pl.pallas_call(
        functools.partial(_fox_kernel, t=t, parts=c.fox_parts),
        grid=(c.B, h, nq),
        in_specs=[
            pl.BlockSpec((t, LANES), lambda b, hh, qi: (b * nq + qi, qcol0 + hh)),
            pl.BlockSpec((c.S, LANES), lambda b, hh, qi: (b, qcol0 + h + hh)),
            pl.BlockSpec((c.S, LANES), lambda b, hh, qi: (b, qcol0 + 2 * h + hh)),
            pl.BlockSpec((1, 1, c.S), lambda b, hh, qi: (b * h + hh, 0, 0)),
            pl.BlockSpec((1, t, 1), lambda b, hh, qi: (b * h + hh, qi, 0)),
        ],
        out_specs=pl.BlockSpec((t, LANES), lambda b, hh, qi: (b * nq + qi, hh)),
        out_shape=jax.ShapeDtypeStruct((n, h * LANES), BF16),
        scratch_shapes=[pltpu.VMEM((t, LANES), F32)] * 4 + [pltpu.VMEM((t, t), F32)],
        compiler_params=_cparams(("parallel", "parallel", "arbitrary")),
        name="fox",
    )(proj, proj, proj, crow, ccol)


def _mix_kernel(x_ref, a_ref, u_ref, v_ref, ga_ref, gb_ref, lng_ref, lnb_ref, ws_ref, bs_ref,
                wa_ref, wb_ref, wo_ref, o_ref, sg_ref, *, sgc, groups):
    tm = x_ref.shape[0]
    v = v_ref[...].astype(F32)
    mu = jnp.mean(v, axis=-1, keepdims=True)
    vc = v - mu
    var = jnp.mean(vc * vc, axis=-1, keepdims=True)
    vn = (vc * lax.rsqrt(var + LN_EPS) * lng_ref[...] + lnb_ref[...]).astype(BF16)
    row = lax.broadcasted_iota(I32, (sgc, sgc), 0)
    col = lax.broadcasted_iota(I32, (sgc, sgc), 1)
    for g in range(groups):
        w = jnp.where(col <= row, ws_ref[g], 0.0).astype(BF16)
        bias = bs_ref[g]
        for ci in range(tm // sgc):
            rs = slice(ci * sgc, (ci + 1) * sgc)
            cs = slice(g * LANES, (g + 1) * LANES)
            mixed = jnp.dot(w, vn[rs, cs], preferred_element_type=F32) + bias
            sg_ref[rs, cs] = (u_ref[rs, cs].astype(F32) * mixed).astype(BF16)
    ya = jnp.dot(a_ref[...], wa_ref[...], preferred_element_type=F32)
    yb = jnp.dot(sg_ref[...], wb_ref[...], preferred_element_type=F32)
    merged = (ga_ref[...].astype(F32) * ya + gb_ref[...].astype(F32) * yb).astype(BF16)
    o_ref[...] = x_ref[...] + jnp.dot(merged, wo_ref[...], preferred_element_type=F32)


def _const_spec(shape):
    nd = len(shape)
    return pl.BlockSpec(shape, lambda i: (0,) * nd, pipeline_mode=pl.Buffered(1))


def _mix(c, x2d, attn, proj, ln_g, ln_b, w_s, b_s_full, w_a, w_b, w_o):
    n, d = x2d.shape
    tm = c.tm_mix
    fw = c.FH * LANES
    sw = c.SG * LANES
    ucol = 2 * d // sw
    return pl.pallas_call(
        functools.partial(_mix_kernel, sgc=c.SGC, groups=c.SG),
        grid=(n // tm,),
        in_specs=[
            pl.BlockSpec((tm, d), lambda i: (i, 0)),
            pl.BlockSpec((tm, fw), lambda i: (i, 0)),
            pl.BlockSpec((tm, sw), lambda i: (i, ucol)),
            pl.BlockSpec((tm, sw), lambda i: (i, ucol + 1)),
            pl.BlockSpec((tm, d), lambda i: (i, 0)),
            pl.BlockSpec((tm, d), lambda i: (i, 1)),
            _const_spec((1, sw)),
            _const_spec((1, sw)),
            _const_spec((c.SG, c.SGC, c.SGC)),
            _const_spec((c.SG, c.SGC, LANES)),
            _const_spec((fw, d)),
            _const_spec((sw, d)),
            _const_spec((d, d)),
        ],
        out_specs=pl.BlockSpec((tm, d), lambda i: (i, 0)),
        out_shape=jax.ShapeDtypeStruct((n, d), F32),
        scratch_shapes=[pltpu.VMEM((tm, sw), BF16)],
        compiler_params=_cparams(("parallel",)),
        name="mix",
    )(x2d, attn, proj, proj, proj, proj, ln_g, ln_b, w_s, b_s_full, w_a, w_b, w_o)


def _mem_kv_kernel(m_ref, g_ref, w_ref, o_ref):
    hm = _rms(m_ref[...], g_ref[...]).astype(BF16)
    o_ref[...] = jnp.dot(hm, w_ref[...], preferred_element_type=F32).astype(o_ref.dtype)


def _mem_kv(c, mem2d, g, w_xkv):
    n, d = mem2d.shape
    nc = w_xkv.shape[1]
    tm = c.MEM
    return pl.pallas_call(
        _mem_kv_kernel,
        grid=(n // tm,),
        in_specs=[pl.BlockSpec((tm, d), lambda i: (i, 0)), _const_spec((1, d)), _const_spec((d, nc))],
        out_specs=pl.BlockSpec((tm, nc), lambda i: (i, 0)),
        out_shape=jax.ShapeDtypeStruct((n, nc), BF16),
        compiler_params=_cparams(("parallel",)),
        name="mem_kv",
    )(mem2d, g, w_xkv)


def _xattn_kernel(x1_ref, gx_ref, wq_ref, kv_ref, wo_ref, gf_ref, wr_ref, br_ref,
                  x2_ref, hfp_ref, idx_ref, gate_ref, rank_ref, cnt_ref, carry_ref,
                  *, heads, n_exp, topk, qscale):
    i = pl.program_id(0)
    tm, d = x1_ref.shape
    xw = heads * LANES
    x1 = x1_ref[...]
    hx = _rms(x1, gx_ref[...]).astype(BF16)
    q = (jnp.dot(hx, wq_ref[...], preferred_element_type=F32) * qscale).astype(BF16)
    outs = []
    for h in range(heads):
        k = kv_ref[:, h * LANES:(h + 1) * LANES]
        v = kv_ref[:, xw + h * LANES:xw + (h + 1) * LANES]
        s = lax.dot_general(q[:, h * LANES:(h + 1) * LANES], k, (((1,), (1,)), ((), ())),
                            preferred_element_type=F32)
        s = s - jnp.max(s, axis=-1, keepdims=True)
        p = jnp.exp(s)
        p = p / jnp.sum(p, axis=-1, keepdims=True)
        outs.append(jnp.dot(p.astype(BF16), v, preferred_element_type=F32).astype(BF16))
    o = jnp.concatenate(outs, axis=-1)
    x2 = x1 + jnp.dot(o, wo_ref[...], preferred_element_type=F32)
    x2_ref[...] = x2

    hf = _rms(x2, gf_ref[...])
    hfp_ref[...] = _pack_bf16_pair(hf[:, :d // 2], hf[:, d // 2:])

    h_hi = hf.astype(BF16)
    h_lo = (hf - h_hi.astype(F32)).astype(BF16)
    l_hi = jnp.dot(h_hi, wr_ref[...], preferred_element_type=F32)
    l_lo = jnp.dot(h_lo, wr_ref[:, :LANES], preferred_element_type=F32)
    logits = l_hi[:, :LANES] + l_hi[:, LANES:] + l_lo + br_ref[...]
    lane = lax.broadcasted_iota(I32, (tm, LANES), 1)
    lg = jnp.where(lane < n_exp, logits, -jnp.inf)
    vals, idxs = [], []
    for _ in range(topk):
        m = jnp.max(lg, axis=-1, keepdims=True)
        ix = jnp.min(jnp.where(lg == m, lane, LANES), axis=-1, keepdims=True)
        vals.append(m)
        idxs.append(ix)
        lg = jnp.where(lane == ix, -jnp.inf, lg)
    es = [jnp.exp(vv - vals[0]) for vv in vals]
    denom = es[0]
    for e in es[1:]:
        denom = denom + e

    @pl.when(i == 0)
    def _():
        carry_ref[...] = jnp.zeros(carry_ref.shape, F32)

    onehot = jnp.zeros((tm, LANES), F32)
    for ix in idxs:
        onehot = onehot + (lane == ix).astype(F32)
    r = lax.broadcasted_iota(I32, (tm, tm), 0)
    cc = lax.broadcasted_iota(I32, (tm, tm), 1)
    strict = (cc < r).astype(BF16)
    rank_full = jnp.dot(strict, onehot.astype(BF16), preferred_element_type=F32) + carry_ref[...]
    idx_out = jnp.zeros((tm, LANES), I32)
    gate_out = jnp.zeros((tm, LANES), F32)
    rank_out = jnp.zeros((tm, LANES), F32)
    for kk in range(topk):
        rk = jnp.sum(jnp.where(lane == idxs[kk], rank_full, 0.0), axis=-1, keepdims=True)
        idx_out = jnp.where(lane == kk, idxs[kk], idx_out)
        gate_out = jnp.where(lane == kk, es[kk] / denom, gate_out)
        rank_out = jnp.where(lane == kk, rk, rank_out)
    idx_ref[...] = idx_out
    gate_ref[...] = gate_out
    rank_ref[...] = rank_out.astype(I32)
    carry_ref[...] = carry_ref[...] + jnp.sum(onehot, axis=0, keepdims=True)
    cnt_ref[...] = carry_ref[...]


def _xattn(c, x1, gx, w_xq, kv, w_xo, gf, w_r, b_r, qscale):
    n, d = x1.shape
    tm = c.tm_x
    xw = c.XH * LANES
    per_b = c.S // tm
    kern = functools.partial(_xattn_kernel, heads=c.XH, n_exp=c.E, topk=c.K, qscale=qscale)
    return pl.pallas_call(
        kern,
        grid=(n // tm,),
        in_specs=[
            pl.BlockSpec((tm, d), lambda i: (i, 0)),
            _const_spec((1, d)),
            _const_spec((d, xw)),
            pl.BlockSpec((c.MEM, 2 * xw), lambda i: (i // per_b, 0)),
            _const_spec((xw, d)),
            _const_spec((1, d)),
            _const_spec((d, 2 * LANES)),
            _const_spec((1, LANES)),
        ],
        out_specs=[
            pl.BlockSpec((tm, d), lambda i: (i, 0)),
            pl.BlockSpec((tm, d // 2), lambda i: (i, 0)),
            pl.BlockSpec((tm, LANES), lambda i: (i, 0)),
            pl.BlockSpec((tm, LANES), lambda i: (i, 0)),
            pl.BlockSpec((tm, LANES), lambda i: (i, 0)),
            pl.BlockSpec((1, LANES), lambda i: (0, 0)),
        ],
        out_shape=[
            jax.ShapeDtypeStruct((n, d), F32),
            jax.ShapeDtypeStruct((n, d // 2), U32),
            jax.ShapeDtypeStruct((n, LANES), I32),
            jax.ShapeDtypeStruct((n, LANES), F32),
            jax.ShapeDtypeStruct((n, LANES), I32),
            jax.ShapeDtypeStruct((1, LANES), F32),
        ],
        scratch_shapes=[pltpu.VMEM((1, LANES), F32)],
        compiler_params=_cparams(("arbitrary",)),
        name="xattn_router",
    )(x1, gx, w_xq, kv, w_xo, gf, w_r, b_r)


def _dispatch_kernel(dest_ref, zrow_ref, hfp_ref, xs_ref, zbuf_ref, sem, zsem, *, n_exp, topk, unroll):
    i = pl.program_id(0)
    tm = hfp_ref.shape[0]
    ts = zbuf_ref.shape[0]

    def zero_copy(e):
        return pltpu.make_async_copy(zbuf_ref, xs_ref.at[pl.ds(pl.multiple_of(zrow_ref[e], ts), ts)], zsem)

    @pl.when(i == 0)
    def _():
        zbuf_ref[...] = jnp.zeros(zbuf_ref.shape, zbuf_ref.dtype)
        for e in range(n_exp):
            @pl.when(zrow_ref[e] >= 0)
            def _():
                zero_copy(e).start()
        for e in range(n_exp):
            @pl.when(zrow_ref[e] >= 0)
            def _():
                zero_copy(e).wait()

    def row_copy(t, a):
        return pltpu.make_async_copy(hfp_ref.at[pl.ds(t, 1)], xs_ref.at[pl.ds(dest_ref[a], 1)], sem)

    def issue(tb, carry):
        for u in range(unroll):
            t = tb * unroll + u
            for kk in range(topk):
                row_copy(t, t * topk + kk).start(priority=kk % 2)
        return carry

    lax.fori_loop(0, tm // unroll, issue, 0)

    def drain(tb, carry):
        for u in range(unroll * topk):
            row_copy(0, 0).wait()
        return carry

    lax.fori_loop(0, tm // unroll, drain, 0)


def _dispatch(c, dest_flat, zrow, hfp, p_rows):
    n, dh = hfp.shape
    tm = c.tm_c
    kern = functools.partial(_dispatch_kernel, n_exp=c.E, topk=c.K, unroll=8)
    return pl.pallas_call(
        kern,
        grid=(n // tm,),
        in_specs=[
            pl.BlockSpec((tm * c.K,), lambda i: (i,), memory_space=pltpu.SMEM),
            pl.BlockSpec(memory_space=pltpu.SMEM),
            pl.BlockSpec((tm, dh), lambda i: (i, 0)),
        ],
        out_specs=pl.BlockSpec(memory_space=pl.ANY),
        out_shape=jax.ShapeDtypeStruct((p_rows, dh), U32),
        scratch_shapes=[pltpu.VMEM((c.ts_e, dh), U32), pltpu.SemaphoreType.DMA(()), pltpu.SemaphoreType.DMA(())],
        compiler_params=_cparams(("arbitrary",)),
        name="dispatch",
    )(dest_flat, zrow, hfp)


def _expert_kernel(te_ref, ns_ref, x_ref, wg_ref, wu_ref, bg_ref, bu_ref, wd_ref, bd_ref, y_ref,
                   xb_ref, wgb_ref, wub_ref, wdb_ref, acc_ref, *, ts):
    t = pl.program_id(0)
    f = pl.program_id(1)
    nf = pl.num_programs(1)
    nsub = ns_ref[t]
    dh = x_ref.shape[1]

    @pl.when(nsub > 0)
    def _():
        @pl.when(f == 0)
        def _():
            acc_ref[...] = jnp.broadcast_to(bd_ref[0], acc_ref.shape)

            def unpack(s, carry):
                r = pl.multiple_of(s * ts, ts)
                p = x_ref[pl.ds(r, ts), :]
                xb_ref[pl.ds(r, ts), :dh] = _unpack_lo(p).astype(BF16)
                xb_ref[pl.ds(r, ts), dh:] = _unpack_hi(p).astype(BF16)
                return carry

            lax.fori_loop(0, nsub, unpack, 0)

        wgb_ref[...] = wg_ref[0].astype(BF16)
        wub_ref[...] = wu_ref[0].astype(BF16)
        wdb_ref[...] = wd_ref[0].astype(BF16)
        bg = bg_ref[0]
        bu = bu_ref[0]

        def sub(s):
            r = s * ts if isinstance(s, int) else pl.multiple_of(s * ts, ts)
            xs = xb_ref[pl.ds(r, ts), :]
            gate = jnp.dot(xs, wgb_ref[...], preferred_element_type=F32) + bg
            up = jnp.dot(xs, wub_ref[...], preferred_element_type=F32) + bu
            gate = jnp.minimum(gate, SWIGLU_LIMIT)
            up = jnp.clip(up, -SWIGLU_LIMIT, SWIGLU_LIMIT)
            act = (up + 1.0) * gate * jax.nn.sigmoid(SWIGLU_ALPHA * gate)
            acc_ref[pl.ds(r, ts), :] += jnp.dot(act.astype(BF16), wdb_ref[...], preferred_element_type=F32)

        sub(0)

        def pair(i, carry):
            sub(2 * i + 1)
            sub(2 * i + 2)
            return carry

        lax.fori_loop(0, (nsub - 1) // 2, pair, 0)

        @pl.when((nsub - 1) % 2 == 1)
        def _():
            sub(nsub - 1)

        @pl.when(f == nf - 1)
        def _():
            y_ref[...] = _pack_bf16_pair(acc_ref[:, :dh], acc_ref[:, dh:])


def _experts(c, tile_expert, tile_nsub, tile_blk, xs, w_gu, b_gu3, w_d, b_d3):
    p_rows, dh = xs.shape
    d = 2 * dh
    tm, tf = c.tm_e, c.tf_e
    nf = c.DFF // tf
    n_tiles = p_rows // tm

    def fblk(t, f, ns):
        return jnp.where(ns[t] > 0, f, nf - 1)

    grid_spec = pltpu.PrefetchScalarGridSpec(
        num_scalar_prefetch=3,
        grid=(n_tiles, nf),
        in_specs=[
            pl.BlockSpec((tm, dh), lambda t, f, te, ns, tb: (tb[t], 0)),
            pl.BlockSpec((1, d, tf), lambda t, f, te, ns, tb: (te[t], 0, fblk(t, f, ns))),
            pl.BlockSpec((1, d, tf), lambda t, f, te, ns, tb: (te[t], 0, nf + fblk(t, f, ns))),
            pl.BlockSpec((1, 1, tf), lambda t, f, te, ns, tb: (te[t], 0, fblk(t, f, ns))),
            pl.BlockSpec((1, 1, tf), lambda t, f, te, ns, tb: (te[t], 0, nf + fblk(t, f, ns))),
            pl.BlockSpec((1, tf, d), lambda t, f, te, ns, tb: (te[t], fblk(t, f, ns), 0)),
            pl.BlockSpec((1, 1, d), lambda t, f, te, ns, tb: (te[t], 0, 0)),
        ],
        out_specs=pl.BlockSpec((tm, dh), lambda t, f, te, ns, tb: (tb[t], 0)),
        scratch_shapes=[
            pltpu.VMEM((tm, d), BF16),
            pltpu.VMEM((d, tf), BF16),
            pltpu.VMEM((d, tf), BF16),
            pltpu.VMEM((tf, d), BF16),
            pltpu.VMEM((tm, d), F32),
        ],
    )

    def kern(te_ref, ns_ref, tb_ref, *rest):
        _expert_kernel(te_ref, ns_ref, *rest, ts=c.ts_e)

    return pl.pallas_call(
        kern,
        grid_spec=grid_spec,
        out_shape=jax.ShapeDtypeStruct((p_rows, dh), U32),
        compiler_params=_cparams(("arbitrary", "arbitrary")),
        name="experts",
    )(tile_expert, tile_nsub, tile_blk, xs, w_gu, w_gu, b_gu3, b_gu3, w_d, b_d3)


def _moe_kernel(te_ref, ns_ref, cpi_ref, xs_ref, wgu_ref, wd_ref, bgu_ref, bd_ref, ys_ref,
                wgub, wdb, stage, act, xin, ybuf, wsem, xsem, ysem, *, ts, tm, nck, fc):
    i = pl.program_id(0)
    n = pl.num_programs(0)
    nsub = ns_ref[i]
    e = te_ref[i]
    cpi = cpi_ref[i]
    d, f2 = wgub.shape
    f = f2 // 2
    dh = d // 2
    ckr = d // nck
    ckd = f // nck

    def gu_copy(ee, c, slot):
        return pltpu.make_async_copy(wgu_ref.at[ee, pl.ds(pl.multiple_of(c * ckr, ckr), ckr), :],
                                     stage.at[slot], wsem.at[slot])

    def d_copy(ee, c, slot):
        return pltpu.make_async_copy(wd_ref.at[ee, pl.ds(pl.multiple_of(c * ckd, ckd), ckd), :],
                                     stage.at[slot, pl.ds(0, ckd), pl.ds(0, d)], wsem.at[slot])

    def gu_step(ee, c):
        slot = c % 2
        gu_copy(ee, c, slot).wait()
        wgub[pl.ds(pl.multiple_of(c * ckr, ckr), ckr), :] = stage[slot].astype(BF16)

        @pl.when(c + 2 < nck)
        def _():
            gu_copy(ee, c + 2, slot).start()

    def d_step(ee, c):
        slot = c % 2
        d_copy(ee, c, slot).wait()
        wdb[pl.ds(pl.multiple_of(c * ckd, ckd), ckd), :] = stage[slot, :ckd, :d].astype(BF16)

        @pl.when(c + 2 < nck)
        def _():
            d_copy(ee, c + 2, slot).start()

    def steps_after(s, step_fn, ee):
        def one(jj, carry):
            c = s * cpi + jj

            @pl.when(c < nck)
            def _():
                step_fn(ee, c)
            return carry
        lax.fori_loop(0, cpi, one, 0)

    def x_copy(s, slot):
        row = pl.multiple_of(i * tm + s * ts, ts)
        return pltpu.make_async_copy(xs_ref.at[pl.ds(row, ts), :], xin.at[slot], xsem.at[slot])

    def y_copy(s, slot):
        row = pl.multiple_of(i * tm + s * ts, ts)
        return pltpu.make_async_copy(ybuf.at[slot], ys_ref.at[pl.ds(row, ts), :], ysem.at[slot])

    @pl.when(nsub > 0)
    def _():
        @pl.when(i == 0)
        def _():
            gu_copy(e, 0, 0).start()
            gu_copy(e, 1, 1).start()

            def first(c, carry):
                gu_step(e, c)
                return carry
            lax.fori_loop(0, nck, first, 0)

        d_copy(e, 0, 0).start()
        d_copy(e, 1, 1).start()
        x_copy(0, 0).start()
        bgu = bgu_ref[e]

        def phase_a(s, carry):
            slot = s % 2
            x_copy(s, slot).wait()

            @pl.when(s + 1 < nsub)
            def _():
                x_copy(s + 1, 1 - slot).start()

            p = xin[slot]
            xb = jnp.concatenate([_unpack_lo(p).astype(BF16), _unpack_hi(p).astype(BF16)], axis=-1)
            for cc in range(f // fc):
                gate = jnp.dot(xb, wgub[:, cc * fc:(cc + 1) * fc], preferred_element_type=F32)
                gate = gate + bgu[:, cc * fc:(cc + 1) * fc]
                up = jnp.dot(xb, wgub[:, f + cc * fc:f + (cc + 1) * fc], preferred_element_type=F32)
                up = up + bgu[:, f + cc * fc:f + (cc + 1) * fc]
                gate = jnp.minimum(gate, SWIGLU_LIMIT)
                up = jnp.clip(up, -SWIGLU_LIMIT, SWIGLU_LIMIT)
                a = (up + 1.0) * gate * jax.nn.sigmoid(SWIGLU_ALPHA * gate)
                act[s, :, cc * fc:(cc + 1) * fc] = a.astype(BF16)
            steps_after(s, d_step, e)
            return carry

        lax.fori_loop(0, nsub, phase_a, 0)

        nxt = jnp.minimum(i + 1, n - 1)
        e_next = te_ref[nxt]
        load_next = (i + 1 < n) & (ns_ref[nxt] > 0) & (e_next != e)

        @pl.when(load_next)
        def _():
            gu_copy(e_next, 0, 0).start()
            gu_copy(e_next, 1, 1).start()

        bd = bd_ref[e]

        def phase_b(s, carry):
            slot = s % 2

            @pl.when(s >= 2)
            def _():
                y_copy(0, slot).wait()

            a = act[s]
            for cc in range(dh // fc):
                lo = slice(cc * fc, (cc + 1) * fc)
                hi = slice(dh + cc * fc, dh + (cc + 1) * fc)
                y_lo = jnp.dot(a, wdb[:, lo], preferred_element_type=F32) + bd[:, lo]
                y_hi = jnp.dot(a, wdb[:, hi], preferred_element_type=F32) + bd[:, hi]
                ybuf[slot, :, lo] = _pack_bf16_pair(y_lo, y_hi)
            y_copy(s, slot).start()

            @pl.when(load_next)
            def _():
                steps_after(s, gu_step, e_next)
            return carry

        lax.fori_loop(0, nsub, phase_b, 0)

        y_copy(0, (nsub - 1) % 2).wait()

        @pl.when(nsub >= 2)
        def _():
            y_copy(0, nsub % 2).wait()


def _moe(c, tile_expert, tile_nsub, tile_cpi, xs, w_gu, b_gu3, w_d, b_d3):
    p_rows, dh = xs.shape
    d = 2 * dh
    f = c.DFF
    tm, ts, nck = c.tm_e, c.ts_e, c.nck_e
    n_tiles = p_rows // tm
    grid_spec = pltpu.PrefetchScalarGridSpec(
        num_scalar_prefetch=3,
        grid=(n_tiles,),
        in_specs=[
            pl.BlockSpec(memory_space=pl.ANY),
            pl.BlockSpec(memory_space=pl.ANY),
            pl.BlockSpec(memory_space=pl.ANY),
            pl.BlockSpec((c.E, 1, 2 * f), lambda i, *_: (0, 0, 0), pipeline_mode=pl.Buffered(1)),
            pl.BlockSpec((c.E, 1, d), lambda i, *_: (0, 0, 0), pipeline_mode=pl.Buffered(1)),
        ],
        out_specs=pl.BlockSpec(memory_space=pl.ANY),
        scratch_shapes=[
            pltpu.VMEM((d, 2 * f), BF16),
            pltpu.VMEM((f, d), BF16),
            pltpu.VMEM((2, d // nck, 2 * f), F32),
            pltpu.VMEM((tm // ts, ts, f), BF16),
            pltpu.VMEM((2, ts, dh), U32),
            pltpu.VMEM((2, ts, dh), U32),
            pltpu.SemaphoreType.DMA((2,)),
            pltpu.SemaphoreType.DMA((2,)),
            pltpu.SemaphoreType.DMA((2,)),
        ],
    )
    kern = functools.partial(_moe_kernel, ts=ts, tm=tm, nck=nck, fc=c.fc_e)
    return pl.pallas_call(
        kern,
        grid_spec=grid_spec,
        out_shape=jax.ShapeDtypeStruct((p_rows, dh), U32),
        compiler_params=_cparams(("arbitrary",)),
        name="experts",
    )(tile_expert, tile_nsub, tile_cpi, xs, w_gu, w_d, b_gu3, b_d3)


def _combine_kernel(dest_ref, x2_ref, gate_ref, g_ref, ys_ref, o_ref, buf_ref, sem, *, topk, unroll):
    tm, d = x2_ref.shape
    dh = d // 2

    def row_copy(t, kk, a):
        return pltpu.make_async_copy(ys_ref.at[pl.ds(dest_ref[a], 1)], buf_ref.at[kk, pl.ds(t, 1)], sem)

    def issue(tb, carry):
        for u in range(unroll):
            t = tb * unroll + u
            for kk in range(topk):
                row_copy(t, kk, t * topk + kk).start(priority=kk % 2)
        return carry

    lax.fori_loop(0, tm // unroll, issue, 0)

    def drain(tb, carry):
        for u in range(unroll * topk):
            row_copy(0, 0, 0).wait()
        return carry

    lax.fori_loop(0, tm // unroll, drain, 0)

    lo = x2_ref[:, :dh]
    hi = x2_ref[:, dh:]
    gates = gate_ref[...]
    for kk in range(topk):
        wk = gates[:, kk:kk + 1]
        p = buf_ref[kk]
        lo = lo + wk * _unpack_lo(p)
        hi = hi + wk * _unpack_hi(p)
    ms = (jnp.sum(lo * lo, axis=-1, keepdims=True) + jnp.sum(hi * hi, axis=-1, keepdims=True)) / d
    inv = lax.rsqrt(ms + RMS_EPS)
    o_ref[:, :dh] = lo * inv * g_ref[:, :dh]
    o_ref[:, dh:] = hi * inv * g_ref[:, dh:]


def _combine(c, dest_flat, x2, gates, g, ys):
    n, d = x2.shape
    tm = c.tm_c
    kern = functools.partial(_combine_kernel, topk=c.K, unroll=8)
    return pl.pallas_call(
        kern,
        grid=(n // tm,),
        in_specs=[
            pl.BlockSpec((tm * c.K,), lambda i: (i,), memory_space=pltpu.SMEM),
            pl.BlockSpec((tm, d), lambda i: (i, 0)),
            pl.BlockSpec((tm, LANES), lambda i: (i, 0)),
            pl.BlockSpec((1, d), lambda i: (0, 0)),
            pl.BlockSpec(memory_space=pl.ANY),
        ],
        out_specs=pl.BlockSpec((tm, d), lambda i: (i, 0)),
        out_shape=jax.ShapeDtypeStruct((n, d), F32),
        scratch_shapes=[pltpu.VMEM((c.K, tm, d // 2), U32), pltpu.SemaphoreType.DMA(())],
        compiler_params=_cparams(("arbitrary",)),
        name="combine",
    )(dest_flat, x2, gates, g, ys)


def _plan(c, idx, rank, counts):
    tm, ts = c.tm_e, c.ts_e
    counts = counts.astype(I32)
    subs_e = (counts + ts - 1) // ts
    tiles_per_e = (counts + tm - 1) // tm
    tile_end = jnp.cumsum(tiles_per_e)
    tile_start = tile_end - tiles_per_e
    row_start = tile_start * tm
    dest = row_start[idx] + rank
    n_tiles = _n_tiles(c)
    t = jnp.arange(n_tiles, dtype=I32)
    n_used = tile_end[-1]
    tc = jnp.minimum(t, n_used - 1)
    te = jnp.minimum(jnp.searchsorted(tile_end, tc, side="right"), c.E - 1).astype(I32)
    rows_left = counts[te] - (tc - tile_start[te]) * tm
    nsub = jnp.clip((rows_left + ts - 1) // ts, 0, tm // ts)
    nsub = jnp.where(t < n_used, nsub, 0).astype(I32)
    zrow = jnp.where(counts > 0, row_start + (subs_e - 1) * ts, -1).astype(I32)
    cpi = (c.nck_e + jnp.maximum(nsub, 1) - 1) // jnp.maximum(nsub, 1)
    return dest.reshape(-1).astype(I32), te, nsub, cpi.astype(I32), zrow


def _n_tiles(c):
    return -(-(c.B * c.S * c.K) // c.tm_e) + c.E


def _forward(c, x, mem, norm_mix_g, w_in, b_forget, sg_ln_g, sg_ln_b, w_spatial, b_spatial,
             w_branch_a, w_branch_b, w_out, norm_x_g, norm_mem_g, w_xq, w_xkv, w_xo,
             norm_ffn_g, w_router, b_router, w_gate_up, b_gate_up, w_down, b_down, norm_final_g):
    B, S, D = x.shape
    n = B * S
    fw = c.FH * LANES
    sw = c.SG * LANES
    x2d = x.reshape(n, D)

    o_f = 3 * fw
    o_z = o_f + c.FH
    o_g = o_z + 2 * sw
    w_main = jnp.concatenate([w_in[:, o_g:], w_in[:, o_z:o_g], w_in[:, :o_f]], axis=1).astype(BF16)
    w_f = jnp.pad(w_in[:, o_f:o_z], ((0, 0), (0, LANES - c.FH))).astype(BF16)
    b_f = jnp.pad(b_forget.astype(F32), (0, LANES - c.FH)).reshape(1, LANES)
    proj, logf = _in_proj(c, x2d, norm_mix_g.reshape(1, D), w_main, w_f, b_f, LOG2E * LANES ** -0.5)

    logf_bhs = logf[:, :c.FH].reshape(B, S, c.FH).transpose(0, 2, 1)
    csum = _cumsum(c, logf_bhs)
    crow = csum.reshape(B * c.FH, 1, S)
    ccol = csum.reshape(B * c.FH, S, 1)
    qcol0 = (2 * D + 2 * sw) // LANES
    attn = _fox(c, proj, crow, ccol, qcol0)

    b_s_full = jnp.broadcast_to(b_spatial.astype(F32)[:, :, None], (c.SG, c.SGC, LANES))
    x1 = _mix(c, x2d, attn, proj, sg_ln_g.reshape(1, sw), sg_ln_b.reshape(1, sw), w_spatial, b_s_full,
              w_branch_a.astype(BF16), w_branch_b.astype(BF16), w_out.astype(BF16))

    kv = _mem_kv(c, mem.reshape(B * c.MEM, D), norm_mem_g.reshape(1, D), w_xkv.astype(BF16))
    w_r32 = jnp.pad(w_router.astype(F32), ((0, 0), (0, LANES - c.E)))
    w_r_hi = w_r32.astype(BF16)
    w_r = jnp.concatenate([w_r_hi, (w_r32 - w_r_hi.astype(F32)).astype(BF16)], axis=1)
    b_r = jnp.pad(b_router.astype(F32), (0, LANES - c.E)).reshape(1, LANES)
    x2, hfp, idx, gates, rank, counts = _xattn(
        c, x1, norm_x_g.reshape(1, D), w_xq.astype(BF16), kv, w_xo.astype(BF16),
        norm_ffn_g.reshape(1, D), w_r, b_r, LANES ** -0.5)

    dest, te, nsub, cpi, zrow = _plan(c, idx[:, :c.K], rank[:, :c.K], counts[0, :c.E])
    p_rows = _n_tiles(c) * c.tm_e
    xs = _dispatch(c, dest, zrow, hfp, p_rows)
    ys = _moe(c, te, nsub, cpi, xs, w_gate_up, b_gate_up.reshape(c.E, 1, 2 * c.DFF),
              w_down, b_down.reshape(c.E, 1, D))
    out = _combine(c, dest, x2, gates, norm_final_g.reshape(1, D), ys)
    return out.reshape(B, S, D)


_CFG = Cfg(B=4, S=4096, D=2048, MEM=256, FH=8, SG=8, SGC=128, XH=4, E=32, K=4, DFF=2048,
           tm_in=1024, tn_in=1024, tq=512, tm_mix=256, tm_x=512, tm_e=2560, ts_e=256, fc_e=512, nck_e=16,
           tm_c=256, tc_cs=512, fox_parts=2)


@jax.jit
def kernel(x, mem, norm_mix_g, w_in, b_forget, sg_ln_g, sg_ln_b, w_spatial, b_spatial, w_branch_a, w_branch_b,
           w_out, norm_x_g, norm_mem_g, w_xq, w_xkv, w_xo, norm_ffn_g, w_router, b_router, w_gate_up,
           b_gate_up, w_down, b_down, norm_final_g):
    return _forward(_CFG, x, mem, norm_mix_g, w_in, b_forget, sg_ln_g, sg_ln_b, w_spatial, b_spatial,
                    w_branch_a, w_branch_b, w_out, norm_x_g, norm_mem_g, w_xq, w_xkv, w_xo,
                    norm_ffn_g, w_router, b_router, w_gate_up, b_gate_up, w_down, b_down, norm_final_g)
```

```python
import functools
from typing import NamedTuple

import jax
import jax.numpy as jnp
from jax import lax
from jax.experimental import pallas as pl
from jax.experimental.pallas import tpu as pltpu

F32 = jnp.float32
BF16 = jnp.bfloat16
U32 = jnp.uint32
I32 = jnp.int32

LANES = 128
VMEM_LIMIT = 56 * 1024 * 1024

RMS_EPS = 1e-6
LN_EPS = 1e-5
SWIGLU_LIMIT = 7.0
SWIGLU_ALPHA = 1.702
GELU_C = 0.7978845608028654
LOG2E = 1.4426950408889634
ID_CHUNK = 1024


class Cfg(NamedTuple):
    B: int
    S: int
    D: int
    MEM: int
    FH: int
    SG: int
    SGC: int
    XH: int
    E: int
    K: int
    DFF: int
    tm_in: int
    tn_in: int
    tq: int
    tm_mix: int
    tm_x: int
    tm_e: int
    ts_e: int
    fc_e: int
    nck_e: int
    tm_c: int
    tc_cs: int
    fox_parts: int


def _cparams(sem):
    return pltpu.CompilerParams(dimension_semantics=sem, vmem_limit_bytes=VMEM_LIMIT)


def _rms(x, g):
    ms = jnp.mean(x * x, axis=-1, keepdims=True)
    return x * lax.rsqrt(ms + RMS_EPS) * g


def _pack_bf16_pair(a, b):
    def rne(v):
        bits = lax.bitcast_convert_type(v, U32)
        return bits + jnp.uint32(0x7FFF) + ((bits >> 16) & jnp.uint32(1))
    return (rne(a) >> 16) | (rne(b) & jnp.uint32(0xFFFF0000))


def _unpack_lo(p):
    return lax.bitcast_convert_type(p << 16, F32)


def _unpack_hi(p):
    return lax.bitcast_convert_type(p & jnp.uint32(0xFFFF0000), F32)


def _in_proj_kernel(x_ref, g_ref, w_ref, wf_ref, bf_ref, o_ref, f_ref, h_ref, *, nj_gate, nj_z, nj_q, qscale):
    j = pl.program_id(1)

    @pl.when(j == 0)
    def _():
        hb = _rms(x_ref[...], g_ref[...]).astype(BF16)
        h_ref[...] = hb
        f = jnp.dot(hb, wf_ref[...], preferred_element_type=F32) + bf_ref[...]
        f_ref[...] = jnp.minimum(f, 0.0) - jnp.log1p(jnp.exp(-jnp.abs(f)))

    acc = jnp.dot(h_ref[...], w_ref[...], preferred_element_type=F32)

    is_gate = j < nj_gate
    is_z = (j >= nj_gate) & (j < nj_gate + nj_z)
    is_q = (j >= nj_gate + nj_z) & (j < nj_gate + nj_z + nj_q)
    a1 = jnp.where(is_gate, 0.5, jnp.where(is_z, GELU_C, 0.0)).astype(F32)
    a3 = jnp.where(is_z, GELU_C * 0.044715, 0.0).astype(F32)
    b0 = jnp.where(is_gate, 0.5, 0.0).astype(F32)
    b1 = jnp.where(is_gate, 0.0, jnp.where(is_z, 0.5, jnp.where(is_q, qscale, 1.0))).astype(F32)
    th = jnp.tanh(acc * (a1 + a3 * (acc * acc)))
    o_ref[...] = ((b0 + b1 * acc) * (1.0 + th)).astype(o_ref.dtype)


def _in_proj(c, x2d, g, w_main, w_f, b_f, qscale):
    n, d = x2d.shape
    nc = w_main.shape[1]
    tm, tn = c.tm_in, c.tn_in
    fw = c.FH * LANES
    kern = functools.partial(_in_proj_kernel, nj_gate=2 * d // tn, nj_z=2 * c.SG * LANES // tn,
                             nj_q=fw // tn, qscale=qscale)
    return pl.pallas_call(
        kern,
        grid=(n // tm, nc // tn),
        in_specs=[
            pl.BlockSpec((tm, d), lambda i, j: (i, 0)),
            pl.BlockSpec((1, d), lambda i, j: (0, 0)),
            pl.BlockSpec((d, tn), lambda i, j: (0, j)),
            pl.BlockSpec((d, LANES), lambda i, j: (0, 0)),
            pl.BlockSpec((1, LANES), lambda i, j: (0, 0)),
        ],
        out_specs=[
            pl.BlockSpec((tm, tn), lambda i, j: (i, j)),
            pl.BlockSpec((tm, LANES), lambda i, j: (i, 0)),
        ],
        out_shape=[jax.ShapeDtypeStruct((n, nc), BF16), jax.ShapeDtypeStruct((n, LANES), F32)],
        scratch_shapes=[pltpu.VMEM((tm, d), BF16)],
        compiler_params=_cparams(("parallel", "arbitrary")),
        name="in_proj",
    )(x2d, g, w_main, w_f, b_f)


def _cumsum_kernel(f_ref, o_ref, *, tc):
    rows, s = f_ref.shape[1], f_ref.shape[2]
    r = lax.broadcasted_iota(I32, (tc, tc), 0)
    col = lax.broadcasted_iota(I32, (tc, tc), 1)
    upper = (r <= col).astype(F32)
    carry = jnp.zeros((rows, 1), F32)
    for i in range(s // tc):
        blk = f_ref[0, :, i * tc:(i + 1) * tc]
        cs = jnp.dot(blk, upper, preferred_element_type=F32, precision=lax.Precision.HIGHEST) + carry
        o_ref[0, :, i * tc:(i + 1) * tc] = cs * LOG2E
        carry = cs[:, tc - 1:tc]


def _cumsum(c, logf_bhs):
    b, h, s = logf_bhs.shape
    return pl.pallas_call(
        functools.partial(_cumsum_kernel, tc=c.tc_cs),
        grid=(b,),
        in_specs=[pl.BlockSpec((1, h, s), lambda i: (i, 0, 0))],
        out_specs=pl.BlockSpec((1, h, s), lambda i: (i, 0, 0)),
        out_shape=jax.ShapeDtypeStruct((b, h, s), F32),
        compiler_params=_cparams(("parallel",)),
        name="cumsum",
    )(logf_bhs)


def _fox_kernel(q_ref, k_ref, v_ref, crow_ref, ccol_ref, o_ref, m_ref, l_ref, acc_ref, cq_ref, s_ref,
                *, t, parts):
    qi = pl.program_id(2)
    hr = t // parts
    m_ref[...] = jnp.full(m_ref.shape, -jnp.inf, F32)
    l_ref[...] = jnp.zeros(l_ref.shape, F32)
    acc_ref[...] = jnp.zeros(acc_ref.shape, F32)
    cq_ref[...] = jnp.broadcast_to(ccol_ref[0], cq_ref.shape)

    def scores(ks):
        k = k_ref[pl.ds(ks, t), :]
        return lax.dot_general(q_ref[...], k, (((1,), (1,)), ((), ())), preferred_element_type=F32)

    def block(part, ks, width, masked):
        rows = slice(part * hr, (part + 1) * hr)
        v = v_ref[pl.ds(ks, width), :]
        s = s_ref[rows, :width]
        cq = cq_ref[rows, :]
        crow = crow_ref[0, :, pl.ds(ks, width)]
        nj = width // LANES
        sj = [s[:, j * LANES:(j + 1) * LANES] + cq - crow[:, j * LANES:(j + 1) * LANES] for j in range(nj)]
        if masked:
            row = lax.broadcasted_iota(I32, (hr, LANES), 0) + part * hr
            col = lax.broadcasted_iota(I32, (hr, LANES), 1)
            sj = [jnp.where(col + j * LANES <= row, sj[j], -jnp.inf) for j in range(nj)]
        mx = sj[0]
        for j in range(1, nj):
            mx = jnp.maximum(mx, sj[j])
        m_prev = m_ref[rows, :]
        m_next = jnp.maximum(m_prev, jnp.max(mx, axis=-1, keepdims=True))
        alpha = jnp.exp2(m_prev - m_next)
        pj = [jnp.exp2(sj[j] - m_next) for j in range(nj)]
        psum = pj[0]
        for j in range(1, nj):
            psum = psum + pj[j]
        p = jnp.concatenate([x.astype(BF16) for x in pj], axis=-1)
        l_ref[rows, :] = alpha * l_ref[rows, :] + psum
        acc_ref[rows, :] = alpha * acc_ref[rows, :] + jnp.dot(p, v, preferred_element_type=F32)
        m_ref[rows, :] = m_next

    s_ref[...] = scores(0)

    def body(kc, carry):
        ks = pl.multiple_of(kc * t, t)
        s_next = scores(pl.multiple_of(ks + t, t))
        for part in range(parts):
            block(part, ks, t, False)
        s_ref[...] = s_next
        return carry

    lax.fori_loop(0, qi, body, 0)
    kd = pl.multiple_of(qi * t, t)
    for part in range(parts):
        block(part, kd, (part + 1) * hr, True)
    l = jnp.sum(l_ref[...], axis=-1, keepdims=True)
    o_ref[...] = (acc_ref[...] / l).astype(o_ref.dtype)


def _fox(c, proj, crow, ccol, qcol0):
    n = c.B * c.S
    t = c.tq
    nq = c.S // t
    h = c.FH
    return pl.pallas_call(
        functools.partial(_fox_kernel, t=t, parts=c.fox_parts),
        grid=(c.B, h, nq),
        in_specs=[
            pl.BlockSpec((t, LANES), lambda b, hh, qi: (b * nq + qi, qcol0 + hh)),
            pl.BlockSpec((c.S, LANES), lambda b, hh, qi: (b, qcol0 + h + hh)),
            pl.BlockSpec((c.S, LANES), lambda b, hh, qi: (b, qcol0 + 2 * h + hh)),
            pl.BlockSpec((1, 1, c.S), lambda b, hh, qi: (b * h + hh, 0, 0)),
            pl.BlockSpec((1, t, 1), lambda b, hh, qi: (b * h + hh, qi, 0)),
        ],
        out_specs=pl.BlockSpec((t, LANES), lambda b, hh, qi: (b * nq + qi, hh)),
        out_shape=jax.ShapeDtypeStruct((n, h * LANES), BF16),
        scratch_shapes=[pltpu.VMEM((t, LANES), F32)] * 4 + [pltpu.VMEM((t, t), F32)],
        compiler_params=_cparams(("parallel", "parallel", "arbitrary")),
        name="fox",
    )(proj, proj, proj, crow, ccol)


def _mix_kernel(x_ref, a_ref, u_ref, v_ref, ga_ref, gb_ref, lng_ref, lnb_ref, ws_ref, bs_ref,
                wa_ref, wb_ref, wo_ref, o_ref, sg_ref, *, sgc, groups):
    tm = x_ref.shape[0]
    v = v_ref[...].astype(F32)
    mu = jnp.mean(v, axis=-1, keepdims=True)
    vc = v - mu
    var = jnp.mean(vc * vc, axis=-1, keepdims=True)
    vn = (vc * lax.rsqrt(var + LN_EPS) * lng_ref[...] + lnb_ref[...]).astype(BF16)
    row = lax.broadcasted_iota(I32, (sgc, sgc), 0)
    col = lax.broadcasted_iota(I32, (sgc, sgc), 1)
    for g in range(groups):
        w = jnp.where(col <= row, ws_ref[g], 0.0).astype(BF16)
        bias = bs_ref[g]
        for ci in range(tm // sgc):
            rs = slice(ci * sgc, (ci + 1) * sgc)
            cs = slice(g * LANES, (g + 1) * LANES)
            mixed = jnp.dot(w, vn[rs, cs], preferred_element_type=F32) + bias
            sg_ref[rs, cs] = (u_ref[rs, cs].astype(F32) * mixed).astype(BF16)
    ya = jnp.dot(a_ref[...], wa_ref[...], preferred_element_type=F32)
    yb = jnp.dot(sg_ref[...], wb_ref[...], preferred_element_type=F32)
    merged = (ga_ref[...].astype(F32) * ya + gb_ref[...].astype(F32) * yb).astype(BF16)
    o_ref[...] = x_ref[...] + jnp.dot(merged, wo_ref[...], preferred_element_type=F32)


def _const_spec(shape):
    nd = len(shape)
    return pl.BlockSpec(shape, lambda i: (0,) * nd, pipeline_mode=pl.Buffered(1))


def _mix(c, x2d, attn, proj, ln_g, ln_b, w_s, b_s_full, w_a, w_b, w_o):
    n, d = x2d.shape
    tm = c.tm_mix
    fw = c.FH * LANES
    sw = c.SG * LANES
    ucol = 2 * d // sw
    return pl.pallas_call(
        functools.partial(_mix_kernel, sgc=c.SGC, groups=c.SG),
        grid=(n // tm,),
        in_specs=[
            pl.BlockSpec((tm, d), lambda i: (i, 0)),
            pl.BlockSpec((tm, fw), lambda i: (i, 0)),
            pl.BlockSpec((tm, sw), lambda i: (i, ucol)),
            pl.BlockSpec((tm, sw), lambda i: (i, ucol + 1)),
            pl.BlockSpec((tm, d), lambda i: (i, 0)),
            pl.BlockSpec((tm, d), lambda i: (i, 1)),
            _const_spec((1, sw)),
            _const_spec((1, sw)),
            _const_spec((c.SG, c.SGC, c.SGC)),
            _const_spec((c.SG, c.SGC, LANES)),
            _const_spec((fw, d)),
            _const_spec((sw, d)),
            _const_spec((d, d)),
        ],
        out_specs=pl.BlockSpec((tm, d), lambda i: (i, 0)),
        out_shape=jax.ShapeDtypeStruct((n, d), F32),
        scratch_shapes=[pltpu.VMEM((tm, sw), BF16)],
        compiler_params=_cparams(("parallel",)),
        name="mix",
    )(x2d, attn, proj, proj, proj, proj, ln_g, ln_b, w_s, b_s_full, w_a, w_b, w_o)


def _mem_kv_kernel(m_ref, g_ref, w_ref, o_ref):
    hm = _rms(m_ref[...], g_ref[...]).astype(BF16)
    o_ref[...] = jnp.dot(hm, w_ref[...], preferred_element_type=F32).astype(o_ref.dtype)


def _mem_kv(c, mem2d, g, w_xkv):
    n, d = mem2d.shape
    nc = w_xkv.shape[1]
    tm = c.MEM
    return pl.pallas_call(
        _mem_kv_kernel,
        grid=(n // tm,),
        in_specs=[pl.BlockSpec((tm, d), lambda i: (i, 0)), _const_spec((1, d)), _const_spec((d, nc))],
        out_specs=pl.BlockSpec((tm, nc), lambda i: (i, 0)),
        out_shape=jax.ShapeDtypeStruct((n, nc), BF16),
        compiler_params=_cparams(("parallel",)),
        name="mem_kv",
    )(mem2d, g, w_xkv)


def _xattn_kernel(x1_ref, gx_ref, wq_ref, kv_ref, wo_ref, gf_ref, wr_ref, br_ref,
                  x2_ref, hfp_ref, idx_ref, gate_ref, cnt_ref, carry_ref,
                  *, heads, n_exp, topk, qscale):
    i = pl.program_id(0)
    tm, d = x1_ref.shape
    xw = heads * LANES
    x1 = x1_ref[...]
    hx = _rms(x1, gx_ref[...]).astype(BF16)
    q = (jnp.dot(hx, wq_ref[...], preferred_element_type=F32) * qscale).astype(BF16)
    outs = []
    for h in range(heads):
        k = kv_ref[:, h * LANES:(h + 1) * LANES]
        v = kv_ref[:, xw + h * LANES:xw + (h + 1) * LANES]
        s = lax.dot_general(q[:, h * LANES:(h + 1) * LANES], k, (((1,), (1,)), ((), ())),
                            preferred_element_type=F32)
        s = s - jnp.max(s, axis=-1, keepdims=True)
        p = jnp.exp(s)
        p = p / jnp.sum(p, axis=-1, keepdims=True)
        outs.append(jnp.dot(p.astype(BF16), v, preferred_element_type=F32).astype(BF16))
    o = jnp.concatenate(outs, axis=-1)
    x2 = x1 + jnp.dot(o, wo_ref[...], preferred_element_type=F32)
    x2_ref[...] = x2

    hf = _rms(x2, gf_ref[...])
    hfp_ref[...] = _pack_bf16_pair(hf[:, :d // 2], hf[:, d // 2:])

    h_hi = hf.astype(BF16)
    h_lo = (hf - h_hi.astype(F32)).astype(BF16)
    l_hi = jnp.dot(h_hi, wr_ref[...], preferred_element_type=F32)
    l_lo = jnp.dot(h_lo, wr_ref[:, :LANES], preferred_element_type=F32)
    logits = l_hi[:, :LANES] + l_hi[:, LANES:] + l_lo + br_ref[...]
    lane = lax.broadcasted_iota(I32, (tm, LANES), 1)
    lg = jnp.where(lane < n_exp, logits, -jnp.inf)
    vals, idxs = [], []
    for _ in range(topk):
        m = jnp.max(lg, axis=-1, keepdims=True)
        ix = jnp.min(jnp.where(lg == m, lane, LANES), axis=-1, keepdims=True)
        vals.append(m)
        idxs.append(ix)
        lg = jnp.where(lane == ix, -jnp.inf, lg)
    es = [jnp.exp(vv - vals[0]) for vv in vals]
    denom = es[0]
    for e in es[1:]:
        denom = denom + e

    @pl.when(i == 0)
    def _():
        carry_ref[...] = jnp.zeros(carry_ref.shape, F32)

    onehot = jnp.zeros((tm, LANES), F32)
    for ix in idxs:
        onehot = onehot + (lane == ix).astype(F32)
    idx_out = jnp.zeros((tm, LANES), I32)
    gate_out = jnp.zeros((tm, LANES), F32)
    for kk in range(topk):
        idx_out = jnp.where(lane == kk, idxs[kk], idx_out)
        gate_out = jnp.where(lane == kk, es[kk] / denom, gate_out)
    idx_ref[...] = idx_out
    gate_ref[...] = gate_out
    carry_ref[...] = carry_ref[...] + jnp.sum(onehot, axis=0, keepdims=True)
    cnt_ref[...] = carry_ref[...]


def _xattn(c, x1, gx, w_xq, kv, w_xo, gf, w_r, b_r, qscale):
    n, d = x1.shape
    tm = c.tm_x
    xw = c.XH * LANES
    per_b = c.S // tm
    kern = functools.partial(_xattn_kernel, heads=c.XH, n_exp=c.E, topk=c.K, qscale=qscale)
    return pl.pallas_call(
        kern,
        grid=(n // tm,),
        in_specs=[
            pl.BlockSpec((tm, d), lambda i: (i, 0)),
            _const_spec((1, d)),
            _const_spec((d, xw)),
            pl.BlockSpec((c.MEM, 2 * xw), lambda i: (i // per_b, 0)),
            _const_spec((xw, d)),
            _const_spec((1, d)),
            _const_spec((d, 2 * LANES)),
            _const_spec((1, LANES)),
        ],
        out_specs=[
            pl.BlockSpec((tm, d), lambda i: (i, 0)),
            pl.BlockSpec((tm, d // 2), lambda i: (i, 0)),
            pl.BlockSpec((tm, LANES), lambda i: (i, 0)),
            pl.BlockSpec((tm, LANES), lambda i: (i, 0)),
            pl.BlockSpec((1, LANES), lambda i: (0, 0)),
        ],
        out_shape=[
            jax.ShapeDtypeStruct((n, d), F32),
            jax.ShapeDtypeStruct((n, d // 2), U32),
            jax.ShapeDtypeStruct((n, LANES), I32),
            jax.ShapeDtypeStruct((n, LANES), F32),
            jax.ShapeDtypeStruct((1, LANES), F32),
        ],
        scratch_shapes=[pltpu.VMEM((1, LANES), F32)],
        compiler_params=_cparams(("arbitrary",)),
        name="xattn_router",
    )(x1, gx, w_xq, kv, w_xo, gf, w_r, b_r)


def _moe_kernel(te_ref, ns_ref, cpi_ref, pos_ref, rows_ref, ord_ref, hfp_ref, wgu_ref, wd_ref, bgu_ref, bd_ref,
                ykt_ref, wgub, wdb, stage, act, xin, ybuf, ids, wsem, xsem, ysem, isem,
                *, ts, tm, nck, fc, n_tok, n_asg):
    i = pl.program_id(0)
    n = pl.num_programs(0)
    nsub = ns_ref[i]
    e = te_ref[i]
    cpi = cpi_ref[i]
    rows = rows_ref[i]
    d, f2 = wgub.shape
    f = f2 // 2
    dh = d // 2
    ckr = d // nck
    ckd = f // nck
    idw = ids.shape[0] // 2
    nq = idw // ID_CHUNK
    id_base = (i % 2) * idw + (pos_ref[i] & (ID_CHUNK - 1))

    def ids_copy(item, q, slot):
        c0 = lax.shift_right_logical(pos_ref[item], ID_CHUNK.bit_length() - 1)
        src = ord_ref.at[pl.ds(pl.multiple_of((c0 + q) * ID_CHUNK, ID_CHUNK), ID_CHUNK)]
        dst = ids.at[pl.ds(pl.multiple_of(slot * idw + q * ID_CHUNK, ID_CHUNK), ID_CHUNK)]
        return pltpu.make_async_copy(src, dst, isem.at[slot])

    def token_of(a):
        return a & (n_tok - 1) if n_tok & (n_tok - 1) == 0 else lax.rem(a, n_tok)

    def gather(s, slot, r0=0, r1=ts):
        for r in range(r0, r1):
            tok = token_of(ids[id_base + s * ts + r])
            pltpu.make_async_copy(hfp_ref.at[pl.ds(tok, 1)], xin.at[slot, pl.ds(r, 1)], xsem.at[slot]).start()

    def gather_wait(slot):
        for r in range(ts):
            pltpu.make_async_copy(hfp_ref.at[pl.ds(0, 1)], xin.at[slot, pl.ds(0, 1)], xsem.at[slot]).wait()

    def scatter(s, slot, r0=0, r1=ts):
        for r in range(r0, r1):
            g = s * ts + r
            dst = jnp.where(g < rows, ids[id_base + g], n_asg + slot * ts + r)
            pltpu.make_async_copy(ybuf.at[slot, pl.ds(r, 1)], ykt_ref.at[pl.ds(dst, 1)], ysem.at[slot]).start()

    def scatter_wait(slot):
        for r in range(ts):
            pltpu.make_async_copy(ybuf.at[slot, pl.ds(0, 1)], ykt_ref.at[pl.ds(0, 1)], ysem.at[slot]).wait()

    def gu_copy(ee, c, slot):
        return pltpu.make_async_copy(wgu_ref.at[ee, pl.ds(pl.multiple_of(c * ckr, ckr), ckr), :],
                                     stage.at[slot], wsem.at[slot])

    def d_copy(ee, c, slot):
        return pltpu.make_async_copy(wd_ref.at[ee, pl.ds(pl.multiple_of(c * ckd, ckd), ckd), :],
                                     stage.at[slot, pl.ds(0, ckd), pl.ds(0, d)], wsem.at[slot])

    def gu_step(ee, c):
        slot = c % 2
        gu_copy(ee, c, slot).wait()
        wgub[pl.ds(pl.multiple_of(c * ckr, ckr), ckr), :] = stage[slot].astype(BF16)

        @pl.when(c + 2 < nck)
        def _():
            gu_copy(ee, c + 2, slot).start()

    def d_step(ee, c):
        slot = c % 2
        d_copy(ee, c, slot).wait()
        wdb[pl.ds(pl.multiple_of(c * ckd, ckd), ckd), :] = stage[slot, :ckd, :d].astype(BF16)

        @pl.when(c + 2 < nck)
        def _():
            d_copy(ee, c + 2, slot).start()

    def steps_after(s, step_fn, ee):
        def one(jj, carry):
            c = s * cpi + jj

            @pl.when(c < nck)
            def _():
                step_fn(ee, c)
            return carry
        lax.fori_loop(0, cpi, one, 0)

    @pl.when(nsub > 0)
    def _():
        nxt = jnp.minimum(i + 1, n - 1)
        has_next = (i + 1 < n) & (ns_ref[nxt] > 0)

        @pl.when(i == 0)
        def _():
            for q in range(nq):
                ids_copy(0, q, 0).start()
            gu_copy(e, 0, 0).start()
            gu_copy(e, 1, 1).start()

            def first(c, carry):
                gu_step(e, c)
                return carry
            lax.fori_loop(0, nck, first, 0)

        for q in range(nq):
            ids_copy(i, q, i % 2).wait()

        @pl.when(has_next)
        def _():
            for q in range(nq):
                ids_copy(nxt, q, nxt % 2).start()

        d_copy(e, 0, 0).start()
        d_copy(e, 1, 1).start()
        gather(0, 0)
        bgu = bgu_ref[e]

        def phase_a(s, carry):
            slot = s % 2
            gather_wait(slot)
            p = xin[slot]
            xb = jnp.concatenate([_unpack_lo(p).astype(BF16), _unpack_hi(p).astype(BF16)], axis=-1)
            nc = f // fc
            for cc in range(nc):
                gather(s + 1, 1 - slot, cc * ts // nc, (cc + 1) * ts // nc)
                gate = jnp.dot(xb, wgub[:, cc * fc:(cc + 1) * fc], preferred_element_type=F32)
                gate = gate + bgu[:, cc * fc:(cc + 1) * fc]
                up = jnp.dot(xb, wgub[:, f + cc * fc:f + (cc + 1) * fc], preferred_element_type=F32)
                up = up + bgu[:, f + cc * fc:f + (cc + 1) * fc]
                gate = jnp.minimum(gate, SWIGLU_LIMIT)
                up = jnp.clip(up, -SWIGLU_LIMIT, SWIGLU_LIMIT)
                a = (up + 1.0) * gate * jax.nn.sigmoid(SWIGLU_ALPHA * gate)
                act[s, :, cc * fc:(cc + 1) * fc] = a.astype(BF16)
            steps_after(s, d_step, e)
            return carry

        lax.fori_loop(0, nsub, phase_a, 0)
        gather_wait(nsub % 2)

        e_next = te_ref[nxt]
        load_next = has_next & (e_next != e)

        @pl.when(load_next)
        def _():
            gu_copy(e_next, 0, 0).start()
            gu_copy(e_next, 1, 1).start()

        bd = bd_ref[e]

        def down(s, slot, scatter_prev):
            a = act[s]
            if scatter_prev:
                scatter(s - 1, 1 - slot)
            for cc in range(dh // fc):
                lo = slice(cc * fc, (cc + 1) * fc)
                hi = slice(dh + cc * fc, dh + (cc + 1) * fc)
                y_lo = jnp.dot(a, wdb[:, lo], preferred_element_type=F32) + bd[:, lo]
                y_hi = jnp.dot(a, wdb[:, hi], preferred_element_type=F32) + bd[:, hi]
                ybuf[slot, :, lo] = _pack_bf16_pair(y_lo, y_hi)

        def next_weights(s):
            @pl.when(load_next)
            def _():
                steps_after(s, gu_step, e_next)

        down(0, 0, False)
        next_weights(0)

        def phase_b(s, carry):
            slot = s % 2

            @pl.when(s >= 2)
            def _():
                scatter_wait(slot)

            down(s, slot, True)
            next_weights(s)
            return carry

        lax.fori_loop(1, nsub, phase_b, 0)
        scatter(nsub - 1, (nsub - 1) % 2)
        scatter_wait((nsub - 1) % 2)

        @pl.when(nsub >= 2)
        def _():
            scatter_wait(nsub % 2)


def _id_window(c):
    return ((ID_CHUNK - 1 + c.tm_e + c.ts_e) // ID_CHUNK + 1) * ID_CHUNK


def _moe(c, sched, order_pad, hfp, w_gu, b_gu3, w_d, b_d3):
    n_tok, dh = hfp.shape
    d = 2 * dh
    f = c.DFF
    tm, ts, nck = c.tm_e, c.ts_e, c.nck_e
    n_asg = n_tok * c.K
    grid_spec = pltpu.PrefetchScalarGridSpec(
        num_scalar_prefetch=5,
        grid=(_n_tiles(c),),
        in_specs=[
            pl.BlockSpec(memory_space=pl.ANY),
            pl.BlockSpec(memory_space=pl.ANY),
            pl.BlockSpec(memory_space=pl.ANY),
            pl.BlockSpec(memory_space=pl.ANY),
            pl.BlockSpec((c.E, 1, 2 * f), lambda i, *_: (0, 0, 0), pipeline_mode=pl.Buffered(1)),
            pl.BlockSpec((c.E, 1, d), lambda i, *_: (0, 0, 0), pipeline_mode=pl.Buffered(1)),
        ],
        out_specs=pl.BlockSpec(memory_space=pl.ANY),
        scratch_shapes=[
            pltpu.VMEM((d, 2 * f), BF16),
            pltpu.VMEM((f, d), BF16),
            pltpu.VMEM((2, d // nck, 2 * f), F32),
            pltpu.VMEM((tm // ts, ts, f), BF16),
            pltpu.VMEM((2, ts, dh), U32),
            pltpu.VMEM((2, ts, dh), U32),
            pltpu.SMEM((2 * _id_window(c),), I32),
            pltpu.SemaphoreType.DMA((2,)),
            pltpu.SemaphoreType.DMA((2,)),
            pltpu.SemaphoreType.DMA((2,)),
            pltpu.SemaphoreType.DMA((2,)),
        ],
    )
    kern = functools.partial(_moe_kernel, ts=ts, tm=tm, nck=nck, fc=c.fc_e, n_tok=n_tok, n_asg=n_asg)
    return pl.pallas_call(
        kern,
        grid_spec=grid_spec,
        out_shape=jax.ShapeDtypeStruct((n_asg + 2 * ts, dh), U32),
        compiler_params=_cparams(("arbitrary",)),
        name="experts",
    )(*sched, order_pad, hfp, w_gu, w_d, b_gu3, b_d3)


def _combine_kernel(x2_ref, gate_ref, g_ref, *refs, topk):
    y_refs, o_ref = refs[:topk], refs[topk]
    d = x2_ref.shape[1]
    dh = d // 2
    lo = x2_ref[:, :dh]
    hi = x2_ref[:, dh:]
    gates = gate_ref[...]
    for kk in range(topk):
        wk = gates[:, kk:kk + 1]
        p = y_refs[kk][...]
        lo = lo + wk * _unpack_lo(p)
        hi = hi + wk * _unpack_hi(p)
    ms = (jnp.sum(lo * lo, axis=-1, keepdims=True) + jnp.sum(hi * hi, axis=-1, keepdims=True)) / d
    inv = lax.rsqrt(ms + RMS_EPS)
    o_ref[:, :dh] = lo * inv * g_ref[:, :dh]
    o_ref[:, dh:] = hi * inv * g_ref[:, dh:]


def _combine(c, x2, gates, g, ykt):
    n, d = x2.shape
    tm = c.tm_c
    nb = n // tm
    y_specs = [pl.BlockSpec((tm, d // 2), functools.partial(lambda i, kk: (kk * nb + i, 0), kk=kk))
               for kk in range(c.K)]
    return pl.pallas_call(
        functools.partial(_combine_kernel, topk=c.K),
        grid=(nb,),
        in_specs=[
            pl.BlockSpec((tm, d), lambda i: (i, 0)),
            pl.BlockSpec((tm, LANES), lambda i: (i, 0)),
            pl.BlockSpec((1, d), lambda i: (0, 0)),
        ] + y_specs,
        out_specs=pl.BlockSpec((tm, d), lambda i: (i, 0)),
        out_shape=jax.ShapeDtypeStruct((n, d), F32),
        compiler_params=_cparams(("parallel",)),
        name="combine",
    )(x2, gates, g, *([ykt] * c.K))


def _plan(c, idx, counts):
    tm, ts = c.tm_e, c.ts_e
    nk = idx.shape[0] * c.K
    order = jnp.argsort(idx.T.reshape(-1)).astype(I32)
    pad = _id_window(c) + (-nk) % ID_CHUNK
    order_pad = jnp.concatenate([order, jnp.zeros((pad,), I32)])
    counts = counts.astype(I32)
    start = jnp.cumsum(counts) - counts
    tiles_per_e = (counts + tm - 1) // tm
    tile_end = jnp.cumsum(tiles_per_e)
    tile_start = tile_end - tiles_per_e
    t = jnp.arange(_n_tiles(c), dtype=I32)
    n_used = tile_end[-1]
    tc = jnp.minimum(t, n_used - 1)
    te = jnp.minimum(jnp.searchsorted(tile_end, tc, side="right"), c.E - 1).astype(I32)
    j = tc - tile_start[te]
    used = t < n_used
    rows = jnp.where(used, jnp.clip(counts[te] - j * tm, 0, tm), 0).astype(I32)
    nsub = (rows + ts - 1) // ts
    pos = jnp.where(used, start[te] + j * tm, 0).astype(I32)
    cpi = (c.nck_e + jnp.maximum(nsub, 1) - 1) // jnp.maximum(nsub, 1)
    return order_pad, (te, nsub.astype(I32), cpi.astype(I32), pos, rows)


def _n_tiles(c):
    return -(-(c.B * c.S * c.K) // c.tm_e) + c.E


def _forward(c, x, mem, norm_mix_g, w_in, b_forget, sg_ln_g, sg_ln_b, w_spatial, b_spatial,
             w_branch_a, w_branch_b, w_out, norm_x_g, norm_mem_g, w_xq, w_xkv, w_xo,
             norm_ffn_g, w_router, b_router, w_gate_up, b_gate_up, w_down, b_down, norm_final_g):
    B, S, D = x.shape
    n = B * S
    fw = c.FH * LANES
    sw = c.SG * LANES
    x2d = x.reshape(n, D)

    o_f = 3 * fw
    o_z = o_f + c.FH
    o_g = o_z + 2 * sw
    w_main = jnp.concatenate([w_in[:, o_g:], w_in[:, o_z:o_g], w_in[:, :o_f]], axis=1).astype(BF16)
    w_f = jnp.pad(w_in[:, o_f:o_z], ((0, 0), (0, LANES - c.FH))).astype(BF16)
    b_f = jnp.pad(b_forget.astype(F32), (0, LANES - c.FH)).reshape(1, LANES)
    proj, logf = _in_proj(c, x2d, norm_mix_g.reshape(1, D), w_main, w_f, b_f, LOG2E * LANES ** -0.5)

    logf_bhs = logf[:, :c.FH].reshape(B, S, c.FH).transpose(0, 2, 1)
    csum = _cumsum(c, logf_bhs)
    crow = csum.reshape(B * c.FH, 1, S)
    ccol = csum.reshape(B * c.FH, S, 1)
    qcol0 = (2 * D + 2 * sw) // LANES
    attn = _fox(c, proj, crow, ccol, qcol0)

    b_s_full = jnp.broadcast_to(b_spatial.astype(F32)[:, :, None], (c.SG, c.SGC, LANES))
    x1 = _mix(c, x2d, attn, proj, sg_ln_g.reshape(1, sw), sg_ln_b.reshape(1, sw), w_spatial, b_s_full,
              w_branch_a.astype(BF16), w_branch_b.astype(BF16), w_out.astype(BF16))

    kv = _mem_kv(c, mem.reshape(B * c.MEM, D), norm_mem_g.reshape(1, D), w_xkv.astype(BF16))
    w_r32 = jnp.pad(w_router.astype(F32), ((0, 0), (0, LANES - c.E)))
    w_r_hi = w_r32.astype(BF16)
    w_r = jnp.concatenate([w_r_hi, (w_r32 - w_r_hi.astype(F32)).astype(BF16)], axis=1)
    b_r = jnp.pad(b_router.astype(F32), (0, LANES - c.E)).reshape(1, LANES)
    x2, hfp, idx, gates, counts = _xattn(
        c, x1, norm_x_g.reshape(1, D), w_xq.astype(BF16), kv, w_xo.astype(BF16),
        norm_ffn_g.reshape(1, D), w_r, b_r, LANES ** -0.5)

    order_pad, sched = _plan(c, idx[:, :c.K], counts[0, :c.E])
    ykt = _moe(c, sched, order_pad, hfp, w_gate_up, b_gate_up.reshape(c.E, 1, 2 * c.DFF),
               w_down, b_down.reshape(c.E, 1, D))
    out = _combine(c, x2, gates, norm_final_g.reshape(1, D), ykt)
    return out.reshape(B, S, D)


_CFG = Cfg(B=4, S=4096, D=2048, MEM=256, FH=8, SG=8, SGC=128, XH=4, E=32, K=4, DFF=2048,
           tm_in=1024, tn_in=1024, tq=512, tm_mix=256, tm_x=512, tm_e=2560, ts_e=256, fc_e=512, nck_e=16,
           tm_c=256, tc_cs=512, fox_parts=2)


@jax.jit
def kernel(x, mem, norm_mix_g, w_in, b_forget, sg_ln_g, sg_ln_b, w_spatial, b_spatial, w_branch_a, w_branch_b,
           w_out, norm_x_g, norm_mem_g, w_xq, w_xkv, w_xo, norm_ffn_g, w_router, b_router, w_gate_up,
           b_gate_up, w_down, b_down, norm_final_g):
    return _forward(_CFG, x, mem, norm_mix_g, w_in, b_forget, sg_ln_g, sg_ln_b, w_spatial, b_spatial,
                    w_branch_a, w_branch_b, w_out, norm_x_g, norm_mem_g, w_xq, w_xkv, w_xo,
                    norm_ffn_g, w_router, b_router, w_gate_up, b_gate_up, w_down, b_down, norm_final_g)
```

```python
import functools
from typing import NamedTuple

import jax
import jax.numpy as jnp
from jax import lax
from jax.experimental import pallas as pl
from jax.experimental.pallas import tpu as pltpu

F32 = jnp.float32
BF16 = jnp.bfloat16
U32 = jnp.uint32
I32 = jnp.int32

LANES = 128
VMEM_LIMIT = 56 * 1024 * 1024

RMS_EPS = 1e-6
LN_EPS = 1e-5
SWIGLU_LIMIT = 7.0
SWIGLU_ALPHA = 1.702
GELU_C = 0.7978845608028654
LOG2E = 1.4426950408889634
ID_CHUNK = 1024
WEIGHT_DMA_PRIORITY = 1


class Cfg(NamedTuple):
    B: int
    S: int
    D: int
    MEM: int
    FH: int
    SG: int
    SGC: int
    XH: int
    E: int
    K: int
    DFF: int
    tm_in: int
    tn_in: int
    tq: int
    tm_mix: int
    tm_x: int
    tm_e: int
    ts_e: int
    fc_e: int
    nck_e: int
    nst_e: int
    tm_c: int
    tc_cs: int
    fox_parts: int


def _cparams(sem):
    return pltpu.CompilerParams(dimension_semantics=sem, vmem_limit_bytes=VMEM_LIMIT)


def _rms(x, g):
    ms = jnp.mean(x * x, axis=-1, keepdims=True)
    return x * lax.rsqrt(ms + RMS_EPS) * g


def _pack_bf16_pair(a, b):
    def rne(v):
        bits = lax.bitcast_convert_type(v, U32)
        return bits + jnp.uint32(0x7FFF) + ((bits >> 16) & jnp.uint32(1))
    return (rne(a) >> 16) | (rne(b) & jnp.uint32(0xFFFF0000))


def _unpack_lo(p):
    return lax.bitcast_convert_type(p << 16, F32)


def _unpack_hi(p):
    return lax.bitcast_convert_type(p & jnp.uint32(0xFFFF0000), F32)


def _in_proj_kernel(x_ref, g_ref, w_ref, wf_ref, bf_ref, o_ref, f_ref, h_ref, *, nj_gate, nj_z, nj_q, qscale):
    j = pl.program_id(1)

    @pl.when(j == 0)
    def _():
        hb = _rms(x_ref[...], g_ref[...]).astype(BF16)
        h_ref[...] = hb
        f = jnp.dot(hb, wf_ref[...], preferred_element_type=F32) + bf_ref[...]
        f_ref[...] = jnp.minimum(f, 0.0) - jnp.log1p(jnp.exp(-jnp.abs(f)))

    acc = jnp.dot(h_ref[...], w_ref[...], preferred_element_type=F32)

    is_gate = j < nj_gate
    is_z = (j >= nj_gate) & (j < nj_gate + nj_z)
    is_q = (j >= nj_gate + nj_z) & (j < nj_gate + nj_z + nj_q)
    a1 = jnp.where(is_gate, 0.5, jnp.where(is_z, GELU_C, 0.0)).astype(F32)
    a3 = jnp.where(is_z, GELU_C * 0.044715, 0.0).astype(F32)
    b0 = jnp.where(is_gate, 0.5, 0.0).astype(F32)
    b1 = jnp.where(is_gate, 0.0, jnp.where(is_z, 0.5, jnp.where(is_q, qscale, 1.0))).astype(F32)
    th = jnp.tanh(acc * (a1 + a3 * (acc * acc)))
    o_ref[...] = ((b0 + b1 * acc) * (1.0 + th)).astype(o_ref.dtype)


def _in_proj(c, x2d, g, w_main, w_f, b_f, qscale):
    n, d = x2d.shape
    nc = w_main.shape[1]
    tm, tn = c.tm_in, c.tn_in
    fw = c.FH * LANES
    kern = functools.partial(_in_proj_kernel, nj_gate=2 * d // tn, nj_z=2 * c.SG * LANES // tn,
                             nj_q=fw // tn, qscale=qscale)
    return pl.pallas_call(
        kern,
        grid=(n // tm, nc // tn),
        in_specs=[
            pl.BlockSpec((tm, d), lambda i, j: (i, 0)),
            pl.BlockSpec((1, d), lambda i, j: (0, 0)),
            pl.BlockSpec((d, tn), lambda i, j: (0, j)),
            pl.BlockSpec((d, LANES), lambda i, j: (0, 0)),
            pl.BlockSpec((1, LANES), lambda i, j: (0, 0)),
        ],
        out_specs=[
            pl.BlockSpec((tm, tn), lambda i, j: (i, j)),
            pl.BlockSpec((tm, LANES), lambda i, j: (i, 0)),
        ],
        out_shape=[jax.ShapeDtypeStruct((n, nc), BF16), jax.ShapeDtypeStruct((n, LANES), F32)],
        scratch_shapes=[pltpu.VMEM((tm, d), BF16)],
        compiler_params=_cparams(("parallel", "arbitrary")),
        name="in_proj",
    )(x2d, g, w_main, w_f, b_f)


def _cumsum_kernel(f_ref, o_ref, *, tc):
    rows, s = f_ref.shape[1], f_ref.shape[2]
    r = lax.broadcasted_iota(I32, (tc, tc), 0)
    col = lax.broadcasted_iota(I32, (tc, tc), 1)
    upper = (r <= col).astype(F32)
    carry = jnp.zeros((rows, 1), F32)
    for i in range(s // tc):
        blk = f_ref[0, :, i * tc:(i + 1) * tc]
        cs = jnp.dot(blk, upper, preferred_element_type=F32, precision=lax.Precision.HIGHEST) + carry
        o_ref[0, :, i * tc:(i + 1) * tc] = cs * LOG2E
        carry = cs[:, tc - 1:tc]


def _cumsum(c, logf_bhs):
    b, h, s = logf_bhs.shape
    return pl.pallas_call(
        functools.partial(_cumsum_kernel, tc=c.tc_cs),
        grid=(b,),
        in_specs=[pl.BlockSpec((1, h, s), lambda i: (i, 0, 0))],
        out_specs=pl.BlockSpec((1, h, s), lambda i: (i, 0, 0)),
        out_shape=jax.ShapeDtypeStruct((b, h, s), F32),
        compiler_params=_cparams(("parallel",)),
        name="cumsum",
    )(logf_bhs)


def _fox_kernel(q_ref, k_ref, v_ref, crow_ref, ccol_ref, o_ref, m_ref, l_ref, acc_ref, cq_ref, s_ref,
                *, t, parts):
    qi = pl.program_id(2)
    hr = t // parts
    m_ref[...] = jnp.full(m_ref.shape, -jnp.inf, F32)
    l_ref[...] = jnp.zeros(l_ref.shape, F32)
    acc_ref[...] = jnp.zeros(acc_ref.shape, F32)
    cq_ref[...] = jnp.broadcast_to(ccol_ref[0], cq_ref.shape)

    def scores(ks):
        k = k_ref[pl.ds(ks, t), :]
        return lax.dot_general(q_ref[...], k, (((1,), (1,)), ((), ())), preferred_element_type=F32)

    def block(part, ks, width, masked):
        rows = slice(part * hr, (part + 1) * hr)
        v = v_ref[pl.ds(ks, width), :]
        s = s_ref[rows, :width]
        cq = cq_ref[rows, :]
        crow = crow_ref[0, :, pl.ds(ks, width)]
        nj = width // LANES
        sj = [s[:, j * LANES:(j + 1) * LANES] + cq - crow[:, j * LANES:(j + 1) * LANES] for j in range(nj)]
        if masked:
            row = lax.broadcasted_iota(I32, (hr, LANES), 0) + part * hr
            col = lax.broadcasted_iota(I32, (hr, LANES), 1)
            sj = [jnp.where(col + j * LANES <= row, sj[j], -jnp.inf) for j in range(nj)]
        mx = sj[0]
        for j in range(1, nj):
            mx = jnp.maximum(mx, sj[j])
        m_prev = m_ref[rows, :]
        m_next = jnp.maximum(m_prev, jnp.max(mx, axis=-1, keepdims=True))
        alpha = jnp.exp2(m_prev - m_next)
        pj = [jnp.exp2(sj[j] - m_next) for j in range(nj)]
        psum = pj[0]
        for j in range(1, nj):
            psum = psum + pj[j]
        p = jnp.concatenate([x.astype(BF16) for x in pj], axis=-1)
        l_ref[rows, :] = alpha * l_ref[rows, :] + psum
        acc_ref[rows, :] = alpha * acc_ref[rows, :] + jnp.dot(p, v, preferred_element_type=F32)
        m_ref[rows, :] = m_next

    s_ref[...] = scores(0)

    def body(kc, carry):
        ks = pl.multiple_of(kc * t, t)
        s_next = scores(pl.multiple_of(ks + t, t))
        for part in range(parts):
            block(part, ks, t, False)
        s_ref[...] = s_next
        return carry

    lax.fori_loop(0, qi, body, 0)
    kd = pl.multiple_of(qi * t, t)
    for part in range(parts):
        block(part, kd, (part + 1) * hr, True)
    l = jnp.sum(l_ref[...], axis=-1, keepdims=True)
    o_ref[...] = (acc_ref[...] / l).astype(o_ref.dtype)


def _fox(c, proj, crow, ccol, qcol0):
    n = c.B * c.S
    t = c.tq
    nq = c.S // t
    h = c.FH
    return pl.pallas_call(
        functools.partial(_fox_kernel, t=t, parts=c.fox_parts),
        grid=(c.B, h, nq),
        in_specs=[
            pl.BlockSpec((t, LANES), lambda b, hh, qi: (b * nq + qi, qcol0 + hh)),
            pl.BlockSpec((c.S, LANES), lambda b, hh, qi: (b, qcol0 + h + hh)),
            pl.BlockSpec((c.S, LANES), lambda b, hh, qi: (b, qcol0 + 2 * h + hh)),
            pl.BlockSpec((1, 1, c.S), lambda b, hh, qi: (b * h + hh, 0, 0)),
            pl.BlockSpec((1, t, 1), lambda b, hh, qi: (b * h + hh, qi, 0)),
        ],
        out_specs=pl.BlockSpec((t, LANES), lambda b, hh, qi: (b * nq + qi, hh)),
        out_shape=jax.ShapeDtypeStruct((n, h * LANES), BF16),
        scratch_shapes=[pltpu.VMEM((t, LANES), F32)] * 4 + [pltpu.VMEM((t, t), F32)],
        compiler_params=_cparams(("parallel", "parallel", "arbitrary")),
        name="fox",
    )(proj, proj, proj, crow, ccol)


def _mix_kernel(x_ref, a_ref, u_ref, v_ref, ga_ref, gb_ref, lng_ref, lnb_ref, ws_ref, bs_ref,
                wa_ref, wb_ref, wo_ref, o_ref, sg_ref, *, sgc, groups):
    tm = x_ref.shape[0]
    v = v_ref[...].astype(F32)
    mu = jnp.mean(v, axis=-1, keepdims=True)
    vc = v - mu
    var = jnp.mean(vc * vc, axis=-1, keepdims=True)
    vn = (vc * lax.rsqrt(var + LN_EPS) * lng_ref[...] + lnb_ref[...]).astype(BF16)
    row = lax.broadcasted_iota(I32, (sgc, sgc), 0)
    col = lax.broadcasted_iota(I32, (sgc, sgc), 1)
    for g in range(groups):
        w = jnp.where(col <= row, ws_ref[g], 0.0).astype(BF16)
        bias = bs_ref[g]
        for ci in range(tm // sgc):
            rs = slice(ci * sgc, (ci + 1) * sgc)
            cs = slice(g * LANES, (g + 1) * LANES)
            mixed = jnp.dot(w, vn[rs, cs], preferred_element_type=F32) + bias
            sg_ref[rs, cs] = (u_ref[rs, cs].astype(F32) * mixed).astype(BF16)
    ya = jnp.dot(a_ref[...], wa_ref[...], preferred_element_type=F32)
    yb = jnp.dot(sg_ref[...], wb_ref[...], preferred_element_type=F32)
    merged = (ga_ref[...].astype(F32) * ya + gb_ref[...].astype(F32) * yb).astype(BF16)
    o_ref[...] = x_ref[...] + jnp.dot(merged, wo_ref[...], preferred_element_type=F32)


def _const_spec(shape):
    nd = len(shape)
    return pl.BlockSpec(shape, lambda i: (0,) * nd, pipeline_mode=pl.Buffered(1))


def _mix(c, x2d, attn, proj, ln_g, ln_b, w_s, b_s_full, w_a, w_b, w_o):
    n, d = x2d.shape
    tm = c.tm_mix
    fw = c.FH * LANES
    sw = c.SG * LANES
    ucol = 2 * d // sw
    return pl.pallas_call(
        functools.partial(_mix_kernel, sgc=c.SGC, groups=c.SG),
        grid=(n // tm,),
        in_specs=[
            pl.BlockSpec((tm, d), lambda i: (i, 0)),
            pl.BlockSpec((tm, fw), lambda i: (i, 0)),
            pl.BlockSpec((tm, sw), lambda i: (i, ucol)),
            pl.BlockSpec((tm, sw), lambda i: (i, ucol + 1)),
            pl.BlockSpec((tm, d), lambda i: (i, 0)),
            pl.BlockSpec((tm, d), lambda i: (i, 1)),
            _const_spec((1, sw)),
            _const_spec((1, sw)),
            _const_spec((c.SG, c.SGC, c.SGC)),
            _const_spec((c.SG, c.SGC, LANES)),
            _const_spec((fw, d)),
            _const_spec((sw, d)),
            _const_spec((d, d)),
        ],
        out_specs=pl.BlockSpec((tm, d), lambda i: (i, 0)),
        out_shape=jax.ShapeDtypeStruct((n, d), F32),
        scratch_shapes=[pltpu.VMEM((tm, sw), BF16)],
        compiler_params=_cparams(("parallel",)),
        name="mix",
    )(x2d, attn, proj, proj, proj, proj, ln_g, ln_b, w_s, b_s_full, w_a, w_b, w_o)


def _mem_kv_kernel(m_ref, g_ref, w_ref, o_ref):
    hm = _rms(m_ref[...], g_ref[...]).astype(BF16)
    o_ref[...] = jnp.dot(hm, w_ref[...], preferred_element_type=F32).astype(o_ref.dtype)


def _mem_kv(c, mem2d, g, w_xkv):
    n, d = mem2d.shape
    nc = w_xkv.shape[1]
    tm = c.MEM
    return pl.pallas_call(
        _mem_kv_kernel,
        grid=(n // tm,),
        in_specs=[pl.BlockSpec((tm, d), lambda i: (i, 0)), _const_spec((1, d)), _const_spec((d, nc))],
        out_specs=pl.BlockSpec((tm, nc), lambda i: (i, 0)),
        out_shape=jax.ShapeDtypeStruct((n, nc), BF16),
        compiler_params=_cparams(("parallel",)),
        name="mem_kv",
    )(mem2d, g, w_xkv)


def _xattn_kernel(x1_ref, gx_ref, wq_ref, kv_ref, wo_ref, gf_ref, wr_ref, br_ref,
                  x2_ref, hfp_ref, idx_ref, gate_ref, cnt_ref, carry_ref,
                  *, heads, n_exp, topk, qscale):
    i = pl.program_id(0)
    tm, d = x1_ref.shape
    xw = heads * LANES
    x1 = x1_ref[...]
    hx = _rms(x1, gx_ref[...]).astype(BF16)
    q = (jnp.dot(hx, wq_ref[...], preferred_element_type=F32) * qscale).astype(BF16)
    outs = []
    for h in range(heads):
        k = kv_ref[:, h * LANES:(h + 1) * LANES]
        v = kv_ref[:, xw + h * LANES:xw + (h + 1) * LANES]
        s = lax.dot_general(q[:, h * LANES:(h + 1) * LANES], k, (((1,), (1,)), ((), ())),
                            preferred_element_type=F32)
        s = s - jnp.max(s, axis=-1, keepdims=True)
        p = jnp.exp(s)
        p = p / jnp.sum(p, axis=-1, keepdims=True)
        outs.append(jnp.dot(p.astype(BF16), v, preferred_element_type=F32).astype(BF16))
    o = jnp.concatenate(outs, axis=-1)
    x2 = x1 + jnp.dot(o, wo_ref[...], preferred_element_type=F32)
    x2_ref[...] = x2

    hf = _rms(x2, gf_ref[...])
    hfp_ref[...] = _pack_bf16_pair(hf[:, :d // 2], hf[:, d // 2:])

    h_hi = hf.astype(BF16)
    h_lo = (hf - h_hi.astype(F32)).astype(BF16)
    l_hi = jnp.dot(h_hi, wr_ref[...], preferred_element_type=F32)
    l_lo = jnp.dot(h_lo, wr_ref[:, :LANES], preferred_element_type=F32)
    logits = l_hi[:, :LANES] + l_hi[:, LANES:] + l_lo + br_ref[...]
    lane = lax.broadcasted_iota(I32, (tm, LANES), 1)
    lg = jnp.where(lane < n_exp, logits, -jnp.inf)
    vals, idxs = [], []
    for _ in range(topk):
        m = jnp.max(lg, axis=-1, keepdims=True)
        ix = jnp.min(jnp.where(lg == m, lane, LANES), axis=-1, keepdims=True)
        vals.append(m)
        idxs.append(ix)
        lg = jnp.where(lane == ix, -jnp.inf, lg)
    es = [jnp.exp(vv - vals[0]) for vv in vals]
    denom = es[0]
    for e in es[1:]:
        denom = denom + e

    @pl.when(i == 0)
    def _():
        carry_ref[...] = jnp.zeros(carry_ref.shape, F32)

    onehot = jnp.zeros((tm, LANES), F32)
    for ix in idxs:
        onehot = onehot + (lane == ix).astype(F32)
    idx_out = jnp.zeros((tm, LANES), I32)
    gate_out = jnp.zeros((tm, LANES), F32)
    for kk in range(topk):
        idx_out = jnp.where(lane == kk, idxs[kk], idx_out)
        gate_out = jnp.where(lane == kk, es[kk] / denom, gate_out)
    idx_ref[...] = idx_out
    gate_ref[...] = gate_out
    carry_ref[...] = carry_ref[...] + jnp.sum(onehot, axis=0, keepdims=True)
    cnt_ref[...] = carry_ref[...]


def _xattn(c, x1, gx, w_xq, kv, w_xo, gf, w_r, b_r, qscale):
    n, d = x1.shape
    tm = c.tm_x
    xw = c.XH * LANES
    per_b = c.S // tm
    kern = functools.partial(_xattn_kernel, heads=c.XH, n_exp=c.E, topk=c.K, qscale=qscale)
    return pl.pallas_call(
        kern,
        grid=(n // tm,),
        in_specs=[
            pl.BlockSpec((tm, d), lambda i: (i, 0)),
            _const_spec((1, d)),
            _const_spec((d, xw)),
            pl.BlockSpec((c.MEM, 2 * xw), lambda i: (i // per_b, 0)),
            _const_spec((xw, d)),
            _const_spec((1, d)),
            _const_spec((d, 2 * LANES)),
            _const_spec((1, LANES)),
        ],
        out_specs=[
            pl.BlockSpec((tm, d), lambda i: (i, 0)),
            pl.BlockSpec((tm, d // 2), lambda i: (i, 0)),
            pl.BlockSpec((tm, LANES), lambda i: (i, 0)),
            pl.BlockSpec((tm, LANES), lambda i: (i, 0)),
            pl.BlockSpec((1, LANES), lambda i: (0, 0)),
        ],
        out_shape=[
            jax.ShapeDtypeStruct((n, d), F32),
            jax.ShapeDtypeStruct((n, d // 2), U32),
            jax.ShapeDtypeStruct((n, LANES), I32),
            jax.ShapeDtypeStruct((n, LANES), F32),
            jax.ShapeDtypeStruct((1, LANES), F32),
        ],
        scratch_shapes=[pltpu.VMEM((1, LANES), F32)],
        compiler_params=_cparams(("arbitrary",)),
        name="xattn_router",
    )(x1, gx, w_xq, kv, w_xo, gf, w_r, b_r)


def _moe_kernel(te_ref, ns_ref, cpi_ref, pos_ref, rows_ref, ord_ref, hfp_ref, wgu_ref, wd_ref, bgu_ref, bd_ref,
                ykt_ref, wgub, wdb, stage, act, xin, ybuf, ids, wsem, xsem, ysem, isem,
                *, ts, tm, nck, fc, n_tok, n_asg):
    i = pl.program_id(0)
    n = pl.num_programs(0)
    nsub = ns_ref[i]
    e = te_ref[i]
    cpi = cpi_ref[i]
    rows = rows_ref[i]
    d, f2 = wgub.shape
    f = f2 // 2
    dh = d // 2
    ckr = d // nck
    ckd = f // nck
    nst = stage.shape[0]
    idw = ids.shape[0] // 2
    nq = idw // ID_CHUNK
    id_base = (i % 2) * idw + (pos_ref[i] & (ID_CHUNK - 1))

    def ids_copy(item, q, slot):
        c0 = lax.shift_right_logical(pos_ref[item], ID_CHUNK.bit_length() - 1)
        src = ord_ref.at[pl.ds(pl.multiple_of((c0 + q) * ID_CHUNK, ID_CHUNK), ID_CHUNK)]
        dst = ids.at[pl.ds(pl.multiple_of(slot * idw + q * ID_CHUNK, ID_CHUNK), ID_CHUNK)]
        return pltpu.make_async_copy(src, dst, isem.at[slot])

    def token_of(a):
        return a & (n_tok - 1) if n_tok & (n_tok - 1) == 0 else lax.rem(a, n_tok)

    def gather(s, slot, r0=0, r1=ts):
        for r in range(r0, r1):
            tok = token_of(ids[id_base + s * ts + r])
            pltpu.make_async_copy(hfp_ref.at[pl.ds(tok, 1)], xin.at[slot, pl.ds(r, 1)], xsem.at[slot]).start()

    def gather_wait(slot):
        for r in range(ts):
            pltpu.make_async_copy(hfp_ref.at[pl.ds(0, 1)], xin.at[slot, pl.ds(0, 1)], xsem.at[slot]).wait()

    def scatter(s, slot, r0=0, r1=ts):
        for r in range(r0, r1):
            g = s * ts + r
            dst = jnp.where(g < rows, ids[id_base + g], n_asg + slot * ts + r)
            pltpu.make_async_copy(ybuf.at[slot, pl.ds(r, 1)], ykt_ref.at[pl.ds(dst, 1)], ysem.at[slot]).start()

    def scatter_wait(slot):
        for r in range(ts):
            pltpu.make_async_copy(ybuf.at[slot, pl.ds(0, 1)], ykt_ref.at[pl.ds(0, 1)], ysem.at[slot]).wait()

    def gu_copy(ee, c, slot):
        return pltpu.make_async_copy(wgu_ref.at[ee, pl.ds(pl.multiple_of(c * ckr, ckr), ckr), :],
                                     stage.at[slot], wsem.at[slot])

    def d_copy(ee, c, slot):
        return pltpu.make_async_copy(wd_ref.at[ee, pl.ds(pl.multiple_of(c * ckd, ckd), ckd), :],
                                     stage.at[slot, pl.ds(0, ckd), pl.ds(0, d)], wsem.at[slot])

    def gu_step(ee, c):
        slot = c % nst
        gu_copy(ee, c, slot).wait()
        wgub[pl.ds(pl.multiple_of(c * ckr, ckr), ckr), :] = stage[slot].astype(BF16)

        @pl.when(c + nst < nck)
        def _():
            gu_copy(ee, c + nst, slot).start(priority=WEIGHT_DMA_PRIORITY)

    def d_step(ee, c):
        slot = c % nst
        d_copy(ee, c, slot).wait()
        wdb[pl.ds(pl.multiple_of(c * ckd, ckd), ckd), :] = stage[slot, :ckd, :d].astype(BF16)

        @pl.when(c + nst < nck)
        def _():
            d_copy(ee, c + nst, slot).start(priority=WEIGHT_DMA_PRIORITY)

    def prime(copy_fn, ee):
        for q in range(min(nst, nck)):
            copy_fn(ee, q, q).start(priority=WEIGHT_DMA_PRIORITY)

    def steps_after(s, step_fn, ee):
        def one(jj, carry):
            c = s * cpi + jj

            @pl.when(c < nck)
            def _():
                step_fn(ee, c)
            return carry
        lax.fori_loop(0, cpi, one, 0)

    @pl.when(nsub > 0)
    def _():
        nxt = jnp.minimum(i + 1, n - 1)
        has_next = (i + 1 < n) & (ns_ref[nxt] > 0)

        @pl.when(i == 0)
        def _():
            for q in range(nq):
                ids_copy(0, q, 0).start()
            prime(gu_copy, e)

            def first(c, carry):
                gu_step(e, c)
                return carry
            lax.fori_loop(0, nck, first, 0)

        for q in range(nq):
            ids_copy(i, q, i % 2).wait()

        @pl.when(has_next)
        def _():
            for q in range(nq):
                ids_copy(nxt, q, nxt % 2).start()

        prime(d_copy, e)
        gather(0, 0)
        bgu = bgu_ref[e]

        def phase_a(s, carry):
            slot = s % 2
            gather_wait(slot)
            p = xin[slot]
            xb = jnp.concatenate([_unpack_lo(p).astype(BF16), _unpack_hi(p).astype(BF16)], axis=-1)
            nc = f // fc
            for cc in range(nc):
                gather(s + 1, 1 - slot, cc * ts // nc, (cc + 1) * ts // nc)
                gate = jnp.dot(xb, wgub[:, cc * fc:(cc + 1) * fc], preferred_element_type=F32)
                gate = gate + bgu[:, cc * fc:(cc + 1) * fc]
                up = jnp.dot(xb, wgub[:, f + cc * fc:f + (cc + 1) * fc], preferred_element_type=F32)
                up = up + bgu[:, f + cc * fc:f + (cc + 1) * fc]
                gate = jnp.minimum(gate, SWIGLU_LIMIT)
                up = jnp.clip(up, -SWIGLU_LIMIT, SWIGLU_LIMIT)
                a = (up + 1.0) * gate * jax.nn.sigmoid(SWIGLU_ALPHA * gate)
                act[s, :, cc * fc:(cc + 1) * fc] = a.astype(BF16)
            steps_after(s, d_step, e)
            return carry

        lax.fori_loop(0, nsub, phase_a, 0)
        gather_wait(nsub % 2)

        e_next = te_ref[nxt]
        load_next = has_next & (e_next != e)

        @pl.when(load_next)
        def _():
            prime(gu_copy, e_next)

        bd = bd_ref[e]

        def down(s, slot, scatter_prev):
            a = act[s]
            if scatter_prev:
                scatter(s - 1, 1 - slot)
            for cc in range(dh // fc):
                lo = slice(cc * fc, (cc + 1) * fc)
                hi = slice(dh + cc * fc, dh + (cc + 1) * fc)
                y_lo = jnp.dot(a, wdb[:, lo], preferred_element_type=F32) + bd[:, lo]
                y_hi = jnp.dot(a, wdb[:, hi], preferred_element_type=F32) + bd[:, hi]
                ybuf[slot, :, lo] = _pack_bf16_pair(y_lo, y_hi)

        def next_weights(s):
            @pl.when(load_next)
            def _():
                steps_after(s, gu_step, e_next)

        down(0, 0, False)
        next_weights(0)

        def phase_b(s, carry):
            slot = s % 2

            @pl.when(s >= 2)
            def _():
                scatter_wait(slot)

            down(s, slot, True)
            next_weights(s)
            return carry

        lax.fori_loop(1, nsub, phase_b, 0)
        scatter(nsub - 1, (nsub - 1) % 2)
        scatter_wait((nsub - 1) % 2)

        @pl.when(nsub >= 2)
        def _():
            scatter_wait(nsub % 2)


def _id_window(c):
    return ((ID_CHUNK - 1 + c.tm_e + c.ts_e) // ID_CHUNK + 1) * ID_CHUNK


def _moe(c, sched, order_pad, hfp, w_gu, b_gu3, w_d, b_d3):
    n_tok, dh = hfp.shape
    d = 2 * dh
    f = c.DFF
    tm, ts, nck = c.tm_e, c.ts_e, c.nck_e
    n_asg = n_tok * c.K
    grid_spec = pltpu.PrefetchScalarGridSpec(
        num_scalar_prefetch=5,
        grid=(_n_tiles(c),),
        in_specs=[
            pl.BlockSpec(memory_space=pl.ANY),
            pl.BlockSpec(memory_space=pl.ANY),
            pl.BlockSpec(memory_space=pl.ANY),
            pl.BlockSpec(memory_space=pl.ANY),
            pl.BlockSpec((c.E, 1, 2 * f), lambda i, *_: (0, 0, 0), pipeline_mode=pl.Buffered(1)),
            pl.BlockSpec((c.E, 1, d), lambda i, *_: (0, 0, 0), pipeline_mode=pl.Buffered(1)),
        ],
        out_specs=pl.BlockSpec(memory_space=pl.ANY),
        scratch_shapes=[
            pltpu.VMEM((d, 2 * f), BF16),
            pltpu.VMEM((f, d), BF16),
            pltpu.VMEM((c.nst_e, d // nck, 2 * f), F32),
            pltpu.VMEM((tm // ts, ts, f), BF16),
            pltpu.VMEM((2, ts, dh), U32),
            pltpu.VMEM((2, ts, dh), U32),
            pltpu.SMEM((2 * _id_window(c),), I32),
            pltpu.SemaphoreType.DMA((c.nst_e,)),
            pltpu.SemaphoreType.DMA((2,)),
            pltpu.SemaphoreType.DMA((2,)),
            pltpu.SemaphoreType.DMA((2,)),
        ],
    )
    kern = functools.partial(_moe_kernel, ts=ts, tm=tm, nck=nck, fc=c.fc_e, n_tok=n_tok, n_asg=n_asg)
    return pl.pallas_call(
        kern,
        grid_spec=grid_spec,
        out_shape=jax.ShapeDtypeStruct((n_asg + 2 * ts, dh), U32),
        compiler_params=_cparams(("arbitrary",)),
        name="experts",
    )(*sched, order_pad, hfp, w_gu, w_d, b_gu3, b_d3)


def _combine_kernel(x2_ref, gate_ref, g_ref, *refs, topk):
    y_refs, o_ref = refs[:topk], refs[topk]
    d = x2_ref.shape[1]
    dh = d // 2
    lo = x2_ref[:, :dh]
    hi = x2_ref[:, dh:]
    gates = gate_ref[...]
    for kk in range(topk):
        wk = gates[:, kk:kk + 1]
        p = y_refs[kk][...]
        lo = lo + wk * _unpack_lo(p)
        hi = hi + wk * _unpack_hi(p)
    ms = (jnp.sum(lo * lo, axis=-1, keepdims=True) + jnp.sum(hi * hi, axis=-1, keepdims=True)) / d
    inv = lax.rsqrt(ms + RMS_EPS)
    o_ref[:, :dh] = lo * inv * g_ref[:, :dh]
    o_ref[:, dh:] = hi * inv * g_ref[:, dh:]


def _combine(c, x2, gates, g, ykt):
    n, d = x2.shape
    tm = c.tm_c
    nb = n // tm
    y_specs = [pl.BlockSpec((tm, d // 2), functools.partial(lambda i, kk: (kk * nb + i, 0), kk=kk))
               for kk in range(c.K)]
    return pl.pallas_call(
        functools.partial(_combine_kernel, topk=c.K),
        grid=(nb,),
        in_specs=[
            pl.BlockSpec((tm, d), lambda i: (i, 0)),
            pl.BlockSpec((tm, LANES), lambda i: (i, 0)),
            pl.BlockSpec((1, d), lambda i: (0, 0)),
        ] + y_specs,
        out_specs=pl.BlockSpec((tm, d), lambda i: (i, 0)),
        out_shape=jax.ShapeDtypeStruct((n, d), F32),
        compiler_params=_cparams(("parallel",)),
        name="combine",
    )(x2, gates, g, *([ykt] * c.K))


def _plan(c, idx, counts):
    tm, ts = c.tm_e, c.ts_e
    nk = idx.shape[0] * c.K
    order = jnp.argsort(idx.T.reshape(-1)).astype(I32)
    pad = _id_window(c) + (-nk) % ID_CHUNK
    order_pad = jnp.concatenate([order, jnp.zeros((pad,), I32)])
    counts = counts.astype(I32)
    start = jnp.cumsum(counts) - counts
    tiles_per_e = (counts + tm - 1) // tm
    tile_end = jnp.cumsum(tiles_per_e)
    tile_start = tile_end - tiles_per_e
    t = jnp.arange(_n_tiles(c), dtype=I32)
    n_used = tile_end[-1]
    tc = jnp.minimum(t, n_used - 1)
    te = jnp.minimum(jnp.searchsorted(tile_end, tc, side="right"), c.E - 1).astype(I32)
    j = tc - tile_start[te]
    used = t < n_used
    rows = jnp.where(used, jnp.clip(counts[te] - j * tm, 0, tm), 0).astype(I32)
    nsub = (rows + ts - 1) // ts
    pos = jnp.where(used, start[te] + j * tm, 0).astype(I32)
    cpi = (c.nck_e + jnp.maximum(nsub, 1) - 1) // jnp.maximum(nsub, 1)
    return order_pad, (te, nsub.astype(I32), cpi.astype(I32), pos, rows)


def _n_tiles(c):
    return -(-(c.B * c.S * c.K) // c.tm_e) + c.E


def _forward(c, x, mem, norm_mix_g, w_in, b_forget, sg_ln_g, sg_ln_b, w_spatial, b_spatial,
             w_branch_a, w_branch_b, w_out, norm_x_g, norm_mem_g, w_xq, w_xkv, w_xo,
             norm_ffn_g, w_router, b_router, w_gate_up, b_gate_up, w_down, b_down, norm_final_g):
    B, S, D = x.shape
    n = B * S
    fw = c.FH * LANES
    sw = c.SG * LANES
    x2d = x.reshape(n, D)

    o_f = 3 * fw
    o_z = o_f + c.FH
    o_g = o_z + 2 * sw
    w_main = jnp.concatenate([w_in[:, o_g:], w_in[:, o_z:o_g], w_in[:, :o_f]], axis=1).astype(BF16)
    w_f = jnp.pad(w_in[:, o_f:o_z], ((0, 0), (0, LANES - c.FH))).astype(BF16)
    b_f = jnp.pad(b_forget.astype(F32), (0, LANES - c.FH)).reshape(1, LANES)
    proj, logf = _in_proj(c, x2d, norm_mix_g.reshape(1, D), w_main, w_f, b_f, LOG2E * LANES ** -0.5)

    logf_bhs = logf[:, :c.FH].reshape(B, S, c.FH).transpose(0, 2, 1)
    csum = _cumsum(c, logf_bhs)
    crow = csum.reshape(B * c.FH, 1, S)
    ccol = csum.reshape(B * c.FH, S, 1)
    qcol0 = (2 * D + 2 * sw) // LANES
    attn = _fox(c, proj, crow, ccol, qcol0)

    b_s_full = jnp.broadcast_to(b_spatial.astype(F32)[:, :, None], (c.SG, c.SGC, LANES))
    x1 = _mix(c, x2d, attn, proj, sg_ln_g.reshape(1, sw), sg_ln_b.reshape(1, sw), w_spatial, b_s_full,
              w_branch_a.astype(BF16), w_branch_b.astype(BF16), w_out.astype(BF16))

    kv = _mem_kv(c, mem.reshape(B * c.MEM, D), norm_mem_g.reshape(1, D), w_xkv.astype(BF16))
    w_r32 = jnp.pad(w_router.astype(F32), ((0, 0), (0, LANES - c.E)))
    w_r_hi = w_r32.astype(BF16)
    w_r = jnp.concatenate([w_r_hi, (w_r32 - w_r_hi.astype(F32)).astype(BF16)], axis=1)
    b_r = jnp.pad(b_router.astype(F32), (0, LANES - c.E)).reshape(1, LANES)
    x2, hfp, idx, gates, counts = _xattn(
        c, x1, norm_x_g.reshape(1, D), w_xq.astype(BF16), kv, w_xo.astype(BF16),
        norm_ffn_g.reshape(1, D), w_r, b_r, LANES ** -0.5)

    order_pad, sched = _plan(c, idx[:, :c.K], counts[0, :c.E])
    ykt = _moe(c, sched, order_pad, hfp, w_gate_up, b_gate_up.reshape(c.E, 1, 2 * c.DFF),
               w_down, b_down.reshape(c.E, 1, D))
    out = _combine(c, x2, gates, norm_final_g.reshape(1, D), ykt)
    return out.reshape(B, S, D)


_CFG = Cfg(B=4, S=4096, D=2048, MEM=256, FH=8, SG=8, SGC=128, XH=4, E=32, K=4, DFF=2048,
           tm_in=1024, tn_in=1024, tq=512, tm_mix=256, tm_x=512, tm_e=2560, ts_e=256, fc_e=512, nck_e=16, nst_e=3,
           tm_c=256, tc_cs=512, fox_parts=2)


@jax.jit
def kernel(x, mem, norm_mix_g, w_in, b_forget, sg_ln_g, sg_ln_b, w_spatial, b_spatial, w_branch_a, w_branch_b,
           w_out, norm_x_g, norm_mem_g, w_xq, w_xkv, w_xo, norm_ffn_g, w_router, b_router, w_gate_up,
           b_gate_up, w_down, b_down, norm_final_g):
    return _forward(_CFG, x, mem, norm_mix_g, w_in, b_forget, sg_ln_g, sg_ln_b, w_spatial, b_spatial,
                    w_branch_a, w_branch_b, w_out, norm_x_g, norm_mem_g, w_xq, w_xkv, w_xo,
                    norm_ffn_g, w_router, b_router, w_gate_up, b_gate_up, w_down, b_down, norm_final_g)
```

```python
import functools
from typing import NamedTuple

import jax
import jax.numpy as jnp
from jax import lax
from jax.experimental import pallas as pl
from jax.experimental.pallas import tpu as pltpu

F32 = jnp.float32
BF16 = jnp.bfloat16
U32 = jnp.uint32
I32 = jnp.int32

LANES = 128
VMEM_LIMIT = 56 * 1024 * 1024

RMS_EPS = 1e-6
LN_EPS = 1e-5
SWIGLU_LIMIT = 7.0
SWIGLU_ALPHA = 1.702
GELU_C = 0.7978845608028654
LOG2E = 1.4426950408889634
ID_CHUNK = 1024
WEIGHT_DMA_PRIORITY = 1


class Cfg(NamedTuple):
    B: int
    S: int
    D: int
    MEM: int
    FH: int
    SG: int
    SGC: int
    XH: int
    E: int
    K: int
    DFF: int
    tm_in: int
    tn_in: int
    tq: int
    tm_mix: int
    tm_x: int
    tm_e: int
    ts_e: int
    fc_e: int
    nck_e: int
    nst_e: int
    tm_c: int
    tc_cs: int
    fox_parts: int


def _cparams(sem):
    return pltpu.CompilerParams(dimension_semantics=sem, vmem_limit_bytes=VMEM_LIMIT)


def _rms(x, g):
    ms = jnp.mean(x * x, axis=-1, keepdims=True)
    return x * lax.rsqrt(ms + RMS_EPS) * g


def _pack_bf16_pair(a, b):
    def rne(v):
        bits = lax.bitcast_convert_type(v, U32)
        return bits + jnp.uint32(0x7FFF) + ((bits >> 16) & jnp.uint32(1))
    return (rne(a) >> 16) | (rne(b) & jnp.uint32(0xFFFF0000))


def _unpack_lo(p):
    return lax.bitcast_convert_type(p << 16, F32)


def _unpack_hi(p):
    return lax.bitcast_convert_type(p & jnp.uint32(0xFFFF0000), F32)


def _in_proj_kernel(x_ref, g_ref, w_ref, wf_ref, bf_ref, o_ref, f_ref, h_ref, *, nj_gate, nj_z, nj_q, qscale):
    j = pl.program_id(1)

    @pl.when(j == 0)
    def _():
        hb = _rms(x_ref[...], g_ref[...]).astype(BF16)
        h_ref[...] = hb
        f = jnp.dot(hb, wf_ref[...], preferred_element_type=F32) + bf_ref[...]
        f_ref[...] = jnp.minimum(f, 0.0) - jnp.log1p(jnp.exp(-jnp.abs(f)))

    acc = jnp.dot(h_ref[...], w_ref[...], preferred_element_type=F32)

    is_gate = j < nj_gate
    is_z = (j >= nj_gate) & (j < nj_gate + nj_z)
    is_q = (j >= nj_gate + nj_z) & (j < nj_gate + nj_z + nj_q)
    a1 = jnp.where(is_gate, 0.5, jnp.where(is_z, GELU_C, 0.0)).astype(F32)
    a3 = jnp.where(is_z, GELU_C * 0.044715, 0.0).astype(F32)
    b0 = jnp.where(is_gate, 0.5, 0.0).astype(F32)
    b1 = jnp.where(is_gate, 0.0, jnp.where(is_z, 0.5, jnp.where(is_q, qscale, 1.0))).astype(F32)
    th = jnp.tanh(acc * (a1 + a3 * (acc * acc)))
    o_ref[...] = ((b0 + b1 * acc) * (1.0 + th)).astype(o_ref.dtype)


def _in_proj(c, x2d, g, w_main, w_f, b_f, qscale):
    n, d = x2d.shape
    nc = w_main.shape[1]
    tm, tn = c.tm_in, c.tn_in
    fw = c.FH * LANES
    kern = functools.partial(_in_proj_kernel, nj_gate=2 * d // tn, nj_z=2 * c.SG * LANES // tn,
                             nj_q=fw // tn, qscale=qscale)
    return pl.pallas_call(
        kern,
        grid=(n // tm, nc // tn),
        in_specs=[
            pl.BlockSpec((tm, d), lambda i, j: (i, 0)),
            pl.BlockSpec((1, d), lambda i, j: (0, 0)),
            pl.BlockSpec((d, tn), lambda i, j: (0, j)),
            pl.BlockSpec((d, LANES), lambda i, j: (0, 0)),
            pl.BlockSpec((1, LANES), lambda i, j: (0, 0)),
        ],
        out_specs=[
            pl.BlockSpec((tm, tn), lambda i, j: (i, j)),
            pl.BlockSpec((tm, LANES), lambda i, j: (i, 0)),
        ],
        out_shape=[jax.ShapeDtypeStruct((n, nc), BF16), jax.ShapeDtypeStruct((n, LANES), F32)],
        scratch_shapes=[pltpu.VMEM((tm, d), BF16)],
        compiler_params=_cparams(("parallel", "arbitrary")),
        name="in_proj",
    )(x2d, g, w_main, w_f, b_f)


def _cumsum_kernel(f_ref, o_ref, *, tc):
    rows, s = f_ref.shape[1], f_ref.shape[2]
    r = lax.broadcasted_iota(I32, (tc, tc), 0)
    col = lax.broadcasted_iota(I32, (tc, tc), 1)
    upper = (r <= col).astype(F32)
    carry = jnp.zeros((rows, 1), F32)
    for i in range(s // tc):
        blk = f_ref[0, :, i * tc:(i + 1) * tc]
        cs = jnp.dot(blk, upper, preferred_element_type=F32, precision=lax.Precision.HIGHEST) + carry
        o_ref[0, :, i * tc:(i + 1) * tc] = cs * LOG2E
        carry = cs[:, tc - 1:tc]


def _cumsum(c, logf_bhs):
    b, h, s = logf_bhs.shape
    return pl.pallas_call(
        functools.partial(_cumsum_kernel, tc=c.tc_cs),
        grid=(b,),
        in_specs=[pl.BlockSpec((1, h, s), lambda i: (i, 0, 0))],
        out_specs=pl.BlockSpec((1, h, s), lambda i: (i, 0, 0)),
        out_shape=jax.ShapeDtypeStruct((b, h, s), F32),
        compiler_params=_cparams(("parallel",)),
        name="cumsum",
    )(logf_bhs)


def _fox_kernel(q_ref, k_ref, v_ref, crow_ref, ccol_ref, o_ref, m_ref, l_ref, acc_ref, cq_ref, s_ref,
                *, t, parts):
    qi = pl.program_id(2)
    hr = t // parts
    m_ref[...] = jnp.full(m_ref.shape, -jnp.inf, F32)
    l_ref[...] = jnp.zeros(l_ref.shape, F32)
    acc_ref[...] = jnp.zeros(acc_ref.shape, F32)
    cq_ref[...] = jnp.broadcast_to(ccol_ref[0], cq_ref.shape)

    def scores(ks):
        k = k_ref[pl.ds(ks, t), :]
        return lax.dot_general(q_ref[...], k, (((1,), (1,)), ((), ())), preferred_element_type=F32)

    def block(part, ks, width, masked):
        rows = slice(part * hr, (part + 1) * hr)
        v = v_ref[pl.ds(ks, width), :]
        s = s_ref[rows, :width]
        cq = cq_ref[rows, :]
        crow = crow_ref[0, :, pl.ds(ks, width)]
        nj = width // LANES
        sj = [s[:, j * LANES:(j + 1) * LANES] + cq - crow[:, j * LANES:(j + 1) * LANES] for j in range(nj)]
        if masked:
            row = lax.broadcasted_iota(I32, (hr, LANES), 0) + part * hr
            col = lax.broadcasted_iota(I32, (hr, LANES), 1)
            sj = [jnp.where(col + j * LANES <= row, sj[j], -jnp.inf) for j in range(nj)]
        mx = sj[0]
        for j in range(1, nj):
            mx = jnp.maximum(mx, sj[j])
        m_prev = m_ref[rows, :]
        m_next = jnp.maximum(m_prev, jnp.max(mx, axis=-1, keepdims=True))
        alpha = jnp.exp2(m_prev - m_next)
        pj = [jnp.exp2(sj[j] - m_next) for j in range(nj)]
        psum = pj[0]
        for j in range(1, nj):
            psum = psum + pj[j]
        p = jnp.concatenate([x.astype(BF16) for x in pj], axis=-1)
        l_ref[rows, :] = alpha * l_ref[rows, :] + psum
        acc_ref[rows, :] = alpha * acc_ref[rows, :] + jnp.dot(p, v, preferred_element_type=F32)
        m_ref[rows, :] = m_next

    s_ref[...] = scores(0)

    def body(kc, carry):
        ks = pl.multiple_of(kc * t, t)
        s_next = scores(pl.multiple_of(ks + t, t))
        for part in range(parts):
            block(part, ks, t, False)
        s_ref[...] = s_next
        return carry

    lax.fori_loop(0, qi, body, 0)
    kd = pl.multiple_of(qi * t, t)
    for part in range(parts):
        block(part, kd, (part + 1) * hr, True)
    l = jnp.sum(l_ref[...], axis=-1, keepdims=True)
    o_ref[...] = (acc_ref[...] / l).astype(o_ref.dtype)


def _fox(c, proj, crow, ccol, qcol0):
    n = c.B * c.S
    t = c.tq
    nq = c.S // t
    h = c.FH
    return pl.pallas_call(
        functools.partial(_fox_kernel, t=t, parts=c.fox_parts),
        grid=(c.B, h, nq),
        in_specs=[
            pl.BlockSpec((t, LANES), lambda b, hh, qi: (b * nq + qi, qcol0 + hh)),
            pl.BlockSpec((c.S, LANES), lambda b, hh, qi: (b, qcol0 + h + hh)),
            pl.BlockSpec((c.S, LANES), lambda b, hh, qi: (b, qcol0 + 2 * h + hh)),
            pl.BlockSpec((1, 1, c.S), lambda b, hh, qi: (b * h + hh, 0, 0)),
            pl.BlockSpec((1, t, 1), lambda b, hh, qi: (b * h + hh, qi, 0)),
        ],
        out_specs=pl.BlockSpec((t, LANES), lambda b, hh, qi: (b * nq + qi, hh)),
        out_shape=jax.ShapeDtypeStruct((n, h * LANES), BF16),
        scratch_shapes=[pltpu.VMEM((t, LANES), F32)] * 4 + [pltpu.VMEM((t, t), F32)],
        compiler_params=_cparams(("parallel", "parallel", "arbitrary")),
        name="fox",
    )(proj, proj, proj, crow, ccol)


def _mix_kernel(x_ref, a_ref, u_ref, v_ref, ga_ref, gb_ref, lng_ref, lnb_ref, ws_ref, bs_ref,
                wa_ref, wb_ref, wo_ref, o_ref, sg_ref, *, sgc, groups):
    tm = x_ref.shape[0]
    v = v_ref[...].astype(F32)
    mu = jnp.mean(v, axis=-1, keepdims=True)
    vc = v - mu
    var = jnp.mean(vc * vc, axis=-1, keepdims=True)
    vn = (vc * lax.rsqrt(var + LN_EPS) * lng_ref[...] + lnb_ref[...]).astype(BF16)
    row = lax.broadcasted_iota(I32, (sgc, sgc), 0)
    col = lax.broadcasted_iota(I32, (sgc, sgc), 1)
    for g in range(groups):
        w = jnp.where(col <= row, ws_ref[g], 0.0).astype(BF16)
        bias = bs_ref[g]
        for ci in range(tm // sgc):
            rs = slice(ci * sgc, (ci + 1) * sgc)
            cs = slice(g * LANES, (g + 1) * LANES)
            mixed = jnp.dot(w, vn[rs, cs], preferred_element_type=F32) + bias
            sg_ref[rs, cs] = (u_ref[rs, cs].astype(F32) * mixed).astype(BF16)
    ya = jnp.dot(a_ref[...], wa_ref[...], preferred_element_type=F32)
    yb = jnp.dot(sg_ref[...], wb_ref[...], preferred_element_type=F32)
    merged = (ga_ref[...].astype(F32) * ya + gb_ref[...].astype(F32) * yb).astype(BF16)
    o_ref[...] = x_ref[...] + jnp.dot(merged, wo_ref[...], preferred_element_type=F32)


def _const_spec(shape):
    nd = len(shape)
    return pl.BlockSpec(shape, lambda i: (0,) * nd, pipeline_mode=pl.Buffered(1))


def _mix(c, x2d, attn, proj, ln_g, ln_b, w_s, b_s_full, w_a, w_b, w_o):
    n, d = x2d.shape
    tm = c.tm_mix
    fw = c.FH * LANES
    sw = c.SG * LANES
    ucol = 2 * d // sw
    return pl.pallas_call(
        functools.partial(_mix_kernel, sgc=c.SGC, groups=c.SG),
        grid=(n // tm,),
        in_specs=[
            pl.BlockSpec((tm, d), lambda i: (i, 0)),
            pl.BlockSpec((tm, fw), lambda i: (i, 0)),
            pl.BlockSpec((tm, sw), lambda i: (i, ucol)),
            pl.BlockSpec((tm, sw), lambda i: (i, ucol + 1)),
            pl.BlockSpec((tm, d), lambda i: (i, 0)),
            pl.BlockSpec((tm, d), lambda i: (i, 1)),
            _const_spec((1, sw)),
            _const_spec((1, sw)),
            _const_spec((c.SG, c.SGC, c.SGC)),
            _const_spec((c.SG, c.SGC, LANES)),
            _const_spec((fw, d)),
            _const_spec((sw, d)),
            _const_spec((d, d)),
        ],
        out_specs=pl.BlockSpec((tm, d), lambda i: (i, 0)),
        out_shape=jax.ShapeDtypeStruct((n, d), F32),
        scratch_shapes=[pltpu.VMEM((tm, sw), BF16)],
        compiler_params=_cparams(("parallel",)),
        name="mix",
    )(x2d, attn, proj, proj, proj, proj, ln_g, ln_b, w_s, b_s_full, w_a, w_b, w_o)


def _mem_kv_kernel(m_ref, g_ref, w_ref, o_ref):
    hm = _rms(m_ref[...], g_ref[...]).astype(BF16)
    o_ref[...] = jnp.dot(hm, w_ref[...], preferred_element_type=F32).astype(o_ref.dtype)


def _mem_kv(c, mem2d, g, w_xkv):
    n, d = mem2d.shape
    nc = w_xkv.shape[1]
    tm = c.MEM
    return pl.pallas_call(
        _mem_kv_kernel,
        grid=(n // tm,),
        in_specs=[pl.BlockSpec((tm, d), lambda i: (i, 0)), _const_spec((1, d)), _const_spec((d, nc))],
        out_specs=pl.BlockSpec((tm, nc), lambda i: (i, 0)),
        out_shape=jax.ShapeDtypeStruct((n, nc), BF16),
        compiler_params=_cparams(("parallel",)),
        name="mem_kv",
    )(mem2d, g, w_xkv)


def _xattn_kernel(x1_ref, gx_ref, wq_ref, kv_ref, wo_ref, gf_ref, wr_ref, br_ref,
                  x2_ref, hfp_ref, idx_ref, gate_ref, cnt_ref, carry_ref,
                  *, heads, n_exp, topk, qscale):
    i = pl.program_id(0)
    tm, d = x1_ref.shape
    xw = heads * LANES
    x1 = x1_ref[...]
    hx = _rms(x1, gx_ref[...]).astype(BF16)
    q = (jnp.dot(hx, wq_ref[...], preferred_element_type=F32) * qscale).astype(BF16)
    outs = []
    for h in range(heads):
        k = kv_ref[:, h * LANES:(h + 1) * LANES]
        v = kv_ref[:, xw + h * LANES:xw + (h + 1) * LANES]
        s = lax.dot_general(q[:, h * LANES:(h + 1) * LANES], k, (((1,), (1,)), ((), ())),
                            preferred_element_type=F32)
        s = s - jnp.max(s, axis=-1, keepdims=True)
        p = jnp.exp(s)
        p = p / jnp.sum(p, axis=-1, keepdims=True)
        outs.append(jnp.dot(p.astype(BF16), v, preferred_element_type=F32).astype(BF16))
    o = jnp.concatenate(outs, axis=-1)
    x2 = x1 + jnp.dot(o, wo_ref[...], preferred_element_type=F32)
    x2_ref[...] = x2

    hf = _rms(x2, gf_ref[...])
    hfp_ref[...] = _pack_bf16_pair(hf[:, :d // 2], hf[:, d // 2:])

    h_hi = hf.astype(BF16)
    h_lo = (hf - h_hi.astype(F32)).astype(BF16)
    l_hi = jnp.dot(h_hi, wr_ref[...], preferred_element_type=F32)
    l_lo = jnp.dot(h_lo, wr_ref[:, :LANES], preferred_element_type=F32)
    logits = l_hi[:, :LANES] + l_hi[:, LANES:] + l_lo + br_ref[...]
    lane = lax.broadcasted_iota(I32, (tm, LANES), 1)
    lg = jnp.where(lane < n_exp, logits, -jnp.inf)
    vals, idxs = [], []
    for _ in range(topk):
        m = jnp.max(lg, axis=-1, keepdims=True)
        ix = jnp.min(jnp.where(lg == m, lane, LANES), axis=-1, keepdims=True)
        vals.append(m)
        idxs.append(ix)
        lg = jnp.where(lane == ix, -jnp.inf, lg)
    es = [jnp.exp(vv - vals[0]) for vv in vals]
    denom = es[0]
    for e in es[1:]:
        denom = denom + e

    @pl.when(i == 0)
    def _():
        carry_ref[...] = jnp.zeros(carry_ref.shape, F32)

    onehot = jnp.zeros((tm, LANES), F32)
    for ix in idxs:
        onehot = onehot + (lane == ix).astype(F32)
    idx_out = jnp.zeros((tm, LANES), I32)
    gate_out = jnp.zeros((tm, LANES), F32)
    for kk in range(topk):
        idx_out = jnp.where(lane == kk, idxs[kk], idx_out)
        gate_out = jnp.where(lane == kk, es[kk] / denom, gate_out)
    idx_ref[...] = idx_out
    gate_ref[...] = gate_out
    carry_ref[...] = carry_ref[...] + jnp.sum(onehot, axis=0, keepdims=True)
    cnt_ref[...] = carry_ref[...]


def _xattn(c, x1, gx, w_xq, kv, w_xo, gf, w_r, b_r, qscale):
    n, d = x1.shape
    tm = c.tm_x
    xw = c.XH * LANES
    per_b = c.S // tm
    kern = functools.partial(_xattn_kernel, heads=c.XH, n_exp=c.E, topk=c.K, qscale=qscale)
    return pl.pallas_call(
        kern,
        grid=(n // tm,),
        in_specs=[
            pl.BlockSpec((tm, d), lambda i: (i, 0)),
            _const_spec((1, d)),
            _const_spec((d, xw)),
            pl.BlockSpec((c.MEM, 2 * xw), lambda i: (i // per_b, 0)),
            _const_spec((xw, d)),
            _const_spec((1, d)),
            _const_spec((d, 2 * LANES)),
            _const_spec((1, LANES)),
        ],
        out_specs=[
            pl.BlockSpec((tm, d), lambda i: (i, 0)),
            pl.BlockSpec((tm, d // 2), lambda i: (i, 0)),
            pl.BlockSpec((tm, LANES), lambda i: (i, 0)),
            pl.BlockSpec((tm, LANES), lambda i: (i, 0)),
            pl.BlockSpec((1, LANES), lambda i: (0, 0)),
        ],
        out_shape=[
            jax.ShapeDtypeStruct((n, d), F32),
            jax.ShapeDtypeStruct((n, d // 2), U32),
            jax.ShapeDtypeStruct((n, LANES), I32),
            jax.ShapeDtypeStruct((n, LANES), F32),
            jax.ShapeDtypeStruct((1, LANES), F32),
        ],
        scratch_shapes=[pltpu.VMEM((1, LANES), F32)],
        compiler_params=_cparams(("arbitrary",)),
        name="xattn_router",
    )(x1, gx, w_xq, kv, w_xo, gf, w_r, b_r)


def _moe_kernel(te_ref, ns_ref, cpi_ref, pos_ref, rows_ref, ord_ref, hfp_ref, wgu_ref, wd_ref, bgu_ref, bd_ref,
                ykt_ref, wgub, wdb, stage, act, xin, ybuf, ids, wsem, xsem, ysem, isem,
                *, ts, tm, nck, fc, n_tok, n_asg):
    i = pl.program_id(0)
    n = pl.num_programs(0)
    nsub = ns_ref[i]
    e = te_ref[i]
    cpi = cpi_ref[i]
    rows = rows_ref[i]
    d, f2 = wgub.shape
    f = f2 // 2
    dh = d // 2
    ckr = d // nck
    ckd = f // nck
    nst = stage.shape[0]
    idw = ids.shape[0] // 2
    nq = idw // ID_CHUNK
    id_base = (i % 2) * idw + (pos_ref[i] & (ID_CHUNK - 1))

    def ids_copy(item, q, slot):
        c0 = lax.shift_right_logical(pos_ref[item], ID_CHUNK.bit_length() - 1)
        src = ord_ref.at[pl.ds(pl.multiple_of((c0 + q) * ID_CHUNK, ID_CHUNK), ID_CHUNK)]
        dst = ids.at[pl.ds(pl.multiple_of(slot * idw + q * ID_CHUNK, ID_CHUNK), ID_CHUNK)]
        return pltpu.make_async_copy(src, dst, isem.at[slot])

    def token_of(a):
        return a & (n_tok - 1) if n_tok & (n_tok - 1) == 0 else lax.rem(a, n_tok)

    def gather(s, slot, r0=0, r1=ts):
        for r in range(r0, r1):
            tok = token_of(ids[id_base + s * ts + r])
            pltpu.make_async_copy(hfp_ref.at[pl.ds(tok, 1)], xin.at[slot, pl.ds(r, 1)], xsem.at[slot]).start()

    def gather_wait(slot):
        for r in range(ts):
            pltpu.make_async_copy(hfp_ref.at[pl.ds(0, 1)], xin.at[slot, pl.ds(0, 1)], xsem.at[slot]).wait()

    def scatter(s, slot, r0=0, r1=ts):
        for r in range(r0, r1):
            g = s * ts + r
            dst = jnp.where(g < rows, ids[id_base + g], n_asg + slot * ts + r)
            pltpu.make_async_copy(ybuf.at[slot, pl.ds(r, 1)], ykt_ref.at[pl.ds(dst, 1)], ysem.at[slot]).start()

    def scatter_wait(slot):
        for r in range(ts):
            pltpu.make_async_copy(ybuf.at[slot, pl.ds(0, 1)], ykt_ref.at[pl.ds(0, 1)], ysem.at[slot]).wait()

    def gu_copy(ee, c, slot):
        return pltpu.make_async_copy(wgu_ref.at[ee, pl.ds(pl.multiple_of(c * ckr, ckr), ckr), :],
                                     stage.at[slot], wsem.at[slot])

    def d_copy(ee, c, slot):
        return pltpu.make_async_copy(wd_ref.at[ee, pl.ds(pl.multiple_of(c * ckd, ckd), ckd), :],
                                     stage.at[slot, pl.ds(0, ckd), pl.ds(0, d)], wsem.at[slot])

    def gu_step(ee, c):
        slot = c % nst
        gu_copy(ee, c, slot).wait()
        wgub[pl.ds(pl.multiple_of(c * ckr, ckr), ckr), :] = stage[slot].astype(BF16)

        @pl.when(c + nst < nck)
        def _():
            gu_copy(ee, c + nst, slot).start(priority=WEIGHT_DMA_PRIORITY)

    def d_step(ee, c):
        slot = c % nst
        d_copy(ee, c, slot).wait()
        wdb[pl.ds(pl.multiple_of(c * ckd, ckd), ckd), :] = stage[slot, :ckd, :d].astype(BF16)

        @pl.when(c + nst < nck)
        def _():
            d_copy(ee, c + nst, slot).start(priority=WEIGHT_DMA_PRIORITY)

    def prime(copy_fn, ee):
        for q in range(min(nst, nck)):
            copy_fn(ee, q, q).start(priority=WEIGHT_DMA_PRIORITY)

    def steps_after(s, step_fn, ee):
        def one(jj, carry):
            c = s * cpi + jj

            @pl.when(c < nck)
            def _():
                step_fn(ee, c)
            return carry
        lax.fori_loop(0, cpi, one, 0)

    @pl.when(nsub > 0)
    def _():
        nxt = jnp.minimum(i + 1, n - 1)
        has_next = (i + 1 < n) & (ns_ref[nxt] > 0)

        @pl.when(i == 0)
        def _():
            for q in range(nq):
                ids_copy(0, q, 0).start()
            prime(gu_copy, e)

            def first(c, carry):
                gu_step(e, c)
                return carry
            lax.fori_loop(0, nck, first, 0)

        for q in range(nq):
            ids_copy(i, q, i % 2).wait()

        @pl.when(has_next)
        def _():
            for q in range(nq):
                ids_copy(nxt, q, nxt % 2).start()

        prime(d_copy, e)
        gather(0, 0)
        bgu = bgu_ref[e]

        def phase_a(s, carry):
            slot = s % 2
            gather_wait(slot)
            p = xin[slot]
            xb = jnp.concatenate([_unpack_lo(p).astype(BF16), _unpack_hi(p).astype(BF16)], axis=-1)
            nc = f // fc
            for cc in range(nc):
                gather(s + 1, 1 - slot, cc * ts // nc, (cc + 1) * ts // nc)
                gate = jnp.dot(xb, wgub[:, cc * fc:(cc + 1) * fc], preferred_element_type=F32)
                gate = gate + bgu[:, cc * fc:(cc + 1) * fc]
                up = jnp.dot(xb, wgub[:, f + cc * fc:f + (cc + 1) * fc], preferred_element_type=F32)
                up = up + bgu[:, f + cc * fc:f + (cc + 1) * fc]
                gate = jnp.minimum(gate, SWIGLU_LIMIT)
                up = jnp.clip(up, -SWIGLU_LIMIT, SWIGLU_LIMIT)
                a = (up + 1.0) * gate * jax.nn.sigmoid(SWIGLU_ALPHA * gate)
                act[s, :, cc * fc:(cc + 1) * fc] = a.astype(BF16)
            steps_after(s, d_step, e)
            return carry

        lax.fori_loop(0, nsub, phase_a, 0)
        gather_wait(nsub % 2)

        e_next = te_ref[nxt]
        load_next = has_next & (e_next != e)

        @pl.when(load_next)
        def _():
            prime(gu_copy, e_next)

        bd = bd_ref[e]

        def down(s, slot, scatter_prev):
            a = act[s]
            if scatter_prev:
                scatter(s - 1, 1 - slot)
            for cc in range(dh // fc):
                lo = slice(cc * fc, (cc + 1) * fc)
                hi = slice(dh + cc * fc, dh + (cc + 1) * fc)
                y_lo = jnp.dot(a, wdb[:, lo], preferred_element_type=F32) + bd[:, lo]
                y_hi = jnp.dot(a, wdb[:, hi], preferred_element_type=F32) + bd[:, hi]
                ybuf[slot, :, lo] = _pack_bf16_pair(y_lo, y_hi)

        def next_weights(s):
            @pl.when(load_next)
            def _():
                steps_after(s, gu_step, e_next)

        down(0, 0, False)
        next_weights(0)

        def phase_b(s, carry):
            slot = s % 2

            @pl.when(s >= 2)
            def _():
                scatter_wait(slot)

            down(s, slot, True)
            next_weights(s)
            return carry

        lax.fori_loop(1, nsub, phase_b, 0)
        scatter(nsub - 1, (nsub - 1) % 2)
        scatter_wait((nsub - 1) % 2)

        @pl.when(nsub >= 2)
        def _():
            scatter_wait(nsub % 2)


def _id_window(c):
    return ((ID_CHUNK - 1 + c.tm_e + c.ts_e) // ID_CHUNK + 1) * ID_CHUNK


def _moe(c, sched, order_pad, hfp, w_gu, b_gu3, w_d, b_d3):
    n_tok, dh = hfp.shape
    d = 2 * dh
    f = c.DFF
    tm, ts, nck = c.tm_e, c.ts_e, c.nck_e
    n_asg = n_tok * c.K
    grid_spec = pltpu.PrefetchScalarGridSpec(
        num_scalar_prefetch=5,
        grid=(_n_tiles(c),),
        in_specs=[
            pl.BlockSpec(memory_space=pl.ANY),
            pl.BlockSpec(memory_space=pl.ANY),
            pl.BlockSpec(memory_space=pl.ANY),
            pl.BlockSpec(memory_space=pl.ANY),
            pl.BlockSpec((c.E, 1, 2 * f), lambda i, *_: (0, 0, 0), pipeline_mode=pl.Buffered(1)),
            pl.BlockSpec((c.E, 1, d), lambda i, *_: (0, 0, 0), pipeline_mode=pl.Buffered(1)),
        ],
        out_specs=pl.BlockSpec(memory_space=pl.ANY),
        scratch_shapes=[
            pltpu.VMEM((d, 2 * f), BF16),
            pltpu.VMEM((f, d), BF16),
            pltpu.VMEM((c.nst_e, d // nck, 2 * f), F32),
            pltpu.VMEM((tm // ts, ts, f), BF16),
            pltpu.VMEM((2, ts, dh), U32),
            pltpu.VMEM((2, ts, dh), U32),
            pltpu.SMEM((2 * _id_window(c),), I32),
            pltpu.SemaphoreType.DMA((c.nst_e,)),
            pltpu.SemaphoreType.DMA((2,)),
            pltpu.SemaphoreType.DMA((2,)),
            pltpu.SemaphoreType.DMA((2,)),
        ],
    )
    kern = functools.partial(_moe_kernel, ts=ts, tm=tm, nck=nck, fc=c.fc_e, n_tok=n_tok, n_asg=n_asg)
    return pl.pallas_call(
        kern,
        grid_spec=grid_spec,
        out_shape=jax.ShapeDtypeStruct((n_asg + 2 * ts, dh), U32),
        compiler_params=_cparams(("arbitrary",)),
        name="experts",
    )(*sched, order_pad, hfp, w_gu, w_d, b_gu3, b_d3)


def _combine_kernel(x2_ref, gate_ref, g_ref, *refs, topk):
    y_refs, o_ref = refs[:topk], refs[topk]
    d = x2_ref.shape[1]
    dh = d // 2
    lo = x2_ref[:, :dh]
    hi = x2_ref[:, dh:]
    gates = gate_ref[...]
    for kk in range(topk):
        wk = gates[:, kk:kk + 1]
        p = y_refs[kk][...]
        lo = lo + wk * _unpack_lo(p)
        hi = hi + wk * _unpack_hi(p)
    ms = (jnp.sum(lo * lo, axis=-1, keepdims=True) + jnp.sum(hi * hi, axis=-1, keepdims=True)) / d
    inv = lax.rsqrt(ms + RMS_EPS)
    o_ref[:, :dh] = lo * inv * g_ref[:, :dh]
    o_ref[:, dh:] = hi * inv * g_ref[:, dh:]


def _combine(c, x2, gates, g, ykt):
    n, d = x2.shape
    tm = c.tm_c
    nb = n // tm
    y_specs = [pl.BlockSpec((tm, d // 2), functools.partial(lambda i, kk: (kk * nb + i, 0), kk=kk))
               for kk in range(c.K)]
    return pl.pallas_call(
        functools.partial(_combine_kernel, topk=c.K),
        grid=(nb,),
        in_specs=[
            pl.BlockSpec((tm, d), lambda i: (i, 0)),
            pl.BlockSpec((tm, LANES), lambda i: (i, 0)),
            pl.BlockSpec((1, d), lambda i: (0, 0)),
        ] + y_specs,
        out_specs=pl.BlockSpec((tm, d), lambda i: (i, 0)),
        out_shape=jax.ShapeDtypeStruct((n, d), F32),
        compiler_params=_cparams(("parallel",)),
        name="combine",
    )(x2, gates, g, *([ykt] * c.K))


def _plan(c, idx, counts):
    tm, ts = c.tm_e, c.ts_e
    nk = idx.shape[0] * c.K
    order = jnp.argsort(idx.T.reshape(-1)).astype(I32)
    pad = _id_window(c) + (-nk) % ID_CHUNK
    order_pad = jnp.concatenate([order, jnp.zeros((pad,), I32)])
    counts = counts.astype(I32)
    start = jnp.cumsum(counts) - counts
    tiles_per_e = (counts + tm - 1) // tm
    tile_end = jnp.cumsum(tiles_per_e)
    tile_start = tile_end - tiles_per_e
    t = jnp.arange(_n_tiles(c), dtype=I32)
    n_used = tile_end[-1]
    tc = jnp.minimum(t, n_used - 1)
    te = jnp.minimum(jnp.searchsorted(tile_end, tc, side="right"), c.E - 1).astype(I32)
    j = tc - tile_start[te]
    used = t < n_used
    rows = jnp.where(used, jnp.clip(counts[te] - j * tm, 0, tm), 0).astype(I32)
    nsub = (rows + ts - 1) // ts
    pos = jnp.where(used, start[te] + j * tm, 0).astype(I32)
    cpi = (c.nck_e + jnp.maximum(nsub, 1) - 1) // jnp.maximum(nsub, 1)
    return order_pad, (te, nsub.astype(I32), cpi.astype(I32), pos, rows)


def _n_tiles(c):
    return -(-(c.B * c.S * c.K) // c.tm_e) + c.E


def _forward(c, x, mem, norm_mix_g, w_in, b_forget, sg_ln_g, sg_ln_b, w_spatial, b_spatial,
             w_branch_a, w_branch_b, w_out, norm_x_g, norm_mem_g, w_xq, w_xkv, w_xo,
             norm_ffn_g, w_router, b_router, w_gate_up, b_gate_up, w_down, b_down, norm_final_g):
    B, S, D = x.shape
    n = B * S
    fw = c.FH * LANES
    sw = c.SG * LANES
    x2d = x.reshape(n, D)

    o_f = 3 * fw
    o_z = o_f + c.FH
    o_g = o_z + 2 * sw
    w_main = jnp.concatenate([w_in[:, o_g:], w_in[:, o_z:o_g], w_in[:, :o_f]], axis=1).astype(BF16)
    w_f = jnp.pad(w_in[:, o_f:o_z], ((0, 0), (0, LANES - c.FH))).astype(BF16)
    b_f = jnp.pad(b_forget.astype(F32), (0, LANES - c.FH)).reshape(1, LANES)
    proj, logf = _in_proj(c, x2d, norm_mix_g.reshape(1, D), w_main, w_f, b_f, LOG2E * LANES ** -0.5)

    logf_bhs = logf[:, :c.FH].reshape(B, S, c.FH).transpose(0, 2, 1)
    csum = _cumsum(c, logf_bhs)
    crow = csum.reshape(B * c.FH, 1, S)
    ccol = csum.reshape(B * c.FH, S, 1)
    qcol0 = (2 * D + 2 * sw) // LANES
    attn = _fox(c, proj, crow, ccol, qcol0)

    b_s_full = jnp.broadcast_to(b_spatial.astype(F32)[:, :, None], (c.SG, c.SGC, LANES))
    x1 = _mix(c, x2d, attn, proj, sg_ln_g.reshape(1, sw), sg_ln_b.reshape(1, sw), w_spatial, b_s_full,
              w_branch_a.astype(BF16), w_branch_b.astype(BF16), w_out.astype(BF16))

    kv = _mem_kv(c, mem.reshape(B * c.MEM, D), norm_mem_g.reshape(1, D), w_xkv.astype(BF16))
    w_r32 = jnp.pad(w_router.astype(F32), ((0, 0), (0, LANES - c.E)))
    w_r_hi = w_r32.astype(BF16)
    w_r = jnp.concatenate([w_r_hi, (w_r32 - w_r_hi.astype(F32)).astype(BF16)], axis=1)
    b_r = jnp.pad(b_router.astype(F32), (0, LANES - c.E)).reshape(1, LANES)
    x2, hfp, idx, gates, counts = _xattn(
        c, x1, norm_x_g.reshape(1, D), w_xq.astype(BF16), kv, w_xo.astype(BF16),
        norm_ffn_g.reshape(1, D), w_r, b_r, LANES ** -0.5)

    order_pad, sched = _plan(c, idx[:, :c.K], counts[0, :c.E])
    ykt = _moe(c, sched, order_pad, hfp, w_gate_up, b_gate_up.reshape(c.E, 1, 2 * c.DFF),
               w_down, b_down.reshape(c.E, 1, D))
    out = _combine(c, x2, gates, norm_final_g.reshape(1, D), ykt)
    return out.reshape(B, S, D)


_CFG = Cfg(B=4, S=4096, D=2048, MEM=256, FH=8, SG=8, SGC=128, XH=4, E=32, K=4, DFF=2048,
           tm_in=1024, tn_in=1024, tq=1024, tm_mix=256, tm_x=512, tm_e=2560, ts_e=256, fc_e=512, nck_e=16, nst_e=3,
           tm_c=256, tc_cs=512, fox_parts=2)


@jax.jit
def kernel(x, mem, norm_mix_g, w_in, b_forget, sg_ln_g, sg_ln_b, w_spatial, b_spatial, w_branch_a, w_branch_b,
           w_out, norm_x_g, norm_mem_g, w_xq, w_xkv, w_xo, norm_ffn_g, w_router, b_router, w_gate_up,
           b_gate_up, w_down, b_down, norm_final_g):
    return _forward(_CFG, x, mem, norm_mix_g, w_in, b_forget, sg_ln_g, sg_ln_b, w_spatial, b_spatial,
                    w_branch_a, w_branch_b, w_out, norm_x_g, norm_mem_g, w_xq, w_xkv, w_xo,
                    norm_ffn_g, w_router, b_router, w_gate_up, b_gate_up, w_down, b_down, norm_final_g)
```

```python
import functools
from typing import NamedTuple

import jax
import jax.numpy as jnp
from jax import lax
from jax.experimental import pallas as pl
from jax.experimental.pallas import tpu as pltpu

F32 = jnp.float32
BF16 = jnp.bfloat16
U32 = jnp.uint32
I32 = jnp.int32

LANES = 128
VMEM_LIMIT = 56 * 1024 * 1024

RMS_EPS = 1e-6
LN_EPS = 1e-5
SWIGLU_LIMIT = 7.0
SWIGLU_ALPHA = 1.702
GELU_C = 0.7978845608028654
LOG2E = 1.4426950408889634
ID_CHUNK = 1024
WEIGHT_DMA_PRIORITY = 1


class Cfg(NamedTuple):
    B: int
    S: int
    D: int
    MEM: int
    FH: int
    SG: int
    SGC: int
    XH: int
    E: int
    K: int
    DFF: int
    tm_in: int
    tn_in: int
    tq: int
    tm_mix: int
    tm_x: int
    tm_e: int
    ts_e: int
    fc_e: int
    nck_e: int
    nst_e: int
    tm_c: int
    tc_cs: int
    fox_parts: int


def _cparams(sem):
    return pltpu.CompilerParams(dimension_semantics=sem, vmem_limit_bytes=VMEM_LIMIT)


def _rms(x, g):
    ms = jnp.mean(x * x, axis=-1, keepdims=True)
    return x * lax.rsqrt(ms + RMS_EPS) * g


def _pack_bf16_pair(a, b):
    def rne(v):
        bits = lax.bitcast_convert_type(v, U32)
        return bits + jnp.uint32(0x7FFF) + ((bits >> 16) & jnp.uint32(1))
    return (rne(a) >> 16) | (rne(b) & jnp.uint32(0xFFFF0000))


def _unpack_lo(p):
    return lax.bitcast_convert_type(p << 16, F32)


def _unpack_hi(p):
    return lax.bitcast_convert_type(p & jnp.uint32(0xFFFF0000), F32)


def _in_proj_kernel(x_ref, g_ref, w_ref, wf_ref, bf_ref, o_ref, f_ref, h_ref, *, nj_gate, nj_z, nj_q, qscale):
    j = pl.program_id(1)

    @pl.when(j == 0)
    def _():
        hb = _rms(x_ref[...], g_ref[...]).astype(BF16)
        h_ref[...] = hb
        f = jnp.dot(hb, wf_ref[...], preferred_element_type=F32) + bf_ref[...]
        f_ref[...] = jnp.minimum(f, 0.0) - jnp.log1p(jnp.exp(-jnp.abs(f)))

    acc = jnp.dot(h_ref[...], w_ref[...], preferred_element_type=F32)

    is_gate = j < nj_gate
    is_z = (j >= nj_gate) & (j < nj_gate + nj_z)
    is_q = (j >= nj_gate + nj_z) & (j < nj_gate + nj_z + nj_q)
    a1 = jnp.where(is_gate, 0.5, jnp.where(is_z, GELU_C, 0.0)).astype(F32)
    a3 = jnp.where(is_z, GELU_C * 0.044715, 0.0).astype(F32)
    b0 = jnp.where(is_gate, 0.5, 0.0).astype(F32)
    b1 = jnp.where(is_gate, 0.0, jnp.where(is_z, 0.5, jnp.where(is_q, qscale, 1.0))).astype(F32)
    th = jnp.tanh(acc * (a1 + a3 * (acc * acc)))
    o_ref[...] = ((b0 + b1 * acc) * (1.0 + th)).astype(o_ref.dtype)


def _in_proj(c, x2d, g, w_main, w_f, b_f, qscale):
    n, d = x2d.shape
    nc = w_main.shape[1]
    tm, tn = c.tm_in, c.tn_in
    fw = c.FH * LANES
    kern = functools.partial(_in_proj_kernel, nj_gate=2 * d // tn, nj_z=2 * c.SG * LANES // tn,
                             nj_q=fw // tn, qscale=qscale)
    return pl.pallas_call(
        kern,
        grid=(n // tm, nc // tn),
        in_specs=[
            pl.BlockSpec((tm, d), lambda i, j: (i, 0)),
            pl.BlockSpec((1, d), lambda i, j: (0, 0)),
            pl.BlockSpec((d, tn), lambda i, j: (0, j)),
            pl.BlockSpec((d, LANES), lambda i, j: (0, 0)),
            pl.BlockSpec((1, LANES), lambda i, j: (0, 0)),
        ],
        out_specs=[
            pl.BlockSpec((tm, tn), lambda i, j: (i, j)),
            pl.BlockSpec((tm, LANES), lambda i, j: (i, 0)),
        ],
        out_shape=[jax.ShapeDtypeStruct((n, nc), BF16), jax.ShapeDtypeStruct((n, LANES), F32)],
        scratch_shapes=[pltpu.VMEM((tm, d), BF16)],
        compiler_params=_cparams(("parallel", "arbitrary")),
        name="in_proj",
    )(x2d, g, w_main, w_f, b_f)


def _cumsum_kernel(f_ref, o_ref, *, tc):
    rows, s = f_ref.shape[1], f_ref.shape[2]
    r = lax.broadcasted_iota(I32, (tc, tc), 0)
    col = lax.broadcasted_iota(I32, (tc, tc), 1)
    upper = (r <= col).astype(F32)
    carry = jnp.zeros((rows, 1), F32)
    for i in range(s // tc):
        blk = f_ref[0, :, i * tc:(i + 1) * tc]
        cs = jnp.dot(blk, upper, preferred_element_type=F32, precision=lax.Precision.HIGHEST) + carry
        o_ref[0, :, i * tc:(i + 1) * tc] = cs * LOG2E
        carry = cs[:, tc - 1:tc]


def _cumsum(c, logf_bhs):
    b, h, s = logf_bhs.shape
    return pl.pallas_call(
        functools.partial(_cumsum_kernel, tc=c.tc_cs),
        grid=(b,),
        in_specs=[pl.BlockSpec((1, h, s), lambda i: (i, 0, 0))],
        out_specs=pl.BlockSpec((1, h, s), lambda i: (i, 0, 0)),
        out_shape=jax.ShapeDtypeStruct((b, h, s), F32),
        compiler_params=_cparams(("parallel",)),
        name="cumsum",
    )(logf_bhs)


def _fox_kernel(q_ref, k_ref, v_ref, crow_ref, ccol_ref, o_ref, m_ref, l_ref, acc_ref, cq_ref, s_ref,
                *, t, parts):
    qi = pl.program_id(2)
    hr = t // parts
    m_ref[...] = jnp.full(m_ref.shape, -jnp.inf, F32)
    l_ref[...] = jnp.zeros(l_ref.shape, F32)
    acc_ref[...] = jnp.zeros(acc_ref.shape, F32)
    cq_ref[...] = jnp.broadcast_to(ccol_ref[0], cq_ref.shape)

    def scores(ks):
        k = k_ref[pl.ds(ks, t), :]
        return lax.dot_general(q_ref[...], k, (((1,), (1,)), ((), ())), preferred_element_type=F32)

    def block(part, ks, width, masked):
        rows = slice(part * hr, (part + 1) * hr)
        v = v_ref[pl.ds(ks, width), :]
        s = s_ref[rows, :width]
        cq = cq_ref[rows, :]
        crow = crow_ref[0, :, pl.ds(ks, width)]
        nj = width // LANES
        sj = [s[:, j * LANES:(j + 1) * LANES] + cq - crow[:, j * LANES:(j + 1) * LANES] for j in range(nj)]
        if masked:
            row = lax.broadcasted_iota(I32, (hr, LANES), 0) + part * hr
            col = lax.broadcasted_iota(I32, (hr, LANES), 1)
            sj = [jnp.where(col + j * LANES <= row, sj[j], -jnp.inf) for j in range(nj)]
        mx = sj[0]
        for j in range(1, nj):
            mx = jnp.maximum(mx, sj[j])
        m_prev = m_ref[rows, :]
        m_next = jnp.maximum(m_prev, jnp.max(mx, axis=-1, keepdims=True))
        alpha = jnp.exp2(m_prev - m_next)
        pj = [jnp.exp2(sj[j] - m_next) for j in range(nj)]
        psum = pj[0]
        for j in range(1, nj):
            psum = psum + pj[j]
        p = jnp.concatenate([x.astype(BF16) for x in pj], axis=-1)
        l_ref[rows, :] = alpha * l_ref[rows, :] + psum
        acc_ref[rows, :] = alpha * acc_ref[rows, :] + jnp.dot(p, v, preferred_element_type=F32)
        m_ref[rows, :] = m_next

    s_ref[...] = scores(0)

    def body(kc, carry):
        ks = pl.multiple_of(kc * t, t)
        s_next = scores(pl.multiple_of(ks + t, t))
        for part in range(parts):
            block(part, ks, t, False)
        s_ref[...] = s_next
        return carry

    lax.fori_loop(0, qi, body, 0)
    kd = pl.multiple_of(qi * t, t)
    for part in range(parts):
        block(part, kd, (part + 1) * hr, True)
    l = jnp.sum(l_ref[...], axis=-1, keepdims=True)
    o_ref[...] = (acc_ref[...] / l).astype(o_ref.dtype)


def _fox(c, proj, crow, ccol, qcol0):
    n = c.B * c.S
    t = c.tq
    nq = c.S // t
    h = c.FH
    return pl.pallas_call(
        functools.partial(_fox_kernel, t=t, parts=c.fox_parts),
        grid=(c.B, h, nq),
        in_specs=[
            pl.BlockSpec((t, LANES), lambda b, hh, qi: (b * nq + qi, qcol0 + hh)),
            pl.BlockSpec((c.S, LANES), lambda b, hh, qi: (b, qcol0 + h + hh)),
            pl.BlockSpec((c.S, LANES), lambda b, hh, qi: (b, qcol0 + 2 * h + hh)),
            pl.BlockSpec((1, 1, c.S), lambda b, hh, qi: (b * h + hh, 0, 0)),
            pl.BlockSpec((1, t, 1), lambda b, hh, qi: (b * h + hh, qi, 0)),
        ],
        out_specs=pl.BlockSpec((t, LANES), lambda b, hh, qi: (b * nq + qi, hh)),
        out_shape=jax.ShapeDtypeStruct((n, h * LANES), BF16),
        scratch_shapes=[pltpu.VMEM((t, LANES), F32)] * 4 + [pltpu.VMEM((t, t), F32)],
        compiler_params=_cparams(("parallel", "parallel", "arbitrary")),
        name="fox",
    )(proj, proj, proj, crow, ccol)


def _mix_kernel(x_ref, a_ref, u_ref, v_ref, ga_ref, gb_ref, lng_ref, lnb_ref, ws_ref, bs_ref,
                wa_ref, wb_ref, wo_ref, o_ref, sg_ref, *, sgc, groups):
    tm = x_ref.shape[0]
    v = v_ref[...].astype(F32)
    mu = jnp.mean(v, axis=-1, keepdims=True)
    vc = v - mu
    var = jnp.mean(vc * vc, axis=-1, keepdims=True)
    vn = (vc * lax.rsqrt(var + LN_EPS) * lng_ref[...] + lnb_ref[...]).astype(BF16)
    row = lax.broadcasted_iota(I32, (sgc, sgc), 0)
    col = lax.broadcasted_iota(I32, (sgc, sgc), 1)
    for g in range(groups):
        w = jnp.where(col <= row, ws_ref[g], 0.0).astype(BF16)
        bias = bs_ref[g]
        for ci in range(tm // sgc):
            rs = slice(ci * sgc, (ci + 1) * sgc)
            cs = slice(g * LANES, (g + 1) * LANES)
            mixed = jnp.dot(w, vn[rs, cs], preferred_element_type=F32) + bias
            sg_ref[rs, cs] = (u_ref[rs, cs].astype(F32) * mixed).astype(BF16)
    ya = jnp.dot(a_ref[...], wa_ref[...], preferred_element_type=F32)
    yb = jnp.dot(sg_ref[...], wb_ref[...], preferred_element_type=F32)
    merged = (ga_ref[...].astype(F32) * ya + gb_ref[...].astype(F32) * yb).astype(BF16)
    o_ref[...] = x_ref[...] + jnp.dot(merged, wo_ref[...], preferred_element_type=F32)


def _const_spec(shape):
    nd = len(shape)
    return pl.BlockSpec(shape, lambda i: (0,) * nd, pipeline_mode=pl.Buffered(1))


def _mix(c, x2d, attn, proj, ln_g, ln_b, w_s, b_s_full, w_a, w_b, w_o):
    n, d = x2d.shape
    tm = c.tm_mix
    fw = c.FH * LANES
    sw = c.SG * LANES
    ucol = 2 * d // sw
    return pl.pallas_call(
        functools.partial(_mix_kernel, sgc=c.SGC, groups=c.SG),
        grid=(n // tm,),
        in_specs=[
            pl.BlockSpec((tm, d), lambda i: (i, 0)),
            pl.BlockSpec((tm, fw), lambda i: (i, 0)),
            pl.BlockSpec((tm, sw), lambda i: (i, ucol)),
            pl.BlockSpec((tm, sw), lambda i: (i, ucol + 1)),
            pl.BlockSpec((tm, d), lambda i: (i, 0)),
            pl.BlockSpec((tm, d), lambda i: (i, 1)),
            _const_spec((1, sw)),
            _const_spec((1, sw)),
            _const_spec((c.SG, c.SGC, c.SGC)),
            _const_spec((c.SG, c.SGC, LANES)),
            _const_spec((fw, d)),
            _const_spec((sw, d)),
            _const_spec((d, d)),
        ],
        out_specs=pl.BlockSpec((tm, d), lambda i: (i, 0)),
        out_shape=jax.ShapeDtypeStruct((n, d), F32),
        scratch_shapes=[pltpu.VMEM((tm, sw), BF16)],
        compiler_params=_cparams(("parallel",)),
        name="mix",
    )(x2d, attn, proj, proj, proj, proj, ln_g, ln_b, w_s, b_s_full, w_a, w_b, w_o)


def _mem_kv_kernel(m_ref, g_ref, w_ref, o_ref):
    hm = _rms(m_ref[...], g_ref[...]).astype(BF16)
    o_ref[...] = jnp.dot(hm, w_ref[...], preferred_element_type=F32).astype(o_ref.dtype)


def _mem_kv(c, mem2d, g, w_xkv):
    n, d = mem2d.shape
    nc = w_xkv.shape[1]
    tm = c.MEM
    return pl.pallas_call(
        _mem_kv_kernel,
        grid=(n // tm,),
        in_specs=[pl.BlockSpec((tm, d), lambda i: (i, 0)), _const_spec((1, d)), _const_spec((d, nc))],
        out_specs=pl.BlockSpec((tm, nc), lambda i: (i, 0)),
        out_shape=jax.ShapeDtypeStruct((n, nc), BF16),
        compiler_params=_cparams(("parallel",)),
        name="mem_kv",
    )(mem2d, g, w_xkv)


def _xattn_kernel(x1_ref, gx_ref, wq_ref, kv_ref, wo_ref, gf_ref, wr_ref, br_ref,
                  x2_ref, hfp_ref, idx_ref, gate_ref, cnt_ref, carry_ref,
                  *, heads, n_exp, topk, qscale):
    i = pl.program_id(0)
    tm, d = x1_ref.shape
    xw = heads * LANES
    x1 = x1_ref[...]
    hx = _rms(x1, gx_ref[...]).astype(BF16)
    q = (jnp.dot(hx, wq_ref[...], preferred_element_type=F32) * qscale).astype(BF16)
    outs = []
    for h in range(heads):
        k = kv_ref[:, h * LANES:(h + 1) * LANES]
        v = kv_ref[:, xw + h * LANES:xw + (h + 1) * LANES]
        s = lax.dot_general(q[:, h * LANES:(h + 1) * LANES], k, (((1,), (1,)), ((), ())),
                            preferred_element_type=F32)
        s = s - jnp.max(s, axis=-1, keepdims=True)
        p = jnp.exp(s)
        p = p / jnp.sum(p, axis=-1, keepdims=True)
        outs.append(jnp.dot(p.astype(BF16), v, preferred_element_type=F32).astype(BF16))
    o = jnp.concatenate(outs, axis=-1)
    x2 = x1 + jnp.dot(o, wo_ref[...], preferred_element_type=F32)
    x2_ref[...] = x2

    hf = _rms(x2, gf_ref[...])
    hfp_ref[...] = _pack_bf16_pair(hf[:, :d // 2], hf[:, d // 2:])

    h_hi = hf.astype(BF16)
    h_lo = (hf - h_hi.astype(F32)).astype(BF16)
    l_hi = jnp.dot(h_hi, wr_ref[...], preferred_element_type=F32)
    l_lo = jnp.dot(h_lo, wr_ref[:, :LANES], preferred_element_type=F32)
    logits = l_hi[:, :LANES] + l_hi[:, LANES:] + l_lo + br_ref[...]
    lane = lax.broadcasted_iota(I32, (tm, LANES), 1)
    lg = jnp.where(lane < n_exp, logits, -jnp.inf)
    vals, idxs = [], []
    for _ in range(topk):
        m = jnp.max(lg, axis=-1, keepdims=True)
        ix = jnp.min(jnp.where(lg == m, lane, LANES), axis=-1, keepdims=True)
        vals.append(m)
        idxs.append(ix)
        lg = jnp.where(lane == ix, -jnp.inf, lg)
    es = [jnp.exp(vv - vals[0]) for vv in vals]
    denom = es[0]
    for e in es[1:]:
        denom = denom + e

    @pl.when(i == 0)
    def _():
        carry_ref[...] = jnp.zeros(carry_ref.shape, F32)

    onehot = jnp.zeros((tm, LANES), F32)
    for ix in idxs:
        onehot = onehot + (lane == ix).astype(F32)
    idx_out = jnp.zeros((tm, LANES), I32)
    gate_out = jnp.zeros((tm, LANES), F32)
    for kk in range(topk):
        idx_out = jnp.where(lane == kk, idxs[kk], idx_out)
        gate_out = jnp.where(lane == kk, es[kk] / denom, gate_out)
    idx_ref[...] = idx_out
    gate_ref[...] = gate_out
    carry_ref[...] = carry_ref[...] + jnp.sum(onehot, axis=0, keepdims=True)
    cnt_ref[...] = carry_ref[...]


def _xattn(c, x1, gx, w_xq, kv, w_xo, gf, w_r, b_r, qscale):
    n, d = x1.shape
    tm = c.tm_x
    xw = c.XH * LANES
    per_b = c.S // tm
    kern = functools.partial(_xattn_kernel, heads=c.XH, n_exp=c.E, topk=c.K, qscale=qscale)
    return pl.pallas_call(
        kern,
        grid=(n // tm,),
        in_specs=[
            pl.BlockSpec((tm, d), lambda i: (i, 0)),
            _const_spec((1, d)),
            _const_spec((d, xw)),
            pl.BlockSpec((c.MEM, 2 * xw), lambda i: (i // per_b, 0)),
            _const_spec((xw, d)),
            _const_spec((1, d)),
            _const_spec((d, 2 * LANES)),
            _const_spec((1, LANES)),
        ],
        out_specs=[
            pl.BlockSpec((tm, d), lambda i: (i, 0)),
            pl.BlockSpec((tm, d // 2), lambda i: (i, 0)),
            pl.BlockSpec((tm, LANES), lambda i: (i, 0)),
            pl.BlockSpec((tm, LANES), lambda i: (i, 0)),
            pl.BlockSpec((1, LANES), lambda i: (0, 0)),
        ],
        out_shape=[
            jax.ShapeDtypeStruct((n, d), F32),
            jax.ShapeDtypeStruct((n, d // 2), U32),
            jax.ShapeDtypeStruct((n, LANES), I32),
            jax.ShapeDtypeStruct((n, LANES), F32),
            jax.ShapeDtypeStruct((1, LANES), F32),
        ],
        scratch_shapes=[pltpu.VMEM((1, LANES), F32)],
        compiler_params=_cparams(("arbitrary",)),
        name="xattn_router",
    )(x1, gx, w_xq, kv, w_xo, gf, w_r, b_r)


def _moe_kernel(te_ref, ns_ref, cpi_ref, pos_ref, rows_ref, ord_ref, hfp_ref, wgu_ref, wd_ref, bgu_ref, bd_ref,
                ykt_ref, wgub, wdb, stage, act, xin, ybuf, ids, wsem, xsem, ysem, isem,
                *, ts, tm, nck, fc, n_tok, n_asg):
    i = pl.program_id(0)
    n = pl.num_programs(0)
    nsub = ns_ref[i]
    e = te_ref[i]
    cpi = cpi_ref[i]
    rows = rows_ref[i]
    d, f2 = wgub.shape
    f = f2 // 2
    dh = d // 2
    ckr = d // nck
    ckd = f // nck
    nst = stage.shape[0]
    idw = ids.shape[0] // 2
    nq = idw // ID_CHUNK
    id_base = (i % 2) * idw + (pos_ref[i] & (ID_CHUNK - 1))

    def ids_copy(item, q, slot):
        c0 = lax.shift_right_logical(pos_ref[item], ID_CHUNK.bit_length() - 1)
        src = ord_ref.at[pl.ds(pl.multiple_of((c0 + q) * ID_CHUNK, ID_CHUNK), ID_CHUNK)]
        dst = ids.at[pl.ds(pl.multiple_of(slot * idw + q * ID_CHUNK, ID_CHUNK), ID_CHUNK)]
        return pltpu.make_async_copy(src, dst, isem.at[slot])

    def token_of(a):
        return a & (n_tok - 1) if n_tok & (n_tok - 1) == 0 else lax.rem(a, n_tok)

    def gather(s, slot, r0=0, r1=ts):
        for r in range(r0, r1):
            tok = token_of(ids[id_base + s * ts + r])
            pltpu.make_async_copy(hfp_ref.at[pl.ds(tok, 1)], xin.at[slot, pl.ds(r, 1)], xsem.at[slot]).start()

    def gather_wait(slot):
        for r in range(ts):
            pltpu.make_async_copy(hfp_ref.at[pl.ds(0, 1)], xin.at[slot, pl.ds(0, 1)], xsem.at[slot]).wait()

    def scatter(s, slot, r0=0, r1=ts):
        for r in range(r0, r1):
            g = s * ts + r
            dst = jnp.where(g < rows, ids[id_base + g], n_asg + slot * ts + r)
            pltpu.make_async_copy(ybuf.at[slot, pl.ds(r, 1)], ykt_ref.at[pl.ds(dst, 1)], ysem.at[slot]).start()

    def scatter_wait(slot):
        for r in range(ts):
            pltpu.make_async_copy(ybuf.at[slot, pl.ds(0, 1)], ykt_ref.at[pl.ds(0, 1)], ysem.at[slot]).wait()

    def gu_copy(ee, c, slot):
        return pltpu.make_async_copy(wgu_ref.at[ee, pl.ds(pl.multiple_of(c * ckr, ckr), ckr), :],
                                     stage.at[slot], wsem.at[slot])

    def d_copy(ee, c, slot):
        return pltpu.make_async_copy(wd_ref.at[ee, pl.ds(pl.multiple_of(c * ckd, ckd), ckd), :],
                                     stage.at[slot, pl.ds(0, ckd), pl.ds(0, d)], wsem.at[slot])

    def gu_step(ee, c):
        slot = c % nst
        gu_copy(ee, c, slot).wait()
        wgub[pl.ds(pl.multiple_of(c * ckr, ckr), ckr), :] = stage[slot].astype(BF16)

        @pl.when(c + nst < nck)
        def _():
            gu_copy(ee, c + nst, slot).start(priority=WEIGHT_DMA_PRIORITY)

    def d_step(ee, c):
        slot = c % nst
        d_copy(ee, c, slot).wait()
        wdb[pl.ds(pl.multiple_of(c * ckd, ckd), ckd), :] = stage[slot, :ckd, :d].astype(BF16)

        @pl.when(c + nst < nck)
        def _():
            d_copy(ee, c + nst, slot).start(priority=WEIGHT_DMA_PRIORITY)

    def prime(copy_fn, ee):
        for q in range(min(nst, nck)):
            copy_fn(ee, q, q).start(priority=WEIGHT_DMA_PRIORITY)

    def steps_after(s, step_fn, ee):
        def one(jj, carry):
            c = s * cpi + jj

            @pl.when(c < nck)
            def _():
                step_fn(ee, c)
            return carry
        lax.fori_loop(0, cpi, one, 0)

    @pl.when(nsub > 0)
    def _():
        nxt = jnp.minimum(i + 1, n - 1)
        has_next = (i + 1 < n) & (ns_ref[nxt] > 0)

        @pl.when(i == 0)
        def _():
            for q in range(nq):
                ids_copy(0, q, 0).start()
            prime(gu_copy, e)

            def first(c, carry):
                gu_step(e, c)
                return carry
            lax.fori_loop(0, nck, first, 0)

        for q in range(nq):
            ids_copy(i, q, i % 2).wait()

        @pl.when(has_next)
        def _():
            for q in range(nq):
                ids_copy(nxt, q, nxt % 2).start()

        prime(d_copy, e)
        gather(0, 0)
        bgu = bgu_ref[e]

        def phase_a(s, carry):
            slot = s % 2
            gather_wait(slot)
            p = xin[slot]
            xb = jnp.concatenate([_unpack_lo(p).astype(BF16), _unpack_hi(p).astype(BF16)], axis=-1)
            nc = f // fc
            for cc in range(nc):
                gather(s + 1, 1 - slot, cc * ts // nc, (cc + 1) * ts // nc)
                gate = jnp.dot(xb, wgub[:, cc * fc:(cc + 1) * fc], preferred_element_type=F32)
                gate = gate + bgu[:, cc * fc:(cc + 1) * fc]
                up = jnp.dot(xb, wgub[:, f + cc * fc:f + (cc + 1) * fc], preferred_element_type=F32)
                up = up + bgu[:, f + cc * fc:f + (cc + 1) * fc]
                gate = jnp.minimum(gate, SWIGLU_LIMIT)
                up = jnp.clip(up, -SWIGLU_LIMIT, SWIGLU_LIMIT)
                a = (up + 1.0) * gate * jax.nn.sigmoid(SWIGLU_ALPHA * gate)
                act[s, :, cc * fc:(cc + 1) * fc] = a.astype(BF16)
            steps_after(s, d_step, e)
            return carry

        lax.fori_loop(0, nsub, phase_a, 0)
        gather_wait(nsub % 2)

        e_next = te_ref[nxt]
        load_next = has_next & (e_next != e)

        @pl.when(load_next)
        def _():
            prime(gu_copy, e_next)

        bd = bd_ref[e]

        def down(s, slot, scatter_prev):
            a = act[s]
            if scatter_prev:
                scatter(s - 1, 1 - slot)
            fcb = min(fc, dh // 2)
            for cc in range(dh // fcb):
                lo = slice(cc * fcb, (cc + 1) * fcb)
                hi = slice(dh + cc * fcb, dh + (cc + 1) * fcb)
                y_lo = jnp.dot(a, wdb[:, lo], preferred_element_type=F32) + bd[:, lo]
                y_hi = jnp.dot(a, wdb[:, hi], preferred_element_type=F32) + bd[:, hi]
                ybuf[slot, :, lo] = _pack_bf16_pair(y_lo, y_hi)

        def next_weights(s):
            @pl.when(load_next)
            def _():
                steps_after(s, gu_step, e_next)

        down(0, 0, False)
        next_weights(0)

        def phase_b(s, carry):
            slot = s % 2

            @pl.when(s >= 2)
            def _():
                scatter_wait(slot)

            down(s, slot, True)
            next_weights(s)
            return carry

        lax.fori_loop(1, nsub, phase_b, 0)
        scatter(nsub - 1, (nsub - 1) % 2)
        scatter_wait((nsub - 1) % 2)

        @pl.when(nsub >= 2)
        def _():
            scatter_wait(nsub % 2)


def _id_window(c):
    return ((ID_CHUNK - 1 + c.tm_e + c.ts_e) // ID_CHUNK + 1) * ID_CHUNK


def _moe(c, sched, order_pad, hfp, w_gu, b_gu3, w_d, b_d3):
    n_tok, dh = hfp.shape
    d = 2 * dh
    f = c.DFF
    tm, ts, nck = c.tm_e, c.ts_e, c.nck_e
    n_asg = n_tok * c.K
    grid_spec = pltpu.PrefetchScalarGridSpec(
        num_scalar_prefetch=5,
        grid=(_n_tiles(c),),
        in_specs=[
            pl.BlockSpec(memory_space=pl.ANY),
            pl.BlockSpec(memory_space=pl.ANY),
            pl.BlockSpec(memory_space=pl.ANY),
            pl.BlockSpec(memory_space=pl.ANY),
            pl.BlockSpec((c.E, 1, 2 * f), lambda i, *_: (0, 0, 0), pipeline_mode=pl.Buffered(1)),
            pl.BlockSpec((c.E, 1, d), lambda i, *_: (0, 0, 0), pipeline_mode=pl.Buffered(1)),
        ],
        out_specs=pl.BlockSpec(memory_space=pl.ANY),
        scratch_shapes=[
            pltpu.VMEM((d, 2 * f), BF16),
            pltpu.VMEM((f, d), BF16),
            pltpu.VMEM((c.nst_e, d // nck, 2 * f), F32),
            pltpu.VMEM((tm // ts, ts, f), BF16),
            pltpu.VMEM((2, ts, dh), U32),
            pltpu.VMEM((2, ts, dh), U32),
            pltpu.SMEM((2 * _id_window(c),), I32),
            pltpu.SemaphoreType.DMA((c.nst_e,)),
            pltpu.SemaphoreType.DMA((2,)),
            pltpu.SemaphoreType.DMA((2,)),
            pltpu.SemaphoreType.DMA((2,)),
        ],
    )
    kern = functools.partial(_moe_kernel, ts=ts, tm=tm, nck=nck, fc=c.fc_e, n_tok=n_tok, n_asg=n_asg)
    return pl.pallas_call(
        kern,
        grid_spec=grid_spec,
        out_shape=jax.ShapeDtypeStruct((n_asg + 2 * ts, dh), U32),
        compiler_params=_cparams(("arbitrary",)),
        name="experts",
    )(*sched, order_pad, hfp, w_gu, w_d, b_gu3, b_d3)


def _combine_kernel(x2_ref, gate_ref, g_ref, *refs, topk):
    y_refs, o_ref = refs[:topk], refs[topk]
    d = x2_ref.shape[1]
    dh = d // 2
    lo = x2_ref[:, :dh]
    hi = x2_ref[:, dh:]
    gates = gate_ref[...]
    for kk in range(topk):
        wk = gates[:, kk:kk + 1]
        p = y_refs[kk][...]
        lo = lo + wk * _unpack_lo(p)
        hi = hi + wk * _unpack_hi(p)
    ms = (jnp.sum(lo * lo, axis=-1, keepdims=True) + jnp.sum(hi * hi, axis=-1, keepdims=True)) / d
    inv = lax.rsqrt(ms + RMS_EPS)
    o_ref[:, :dh] = lo * inv * g_ref[:, :dh]
    o_ref[:, dh:] = hi * inv * g_ref[:, dh:]


def _combine(c, x2, gates, g, ykt):
    n, d = x2.shape
    tm = c.tm_c
    nb = n // tm
    y_specs = [pl.BlockSpec((tm, d // 2), functools.partial(lambda i, kk: (kk * nb + i, 0), kk=kk))
               for kk in range(c.K)]
    return pl.pallas_call(
        functools.partial(_combine_kernel, topk=c.K),
        grid=(nb,),
        in_specs=[
            pl.BlockSpec((tm, d), lambda i: (i, 0)),
            pl.BlockSpec((tm, LANES), lambda i: (i, 0)),
            pl.BlockSpec((1, d), lambda i: (0, 0)),
        ] + y_specs,
        out_specs=pl.BlockSpec((tm, d), lambda i: (i, 0)),
        out_shape=jax.ShapeDtypeStruct((n, d), F32),
        compiler_params=_cparams(("parallel",)),
        name="combine",
    )(x2, gates, g, *([ykt] * c.K))


def _plan(c, idx, counts):
    tm, ts = c.tm_e, c.ts_e
    nk = idx.shape[0] * c.K
    order = jnp.argsort(idx.T.reshape(-1)).astype(I32)
    pad = _id_window(c) + (-nk) % ID_CHUNK
    order_pad = jnp.concatenate([order, jnp.zeros((pad,), I32)])
    counts = counts.astype(I32)
    start = jnp.cumsum(counts) - counts
    tiles_per_e = (counts + tm - 1) // tm
    tile_end = jnp.cumsum(tiles_per_e)
    tile_start = tile_end - tiles_per_e
    t = jnp.arange(_n_tiles(c), dtype=I32)
    n_used = tile_end[-1]
    tc = jnp.minimum(t, n_used - 1)
    te = jnp.minimum(jnp.searchsorted(tile_end, tc, side="right"), c.E - 1).astype(I32)
    j = tc - tile_start[te]
    used = t < n_used
    rows = jnp.where(used, jnp.clip(counts[te] - j * tm, 0, tm), 0).astype(I32)
    nsub = (rows + ts - 1) // ts
    pos = jnp.where(used, start[te] + j * tm, 0).astype(I32)
    cpi = (c.nck_e + jnp.maximum(nsub, 1) - 1) // jnp.maximum(nsub, 1)
    return order_pad, (te, nsub.astype(I32), cpi.astype(I32), pos, rows)


def _n_tiles(c):
    return -(-(c.B * c.S * c.K) // c.tm_e) + c.E


def _forward(c, x, mem, norm_mix_g, w_in, b_forget, sg_ln_g, sg_ln_b, w_spatial, b_spatial,
             w_branch_a, w_branch_b, w_out, norm_x_g, norm_mem_g, w_xq, w_xkv, w_xo,
             norm_ffn_g, w_router, b_router, w_gate_up, b_gate_up, w_down, b_down, norm_final_g):
    B, S, D = x.shape
    n = B * S
    fw = c.FH * LANES
    sw = c.SG * LANES
    x2d = x.reshape(n, D)

    o_f = 3 * fw
    o_z = o_f + c.FH
    o_g = o_z + 2 * sw
    w_main = jnp.concatenate([w_in[:, o_g:], w_in[:, o_z:o_g], w_in[:, :o_f]], axis=1).astype(BF16)
    w_f = jnp.pad(w_in[:, o_f:o_z], ((0, 0), (0, LANES - c.FH))).astype(BF16)
    b_f = jnp.pad(b_forget.astype(F32), (0, LANES - c.FH)).reshape(1, LANES)
    proj, logf = _in_proj(c, x2d, norm_mix_g.reshape(1, D), w_main, w_f, b_f, LOG2E * LANES ** -0.5)

    logf_bhs = logf[:, :c.FH].reshape(B, S, c.FH).transpose(0, 2, 1)
    csum = _cumsum(c, logf_bhs)
    crow = csum.reshape(B * c.FH, 1, S)
    ccol = csum.reshape(B * c.FH, S, 1)
    qcol0 = (2 * D + 2 * sw) // LANES
    attn = _fox(c, proj, crow, ccol, qcol0)

    b_s_full = jnp.broadcast_to(b_spatial.astype(F32)[:, :, None], (c.SG, c.SGC, LANES))
    x1 = _mix(c, x2d, attn, proj, sg_ln_g.reshape(1, sw), sg_ln_b.reshape(1, sw), w_spatial, b_s_full,
              w_branch_a.astype(BF16), w_branch_b.astype(BF16), w_out.astype(BF16))

    kv = _mem_kv(c, mem.reshape(B * c.MEM, D), norm_mem_g.reshape(1, D), w_xkv.astype(BF16))
    w_r32 = jnp.pad(w_router.astype(F32), ((0, 0), (0, LANES - c.E)))
    w_r_hi = w_r32.astype(BF16)
    w_r = jnp.concatenate([w_r_hi, (w_r32 - w_r_hi.astype(F32)).astype(BF16)], axis=1)
    b_r = jnp.pad(b_router.astype(F32), (0, LANES - c.E)).reshape(1, LANES)
    x2, hfp, idx, gates, counts = _xattn(
        c, x1, norm_x_g.reshape(1, D), w_xq.astype(BF16), kv, w_xo.astype(BF16),
        norm_ffn_g.reshape(1, D), w_r, b_r, LANES ** -0.5)

    order_pad, sched = _plan(c, idx[:, :c.K], counts[0, :c.E])
    ykt = _moe(c, sched, order_pad, hfp, w_gate_up, b_gate_up.reshape(c.E, 1, 2 * c.DFF),
               w_down, b_down.reshape(c.E, 1, D))
    out = _combine(c, x2, gates, norm_final_g.reshape(1, D), ykt)
    return out.reshape(B, S, D)


_CFG = Cfg(B=4, S=4096, D=2048, MEM=256, FH=8, SG=8, SGC=128, XH=4, E=32, K=4, DFF=2048,
           tm_in=1024, tn_in=1024, tq=1024, tm_mix=256, tm_x=512, tm_e=2560, ts_e=256, fc_e=1024, nck_e=16, nst_e=3,
           tm_c=256, tc_cs=512, fox_parts=4)


@jax.jit
def kernel(x, mem, norm_mix_g, w_in, b_forget, sg_ln_g, sg_ln_b, w_spatial, b_spatial, w_branch_a, w_branch_b,
           w_out, norm_x_g, norm_mem_g, w_xq, w_xkv, w_xo, norm_ffn_g, w_router, b_router, w_gate_up,
           b_gate_up, w_down, b_down, norm_final_g):
    return _forward(_CFG, x, mem, norm_mix_g, w_in, b_forget, sg_ln_g, sg_ln_b, w_spatial, b_spatial,
                    w_branch_a, w_branch_b, w_out, norm_x_g, norm_mem_g, w_xq, w_xkv, w_xo,
                    norm_ffn_g, w_router, b_router, w_gate_up, b_gate_up, w_down, b_down, norm_final_g)
```

```python
import functools
from typing import NamedTuple

import jax
import jax.numpy as jnp
from jax import lax
from jax.experimental import pallas as pl
from jax.experimental.pallas import tpu as pltpu

F32 = jnp.float32
BF16 = jnp.bfloat16
U32 = jnp.uint32
I32 = jnp.int32

LANES = 128
VMEM_LIMIT = 56 * 1024 * 1024

RMS_EPS = 1e-6
LN_EPS = 1e-5
SWIGLU_LIMIT = 7.0
SWIGLU_ALPHA = 1.702
GELU_C = 0.7978845608028654
LOG2E = 1.4426950408889634
ID_CHUNK = 1024
WEIGHT_DMA_PRIORITY = 1


class Cfg(NamedTuple):
    B: int
    S: int
    D: int
    MEM: int
    FH: int
    SG: int
    SGC: int
    XH: int
    E: int
    K: int
    DFF: int
    tm_in: int
    tn_in: int
    tq: int
    tm_mix: int
    tm_x: int
    tm_e: int
    ts_e: int
    fc_e: int
    nck_e: int
    nst_e: int
    tm_c: int
    tc_cs: int
    fox_parts: int


def _cparams(sem):
    return pltpu.CompilerParams(dimension_semantics=sem, vmem_limit_bytes=VMEM_LIMIT)


def _rms(x, g):
    ms = jnp.mean(x * x, axis=-1, keepdims=True)
    return x * lax.rsqrt(ms + RMS_EPS) * g


def _pack_bf16_pair(a, b):
    def rne(v):
        bits = lax.bitcast_convert_type(v, U32)
        return bits + jnp.uint32(0x7FFF) + ((bits >> 16) & jnp.uint32(1))
    return (rne(a) >> 16) | (rne(b) & jnp.uint32(0xFFFF0000))


def _unpack_lo(p):
    return lax.bitcast_convert_type(p << 16, F32)


def _unpack_hi(p):
    return lax.bitcast_convert_type(p & jnp.uint32(0xFFFF0000), F32)


def _in_proj_kernel(x_ref, g_ref, w_ref, wf_ref, bf_ref, o_ref, f_ref, h_ref, *, nj_gate, nj_z, nj_q, qscale):
    j = pl.program_id(1)

    @pl.when(j == 0)
    def _():
        hb = _rms(x_ref[...], g_ref[...]).astype(BF16)
        h_ref[...] = hb
        f = jnp.dot(hb, wf_ref[...], preferred_element_type=F32) + bf_ref[...]
        f_ref[...] = jnp.minimum(f, 0.0) - jnp.log1p(jnp.exp(-jnp.abs(f)))

    acc = jnp.dot(h_ref[...], w_ref[...], preferred_element_type=F32)

    is_gate = j < nj_gate
    is_z = (j >= nj_gate) & (j < nj_gate + nj_z)
    is_q = (j >= nj_gate + nj_z) & (j < nj_gate + nj_z + nj_q)
    a1 = jnp.where(is_gate, 0.5, jnp.where(is_z, GELU_C, 0.0)).astype(F32)
    a3 = jnp.where(is_z, GELU_C * 0.044715, 0.0).astype(F32)
    b0 = jnp.where(is_gate, 0.5, 0.0).astype(F32)
    b1 = jnp.where(is_gate, 0.0, jnp.where(is_z, 0.5, jnp.where(is_q, qscale, 1.0))).astype(F32)
    th = jnp.tanh(acc * (a1 + a3 * (acc * acc)))
    o_ref[...] = ((b0 + b1 * acc) * (1.0 + th)).astype(o_ref.dtype)


def _in_proj(c, x2d, g, w_main, w_f, b_f, qscale):
    n, d = x2d.shape
    nc = w_main.shape[1]
    tm, tn = c.tm_in, c.tn_in
    fw = c.FH * LANES
    kern = functools.partial(_in_proj_kernel, nj_gate=2 * d // tn, nj_z=2 * c.SG * LANES // tn,
                             nj_q=fw // tn, qscale=qscale)
    return pl.pallas_call(
        kern,
        grid=(n // tm, nc // tn),
        in_specs=[
            pl.BlockSpec((tm, d), lambda i, j: (i, 0)),
            pl.BlockSpec((1, d), lambda i, j: (0, 0)),
            pl.BlockSpec((d, tn), lambda i, j: (0, j)),
            pl.BlockSpec((d, LANES), lambda i, j: (0, 0)),
            pl.BlockSpec((1, LANES), lambda i, j: (0, 0)),
        ],
        out_specs=[
            pl.BlockSpec((tm, tn), lambda i, j: (i, j)),
            pl.BlockSpec((tm, LANES), lambda i, j: (i, 0)),
        ],
        out_shape=[jax.ShapeDtypeStruct((n, nc), BF16), jax.ShapeDtypeStruct((n, LANES), F32)],
        scratch_shapes=[pltpu.VMEM((tm, d), BF16)],
        compiler_params=_cparams(("parallel", "arbitrary")),
        name="in_proj",
    )(x2d, g, w_main, w_f, b_f)


def _cumsum_kernel(f_ref, o_ref, *, tc):
    rows, s = f_ref.shape[1], f_ref.shape[2]
    r = lax.broadcasted_iota(I32, (tc, tc), 0)
    col = lax.broadcasted_iota(I32, (tc, tc), 1)
    upper = (r <= col).astype(F32)
    carry = jnp.zeros((rows, 1), F32)
    for i in range(s // tc):
        blk = f_ref[0, :, i * tc:(i + 1) * tc]
        cs = jnp.dot(blk, upper, preferred_element_type=F32, precision=lax.Precision.HIGHEST) + carry
        o_ref[0, :, i * tc:(i + 1) * tc] = cs * LOG2E
        carry = cs[:, tc - 1:tc]


def _cumsum(c, logf_bhs):
    b, h, s = logf_bhs.shape
    return pl.pallas_call(
        functools.partial(_cumsum_kernel, tc=c.tc_cs),
        grid=(b,),
        in_specs=[pl.BlockSpec((1, h, s), lambda i: (i, 0, 0))],
        out_specs=pl.BlockSpec((1, h, s), lambda i: (i, 0, 0)),
        out_shape=jax.ShapeDtypeStruct((b, h, s), F32),
        compiler_params=_cparams(("parallel",)),
        name="cumsum",
    )(logf_bhs)


def _fox_kernel(q_ref, k_ref, v_ref, crow_ref, ccol_ref, o_ref, m_ref, l_ref, acc_ref, cq_ref, s_ref,
                *, t, parts):
    qi = pl.program_id(2)
    hr = t // parts
    m_ref[...] = jnp.full(m_ref.shape, -jnp.inf, F32)
    l_ref[...] = jnp.zeros(l_ref.shape, F32)
    acc_ref[...] = jnp.zeros(acc_ref.shape, F32)
    cq_ref[...] = jnp.broadcast_to(ccol_ref[0], cq_ref.shape)

    def scores(ks):
        k = k_ref[pl.ds(ks, t), :]
        return lax.dot_general(q_ref[...], k, (((1,), (1,)), ((), ())), preferred_element_type=F32)

    def block(part, ks, width, masked):
        rows = slice(part * hr, (part + 1) * hr)
        v = v_ref[pl.ds(ks, width), :]
        s = s_ref[rows, :width]
        cq = cq_ref[rows, :]
        crow = crow_ref[0, :, pl.ds(ks, width)]
        nj = width // LANES
        sj = [s[:, j * LANES:(j + 1) * LANES] + cq - crow[:, j * LANES:(j + 1) * LANES] for j in range(nj)]
        if masked:
            row = lax.broadcasted_iota(I32, (hr, LANES), 0) + part * hr
            col = lax.broadcasted_iota(I32, (hr, LANES), 1)
            sj = [jnp.where(col + j * LANES <= row, sj[j], -jnp.inf) for j in range(nj)]
        mx = sj[0]
        for j in range(1, nj):
            mx = jnp.maximum(mx, sj[j])
        m_prev = m_ref[rows, :]
        m_next = jnp.maximum(m_prev, jnp.max(mx, axis=-1, keepdims=True))
        alpha = jnp.exp2(m_prev - m_next)
        pj = [jnp.exp2(sj[j] - m_next) for j in range(nj)]
        psum = pj[0]
        for j in range(1, nj):
            psum = psum + pj[j]
        p = jnp.concatenate([x.astype(BF16) for x in pj], axis=-1)
        l_ref[rows, :] = alpha * l_ref[rows, :] + psum
        acc_ref[rows, :] = alpha * acc_ref[rows, :] + jnp.dot(p, v, preferred_element_type=F32)
        m_ref[rows, :] = m_next

    s_ref[...] = scores(0)

    def body(kc, carry):
        ks = pl.multiple_of(kc * t, t)
        s_next = scores(pl.multiple_of(ks + t, t))
        for part in range(parts):
            block(part, ks, t, False)
        s_ref[...] = s_next
        return carry

    lax.fori_loop(0, qi, body, 0)
    kd = pl.multiple_of(qi * t, t)
    for part in range(parts):
        block(part, kd, (part + 1) * hr, True)
    l = jnp.sum(l_ref[...], axis=-1, keepdims=True)
    o_ref[...] = (acc_ref[...] / l).astype(o_ref.dtype)


def _fox(c, proj, crow, ccol, qcol0):
    n = c.B * c.S
    t = c.tq
    nq = c.S // t
    h = c.FH
    return pl.pallas_call(
        functools.partial(_fox_kernel, t=t, parts=c.fox_parts),
        grid=(c.B, h, nq),
        in_specs=[
            pl.BlockSpec((t, LANES), lambda b, hh, qi: (b * nq + qi, qcol0 + hh)),
            pl.BlockSpec((c.S, LANES), lambda b, hh, qi: (b, qcol0 + h + hh)),
            pl.BlockSpec((c.S, LANES), lambda b, hh, qi: (b, qcol0 + 2 * h + hh)),
            pl.BlockSpec((1, 1, c.S), lambda b, hh, qi: (b * h + hh, 0, 0)),
            pl.BlockSpec((1, t, 1), lambda b, hh, qi: (b * h + hh, qi, 0)),
        ],
        out_specs=pl.BlockSpec((t, LANES), lambda b, hh, qi: (b * nq + qi, hh)),
        out_shape=jax.ShapeDtypeStruct((n, h * LANES), BF16),
        scratch_shapes=[pltpu.VMEM((t, LANES), F32)] * 4 + [pltpu.VMEM((t, t), F32)],
        compiler_params=_cparams(("parallel", "parallel", "arbitrary")),
        name="fox",
    )(proj, proj, proj, crow, ccol)


def _mix_kernel(x_ref, a_ref, u_ref, v_ref, ga_ref, gb_ref, lng_ref, lnb_ref, ws_ref, bs_ref,
                wa_ref, wb_ref, wo_ref, o_ref, sg_ref, *, sgc, groups):
    tm = x_ref.shape[0]
    v = v_ref[...].astype(F32)
    mu = jnp.mean(v, axis=-1, keepdims=True)
    vc = v - mu
    var = jnp.mean(vc * vc, axis=-1, keepdims=True)
    vn = (vc * lax.rsqrt(var + LN_EPS) * lng_ref[...] + lnb_ref[...]).astype(BF16)
    row = lax.broadcasted_iota(I32, (sgc, sgc), 0)
    col = lax.broadcasted_iota(I32, (sgc, sgc), 1)
    for g in range(groups):
        w = jnp.where(col <= row, ws_ref[g], 0.0).astype(BF16)
        bias = bs_ref[g]
        for ci in range(tm // sgc):
            rs = slice(ci * sgc, (ci + 1) * sgc)
            cs = slice(g * LANES, (g + 1) * LANES)
            mixed = jnp.dot(w, vn[rs, cs], preferred_element_type=F32) + bias
            sg_ref[rs, cs] = (u_ref[rs, cs].astype(F32) * mixed).astype(BF16)
    ya = jnp.dot(a_ref[...], wa_ref[...], preferred_element_type=F32)
    yb = jnp.dot(sg_ref[...], wb_ref[...], preferred_element_type=F32)
    merged = (ga_ref[...].astype(F32) * ya + gb_ref[...].astype(F32) * yb).astype(BF16)
    o_ref[...] = x_ref[...] + jnp.dot(merged, wo_ref[...], preferred_element_type=F32)


def _const_spec(shape):
    nd = len(shape)
    return pl.BlockSpec(shape, lambda i: (0,) * nd, pipeline_mode=pl.Buffered(1))


def _mix(c, x2d, attn, proj, ln_g, ln_b, w_s, b_s_full, w_a, w_b, w_o):
    n, d = x2d.shape
    tm = c.tm_mix
    fw = c.FH * LANES
    sw = c.SG * LANES
    ucol = 2 * d // sw
    return pl.pallas_call(
        functools.partial(_mix_kernel, sgc=c.SGC, groups=c.SG),
        grid=(n // tm,),
        in_specs=[
            pl.BlockSpec((tm, d), lambda i: (i, 0)),
            pl.BlockSpec((tm, fw), lambda i: (i, 0)),
            pl.BlockSpec((tm, sw), lambda i: (i, ucol)),
            pl.BlockSpec((tm, sw), lambda i: (i, ucol + 1)),
            pl.BlockSpec((tm, d), lambda i: (i, 0)),
            pl.BlockSpec((tm, d), lambda i: (i, 1)),
            _const_spec((1, sw)),
            _const_spec((1, sw)),
            _const_spec((c.SG, c.SGC, c.SGC)),
            _const_spec((c.SG, c.SGC, LANES)),
            _const_spec((fw, d)),
            _const_spec((sw, d)),
            _const_spec((d, d)),
        ],
        out_specs=pl.BlockSpec((tm, d), lambda i: (i, 0)),
        out_shape=jax.ShapeDtypeStruct((n, d), F32),
        scratch_shapes=[pltpu.VMEM((tm, sw), BF16)],
        compiler_params=_cparams(("parallel",)),
        name="mix",
    )(x2d, attn, proj, proj, proj, proj, ln_g, ln_b, w_s, b_s_full, w_a, w_b, w_o)


def _mem_kv_kernel(m_ref, g_ref, w_ref, o_ref):
    hm = _rms(m_ref[...], g_ref[...]).astype(BF16)
    o_ref[...] = jnp.dot(hm, w_ref[...], preferred_element_type=F32).astype(o_ref.dtype)


def _mem_kv(c, mem2d, g, w_xkv):
    n, d = mem2d.shape
    nc = w_xkv.shape[1]
    tm = c.MEM
    return pl.pallas_call(
        _mem_kv_kernel,
        grid=(n // tm,),
        in_specs=[pl.BlockSpec((tm, d), lambda i: (i, 0)), _const_spec((1, d)), _const_spec((d, nc))],
        out_specs=pl.BlockSpec((tm, nc), lambda i: (i, 0)),
        out_shape=jax.ShapeDtypeStruct((n, nc), BF16),
        compiler_params=_cparams(("parallel",)),
        name="mem_kv",
    )(mem2d, g, w_xkv)


def _xattn_kernel(x1_ref, gx_ref, wq_ref, kv_ref, wo_ref, gf_ref, wr_ref, br_ref,
                  x2_ref, hfp_ref, idx_ref, gate_ref, cnt_ref, carry_ref,
                  *, heads, n_exp, topk, qscale):
    i = pl.program_id(0)
    tm, d = x1_ref.shape
    xw = heads * LANES
    x1 = x1_ref[...]
    hx = _rms(x1, gx_ref[...]).astype(BF16)
    q = (jnp.dot(hx, wq_ref[...], preferred_element_type=F32) * qscale).astype(BF16)
    outs = []
    for h in range(heads):
        k = kv_ref[:, h * LANES:(h + 1) * LANES]
        v = kv_ref[:, xw + h * LANES:xw + (h + 1) * LANES]
        s = lax.dot_general(q[:, h * LANES:(h + 1) * LANES], k, (((1,), (1,)), ((), ())),
                            preferred_element_type=F32)
        s = s - jnp.max(s, axis=-1, keepdims=True)
        p = jnp.exp(s)
        p = p / jnp.sum(p, axis=-1, keepdims=True)
        outs.append(jnp.dot(p.astype(BF16), v, preferred_element_type=F32).astype(BF16))
    o = jnp.concatenate(outs, axis=-1)
    x2 = x1 + jnp.dot(o, wo_ref[...], preferred_element_type=F32)
    x2_ref[...] = x2

    hf = _rms(x2, gf_ref[...])
    hfp_ref[...] = _pack_bf16_pair(hf[:, :d // 2], hf[:, d // 2:])

    h_hi = hf.astype(BF16)
    h_lo = (hf - h_hi.astype(F32)).astype(BF16)
    l_hi = jnp.dot(h_hi, wr_ref[...], preferred_element_type=F32)
    l_lo = jnp.dot(h_lo, wr_ref[:, :LANES], preferred_element_type=F32)
    logits = l_hi[:, :LANES] + l_hi[:, LANES:] + l_lo + br_ref[...]
    lane = lax.broadcasted_iota(I32, (tm, LANES), 1)
    lg = jnp.where(lane < n_exp, logits, -jnp.inf)
    vals, idxs = [], []
    for _ in range(topk):
        m = jnp.max(lg, axis=-1, keepdims=True)
        ix = jnp.min(jnp.where(lg == m, lane, LANES), axis=-1, keepdims=True)
        vals.append(m)
        idxs.append(ix)
        lg = jnp.where(lane == ix, -jnp.inf, lg)
    es = [jnp.exp(vv - vals[0]) for vv in vals]
    denom = es[0]
    for e in es[1:]:
        denom = denom + e

    @pl.when(i == 0)
    def _():
        carry_ref[...] = jnp.zeros(carry_ref.shape, F32)

    onehot = jnp.zeros((tm, LANES), F32)
    for ix in idxs:
        onehot = onehot + (lane == ix).astype(F32)
    idx_out = jnp.zeros((tm, LANES), I32)
    gate_out = jnp.zeros((tm, LANES), F32)
    for kk in range(topk):
        idx_out = jnp.where(lane == kk, idxs[kk], idx_out)
        gate_out = jnp.where(lane == kk, es[kk] / denom, gate_out)
    idx_ref[...] = idx_out
    gate_ref[...] = gate_out
    carry_ref[...] = carry_ref[...] + jnp.sum(onehot, axis=0, keepdims=True)
    cnt_ref[...] = carry_ref[...]


def _xattn(c, x1, gx, w_xq, kv, w_xo, gf, w_r, b_r, qscale):
    n, d = x1.shape
    tm = c.tm_x
    xw = c.XH * LANES
    per_b = c.S // tm
    kern = functools.partial(_xattn_kernel, heads=c.XH, n_exp=c.E, topk=c.K, qscale=qscale)
    return pl.pallas_call(
        kern,
        grid=(n // tm,),
        in_specs=[
            pl.BlockSpec((tm, d), lambda i: (i, 0)),
            _const_spec((1, d)),
            _const_spec((d, xw)),
            pl.BlockSpec((c.MEM, 2 * xw), lambda i: (i // per_b, 0)),
            _const_spec((xw, d)),
            _const_spec((1, d)),
            _const_spec((d, 2 * LANES)),
            _const_spec((1, LANES)),
        ],
        out_specs=[
            pl.BlockSpec((tm, d), lambda i: (i, 0)),
            pl.BlockSpec((tm, d // 2), lambda i: (i, 0)),
            pl.BlockSpec((tm, LANES), lambda i: (i, 0)),
            pl.BlockSpec((tm, LANES), lambda i: (i, 0)),
            pl.BlockSpec((1, LANES), lambda i: (0, 0)),
        ],
        out_shape=[
            jax.ShapeDtypeStruct((n, d), F32),
            jax.ShapeDtypeStruct((n, d // 2), U32),
            jax.ShapeDtypeStruct((n, LANES), I32),
            jax.ShapeDtypeStruct((n, LANES), F32),
            jax.ShapeDtypeStruct((1, LANES), F32),
        ],
        scratch_shapes=[pltpu.VMEM((1, LANES), F32)],
        compiler_params=_cparams(("arbitrary",)),
        name="xattn_router",
    )(x1, gx, w_xq, kv, w_xo, gf, w_r, b_r)


def _moe_kernel(te_ref, ns_ref, cpi_ref, pos_ref, rows_ref, ord_ref, hfp_ref, wgu_ref, wd_ref, bgu_ref, bd_ref,
                ykt_ref, wgub, wdb, stage, act, xin, ybuf, ids, wsem, xsem, ysem, isem,
                *, ts, tm, nck, fc, n_tok, n_asg):
    i = pl.program_id(0)
    n = pl.num_programs(0)
    nsub = ns_ref[i]
    e = te_ref[i]
    cpi = cpi_ref[i]
    rows = rows_ref[i]
    d, f2 = wgub.shape
    f = f2 // 2
    dh = d // 2
    ckr = d // nck
    ckd = f // nck
    nst = stage.shape[0]
    idw = ids.shape[0] // 2
    nq = idw // ID_CHUNK
    id_base = (i % 2) * idw + (pos_ref[i] & (ID_CHUNK - 1))

    def ids_copy(item, q, slot):
        c0 = lax.shift_right_logical(pos_ref[item], ID_CHUNK.bit_length() - 1)
        src = ord_ref.at[pl.ds(pl.multiple_of((c0 + q) * ID_CHUNK, ID_CHUNK), ID_CHUNK)]
        dst = ids.at[pl.ds(pl.multiple_of(slot * idw + q * ID_CHUNK, ID_CHUNK), ID_CHUNK)]
        return pltpu.make_async_copy(src, dst, isem.at[slot])

    def token_of(a):
        return a & (n_tok - 1) if n_tok & (n_tok - 1) == 0 else lax.rem(a, n_tok)

    def gather(s, slot, r0=0, r1=ts):
        for r in range(r0, r1):
            tok = token_of(ids[id_base + s * ts + r])
            pltpu.make_async_copy(hfp_ref.at[pl.ds(tok, 1)], xin.at[slot, pl.ds(r, 1)], xsem.at[slot]).start()

    def gather_wait(slot):
        for r in range(ts):
            pltpu.make_async_copy(hfp_ref.at[pl.ds(0, 1)], xin.at[slot, pl.ds(0, 1)], xsem.at[slot]).wait()

    def scatter(s, slot, r0=0, r1=ts):
        for r in range(r0, r1):
            g = s * ts + r
            dst = jnp.where(g < rows, ids[id_base + g], n_asg + slot * ts + r)
            pltpu.make_async_copy(ybuf.at[slot, pl.ds(r, 1)], ykt_ref.at[pl.ds(dst, 1)], ysem.at[slot]).start()

    def scatter_wait(slot):
        for r in range(ts):
            pltpu.make_async_copy(ybuf.at[slot, pl.ds(0, 1)], ykt_ref.at[pl.ds(0, 1)], ysem.at[slot]).wait()

    def gu_copy(ee, c, slot):
        return pltpu.make_async_copy(wgu_ref.at[ee, pl.ds(pl.multiple_of(c * ckr, ckr), ckr), :],
                                     stage.at[slot], wsem.at[slot])

    def d_copy(ee, c, slot):
        return pltpu.make_async_copy(wd_ref.at[ee, pl.ds(pl.multiple_of(c * ckd, ckd), ckd), :],
                                     stage.at[slot, pl.ds(0, ckd), pl.ds(0, d)], wsem.at[slot])

    def gu_step(ee, c):
        slot = c % nst
        gu_copy(ee, c, slot).wait()
        wgub[pl.ds(pl.multiple_of(c * ckr, ckr), ckr), :] = stage[slot].astype(BF16)

        @pl.when(c + nst < nck)
        def _():
            gu_copy(ee, c + nst, slot).start(priority=WEIGHT_DMA_PRIORITY)

    def d_step(ee, c):
        slot = c % nst
        d_copy(ee, c, slot).wait()
        wdb[pl.ds(pl.multiple_of(c * ckd, ckd), ckd), :] = stage[slot, :ckd, :d].astype(BF16)

        @pl.when(c + nst < nck)
        def _():
            d_copy(ee, c + nst, slot).start(priority=WEIGHT_DMA_PRIORITY)

    def prime(copy_fn, ee):
        for q in range(min(nst, nck)):
            copy_fn(ee, q, q).start(priority=WEIGHT_DMA_PRIORITY)

    def steps_after(s, step_fn, ee):
        def one(jj, carry):
            c = s * cpi + jj

            @pl.when(c < nck)
            def _():
                step_fn(ee, c)
            return carry
        lax.fori_loop(0, cpi, one, 0)

    @pl.when(nsub > 0)
    def _():
        nxt = jnp.minimum(i + 1, n - 1)
        has_next = (i + 1 < n) & (ns_ref[nxt] > 0)

        @pl.when(i == 0)
        def _():
            for q in range(nq):
                ids_copy(0, q, 0).start()
            prime(gu_copy, e)

            def first(c, carry):
                gu_step(e, c)
                return carry
            lax.fori_loop(0, nck, first, 0)

        for q in range(nq):
            ids_copy(i, q, i % 2).wait()

        @pl.when(has_next)
        def _():
            for q in range(nq):
                ids_copy(nxt, q, nxt % 2).start()

        prime(d_copy, e)
        gather(0, 0)
        bgu = bgu_ref[e]

        def phase_a(s, carry):
            slot = s % 2
            gather_wait(slot)
            p = xin[slot]
            xb = jnp.concatenate([_unpack_lo(p).astype(BF16), _unpack_hi(p).astype(BF16)], axis=-1)
            nc = f // fc
            for cc in range(nc):
                gather(s + 1, 1 - slot, cc * ts // nc, (cc + 1) * ts // nc)
                gate = jnp.dot(xb, wgub[:, cc * fc:(cc + 1) * fc], preferred_element_type=F32)
                gate = gate + bgu[:, cc * fc:(cc + 1) * fc]
                up = jnp.dot(xb, wgub[:, f + cc * fc:f + (cc + 1) * fc], preferred_element_type=F32)
                up = up + bgu[:, f + cc * fc:f + (cc + 1) * fc]
                gate = jnp.minimum(gate, SWIGLU_LIMIT)
                up = jnp.clip(up, -SWIGLU_LIMIT, SWIGLU_LIMIT)
                a = (up + 1.0) * gate * jax.nn.sigmoid(SWIGLU_ALPHA * gate)
                act[s, :, cc * fc:(cc + 1) * fc] = a.astype(BF16)
            steps_after(s, d_step, e)
            return carry

        lax.fori_loop(0, nsub, phase_a, 0)
        gather_wait(nsub % 2)

        e_next = te_ref[nxt]
        load_next = has_next & (e_next != e)

        @pl.when(load_next)
        def _():
            prime(gu_copy, e_next)

        bd = bd_ref[e]

        def down(s, slot, scatter_prev):
            a = act[s]
            if scatter_prev:
                scatter(s - 1, 1 - slot)
            fcb = min(fc, dh // 2)
            for cc in range(dh // fcb):
                lo = slice(cc * fcb, (cc + 1) * fcb)
                hi = slice(dh + cc * fcb, dh + (cc + 1) * fcb)
                y_lo = jnp.dot(a, wdb[:, lo], preferred_element_type=F32) + bd[:, lo]
                y_hi = jnp.dot(a, wdb[:, hi], preferred_element_type=F32) + bd[:, hi]
                ybuf[slot, :, lo] = _pack_bf16_pair(y_lo, y_hi)

        def next_weights(s):
            @pl.when(load_next)
            def _():
                steps_after(s, gu_step, e_next)

        down(0, 0, False)
        next_weights(0)

        def phase_b(s, carry):
            slot = s % 2

            @pl.when(s >= 2)
            def _():
                scatter_wait(slot)

            down(s, slot, True)
            next_weights(s)
            return carry

        lax.fori_loop(1, nsub, phase_b, 0)
        scatter(nsub - 1, (nsub - 1) % 2)
        scatter_wait((nsub - 1) % 2)

        @pl.when(nsub >= 2)
        def _():
            scatter_wait(nsub % 2)


def _id_window(c):
    return ((ID_CHUNK - 1 + c.tm_e + c.ts_e) // ID_CHUNK + 1) * ID_CHUNK


def _moe(c, sched, order_pad, hfp, w_gu, b_gu3, w_d, b_d3):
    n_tok, dh = hfp.shape
    d = 2 * dh
    f = c.DFF
    tm, ts, nck = c.tm_e, c.ts_e, c.nck_e
    n_asg = n_tok * c.K
    grid_spec = pltpu.PrefetchScalarGridSpec(
        num_scalar_prefetch=5,
        grid=(_n_tiles(c),),
        in_specs=[
            pl.BlockSpec(memory_space=pl.ANY),
            pl.BlockSpec(memory_space=pl.ANY),
            pl.BlockSpec(memory_space=pl.ANY),
            pl.BlockSpec(memory_space=pl.ANY),
            pl.BlockSpec((c.E, 1, 2 * f), lambda i, *_: (0, 0, 0), pipeline_mode=pl.Buffered(1)),
            pl.BlockSpec((c.E, 1, d), lambda i, *_: (0, 0, 0), pipeline_mode=pl.Buffered(1)),
        ],
        out_specs=pl.BlockSpec(memory_space=pl.ANY),
        scratch_shapes=[
            pltpu.VMEM((d, 2 * f), BF16),
            pltpu.VMEM((f, d), BF16),
            pltpu.VMEM((c.nst_e, d // nck, 2 * f), F32),
            pltpu.VMEM((tm // ts, ts, f), BF16),
            pltpu.VMEM((2, ts, dh), U32),
            pltpu.VMEM((2, ts, dh), U32),
            pltpu.SMEM((2 * _id_window(c),), I32),
            pltpu.SemaphoreType.DMA((c.nst_e,)),
            pltpu.SemaphoreType.DMA((2,)),
            pltpu.SemaphoreType.DMA((2,)),
            pltpu.SemaphoreType.DMA((2,)),
        ],
    )
    kern = functools.partial(_moe_kernel, ts=ts, tm=tm, nck=nck, fc=c.fc_e, n_tok=n_tok, n_asg=n_asg)
    return pl.pallas_call(
        kern,
        grid_spec=grid_spec,
        out_shape=jax.ShapeDtypeStruct((n_asg + 2 * ts, dh), U32),
        compiler_params=_cparams(("arbitrary",)),
        name="experts",
    )(*sched, order_pad, hfp, w_gu, w_d, b_gu3, b_d3)


def _combine_kernel(x2_ref, gate_ref, g_ref, *refs, topk):
    y_refs, o_ref = refs[:topk], refs[topk]
    d = x2_ref.shape[1]
    dh = d // 2
    lo = x2_ref[:, :dh]
    hi = x2_ref[:, dh:]
    gates = gate_ref[...]
    for kk in range(topk):
        wk = gates[:, kk:kk + 1]
        p = y_refs[kk][...]
        lo = lo + wk * _unpack_lo(p)
        hi = hi + wk * _unpack_hi(p)
    ms = (jnp.sum(lo * lo, axis=-1, keepdims=True) + jnp.sum(hi * hi, axis=-1, keepdims=True)) / d
    inv = lax.rsqrt(ms + RMS_EPS)
    o_ref[:, :dh] = lo * inv * g_ref[:, :dh]
    o_ref[:, dh:] = hi * inv * g_ref[:, dh:]


def _combine(c, x2, gates, g, ykt):
    n, d = x2.shape
    tm = c.tm_c
    nb = n // tm
    y_specs = [pl.BlockSpec((tm, d // 2), functools.partial(lambda i, kk: (kk * nb + i, 0), kk=kk))
               for kk in range(c.K)]
    return pl.pallas_call(
        functools.partial(_combine_kernel, topk=c.K),
        grid=(nb,),
        in_specs=[
            pl.BlockSpec((tm, d), lambda i: (i, 0)),
            pl.BlockSpec((tm, LANES), lambda i: (i, 0)),
            pl.BlockSpec((1, d), lambda i: (0, 0)),
        ] + y_specs,
        out_specs=pl.BlockSpec((tm, d), lambda i: (i, 0)),
        out_shape=jax.ShapeDtypeStruct((n, d), F32),
        compiler_params=_cparams(("parallel",)),
        name="combine",
    )(x2, gates, g, *([ykt] * c.K))


def _plan(c, idx, counts):
    tm, ts = c.tm_e, c.ts_e
    nk = idx.shape[0] * c.K
    order = jnp.argsort(idx.T.reshape(-1)).astype(I32)
    pad = _id_window(c) + (-nk) % ID_CHUNK
    order_pad = jnp.concatenate([order, jnp.zeros((pad,), I32)])
    counts = counts.astype(I32)
    start = jnp.cumsum(counts) - counts
    tiles_per_e = (counts + tm - 1) // tm
    tile_end = jnp.cumsum(tiles_per_e)
    tile_start = tile_end - tiles_per_e
    t = jnp.arange(_n_tiles(c), dtype=I32)
    n_used = tile_end[-1]
    tc = jnp.minimum(t, n_used - 1)
    te = jnp.minimum(jnp.searchsorted(tile_end, tc, side="right"), c.E - 1).astype(I32)
    j = tc - tile_start[te]
    used = t < n_used
    rows = jnp.where(used, jnp.clip(counts[te] - j * tm, 0, tm), 0).astype(I32)
    nsub = (rows + ts - 1) // ts
    pos = jnp.where(used, start[te] + j * tm, 0).astype(I32)
    cpi = (c.nck_e + jnp.maximum(nsub, 1) - 1) // jnp.maximum(nsub, 1)
    return order_pad, (te, nsub.astype(I32), cpi.astype(I32), pos, rows)


def _n_tiles(c):
    return -(-(c.B * c.S * c.K) // c.tm_e) + c.E


def _forward(c, x, mem, norm_mix_g, w_in, b_forget, sg_ln_g, sg_ln_b, w_spatial, b_spatial,
             w_branch_a, w_branch_b, w_out, norm_x_g, norm_mem_g, w_xq, w_xkv, w_xo,
             norm_ffn_g, w_router, b_router, w_gate_up, b_gate_up, w_down, b_down, norm_final_g):
    B, S, D = x.shape
    n = B * S
    fw = c.FH * LANES
    sw = c.SG * LANES
    x2d = x.reshape(n, D)

    o_f = 3 * fw
    o_z = o_f + c.FH
    o_g = o_z + 2 * sw
    w_main = jnp.concatenate([w_in[:, o_g:], w_in[:, o_z:o_g], w_in[:, :o_f]], axis=1).astype(BF16)
    w_f = jnp.pad(w_in[:, o_f:o_z], ((0, 0), (0, LANES - c.FH))).astype(BF16)
    b_f = jnp.pad(b_forget.astype(F32), (0, LANES - c.FH)).reshape(1, LANES)
    proj, logf = _in_proj(c, x2d, norm_mix_g.reshape(1, D), w_main, w_f, b_f, LOG2E * LANES ** -0.5)

    logf_bhs = logf[:, :c.FH].reshape(B, S, c.FH).transpose(0, 2, 1)
    csum = _cumsum(c, logf_bhs)
    crow = csum.reshape(B * c.FH, 1, S)
    ccol = csum.reshape(B * c.FH, S, 1)
    qcol0 = (2 * D + 2 * sw) // LANES
    attn = _fox(c, proj, crow, ccol, qcol0)

    b_s_full = jnp.broadcast_to(b_spatial.astype(F32)[:, :, None], (c.SG, c.SGC, LANES))
    x1 = _mix(c, x2d, attn, proj, sg_ln_g.reshape(1, sw), sg_ln_b.reshape(1, sw), w_spatial, b_s_full,
              w_branch_a.astype(BF16), w_branch_b.astype(BF16), w_out.astype(BF16))

    kv = _mem_kv(c, mem.reshape(B * c.MEM, D), norm_mem_g.reshape(1, D), w_xkv.astype(BF16))
    w_r32 = jnp.pad(w_router.astype(F32), ((0, 0), (0, LANES - c.E)))
    w_r_hi = w_r32.astype(BF16)
    w_r = jnp.concatenate([w_r_hi, (w_r32 - w_r_hi.astype(F32)).astype(BF16)], axis=1)
    b_r = jnp.pad(b_router.astype(F32), (0, LANES - c.E)).reshape(1, LANES)
    x2, hfp, idx, gates, counts = _xattn(
        c, x1, norm_x_g.reshape(1, D), w_xq.astype(BF16), kv, w_xo.astype(BF16),
        norm_ffn_g.reshape(1, D), w_r, b_r, LANES ** -0.5)

    order_pad, sched = _plan(c, idx[:, :c.K], counts[0, :c.E])
    ykt = _moe(c, sched, order_pad, hfp, w_gate_up, b_gate_up.reshape(c.E, 1, 2 * c.DFF),
               w_down, b_down.reshape(c.E, 1, D))
    out = _combine(c, x2, gates, norm_final_g.reshape(1, D), ykt)
    return out.reshape(B, S, D)


_CFG = Cfg(B=4, S=4096, D=2048, MEM=256, FH=8, SG=8, SGC=128, XH=4, E=32, K=4, DFF=2048,
           tm_in=1024, tn_in=1024, tq=1024, tm_mix=256, tm_x=512, tm_e=2560, ts_e=256, fc_e=512, nck_e=16, nst_e=4,
           tm_c=256, tc_cs=512, fox_parts=2)


@jax.jit
def kernel(x, mem, norm_mix_g, w_in, b_forget, sg_ln_g, sg_ln_b, w_spatial, b_spatial, w_branch_a, w_branch_b,
           w_out, norm_x_g, norm_mem_g, w_xq, w_xkv, w_xo, norm_ffn_g, w_router, b_router, w_gate_up,
           b_gate_up, w_down, b_down, norm_final_g):
    return _forward(_CFG, x, mem, norm_mix_g, w_in, b_forget, sg_ln_g, sg_ln_b, w_spatial, b_spatial,
                    w_branch_a, w_branch_b, w_out, norm_x_g, norm_mem_g, w_xq, w_xkv, w_xo,
                    norm_ffn_g, w_router, b_router, w_gate_up, b_gate_up, w_down, b_down, norm_final_g)
```

```python
import functools
from typing import NamedTuple

import jax
import jax.numpy as jnp
from jax import lax
from jax.experimental import pallas as pl
from jax.experimental.pallas import tpu as pltpu

F32 = jnp.float32
BF16 = jnp.bfloat16
U32 = jnp.uint32
I32 = jnp.int32

LANES = 128
VMEM_LIMIT = 56 * 1024 * 1024

RMS_EPS = 1e-6
LN_EPS = 1e-5
SWIGLU_LIMIT = 7.0
SWIGLU_ALPHA = 1.702
GELU_C = 0.7978845608028654
LOG2E = 1.4426950408889634
ID_CHUNK = 1024
WEIGHT_DMA_PRIORITY = 1


class Cfg(NamedTuple):
    B: int
    S: int
    D: int
    MEM: int
    FH: int
    SG: int
    SGC: int
    XH: int
    E: int
    K: int
    DFF: int
    tm_in: int
    tn_in: int
    tq: int
    tm_mix: int
    tm_x: int
    tm_e: int
    ts_e: int
    fc_e: int
    nck_e: int
    nst_e: int
    cps_e: int
    tm_c: int
    tc_cs: int
    fox_parts: int


def _cparams(sem):
    return pltpu.CompilerParams(dimension_semantics=sem, vmem_limit_bytes=VMEM_LIMIT)


def _rms(x, g):
    ms = jnp.mean(x * x, axis=-1, keepdims=True)
    return x * lax.rsqrt(ms + RMS_EPS) * g


def _pack_bf16_pair(a, b):
    def rne(v):
        bits = lax.bitcast_convert_type(v, U32)
        return bits + jnp.uint32(0x7FFF) + ((bits >> 16) & jnp.uint32(1))
    return (rne(a) >> 16) | (rne(b) & jnp.uint32(0xFFFF0000))


def _unpack_lo(p):
    return lax.bitcast_convert_type(p << 16, F32)


def _unpack_hi(p):
    return lax.bitcast_convert_type(p & jnp.uint32(0xFFFF0000), F32)


def _in_proj_kernel(x_ref, g_ref, w_ref, wf_ref, bf_ref, o_ref, f_ref, h_ref, *, nj_gate, nj_z, nj_q, qscale):
    j = pl.program_id(1)

    @pl.when(j == 0)
    def _():
        hb = _rms(x_ref[...], g_ref[...]).astype(BF16)
        h_ref[...] = hb
        f = jnp.dot(hb, wf_ref[...], preferred_element_type=F32) + bf_ref[...]
        f_ref[...] = jnp.minimum(f, 0.0) - jnp.log1p(jnp.exp(-jnp.abs(f)))

    acc = jnp.dot(h_ref[...], w_ref[...], preferred_element_type=F32)

    is_gate = j < nj_gate
    is_z = (j >= nj_gate) & (j < nj_gate + nj_z)
    is_q = (j >= nj_gate + nj_z) & (j < nj_gate + nj_z + nj_q)
    a1 = jnp.where(is_gate, 0.5, jnp.where(is_z, GELU_C, 0.0)).astype(F32)
    a3 = jnp.where(is_z, GELU_C * 0.044715, 0.0).astype(F32)
    b0 = jnp.where(is_gate, 0.5, 0.0).astype(F32)
    b1 = jnp.where(is_gate, 0.0, jnp.where(is_z, 0.5, jnp.where(is_q, qscale, 1.0))).astype(F32)
    th = jnp.tanh(acc * (a1 + a3 * (acc * acc)))
    o_ref[...] = ((b0 + b1 * acc) * (1.0 + th)).astype(o_ref.dtype)


def _in_proj(c, x2d, g, w_main, w_f, b_f, qscale):
    n, d = x2d.shape
    nc = w_main.shape[1]
    tm, tn = c.tm_in, c.tn_in
    fw = c.FH * LANES
    kern = functools.partial(_in_proj_kernel, nj_gate=2 * d // tn, nj_z=2 * c.SG * LANES // tn,
                             nj_q=fw // tn, qscale=qscale)
    return pl.pallas_call(
        kern,
        grid=(n // tm, nc // tn),
        in_specs=[
            pl.BlockSpec((tm, d), lambda i, j: (i, 0)),
            pl.BlockSpec((1, d), lambda i, j: (0, 0)),
            pl.BlockSpec((d, tn), lambda i, j: (0, j)),
            pl.BlockSpec((d, LANES), lambda i, j: (0, 0)),
            pl.BlockSpec((1, LANES), lambda i, j: (0, 0)),
        ],
        out_specs=[
            pl.BlockSpec((tm, tn), lambda i, j: (i, j)),
            pl.BlockSpec((tm, LANES), lambda i, j: (i, 0)),
        ],
        out_shape=[jax.ShapeDtypeStruct((n, nc), BF16), jax.ShapeDtypeStruct((n, LANES), F32)],
        scratch_shapes=[pltpu.VMEM((tm, d), BF16)],
        compiler_params=_cparams(("parallel", "arbitrary")),
        name="in_proj",
    )(x2d, g, w_main, w_f, b_f)


def _cumsum_kernel(f_ref, o_ref, *, tc):
    rows, s = f_ref.shape[1], f_ref.shape[2]
    r = lax.broadcasted_iota(I32, (tc, tc), 0)
    col = lax.broadcasted_iota(I32, (tc, tc), 1)
    upper = (r <= col).astype(F32)
    carry = jnp.zeros((rows, 1), F32)
    for i in range(s // tc):
        blk = f_ref[0, :, i * tc:(i + 1) * tc]
        cs = jnp.dot(blk, upper, preferred_element_type=F32, precision=lax.Precision.HIGHEST) + carry
        o_ref[0, :, i * tc:(i + 1) * tc] = cs * LOG2E
        carry = cs[:, tc - 1:tc]


def _cumsum(c, logf_bhs):
    b, h, s = logf_bhs.shape
    return pl.pallas_call(
        functools.partial(_cumsum_kernel, tc=c.tc_cs),
        grid=(b,),
        in_specs=[pl.BlockSpec((1, h, s), lambda i: (i, 0, 0))],
        out_specs=pl.BlockSpec((1, h, s), lambda i: (i, 0, 0)),
        out_shape=jax.ShapeDtypeStruct((b, h, s), F32),
        compiler_params=_cparams(("parallel",)),
        name="cumsum",
    )(logf_bhs)


def _fox_kernel(q_ref, k_ref, v_ref, crow_ref, ccol_ref, o_ref, m_ref, l_ref, acc_ref, cq_ref, s_ref,
                *, t, parts):
    qi = pl.program_id(2)
    hr = t // parts
    m_ref[...] = jnp.full(m_ref.shape, -jnp.inf, F32)
    l_ref[...] = jnp.zeros(l_ref.shape, F32)
    acc_ref[...] = jnp.zeros(acc_ref.shape, F32)
    cq_ref[...] = jnp.broadcast_to(ccol_ref[0], cq_ref.shape)

    def scores(ks):
        k = k_ref[pl.ds(ks, t), :]
        return lax.dot_general(q_ref[...], k, (((1,), (1,)), ((), ())), preferred_element_type=F32)

    def block(part, ks, width, masked):
        rows = slice(part * hr, (part + 1) * hr)
        v = v_ref[pl.ds(ks, width), :]
        s = s_ref[rows, :width]
        cq = cq_ref[rows, :]
        crow = crow_ref[0, :, pl.ds(ks, width)]
        nj = width // LANES
        sj = [s[:, j * LANES:(j + 1) * LANES] + cq - crow[:, j * LANES:(j + 1) * LANES] for j in range(nj)]
        if masked:
            row = lax.broadcasted_iota(I32, (hr, LANES), 0) + part * hr
            col = lax.broadcasted_iota(I32, (hr, LANES), 1)
            sj = [jnp.where(col + j * LANES <= row, sj[j], -jnp.inf) for j in range(nj)]
        mx = sj[0]
        for j in range(1, nj):
            mx = jnp.maximum(mx, sj[j])
        m_prev = m_ref[rows, :]
        m_next = jnp.maximum(m_prev, jnp.max(mx, axis=-1, keepdims=True))
        alpha = jnp.exp2(m_prev - m_next)
        pj = [jnp.exp2(sj[j] - m_next) for j in range(nj)]
        psum = pj[0]
        for j in range(1, nj):
            psum = psum + pj[j]
        p = jnp.concatenate([x.astype(BF16) for x in pj], axis=-1)
        l_ref[rows, :] = alpha * l_ref[rows, :] + psum
        acc_ref[rows, :] = alpha * acc_ref[rows, :] + jnp.dot(p, v, preferred_element_type=F32)
        m_ref[rows, :] = m_next

    s_ref[...] = scores(0)

    def body(kc, carry):
        ks = pl.multiple_of(kc * t, t)
        s_next = scores(pl.multiple_of(ks + t, t))
        for part in range(parts):
            block(part, ks, t, False)
        s_ref[...] = s_next
        return carry

    lax.fori_loop(0, qi, body, 0)
    kd = pl.multiple_of(qi * t, t)
    for part in range(parts):
        block(part, kd, (part + 1) * hr, True)
    l = jnp.sum(l_ref[...], axis=-1, keepdims=True)
    o_ref[...] = (acc_ref[...] / l).astype(o_ref.dtype)


def _fox(c, proj, crow, ccol, qcol0):
    n = c.B * c.S
    t = c.tq
    nq = c.S // t
    h = c.FH
    return pl.pallas_call(
        functools.partial(_fox_kernel, t=t, parts=c.fox_parts),
        grid=(c.B, h, nq),
        in_specs=[
            pl.BlockSpec((t, LANES), lambda b, hh, qi: (b * nq + qi, qcol0 + hh)),
            pl.BlockSpec((c.S, LANES), lambda b, hh, qi: (b, qcol0 + h + hh)),
            pl.BlockSpec((c.S, LANES), lambda b, hh, qi: (b, qcol0 + 2 * h + hh)),
            pl.BlockSpec((1, 1, c.S), lambda b, hh, qi: (b * h + hh, 0, 0)),
            pl.BlockSpec((1, t, 1), lambda b, hh, qi: (b * h + hh, qi, 0)),
        ],
        out_specs=pl.BlockSpec((t, LANES), lambda b, hh, qi: (b * nq + qi, hh)),
        out_shape=jax.ShapeDtypeStruct((n, h * LANES), BF16),
        scratch_shapes=[pltpu.VMEM((t, LANES), F32)] * 4 + [pltpu.VMEM((t, t), F32)],
        compiler_params=_cparams(("parallel", "parallel", "arbitrary")),
        name="fox",
    )(proj, proj, proj, crow, ccol)


def _mix_kernel(x_ref, a_ref, u_ref, v_ref, ga_ref, gb_ref, lng_ref, lnb_ref, ws_ref, bs_ref,
                wa_ref, wb_ref, wo_ref, o_ref, sg_ref, *, sgc, groups):
    tm = x_ref.shape[0]
    v = v_ref[...].astype(F32)
    mu = jnp.mean(v, axis=-1, keepdims=True)
    vc = v - mu
    var = jnp.mean(vc * vc, axis=-1, keepdims=True)
    vn = (vc * lax.rsqrt(var + LN_EPS) * lng_ref[...] + lnb_ref[...]).astype(BF16)
    row = lax.broadcasted_iota(I32, (sgc, sgc), 0)
    col = lax.broadcasted_iota(I32, (sgc, sgc), 1)
    for g in range(groups):
        w = jnp.where(col <= row, ws_ref[g], 0.0).astype(BF16)
        bias = bs_ref[g]
        for ci in range(tm // sgc):
            rs = slice(ci * sgc, (ci + 1) * sgc)
            cs = slice(g * LANES, (g + 1) * LANES)
            mixed = jnp.dot(w, vn[rs, cs], preferred_element_type=F32) + bias
            sg_ref[rs, cs] = (u_ref[rs, cs].astype(F32) * mixed).astype(BF16)
    ya = jnp.dot(a_ref[...], wa_ref[...], preferred_element_type=F32)
    yb = jnp.dot(sg_ref[...], wb_ref[...], preferred_element_type=F32)
    merged = (ga_ref[...].astype(F32) * ya + gb_ref[...].astype(F32) * yb).astype(BF16)
    o_ref[...] = x_ref[...] + jnp.dot(merged, wo_ref[...], preferred_element_type=F32)


def _const_spec(shape):
    nd = len(shape)
    return pl.BlockSpec(shape, lambda i: (0,) * nd, pipeline_mode=pl.Buffered(1))


def _mix(c, x2d, attn, proj, ln_g, ln_b, w_s, b_s_full, w_a, w_b, w_o):
    n, d = x2d.shape
    tm = c.tm_mix
    fw = c.FH * LANES
    sw = c.SG * LANES
    ucol = 2 * d // sw
    return pl.pallas_call(
        functools.partial(_mix_kernel, sgc=c.SGC, groups=c.SG),
        grid=(n // tm,),
        in_specs=[
            pl.BlockSpec((tm, d), lambda i: (i, 0)),
            pl.BlockSpec((tm, fw), lambda i: (i, 0)),
            pl.BlockSpec((tm, sw), lambda i: (i, ucol)),
            pl.BlockSpec((tm, sw), lambda i: (i, ucol + 1)),
            pl.BlockSpec((tm, d), lambda i: (i, 0)),
            pl.BlockSpec((tm, d), lambda i: (i, 1)),
            _const_spec((1, sw)),
            _const_spec((1, sw)),
            _const_spec((c.SG, c.SGC, c.SGC)),
            _const_spec((c.SG, c.SGC, LANES)),
            _const_spec((fw, d)),
            _const_spec((sw, d)),
            _const_spec((d, d)),
        ],
        out_specs=pl.BlockSpec((tm, d), lambda i: (i, 0)),
        out_shape=jax.ShapeDtypeStruct((n, d), F32),
        scratch_shapes=[pltpu.VMEM((tm, sw), BF16)],
        compiler_params=_cparams(("parallel",)),
        name="mix",
    )(x2d, attn, proj, proj, proj, proj, ln_g, ln_b, w_s, b_s_full, w_a, w_b, w_o)


def _mem_kv_kernel(m_ref, g_ref, w_ref, o_ref):
    hm = _rms(m_ref[...], g_ref[...]).astype(BF16)
    o_ref[...] = jnp.dot(hm, w_ref[...], preferred_element_type=F32).astype(o_ref.dtype)


def _mem_kv(c, mem2d, g, w_xkv):
    n, d = mem2d.shape
    nc = w_xkv.shape[1]
    tm = c.MEM
    return pl.pallas_call(
        _mem_kv_kernel,
        grid=(n // tm,),
        in_specs=[pl.BlockSpec((tm, d), lambda i: (i, 0)), _const_spec((1, d)), _const_spec((d, nc))],
        out_specs=pl.BlockSpec((tm, nc), lambda i: (i, 0)),
        out_shape=jax.ShapeDtypeStruct((n, nc), BF16),
        compiler_params=_cparams(("parallel",)),
        name="mem_kv",
    )(mem2d, g, w_xkv)


def _xattn_kernel(x1_ref, gx_ref, wq_ref, kv_ref, wo_ref, gf_ref, wr_ref, br_ref,
                  x2_ref, hfp_ref, idx_ref, gate_ref, cnt_ref, carry_ref,
                  *, heads, n_exp, topk, qscale):
    i = pl.program_id(0)
    tm, d = x1_ref.shape
    xw = heads * LANES
    x1 = x1_ref[...]
    hx = _rms(x1, gx_ref[...]).astype(BF16)
    q = (jnp.dot(hx, wq_ref[...], preferred_element_type=F32) * qscale).astype(BF16)
    outs = []
    for h in range(heads):
        k = kv_ref[:, h * LANES:(h + 1) * LANES]
        v = kv_ref[:, xw + h * LANES:xw + (h + 1) * LANES]
        s = lax.dot_general(q[:, h * LANES:(h + 1) * LANES], k, (((1,), (1,)), ((), ())),
                            preferred_element_type=F32)
        s = s - jnp.max(s, axis=-1, keepdims=True)
        p = jnp.exp(s)
        p = p / jnp.sum(p, axis=-1, keepdims=True)
        outs.append(jnp.dot(p.astype(BF16), v, preferred_element_type=F32).astype(BF16))
    o = jnp.concatenate(outs, axis=-1)
    x2 = x1 + jnp.dot(o, wo_ref[...], preferred_element_type=F32)
    x2_ref[...] = x2

    hf = _rms(x2, gf_ref[...])
    hfp_ref[...] = _pack_bf16_pair(hf[:, :d // 2], hf[:, d // 2:])

    h_hi = hf.astype(BF16)
    h_lo = (hf - h_hi.astype(F32)).astype(BF16)
    l_hi = jnp.dot(h_hi, wr_ref[...], preferred_element_type=F32)
    l_lo = jnp.dot(h_lo, wr_ref[:, :LANES], preferred_element_type=F32)
    logits = l_hi[:, :LANES] + l_hi[:, LANES:] + l_lo + br_ref[...]
    lane = lax.broadcasted_iota(I32, (tm, LANES), 1)
    lg = jnp.where(lane < n_exp, logits, -jnp.inf)
    vals, idxs = [], []
    for _ in range(topk):
        m = jnp.max(lg, axis=-1, keepdims=True)
        ix = jnp.min(jnp.where(lg == m, lane, LANES), axis=-1, keepdims=True)
        vals.append(m)
        idxs.append(ix)
        lg = jnp.where(lane == ix, -jnp.inf, lg)
    es = [jnp.exp(vv - vals[0]) for vv in vals]
    denom = es[0]
    for e in es[1:]:
        denom = denom + e

    @pl.when(i == 0)
    def _():
        carry_ref[...] = jnp.zeros(carry_ref.shape, F32)

    onehot = jnp.zeros((tm, LANES), F32)
    for ix in idxs:
        onehot = onehot + (lane == ix).astype(F32)
    idx_out = jnp.zeros((tm, LANES), I32)
    gate_out = jnp.zeros((tm, LANES), F32)
    for kk in range(topk):
        idx_out = jnp.where(lane == kk, idxs[kk], idx_out)
        gate_out = jnp.where(lane == kk, es[kk] / denom, gate_out)
    idx_ref[...] = idx_out
    gate_ref[...] = gate_out
    carry_ref[...] = carry_ref[...] + jnp.sum(onehot, axis=0, keepdims=True)
    cnt_ref[...] = carry_ref[...]


def _xattn(c, x1, gx, w_xq, kv, w_xo, gf, w_r, b_r, qscale):
    n, d = x1.shape
    tm = c.tm_x
    xw = c.XH * LANES
    per_b = c.S // tm
    kern = functools.partial(_xattn_kernel, heads=c.XH, n_exp=c.E, topk=c.K, qscale=qscale)
    return pl.pallas_call(
        kern,
        grid=(n // tm,),
        in_specs=[
            pl.BlockSpec((tm, d), lambda i: (i, 0)),
            _const_spec((1, d)),
            _const_spec((d, xw)),
            pl.BlockSpec((c.MEM, 2 * xw), lambda i: (i // per_b, 0)),
            _const_spec((xw, d)),
            _const_spec((1, d)),
            _const_spec((d, 2 * LANES)),
            _const_spec((1, LANES)),
        ],
        out_specs=[
            pl.BlockSpec((tm, d), lambda i: (i, 0)),
            pl.BlockSpec((tm, d // 2), lambda i: (i, 0)),
            pl.BlockSpec((tm, LANES), lambda i: (i, 0)),
            pl.BlockSpec((tm, LANES), lambda i: (i, 0)),
            pl.BlockSpec((1, LANES), lambda i: (0, 0)),
        ],
        out_shape=[
            jax.ShapeDtypeStruct((n, d), F32),
            jax.ShapeDtypeStruct((n, d // 2), U32),
            jax.ShapeDtypeStruct((n, LANES), I32),
            jax.ShapeDtypeStruct((n, LANES), F32),
            jax.ShapeDtypeStruct((1, LANES), F32),
        ],
        scratch_shapes=[pltpu.VMEM((1, LANES), F32)],
        compiler_params=_cparams(("arbitrary",)),
        name="xattn_router",
    )(x1, gx, w_xq, kv, w_xo, gf, w_r, b_r)


def _moe_kernel(te_ref, ns_ref, pos_ref, rows_ref, ord_ref, hfp_ref, wgu_ref, wd_ref, bgu_ref, bd_ref,
                ykt_ref, wgub, wdb, stage, act, xin, ybuf, ids, wsem, xsem, ysem, isem,
                *, ts, tm, nck, cps, fc, n_tok, n_asg):
    i = pl.program_id(0)
    n = pl.num_programs(0)
    nsub = ns_ref[i]
    e = te_ref[i]
    rows = rows_ref[i]
    dh = xin.shape[2]
    d = 2 * dh
    f = act.shape[2]
    ckr = d // nck
    ckd = f // nck
    nst = stage.shape[0]
    idw = ids.shape[0] // 2
    nq = idw // ID_CHUNK
    id_base = (i % 2) * idw + (pos_ref[i] & (ID_CHUNK - 1))

    def ids_copy(item, q, slot):
        c0 = lax.shift_right_logical(pos_ref[item], ID_CHUNK.bit_length() - 1)
        src = ord_ref.at[pl.ds(pl.multiple_of((c0 + q) * ID_CHUNK, ID_CHUNK), ID_CHUNK)]
        dst = ids.at[pl.ds(pl.multiple_of(slot * idw + q * ID_CHUNK, ID_CHUNK), ID_CHUNK)]
        return pltpu.make_async_copy(src, dst, isem.at[slot])

    def token_of(a):
        return a & (n_tok - 1) if n_tok & (n_tok - 1) == 0 else lax.rem(a, n_tok)

    def gather(s, slot, r0=0, r1=ts):
        for r in range(r0, r1):
            tok = token_of(ids[id_base + s * ts + r])
            pltpu.make_async_copy(hfp_ref.at[pl.ds(tok, 1)], xin.at[slot, pl.ds(r, 1)], xsem.at[slot]).start()

    def gather_wait(slot):
        for r in range(ts):
            pltpu.make_async_copy(hfp_ref.at[pl.ds(0, 1)], xin.at[slot, pl.ds(0, 1)], xsem.at[slot]).wait()

    def scatter(s, slot, r0=0, r1=ts):
        for r in range(r0, r1):
            g = s * ts + r
            dst = jnp.where(g < rows, ids[id_base + g], n_asg + slot * ts + r)
            pltpu.make_async_copy(ybuf.at[slot, pl.ds(r, 1)], ykt_ref.at[pl.ds(dst, 1)], ysem.at[slot]).start()

    def scatter_wait(slot):
        for r in range(ts):
            pltpu.make_async_copy(ybuf.at[slot, pl.ds(0, 1)], ykt_ref.at[pl.ds(0, 1)], ysem.at[slot]).wait()

    def gu_copy(ee, c, slot):
        return pltpu.make_async_copy(wgu_ref.at[ee, pl.ds(pl.multiple_of(c * ckr, ckr), ckr), :],
                                     stage.at[slot], wsem.at[slot])

    def d_copy(ee, c, slot):
        return pltpu.make_async_copy(wd_ref.at[ee, pl.ds(pl.multiple_of(c * ckd, ckd), ckd), :],
                                     stage.at[slot, pl.ds(0, ckd), pl.ds(0, d)], wsem.at[slot])

    def gu_step(ee, c):
        slot = c % nst
        gu_copy(ee, c, slot).wait()
        wgub[pl.ds(pl.multiple_of(c * ckr, ckr), ckr), :] = stage[slot].astype(BF16)

        @pl.when(c + nst < nck)
        def _():
            gu_copy(ee, c + nst, slot).start(priority=WEIGHT_DMA_PRIORITY)

    def d_step(ee, c):
        slot = c % nst
        d_copy(ee, c, slot).wait()
        wdb[pl.ds(pl.multiple_of(c * ckd, ckd), ckd), :] = stage[slot, :ckd, :d].astype(BF16)

        @pl.when(c + nst < nck)
        def _():
            d_copy(ee, c + nst, slot).start(priority=WEIGHT_DMA_PRIORITY)

    def prime(copy_fn, ee):
        for q in range(min(nst, nck)):
            copy_fn(ee, q, q).start(priority=WEIGHT_DMA_PRIORITY)

    def live_chunk(c, on):
        return (c < nck) if on is True else (on & (c < nck))

    def chunk_waits(s, copy_fn, ee, on):
        for jj in range(cps):
            c = s * cps + jj

            @pl.when(live_chunk(c, on))
            def _():
                copy_fn(ee, c, c % nst).wait()

    def chunk_casts(s, dst_ref, rows_per, width, on):
        spare = dst_ref.shape[0] - rows_per
        for jj in range(cps):
            c = s * cps + jj
            off = pl.multiple_of(jnp.where(live_chunk(c, on), c * rows_per, spare), rows_per)
            dst_ref[pl.ds(off, rows_per), :] = stage[c % nst, :rows_per, :width].astype(BF16)

    def chunk_starts(s, copy_fn, ee, on):
        for jj in range(cps):
            c = s * cps + jj

            @pl.when(live_chunk(c + nst, on))
            def _():
                copy_fn(ee, c + nst, c % nst).start(priority=WEIGHT_DMA_PRIORITY)

    def chunk_drain(first, step_fn, ee):
        def one(c, carry):
            step_fn(ee, c)
            return carry
        lax.fori_loop(first, nck, one, 0)

    @pl.when(nsub > 0)
    def _():
        nxt = jnp.minimum(i + 1, n - 1)
        has_next = (i + 1 < n) & (ns_ref[nxt] > 0)

        @pl.when(i == 0)
        def _():
            for q in range(nq):
                ids_copy(0, q, 0).start()
            prime(gu_copy, e)

            def first(c, carry):
                gu_step(e, c)
                return carry
            lax.fori_loop(0, nck, first, 0)

        for q in range(nq):
            ids_copy(i, q, i % 2).wait()

        @pl.when(has_next)
        def _():
            for q in range(nq):
                ids_copy(nxt, q, nxt % 2).start()

        prime(d_copy, e)
        gather(0, 0)
        bgu = bgu_ref[e]

        def phase_a(s, carry):
            slot = s % 2
            chunk_waits(s, d_copy, e, True)
            gather_wait(slot)
            p = xin[slot]
            xb = jnp.concatenate([_unpack_lo(p).astype(BF16), _unpack_hi(p).astype(BF16)], axis=-1)
            nc = f // fc
            for cc in range(nc):
                gather(s + 1, 1 - slot, cc * ts // nc, (cc + 1) * ts // nc)
                gate = jnp.dot(xb, wgub[:d, cc * fc:(cc + 1) * fc], preferred_element_type=F32)
                gate = gate + bgu[:, cc * fc:(cc + 1) * fc]
                up = jnp.dot(xb, wgub[:d, f + cc * fc:f + (cc + 1) * fc], preferred_element_type=F32)
                up = up + bgu[:, f + cc * fc:f + (cc + 1) * fc]
                gate = jnp.minimum(gate, SWIGLU_LIMIT)
                up = jnp.clip(up, -SWIGLU_LIMIT, SWIGLU_LIMIT)
                a = (up + 1.0) * gate * jax.nn.sigmoid(SWIGLU_ALPHA * gate)
                act[s, :, cc * fc:(cc + 1) * fc] = a.astype(BF16)
            chunk_casts(s, wdb, ckd, d, True)
            chunk_starts(s, d_copy, e, True)
            return carry

        lax.fori_loop(0, nsub, phase_a, 0)
        chunk_drain(cps * nsub, d_step, e)
        gather_wait(nsub % 2)

        e_next = te_ref[nxt]
        load_next = has_next & (e_next != e)

        @pl.when(load_next)
        def _():
            prime(gu_copy, e_next)

        bd = bd_ref[e]

        def down(s, slot, scatter_prev):
            a = act[s]
            if scatter_prev:
                scatter(s - 1, 1 - slot)
            fcb = min(fc, dh // 2)
            for cc in range(dh // fcb):
                lo = slice(cc * fcb, (cc + 1) * fcb)
                hi = slice(dh + cc * fcb, dh + (cc + 1) * fcb)
                y_lo = jnp.dot(a, wdb[:f, lo], preferred_element_type=F32) + bd[:, lo]
                y_hi = jnp.dot(a, wdb[:f, hi], preferred_element_type=F32) + bd[:, hi]
                ybuf[slot, :, lo] = _pack_bf16_pair(y_lo, y_hi)
            chunk_casts(s, wgub, ckr, 2 * f, load_next)

        chunk_waits(0, gu_copy, e_next, load_next)
        down(0, 0, False)
        chunk_starts(0, gu_copy, e_next, load_next)

        def phase_b(s, carry):
            slot = s % 2

            @pl.when(s >= 2)
            def _():
                scatter_wait(slot)

            chunk_waits(s, gu_copy, e_next, load_next)
            down(s, slot, True)
            chunk_starts(s, gu_copy, e_next, load_next)
            return carry

        lax.fori_loop(1, nsub, phase_b, 0)

        @pl.when(load_next)
        def _():
            chunk_drain(cps * nsub, gu_step, e_next)

        scatter(nsub - 1, (nsub - 1) % 2)
        scatter_wait((nsub - 1) % 2)

        @pl.when(nsub >= 2)
        def _():
            scatter_wait(nsub % 2)


def _id_window(c):
    return ((ID_CHUNK - 1 + c.tm_e + c.ts_e) // ID_CHUNK + 1) * ID_CHUNK


def _moe(c, sched, order_pad, hfp, w_gu, b_gu3, w_d, b_d3):
    n_tok, dh = hfp.shape
    d = 2 * dh
    f = c.DFF
    tm, ts, nck = c.tm_e, c.ts_e, c.nck_e
    n_asg = n_tok * c.K
    grid_spec = pltpu.PrefetchScalarGridSpec(
        num_scalar_prefetch=4,
        grid=(_n_tiles(c),),
        in_specs=[
            pl.BlockSpec(memory_space=pl.ANY),
            pl.BlockSpec(memory_space=pl.ANY),
            pl.BlockSpec(memory_space=pl.ANY),
            pl.BlockSpec(memory_space=pl.ANY),
            pl.BlockSpec((c.E, 1, 2 * f), lambda i, *_: (0, 0, 0), pipeline_mode=pl.Buffered(1)),
            pl.BlockSpec((c.E, 1, d), lambda i, *_: (0, 0, 0), pipeline_mode=pl.Buffered(1)),
        ],
        out_specs=pl.BlockSpec(memory_space=pl.ANY),
        scratch_shapes=[
            pltpu.VMEM((d + d // nck, 2 * f), BF16),
            pltpu.VMEM((f + f // nck, d), BF16),
            pltpu.VMEM((c.nst_e, d // nck, 2 * f), F32),
            pltpu.VMEM((tm // ts, ts, f), BF16),
            pltpu.VMEM((2, ts, dh), U32),
            pltpu.VMEM((2, ts, dh), U32),
            pltpu.SMEM((2 * _id_window(c),), I32),
            pltpu.SemaphoreType.DMA((c.nst_e,)),
            pltpu.SemaphoreType.DMA((2,)),
            pltpu.SemaphoreType.DMA((2,)),
            pltpu.SemaphoreType.DMA((2,)),
        ],
    )
    kern = functools.partial(_moe_kernel, ts=ts, tm=tm, nck=nck, cps=c.cps_e, fc=c.fc_e, n_tok=n_tok, n_asg=n_asg)
    return pl.pallas_call(
        kern,
        grid_spec=grid_spec,
        out_shape=jax.ShapeDtypeStruct((n_asg + 2 * ts, dh), U32),
        compiler_params=_cparams(("arbitrary",)),
        name="experts",
    )(*sched, order_pad, hfp, w_gu, w_d, b_gu3, b_d3)


def _combine_kernel(x2_ref, gate_ref, g_ref, *refs, topk):
    y_refs, o_ref = refs[:topk], refs[topk]
    d = x2_ref.shape[1]
    dh = d // 2
    lo = x2_ref[:, :dh]
    hi = x2_ref[:, dh:]
    gates = gate_ref[...]
    for kk in range(topk):
        wk = gates[:, kk:kk + 1]
        p = y_refs[kk][...]
        lo = lo + wk * _unpack_lo(p)
        hi = hi + wk * _unpack_hi(p)
    ms = (jnp.sum(lo * lo, axis=-1, keepdims=True) + jnp.sum(hi * hi, axis=-1, keepdims=True)) / d
    inv = lax.rsqrt(ms + RMS_EPS)
    o_ref[:, :dh] = lo * inv * g_ref[:, :dh]
    o_ref[:, dh:] = hi * inv * g_ref[:, dh:]


def _combine(c, x2, gates, g, ykt):
    n, d = x2.shape
    tm = c.tm_c
    nb = n // tm
    y_specs = [pl.BlockSpec((tm, d // 2), functools.partial(lambda i, kk: (kk * nb + i, 0), kk=kk))
               for kk in range(c.K)]
    return pl.pallas_call(
        functools.partial(_combine_kernel, topk=c.K),
        grid=(nb,),
        in_specs=[
            pl.BlockSpec((tm, d), lambda i: (i, 0)),
            pl.BlockSpec((tm, LANES), lambda i: (i, 0)),
            pl.BlockSpec((1, d), lambda i: (0, 0)),
        ] + y_specs,
        out_specs=pl.BlockSpec((tm, d), lambda i: (i, 0)),
        out_shape=jax.ShapeDtypeStruct((n, d), F32),
        compiler_params=_cparams(("parallel",)),
        name="combine",
    )(x2, gates, g, *([ykt] * c.K))


def _plan(c, idx, counts):
    tm, ts = c.tm_e, c.ts_e
    nk = idx.shape[0] * c.K
    order = jnp.argsort(idx.T.reshape(-1)).astype(I32)
    pad = _id_window(c) + (-nk) % ID_CHUNK
    order_pad = jnp.concatenate([order, jnp.zeros((pad,), I32)])
    counts = counts.astype(I32)
    start = jnp.cumsum(counts) - counts
    tiles_per_e = (counts + tm - 1) // tm
    tile_end = jnp.cumsum(tiles_per_e)
    tile_start = tile_end - tiles_per_e
    t = jnp.arange(_n_tiles(c), dtype=I32)
    n_used = tile_end[-1]
    tc = jnp.minimum(t, n_used - 1)
    te = jnp.minimum(jnp.searchsorted(tile_end, tc, side="right"), c.E - 1).astype(I32)
    j = tc - tile_start[te]
    used = t < n_used
    rows = jnp.where(used, jnp.clip(counts[te] - j * tm, 0, tm), 0).astype(I32)
    nsub = (rows + ts - 1) // ts
    pos = jnp.where(used, start[te] + j * tm, 0).astype(I32)
    return order_pad, (te, nsub.astype(I32), pos, rows)


def _n_tiles(c):
    return -(-(c.B * c.S * c.K) // c.tm_e) + c.E


def _forward(c, x, mem, norm_mix_g, w_in, b_forget, sg_ln_g, sg_ln_b, w_spatial, b_spatial,
             w_branch_a, w_branch_b, w_out, norm_x_g, norm_mem_g, w_xq, w_xkv, w_xo,
             norm_ffn_g, w_router, b_router, w_gate_up, b_gate_up, w_down, b_down, norm_final_g):
    B, S, D = x.shape
    n = B * S
    fw = c.FH * LANES
    sw = c.SG * LANES
    x2d = x.reshape(n, D)

    o_f = 3 * fw
    o_z = o_f + c.FH
    o_g = o_z + 2 * sw
    w_main = jnp.concatenate([w_in[:, o_g:], w_in[:, o_z:o_g], w_in[:, :o_f]], axis=1).astype(BF16)
    w_f = jnp.pad(w_in[:, o_f:o_z], ((0, 0), (0, LANES - c.FH))).astype(BF16)
    b_f = jnp.pad(b_forget.astype(F32), (0, LANES - c.FH)).reshape(1, LANES)
    proj, logf = _in_proj(c, x2d, norm_mix_g.reshape(1, D), w_main, w_f, b_f, LOG2E * LANES ** -0.5)

    logf_bhs = logf[:, :c.FH].reshape(B, S, c.FH).transpose(0, 2, 1)
    csum = _cumsum(c, logf_bhs)
    crow = csum.reshape(B * c.FH, 1, S)
    ccol = csum.reshape(B * c.FH, S, 1)
    qcol0 = (2 * D + 2 * sw) // LANES
    attn = _fox(c, proj, crow, ccol, qcol0)

    b_s_full = jnp.broadcast_to(b_spatial.astype(F32)[:, :, None], (c.SG, c.SGC, LANES))
    x1 = _mix(c, x2d, attn, proj, sg_ln_g.reshape(1, sw), sg_ln_b.reshape(1, sw), w_spatial, b_s_full,
              w_branch_a.astype(BF16), w_branch_b.astype(BF16), w_out.astype(BF16))

    kv = _mem_kv(c, mem.reshape(B * c.MEM, D), norm_mem_g.reshape(1, D), w_xkv.astype(BF16))
    w_r32 = jnp.pad(w_router.astype(F32), ((0, 0), (0, LANES - c.E)))
    w_r_hi = w_r32.astype(BF16)
    w_r = jnp.concatenate([w_r_hi, (w_r32 - w_r_hi.astype(F32)).astype(BF16)], axis=1)
    b_r = jnp.pad(b_router.astype(F32), (0, LANES - c.E)).reshape(1, LANES)
    x2, hfp, idx, gates, counts = _xattn(
        c, x1, norm_x_g.reshape(1, D), w_xq.astype(BF16), kv, w_xo.astype(BF16),
        norm_ffn_g.reshape(1, D), w_r, b_r, LANES ** -0.5)

    order_pad, sched = _plan(c, idx[:, :c.K], counts[0, :c.E])
    ykt = _moe(c, sched, order_pad, hfp, w_gate_up, b_gate_up.reshape(c.E, 1, 2 * c.DFF),
               w_down, b_down.reshape(c.E, 1, D))
    out = _combine(c, x2, gates, norm_final_g.reshape(1, D), ykt)
    return out.reshape(B, S, D)


_CFG = Cfg(B=4, S=4096, D=2048, MEM=256, FH=8, SG=8, SGC=128, XH=4, E=32, K=4, DFF=2048,
           tm_in=1024, tn_in=1024, tq=1024, tm_mix=256, tm_x=512, tm_e=2560, ts_e=256, fc_e=512, nck_e=16, nst_e=4, cps_e=2,
           tm_c=256, tc_cs=512, fox_parts=2)


@jax.jit
def kernel(x, mem, norm_mix_g, w_in, b_forget, sg_ln_g, sg_ln_b, w_spatial, b_spatial, w_branch_a, w_branch_b,
           w_out, norm_x_g, norm_mem_g, w_xq, w_xkv, w_xo, norm_ffn_g, w_router, b_router, w_gate_up,
           b_gate_up, w_down, b_down, norm_final_g):
    return _forward(_CFG, x, mem, norm_mix_g, w_in, b_forget, sg_ln_g, sg_ln_b, w_spatial, b_spatial,
                    w_branch_a, w_branch_b, w_out, norm_x_g, norm_mem_g, w_xq, w_xkv, w_xo,
                    norm_ffn_g, w_router, b_router, w_gate_up, b_gate_up, w_down, b_down, norm_final_g)
```

```python
import functools
from typing import NamedTuple

import jax
import jax.numpy as jnp
from jax import lax
from jax.experimental import pallas as pl
from jax.experimental.pallas import tpu as pltpu

F32 = jnp.float32
BF16 = jnp.bfloat16
U32 = jnp.uint32
I32 = jnp.int32

LANES = 128
VMEM_LIMIT = 56 * 1024 * 1024

RMS_EPS = 1e-6
LN_EPS = 1e-5
SWIGLU_LIMIT = 7.0
SWIGLU_ALPHA = 1.702
GELU_C = 0.7978845608028654
LOG2E = 1.4426950408889634
ID_CHUNK = 1024
WEIGHT_DMA_PRIORITY = 1


class Cfg(NamedTuple):
    B: int
    S: int
    D: int
    MEM: int
    FH: int
    SG: int
    SGC: int
    XH: int
    E: int
    K: int
    DFF: int
    tm_in: int
    tn_in: int
    tq: int
    tm_mix: int
    tm_x: int
    tm_e: int
    ts_e: int
    fc_e: int
    nck_e: int
    nst_e: int
    tm_c: int
    tc_cs: int
    fox_parts: int


def _cparams(sem):
    return pltpu.CompilerParams(dimension_semantics=sem, vmem_limit_bytes=VMEM_LIMIT)


def _rms(x, g):
    ms = jnp.mean(x * x, axis=-1, keepdims=True)
    return x * lax.rsqrt(ms + RMS_EPS) * g


def _pack_bf16_pair(a, b):
    def rne(v):
        bits = lax.bitcast_convert_type(v, U32)
        return bits + jnp.uint32(0x7FFF) + ((bits >> 16) & jnp.uint32(1))
    return (rne(a) >> 16) | (rne(b) & jnp.uint32(0xFFFF0000))


def _unpack_lo(p):
    return lax.bitcast_convert_type(p << 16, F32)


def _unpack_hi(p):
    return lax.bitcast_convert_type(p & jnp.uint32(0xFFFF0000), F32)


def _in_proj_kernel(x_ref, g_ref, w_ref, wf_ref, bf_ref, o_ref, f_ref, h_ref, *, nj_gate, nj_z, nj_q, qscale):
    j = pl.program_id(1)

    @pl.when(j == 0)
    def _():
        hb = _rms(x_ref[...], g_ref[...]).astype(BF16)
        h_ref[...] = hb
        f = jnp.dot(hb, wf_ref[...], preferred_element_type=F32) + bf_ref[...]
        f_ref[...] = jnp.minimum(f, 0.0) - jnp.log1p(jnp.exp(-jnp.abs(f)))

    acc = jnp.dot(h_ref[...], w_ref[...], preferred_element_type=F32)

    is_gate = j < nj_gate
    is_z = (j >= nj_gate) & (j < nj_gate + nj_z)
    is_q = (j >= nj_gate + nj_z) & (j < nj_gate + nj_z + nj_q)
    a1 = jnp.where(is_gate, 0.5, jnp.where(is_z, GELU_C, 0.0)).astype(F32)
    a3 = jnp.where(is_z, GELU_C * 0.044715, 0.0).astype(F32)
    b0 = jnp.where(is_gate, 0.5, 0.0).astype(F32)
    b1 = jnp.where(is_gate, 0.0, jnp.where(is_z, 0.5, jnp.where(is_q, qscale, 1.0))).astype(F32)
    th = jnp.tanh(acc * (a1 + a3 * (acc * acc)))
    o_ref[...] = ((b0 + b1 * acc) * (1.0 + th)).astype(o_ref.dtype)


def _in_proj(c, x2d, g, w_main, w_f, b_f, qscale):
    n, d = x2d.shape
    nc = w_main.shape[1]
    tm, tn = c.tm_in, c.tn_in
    fw = c.FH * LANES
    kern = functools.partial(_in_proj_kernel, nj_gate=2 * d // tn, nj_z=2 * c.SG * LANES // tn,
                             nj_q=fw // tn, qscale=qscale)
    return pl.pallas_call(
        kern,
        grid=(n // tm, nc // tn),
        in_specs=[
            pl.BlockSpec((tm, d), lambda i, j: (i, 0)),
            pl.BlockSpec((1, d), lambda i, j: (0, 0)),
            pl.BlockSpec((d, tn), lambda i, j: (0, j)),
            pl.BlockSpec((d, LANES), lambda i, j: (0, 0)),
            pl.BlockSpec((1, LANES), lambda i, j: (0, 0)),
        ],
        out_specs=[
            pl.BlockSpec((tm, tn), lambda i, j: (i, j)),
            pl.BlockSpec((tm, LANES), lambda i, j: (i, 0)),
        ],
        out_shape=[jax.ShapeDtypeStruct((n, nc), BF16), jax.ShapeDtypeStruct((n, LANES), F32)],
        scratch_shapes=[pltpu.VMEM((tm, d), BF16)],
        compiler_params=_cparams(("parallel", "arbitrary")),
        name="in_proj",
    )(x2d, g, w_main, w_f, b_f)


def _cumsum_kernel(f_ref, o_ref, *, tc):
    rows, s = f_ref.shape[1], f_ref.shape[2]
    r = lax.broadcasted_iota(I32, (tc, tc), 0)
    col = lax.broadcasted_iota(I32, (tc, tc), 1)
    upper = (r <= col).astype(F32)
    carry = jnp.zeros((rows, 1), F32)
    for i in range(s // tc):
        blk = f_ref[0, :, i * tc:(i + 1) * tc]
        cs = jnp.dot(blk, upper, preferred_element_type=F32, precision=lax.Precision.HIGHEST) + carry
        o_ref[0, :, i * tc:(i + 1) * tc] = cs * LOG2E
        carry = cs[:, tc - 1:tc]


def _cumsum(c, logf_bhs):
    b, h, s = logf_bhs.shape
    return pl.pallas_call(
        functools.partial(_cumsum_kernel, tc=c.tc_cs),
        grid=(b,),
        in_specs=[pl.BlockSpec((1, h, s), lambda i: (i, 0, 0))],
        out_specs=pl.BlockSpec((1, h, s), lambda i: (i, 0, 0)),
        out_shape=jax.ShapeDtypeStruct((b, h, s), F32),
        compiler_params=_cparams(("parallel",)),
        name="cumsum",
    )(logf_bhs)


def _fox_kernel(q_ref, k_ref, v_ref, crow_ref, ccol_ref, o_ref, m_ref, l_ref, acc_ref, cq_ref, s_ref,
                *, t, parts):
    qi = pl.program_id(2)
    hr = t // parts
    m_ref[...] = jnp.full(m_ref.shape, -jnp.inf, F32)
    l_ref[...] = jnp.zeros(l_ref.shape, F32)
    acc_ref[...] = jnp.zeros(acc_ref.shape, F32)
    cq_ref[...] = jnp.broadcast_to(ccol_ref[0], cq_ref.shape)

    def scores(ks):
        k = k_ref[pl.ds(ks, t), :]
        return lax.dot_general(q_ref[...], k, (((1,), (1,)), ((), ())), preferred_element_type=F32)

    def block(part, ks, width, masked):
        rows = slice(part * hr, (part + 1) * hr)
        v = v_ref[pl.ds(ks, width), :]
        s = s_ref[rows, :width]
        cq = cq_ref[rows, :]
        crow = crow_ref[0, :, pl.ds(ks, width)]
        nj = width // LANES
        sj = [s[:, j * LANES:(j + 1) * LANES] + cq - crow[:, j * LANES:(j + 1) * LANES] for j in range(nj)]
        if masked:
            row = lax.broadcasted_iota(I32, (hr, LANES), 0) + part * hr
            col = lax.broadcasted_iota(I32, (hr, LANES), 1)
            sj = [jnp.where(col + j * LANES <= row, sj[j], -jnp.inf) for j in range(nj)]
        mx = sj[0]
        for j in range(1, nj):
            mx = jnp.maximum(mx, sj[j])
        m_prev = m_ref[rows, :]
        m_next = jnp.maximum(m_prev, jnp.max(mx, axis=-1, keepdims=True))
        alpha = jnp.exp2(m_prev - m_next)
        pj = [jnp.exp2(sj[j] - m_next) for j in range(nj)]
        psum = pj[0]
        for j in range(1, nj):
            psum = psum + pj[j]
        p = jnp.concatenate([x.astype(BF16) for x in pj], axis=-1)
        l_ref[rows, :] = alpha * l_ref[rows, :] + psum
        acc_ref[rows, :] = alpha * acc_ref[rows, :] + jnp.dot(p, v, preferred_element_type=F32)
        m_ref[rows, :] = m_next

    s_ref[...] = scores(0)

    def body(kc, carry):
        ks = pl.multiple_of(kc * t, t)
        s_next = scores(pl.multiple_of(ks + t, t))
        for part in range(parts):
            block(part, ks, t, False)
        s_ref[...] = s_next
        return carry

    lax.fori_loop(0, qi, body, 0)
    kd = pl.multiple_of(qi * t, t)
    for part in range(parts):
        block(part, kd, (part + 1) * hr, True)
    l = jnp.sum(l_ref[...], axis=-1, keepdims=True)
    o_ref[...] = (acc_ref[...] / l).astype(o_ref.dtype)


def _fox(c, proj, crow, ccol, qcol0):
    n = c.B * c.S
    t = c.tq
    nq = c.S // t
    h = c.FH
    return pl.pallas_call(
        functools.partial(_fox_kernel, t=t, parts=c.fox_parts),
        grid=(c.B, h, nq),
        in_specs=[
            pl.BlockSpec((t, LANES), lambda b, hh, qi: (b * nq + qi, qcol0 + hh)),
            pl.BlockSpec((c.S, LANES), lambda b, hh, qi: (b, qcol0 + h + hh)),
            pl.BlockSpec((c.S, LANES), lambda b, hh, qi: (b, qcol0 + 2 * h + hh)),
            pl.BlockSpec((1, 1, c.S), lambda b, hh, qi: (b * h + hh, 0, 0)),
            pl.BlockSpec((1, t, 1), lambda b, hh, qi: (b * h + hh, qi, 0)),
        ],
        out_specs=pl.BlockSpec((t, LANES), lambda b, hh, qi: (b * nq + qi, hh)),
        out_shape=jax.ShapeDtypeStruct((n, h * LANES), BF16),
        scratch_shapes=[pltpu.VMEM((t, LANES), F32)] * 4 + [pltpu.VMEM((t, t), F32)],
        compiler_params=_cparams(("parallel", "parallel", "arbitrary")),
        name="fox",
    )(proj, proj, proj, crow, ccol)


def _mix_kernel(x_ref, a_ref, u_ref, v_ref, ga_ref, gb_ref, lng_ref, lnb_ref, ws_ref, bs_ref,
                wa_ref, wb_ref, wo_ref, o_ref, sg_ref, *, sgc, groups):
    tm = x_ref.shape[0]
    v = v_ref[...].astype(F32)
    mu = jnp.mean(v, axis=-1, keepdims=True)
    vc = v - mu
    var = jnp.mean(vc * vc, axis=-1, keepdims=True)
    vn = (vc * lax.rsqrt(var + LN_EPS) * lng_ref[...] + lnb_ref[...]).astype(BF16)
    row = lax.broadcasted_iota(I32, (sgc, sgc), 0)
    col = lax.broadcasted_iota(I32, (sgc, sgc), 1)
    for g in range(groups):
        w = jnp.where(col <= row, ws_ref[g], 0.0).astype(BF16)
        bias = bs_ref[g]
        for ci in range(tm // sgc):
            rs = slice(ci * sgc, (ci + 1) * sgc)
            cs = slice(g * LANES, (g + 1) * LANES)
            mixed = jnp.dot(w, vn[rs, cs], preferred_element_type=F32) + bias
            sg_ref[rs, cs] = (u_ref[rs, cs].astype(F32) * mixed).astype(BF16)
    ya = jnp.dot(a_ref[...], wa_ref[...], preferred_element_type=F32)
    yb = jnp.dot(sg_ref[...], wb_ref[...], preferred_element_type=F32)
    merged = (ga_ref[...].astype(F32) * ya + gb_ref[...].astype(F32) * yb).astype(BF16)
    o_ref[...] = x_ref[...] + jnp.dot(merged, wo_ref[...], preferred_element_type=F32)


def _const_spec(shape):
    nd = len(shape)
    return pl.BlockSpec(shape, lambda i: (0,) * nd, pipeline_mode=pl.Buffered(1))


def _mix(c, x2d, attn, proj, ln_g, ln_b, w_s, b_s_full, w_a, w_b, w_o):
    n, d = x2d.shape
    tm = c.tm_mix
    fw = c.FH * LANES
    sw = c.SG * LANES
    ucol = 2 * d // sw
    return pl.pallas_call(
        functools.partial(_mix_kernel, sgc=c.SGC, groups=c.SG),
        grid=(n // tm,),
        in_specs=[
            pl.BlockSpec((tm, d), lambda i: (i, 0)),
            pl.BlockSpec((tm, fw), lambda i: (i, 0)),
            pl.BlockSpec((tm, sw), lambda i: (i, ucol)),
            pl.BlockSpec((tm, sw), lambda i: (i, ucol + 1)),
            pl.BlockSpec((tm, d), lambda i: (i, 0)),
            pl.BlockSpec((tm, d), lambda i: (i, 1)),
            _const_spec((1, sw)),
            _const_spec((1, sw)),
            _const_spec((c.SG, c.SGC, c.SGC)),
            _const_spec((c.SG, c.SGC, LANES)),
            _const_spec((fw, d)),
            _const_spec((sw, d)),
            _const_spec((d, d)),
        ],
        out_specs=pl.BlockSpec((tm, d), lambda i: (i, 0)),
        out_shape=jax.ShapeDtypeStruct((n, d), F32),
        scratch_shapes=[pltpu.VMEM((tm, sw), BF16)],
        compiler_params=_cparams(("parallel",)),
        name="mix",
    )(x2d, attn, proj, proj, proj, proj, ln_g, ln_b, w_s, b_s_full, w_a, w_b, w_o)


def _mem_kv_kernel(m_ref, g_ref, w_ref, o_ref):
    hm = _rms(m_ref[...], g_ref[...]).astype(BF16)
    o_ref[...] = jnp.dot(hm, w_ref[...], preferred_element_type=F32).astype(o_ref.dtype)


def _mem_kv(c, mem2d, g, w_xkv):
    n, d = mem2d.shape
    nc = w_xkv.shape[1]
    tm = c.MEM
    return pl.pallas_call(
        _mem_kv_kernel,
        grid=(n // tm,),
        in_specs=[pl.BlockSpec((tm, d), lambda i: (i, 0)), _const_spec((1, d)), _const_spec((d, nc))],
        out_specs=pl.BlockSpec((tm, nc), lambda i: (i, 0)),
        out_shape=jax.ShapeDtypeStruct((n, nc), BF16),
        compiler_params=_cparams(("parallel",)),
        name="mem_kv",
    )(mem2d, g, w_xkv)


def _xattn_kernel(x1_ref, gx_ref, wq_ref, kv_ref, wo_ref, gf_ref, wr_ref, br_ref,
                  x2_ref, hfp_ref, idx_ref, gate_ref, cnt_ref, carry_ref,
                  *, heads, n_exp, topk, qscale):
    i = pl.program_id(0)
    tm, d = x1_ref.shape
    xw = heads * LANES
    x1 = x1_ref[...]
    hx = _rms(x1, gx_ref[...]).astype(BF16)
    q = (jnp.dot(hx, wq_ref[...], preferred_element_type=F32) * qscale).astype(BF16)
    outs = []
    for h in range(heads):
        k = kv_ref[:, h * LANES:(h + 1) * LANES]
        v = kv_ref[:, xw + h * LANES:xw + (h + 1) * LANES]
        s = lax.dot_general(q[:, h * LANES:(h + 1) * LANES], k, (((1,), (1,)), ((), ())),
                            preferred_element_type=F32)
        s = s - jnp.max(s, axis=-1, keepdims=True)
        p = jnp.exp(s)
        p = p / jnp.sum(p, axis=-1, keepdims=True)
        outs.append(jnp.dot(p.astype(BF16), v, preferred_element_type=F32).astype(BF16))
    o = jnp.concatenate(outs, axis=-1)
    x2 = x1 + jnp.dot(o, wo_ref[...], preferred_element_type=F32)
    x2_ref[...] = x2

    hf = _rms(x2, gf_ref[...])
    hfp_ref[...] = _pack_bf16_pair(hf[:, :d // 2], hf[:, d // 2:])

    h_hi = hf.astype(BF16)
    h_lo = (hf - h_hi.astype(F32)).astype(BF16)
    l_hi = jnp.dot(h_hi, wr_ref[...], preferred_element_type=F32)
    l_lo = jnp.dot(h_lo, wr_ref[:, :LANES], preferred_element_type=F32)
    logits = l_hi[:, :LANES] + l_hi[:, LANES:] + l_lo + br_ref[...]
    lane = lax.broadcasted_iota(I32, (tm, LANES), 1)
    lg = jnp.where(lane < n_exp, logits, -jnp.inf)
    vals, idxs = [], []
    for _ in range(topk):
        m = jnp.max(lg, axis=-1, keepdims=True)
        ix = jnp.min(jnp.where(lg == m, lane, LANES), axis=-1, keepdims=True)
        vals.append(m)
        idxs.append(ix)
        lg = jnp.where(lane == ix, -jnp.inf, lg)
    es = [jnp.exp(vv - vals[0]) for vv in vals]
    denom = es[0]
    for e in es[1:]:
        denom = denom + e

    @pl.when(i == 0)
    def _():
        carry_ref[...] = jnp.zeros(carry_ref.shape, F32)

    onehot = jnp.zeros((tm, LANES), F32)
    for ix in idxs:
        onehot = onehot + (lane == ix).astype(F32)
    idx_out = jnp.zeros((tm, LANES), I32)
    gate_out = jnp.zeros((tm, LANES), F32)
    for kk in range(topk):
        idx_out = jnp.where(lane == kk, idxs[kk], idx_out)
        gate_out = jnp.where(lane == kk, es[kk] / denom, gate_out)
    idx_ref[...] = idx_out
    gate_ref[...] = gate_out
    carry_ref[...] = carry_ref[...] + jnp.sum(onehot, axis=0, keepdims=True)
    cnt_ref[...] = carry_ref[...]


def _xattn(c, x1, gx, w_xq, kv, w_xo, gf, w_r, b_r, qscale):
    n, d = x1.shape
    tm = c.tm_x
    xw = c.XH * LANES
    per_b = c.S // tm
    kern = functools.partial(_xattn_kernel, heads=c.XH, n_exp=c.E, topk=c.K, qscale=qscale)
    return pl.pallas_call(
        kern,
        grid=(n // tm,),
        in_specs=[
            pl.BlockSpec((tm, d), lambda i: (i, 0)),
            _const_spec((1, d)),
            _const_spec((d, xw)),
            pl.BlockSpec((c.MEM, 2 * xw), lambda i: (i // per_b, 0)),
            _const_spec((xw, d)),
            _const_spec((1, d)),
            _const_spec((d, 2 * LANES)),
            _const_spec((1, LANES)),
        ],
        out_specs=[
            pl.BlockSpec((tm, d), lambda i: (i, 0)),
            pl.BlockSpec((tm, d // 2), lambda i: (i, 0)),
            pl.BlockSpec((tm, LANES), lambda i: (i, 0)),
            pl.BlockSpec((tm, LANES), lambda i: (i, 0)),
            pl.BlockSpec((1, LANES), lambda i: (0, 0)),
        ],
        out_shape=[
            jax.ShapeDtypeStruct((n, d), F32),
            jax.ShapeDtypeStruct((n, d // 2), U32),
            jax.ShapeDtypeStruct((n, LANES), I32),
            jax.ShapeDtypeStruct((n, LANES), F32),
            jax.ShapeDtypeStruct((1, LANES), F32),
        ],
        scratch_shapes=[pltpu.VMEM((1, LANES), F32)],
        compiler_params=_cparams(("arbitrary",)),
        name="xattn_router",
    )(x1, gx, w_xq, kv, w_xo, gf, w_r, b_r)


def _moe_kernel(te_ref, ns_ref, cpi_ref, pos_ref, rows_ref, ord_ref, hfp_ref, wgu_ref, wd_ref, bgu_ref, bd_ref,
                ykt_ref, wgub, wdb, stage, act, xin, ybuf, ids, wsem, xsem, ysem, isem,
                *, ts, tm, nck, fc, n_tok, n_asg):
    i = pl.program_id(0)
    n = pl.num_programs(0)
    nsub = ns_ref[i]
    e = te_ref[i]
    cpi = cpi_ref[i]
    rows = rows_ref[i]
    d, f2 = wgub.shape
    f = f2 // 2
    dh = d // 2
    ckr = d // nck
    ckd = f // nck
    nst = stage.shape[0]
    idw = ids.shape[0] // 2
    nq = idw // ID_CHUNK
    id_base = (i % 2) * idw + (pos_ref[i] & (ID_CHUNK - 1))

    def ids_copy(item, q, slot):
        c0 = lax.shift_right_logical(pos_ref[item], ID_CHUNK.bit_length() - 1)
        src = ord_ref.at[pl.ds(pl.multiple_of((c0 + q) * ID_CHUNK, ID_CHUNK), ID_CHUNK)]
        dst = ids.at[pl.ds(pl.multiple_of(slot * idw + q * ID_CHUNK, ID_CHUNK), ID_CHUNK)]
        return pltpu.make_async_copy(src, dst, isem.at[slot])

    def token_of(a):
        return a & (n_tok - 1) if n_tok & (n_tok - 1) == 0 else lax.rem(a, n_tok)

    def gather(s, slot, r0=0, r1=ts, base=None):
        base = id_base if base is None else base
        for r in range(r0, r1):
            tok = token_of(ids[base + s * ts + r])
            pltpu.make_async_copy(hfp_ref.at[pl.ds(tok, 1)], xin.at[slot, pl.ds(r, 1)], xsem.at[slot]).start()

    def gather_wait(slot):
        for r in range(ts):
            pltpu.make_async_copy(hfp_ref.at[pl.ds(0, 1)], xin.at[slot, pl.ds(0, 1)], xsem.at[slot]).wait()

    def scatter(s, slot, r0=0, r1=ts):
        for r in range(r0, r1):
            g = s * ts + r
            dst = jnp.where(g < rows, ids[id_base + g], n_asg + slot * ts + r)
            pltpu.make_async_copy(ybuf.at[slot, pl.ds(r, 1)], ykt_ref.at[pl.ds(dst, 1)], ysem.at[slot]).start()

    def scatter_wait(slot):
        for r in range(ts):
            pltpu.make_async_copy(ybuf.at[slot, pl.ds(0, 1)], ykt_ref.at[pl.ds(0, 1)], ysem.at[slot]).wait()

    def gu_copy(ee, c, slot):
        return pltpu.make_async_copy(wgu_ref.at[ee, pl.ds(pl.multiple_of(c * ckr, ckr), ckr), :],
                                     stage.at[slot], wsem.at[slot])

    def d_copy(ee, c, slot):
        return pltpu.make_async_copy(wd_ref.at[ee, pl.ds(pl.multiple_of(c * ckd, ckd), ckd), :],
                                     stage.at[slot, pl.ds(0, ckd), pl.ds(0, d)], wsem.at[slot])

    def gu_step(ee, c):
        slot = c % nst
        gu_copy(ee, c, slot).wait()
        wgub[pl.ds(pl.multiple_of(c * ckr, ckr), ckr), :] = stage[slot].astype(BF16)

        @pl.when(c + nst < nck)
        def _():
            gu_copy(ee, c + nst, slot).start(priority=WEIGHT_DMA_PRIORITY)

    def d_step(ee, c):
        slot = c % nst
        d_copy(ee, c, slot).wait()
        wdb[pl.ds(pl.multiple_of(c * ckd, ckd), ckd), :] = stage[slot, :ckd, :d].astype(BF16)

        @pl.when(c + nst < nck)
        def _():
            d_copy(ee, c + nst, slot).start(priority=WEIGHT_DMA_PRIORITY)

    def prime(copy_fn, ee):
        for q in range(min(nst, nck)):
            copy_fn(ee, q, q).start(priority=WEIGHT_DMA_PRIORITY)

    def steps_after(s, step_fn, ee):
        def one(jj, carry):
            c = s * cpi + jj

            @pl.when(c < nck)
            def _():
                step_fn(ee, c)
            return carry
        lax.fori_loop(0, cpi, one, 0)

    @pl.when(nsub > 0)
    def _():
        nxt = jnp.minimum(i + 1, n - 1)
        has_next = (i + 1 < n) & (ns_ref[nxt] > 0)

        @pl.when(i == 0)
        def _():
            for q in range(nq):
                ids_copy(0, q, 0).start()
            prime(gu_copy, e)

            def first(c, carry):
                gu_step(e, c)
                return carry
            lax.fori_loop(0, nck, first, 0)
            for q in range(nq):
                ids_copy(0, q, 0).wait()
            gather(0, 0)

        nslot = 1 - i % 2
        for q in range(nq):
            ids_copy(nxt, q, nslot).start()

        prime(d_copy, e)
        bgu = bgu_ref[e]

        def phase_a(s, carry):
            slot = s % 2
            gather_wait(slot)
            p = xin[slot]
            xb = jnp.concatenate([_unpack_lo(p).astype(BF16), _unpack_hi(p).astype(BF16)], axis=-1)
            nc = f // fc
            for cc in range(nc):
                gather(s + 1, 1 - slot, cc * ts // nc, (cc + 1) * ts // nc)
                gate = jnp.dot(xb, wgub[:, cc * fc:(cc + 1) * fc], preferred_element_type=F32)
                gate = gate + bgu[:, cc * fc:(cc + 1) * fc]
                up = jnp.dot(xb, wgub[:, f + cc * fc:f + (cc + 1) * fc], preferred_element_type=F32)
                up = up + bgu[:, f + cc * fc:f + (cc + 1) * fc]
                gate = jnp.minimum(gate, SWIGLU_LIMIT)
                up = jnp.clip(up, -SWIGLU_LIMIT, SWIGLU_LIMIT)
                a = (up + 1.0) * gate * jax.nn.sigmoid(SWIGLU_ALPHA * gate)
                act[s, :, cc * fc:(cc + 1) * fc] = a.astype(BF16)
            steps_after(s, d_step, e)
            return carry

        lax.fori_loop(0, nsub, phase_a, 0)
        gather_wait(nsub % 2)

        e_next = te_ref[nxt]
        load_next = has_next & (e_next != e)

        @pl.when(load_next)
        def _():
            prime(gu_copy, e_next)

        bd = bd_ref[e]

        def down(s, slot, scatter_prev):
            a = act[s]
            if scatter_prev:
                scatter(s - 1, 1 - slot)
            fcb = min(fc, dh // 2)
            for cc in range(dh // fcb):
                lo = slice(cc * fcb, (cc + 1) * fcb)
                hi = slice(dh + cc * fcb, dh + (cc + 1) * fcb)
                y_lo = jnp.dot(a, wdb[:, lo], preferred_element_type=F32) + bd[:, lo]
                y_hi = jnp.dot(a, wdb[:, hi], preferred_element_type=F32) + bd[:, hi]
                ybuf[slot, :, lo] = _pack_bf16_pair(y_lo, y_hi)

        def next_weights(s):
            @pl.when(load_next)
            def _():
                steps_after(s, gu_step, e_next)

        for q in range(nq):
            ids_copy(nxt, q, nslot).wait()
        gather(0, 0, base=nslot * idw + (pos_ref[nxt] & (ID_CHUNK - 1)))
        down(0, 0, False)
        next_weights(0)

        def phase_b(s, carry):
            slot = s % 2

            @pl.when(s >= 2)
            def _():
                scatter_wait(slot)

            down(s, slot, True)
            next_weights(s)
            return carry

        lax.fori_loop(1, nsub, phase_b, 0)
        scatter(nsub - 1, (nsub - 1) % 2)
        scatter_wait((nsub - 1) % 2)

        @pl.when(nsub >= 2)
        def _():
            scatter_wait(nsub % 2)

        @pl.when(jnp.logical_not(has_next))
        def _():
            gather_wait(0)


def _id_window(c):
    return ((ID_CHUNK - 1 + c.tm_e + c.ts_e) // ID_CHUNK + 1) * ID_CHUNK


def _moe(c, sched, order_pad, hfp, w_gu, b_gu3, w_d, b_d3):
    n_tok, dh = hfp.shape
    d = 2 * dh
    f = c.DFF
    tm, ts, nck = c.tm_e, c.ts_e, c.nck_e
    n_asg = n_tok * c.K
    grid_spec = pltpu.PrefetchScalarGridSpec(
        num_scalar_prefetch=5,
        grid=(_n_tiles(c),),
        in_specs=[
            pl.BlockSpec(memory_space=pl.ANY),
            pl.BlockSpec(memory_space=pl.ANY),
            pl.BlockSpec(memory_space=pl.ANY),
            pl.BlockSpec(memory_space=pl.ANY),
            pl.BlockSpec((c.E, 1, 2 * f), lambda i, *_: (0, 0, 0), pipeline_mode=pl.Buffered(1)),
            pl.BlockSpec((c.E, 1, d), lambda i, *_: (0, 0, 0), pipeline_mode=pl.Buffered(1)),
        ],
        out_specs=pl.BlockSpec(memory_space=pl.ANY),
        scratch_shapes=[
            pltpu.VMEM((d, 2 * f), BF16),
            pltpu.VMEM((f, d), BF16),
            pltpu.VMEM((c.nst_e, d // nck, 2 * f), F32),
            pltpu.VMEM((tm // ts, ts, f), BF16),
            pltpu.VMEM((2, ts, dh), U32),
            pltpu.VMEM((2, ts, dh), U32),
            pltpu.SMEM((2 * _id_window(c),), I32),
            pltpu.SemaphoreType.DMA((c.nst_e,)),
            pltpu.SemaphoreType.DMA((2,)),
            pltpu.SemaphoreType.DMA((2,)),
            pltpu.SemaphoreType.DMA((2,)),
        ],
    )
    kern = functools.partial(_moe_kernel, ts=ts, tm=tm, nck=nck, fc=c.fc_e, n_tok=n_tok, n_asg=n_asg)
    return pl.pallas_call(
        kern,
        grid_spec=grid_spec,
        out_shape=jax.ShapeDtypeStruct((n_asg + 2 * ts, dh), U32),
        compiler_params=_cparams(("arbitrary",)),
        name="experts",
    )(*sched, order_pad, hfp, w_gu, w_d, b_gu3, b_d3)


def _combine_kernel(x2_ref, gate_ref, g_ref, *refs, topk):
    y_refs, o_ref = refs[:topk], refs[topk]
    d = x2_ref.shape[1]
    dh = d // 2
    lo = x2_ref[:, :dh]
    hi = x2_ref[:, dh:]
    gates = gate_ref[...]
    for kk in range(topk):
        wk = gates[:, kk:kk + 1]
        p = y_refs[kk][...]
        lo = lo + wk * _unpack_lo(p)
        hi = hi + wk * _unpack_hi(p)
    ms = (jnp.sum(lo * lo, axis=-1, keepdims=True) + jnp.sum(hi * hi, axis=-1, keepdims=True)) / d
    inv = lax.rsqrt(ms + RMS_EPS)
    o_ref[:, :dh] = lo * inv * g_ref[:, :dh]
    o_ref[:, dh:] = hi * inv * g_ref[:, dh:]


def _combine(c, x2, gates, g, ykt):
    n, d = x2.shape
    tm = c.tm_c
    nb = n // tm
    y_specs = [pl.BlockSpec((tm, d // 2), functools.partial(lambda i, kk: (kk * nb + i, 0), kk=kk))
               for kk in range(c.K)]
    return pl.pallas_call(
        functools.partial(_combine_kernel, topk=c.K),
        grid=(nb,),
        in_specs=[
            pl.BlockSpec((tm, d), lambda i: (i, 0)),
            pl.BlockSpec((tm, LANES), lambda i: (i, 0)),
            pl.BlockSpec((1, d), lambda i: (0, 0)),
        ] + y_specs,
        out_specs=pl.BlockSpec((tm, d), lambda i: (i, 0)),
        out_shape=jax.ShapeDtypeStruct((n, d), F32),
        compiler_params=_cparams(("parallel",)),
        name="combine",
    )(x2, gates, g, *([ykt] * c.K))


def _plan(c, idx, counts):
    tm, ts = c.tm_e, c.ts_e
    nk = idx.shape[0] * c.K
    order = jnp.argsort(idx.T.reshape(-1)).astype(I32)
    pad = _id_window(c) + (-nk) % ID_CHUNK
    order_pad = jnp.concatenate([order, jnp.zeros((pad,), I32)])
    counts = counts.astype(I32)
    start = jnp.cumsum(counts) - counts
    tiles_per_e = (counts + tm - 1) // tm
    tile_end = jnp.cumsum(tiles_per_e)
    tile_start = tile_end - tiles_per_e
    t = jnp.arange(_n_tiles(c), dtype=I32)
    n_used = tile_end[-1]
    tc = jnp.minimum(t, n_used - 1)
    te = jnp.minimum(jnp.searchsorted(tile_end, tc, side="right"), c.E - 1).astype(I32)
    j = tc - tile_start[te]
    used = t < n_used
    rows = jnp.where(used, jnp.clip(counts[te] - j * tm, 0, tm), 0).astype(I32)
    nsub = (rows + ts - 1) // ts
    pos = jnp.where(used, start[te] + j * tm, 0).astype(I32)
    cpi = (c.nck_e + jnp.maximum(nsub, 1) - 1) // jnp.maximum(nsub, 1)
    return order_pad, (te, nsub.astype(I32), cpi.astype(I32), pos, rows)


def _n_tiles(c):
    return -(-(c.B * c.S * c.K) // c.tm_e) + c.E


def _forward(c, x, mem, norm_mix_g, w_in, b_forget, sg_ln_g, sg_ln_b, w_spatial, b_spatial,
             w_branch_a, w_branch_b, w_out, norm_x_g, norm_mem_g, w_xq, w_xkv, w_xo,
             norm_ffn_g, w_router, b_router, w_gate_up, b_gate_up, w_down, b_down, norm_final_g):
    B, S, D = x.shape
    n = B * S
    fw = c.FH * LANES
    sw = c.SG * LANES
    x2d = x.reshape(n, D)

    o_f = 3 * fw
    o_z = o_f + c.FH
    o_g = o_z + 2 * sw
    w_main = jnp.concatenate([w_in[:, o_g:], w_in[:, o_z:o_g], w_in[:, :o_f]], axis=1).astype(BF16)
    w_f = jnp.pad(w_in[:, o_f:o_z], ((0, 0), (0, LANES - c.FH))).astype(BF16)
    b_f = jnp.pad(b_forget.astype(F32), (0, LANES - c.FH)).reshape(1, LANES)
    proj, logf = _in_proj(c, x2d, norm_mix_g.reshape(1, D), w_main, w_f, b_f, LOG2E * LANES ** -0.5)

    logf_bhs = logf[:, :c.FH].reshape(B, S, c.FH).transpose(0, 2, 1)
    csum = _cumsum(c, logf_bhs)
    crow = csum.reshape(B * c.FH, 1, S)
    ccol = csum.reshape(B * c.FH, S, 1)
    qcol0 = (2 * D + 2 * sw) // LANES
    attn = _fox(c, proj, crow, ccol, qcol0)

    b_s_full = jnp.broadcast_to(b_spatial.astype(F32)[:, :, None], (c.SG, c.SGC, LANES))
    x1 = _mix(c, x2d, attn, proj, sg_ln_g.reshape(1, sw), sg_ln_b.reshape(1, sw), w_spatial, b_s_full,
              w_branch_a.astype(BF16), w_branch_b.astype(BF16), w_out.astype(BF16))

    kv = _mem_kv(c, mem.reshape(B * c.MEM, D), norm_mem_g.reshape(1, D), w_xkv.astype(BF16))
    w_r32 = jnp.pad(w_router.astype(F32), ((0, 0), (0, LANES - c.E)))
    w_r_hi = w_r32.astype(BF16)
    w_r = jnp.concatenate([w_r_hi, (w_r32 - w_r_hi.astype(F32)).astype(BF16)], axis=1)
    b_r = jnp.pad(b_router.astype(F32), (0, LANES - c.E)).reshape(1, LANES)
    x2, hfp, idx, gates, counts = _xattn(
        c, x1, norm_x_g.reshape(1, D), w_xq.astype(BF16), kv, w_xo.astype(BF16),
        norm_ffn_g.reshape(1, D), w_r, b_r, LANES ** -0.5)

    order_pad, sched = _plan(c, idx[:, :c.K], counts[0, :c.E])
    ykt = _moe(c, sched, order_pad, hfp, w_gate_up, b_gate_up.reshape(c.E, 1, 2 * c.DFF),
               w_down, b_down.reshape(c.E, 1, D))
    out = _combine(c, x2, gates, norm_final_g.reshape(1, D), ykt)
    return out.reshape(B, S, D)


_CFG = Cfg(B=4, S=4096, D=2048, MEM=256, FH=8, SG=8, SGC=128, XH=4, E=32, K=4, DFF=2048,
           tm_in=1024, tn_in=1024, tq=1024, tm_mix=256, tm_x=512, tm_e=2560, ts_e=256, fc_e=512, nck_e=16, nst_e=4,
           tm_c=256, tc_cs=512, fox_parts=2)


@jax.jit
def kernel(x, mem, norm_mix_g, w_in, b_forget, sg_ln_g, sg_ln_b, w_spatial, b_spatial, w_branch_a, w_branch_b,
           w_out, norm_x_g, norm_mem_g, w_xq, w_xkv, w_xo, norm_ffn_g, w_router, b_router, w_gate_up,
           b_gate_up, w_down, b_down, norm_final_g):
    return _forward(_CFG, x, mem, norm_mix_g, w_in, b_forget, sg_ln_g, sg_ln_b, w_spatial, b_spatial,
                    w_branch_a, w_branch_b, w_out, norm_x_g, norm_mem_g, w_xq, w_xkv, w_xo,
                    norm_ffn_g, w_router, b_router, w_gate_up, b_gate_up, w_down, b_down, norm_final_g)
```

```python
import functools
from typing import NamedTuple

import jax
import jax.numpy as jnp
from jax import lax
from jax.experimental import pallas as pl
from jax.experimental.pallas import tpu as pltpu

F32 = jnp.float32
BF16 = jnp.bfloat16
U32 = jnp.uint32
I32 = jnp.int32

LANES = 128
VMEM_LIMIT = 56 * 1024 * 1024

RMS_EPS = 1e-6
LN_EPS = 1e-5
SWIGLU_LIMIT = 7.0
SWIGLU_ALPHA = 1.702
GELU_C = 0.7978845608028654
LOG2E = 1.4426950408889634
ID_CHUNK = 1024
WEIGHT_DMA_PRIORITY = 1


class Cfg(NamedTuple):
    B: int
    S: int
    D: int
    MEM: int
    FH: int
    SG: int
    SGC: int
    XH: int
    E: int
    K: int
    DFF: int
    tm_in: int
    tn_in: int
    tq: int
    tm_mix: int
    tm_x: int
    tm_e: int
    ts_e: int
    fc_e: int
    nck_e: int
    nst_e: int
    tm_c: int
    tc_cs: int
    fox_parts: int


def _cparams(sem):
    return pltpu.CompilerParams(dimension_semantics=sem, vmem_limit_bytes=VMEM_LIMIT)


def _rms(x, g):
    ms = jnp.mean(x * x, axis=-1, keepdims=True)
    return x * lax.rsqrt(ms + RMS_EPS) * g


def _pack_bf16_pair(a, b):
    def rne(v):
        bits = lax.bitcast_convert_type(v, U32)
        return bits + jnp.uint32(0x7FFF) + ((bits >> 16) & jnp.uint32(1))
    return (rne(a) >> 16) | (rne(b) & jnp.uint32(0xFFFF0000))


def _unpack_lo(p):
    return lax.bitcast_convert_type(p << 16, F32)


def _unpack_hi(p):
    return lax.bitcast_convert_type(p & jnp.uint32(0xFFFF0000), F32)


def _in_proj_kernel(x_ref, g_ref, w_ref, wf_ref, bf_ref, o_ref, f_ref, h_ref, *, nj_gate, nj_z, nj_q, qscale):
    j = pl.program_id(1)

    @pl.when(j == 0)
    def _():
        hb = _rms(x_ref[...], g_ref[...]).astype(BF16)
        h_ref[...] = hb
        f = jnp.dot(hb, wf_ref[...], preferred_element_type=F32) + bf_ref[...]
        f_ref[...] = jnp.minimum(f, 0.0) - jnp.log1p(jnp.exp(-jnp.abs(f)))

    acc = jnp.dot(h_ref[...], w_ref[...], preferred_element_type=F32)

    is_gate = j < nj_gate
    is_z = (j >= nj_gate) & (j < nj_gate + nj_z)
    is_q = (j >= nj_gate + nj_z) & (j < nj_gate + nj_z + nj_q)
    a1 = jnp.where(is_gate, 0.5, jnp.where(is_z, GELU_C, 0.0)).astype(F32)
    a3 = jnp.where(is_z, GELU_C * 0.044715, 0.0).astype(F32)
    b0 = jnp.where(is_gate, 0.5, 0.0).astype(F32)
    b1 = jnp.where(is_gate, 0.0, jnp.where(is_z, 0.5, jnp.where(is_q, qscale, 1.0))).astype(F32)
    th = jnp.tanh(acc * (a1 + a3 * (acc * acc)))
    o_ref[...] = ((b0 + b1 * acc) * (1.0 + th)).astype(o_ref.dtype)


def _in_proj(c, x2d, g, w_main, w_f, b_f, qscale):
    n, d = x2d.shape
    nc = w_main.shape[1]
    tm, tn = c.tm_in, c.tn_in
    fw = c.FH * LANES
    kern = functools.partial(_in_proj_kernel, nj_gate=2 * d // tn, nj_z=2 * c.SG * LANES // tn,
                             nj_q=fw // tn, qscale=qscale)
    return pl.pallas_call(
        kern,
        grid=(n // tm, nc // tn),
        in_specs=[
            pl.BlockSpec((tm, d), lambda i, j: (i, 0)),
            pl.BlockSpec((1, d), lambda i, j: (0, 0)),
            pl.BlockSpec((d, tn), lambda i, j: (0, j)),
            pl.BlockSpec((d, LANES), lambda i, j: (0, 0)),
            pl.BlockSpec((1, LANES), lambda i, j: (0, 0)),
        ],
        out_specs=[
            pl.BlockSpec((tm, tn), lambda i, j: (i, j)),
            pl.BlockSpec((tm, LANES), lambda i, j: (i, 0)),
        ],
        out_shape=[jax.ShapeDtypeStruct((n, nc), BF16), jax.ShapeDtypeStruct((n, LANES), F32)],
        scratch_shapes=[pltpu.VMEM((tm, d), BF16)],
        compiler_params=_cparams(("parallel", "arbitrary")),
        name="in_proj",
    )(x2d, g, w_main, w_f, b_f)


def _cumsum_kernel(f_ref, o_ref, *, tc):
    rows, s = f_ref.shape[1], f_ref.shape[2]
    r = lax.broadcasted_iota(I32, (tc, tc), 0)
    col = lax.broadcasted_iota(I32, (tc, tc), 1)
    upper = (r <= col).astype(F32)
    carry = jnp.zeros((rows, 1), F32)
    for i in range(s // tc):
        blk = f_ref[0, :, i * tc:(i + 1) * tc]
        cs = jnp.dot(blk, upper, preferred_element_type=F32, precision=lax.Precision.HIGHEST) + carry
        o_ref[0, :, i * tc:(i + 1) * tc] = cs * LOG2E
        carry = cs[:, tc - 1:tc]


def _cumsum(c, logf_bhs):
    b, h, s = logf_bhs.shape
    return pl.pallas_call(
        functools.partial(_cumsum_kernel, tc=c.tc_cs),
        grid=(b,),
        in_specs=[pl.BlockSpec((1, h, s), lambda i: (i, 0, 0))],
        out_specs=pl.BlockSpec((1, h, s), lambda i: (i, 0, 0)),
        out_shape=jax.ShapeDtypeStruct((b, h, s), F32),
        compiler_params=_cparams(("parallel",)),
        name="cumsum",
    )(logf_bhs)


def _fox_kernel(q_ref, k_ref, v_ref, crow_ref, ccol_ref, o_ref, m_ref, l_ref, acc_ref, cq_ref, s_ref,
                *, t, parts):
    qi = pl.program_id(2)
    hr = t // parts
    m_ref[...] = jnp.full(m_ref.shape, -jnp.inf, F32)
    l_ref[...] = jnp.zeros(l_ref.shape, F32)
    acc_ref[...] = jnp.zeros(acc_ref.shape, F32)
    cq_ref[...] = jnp.broadcast_to(ccol_ref[0], cq_ref.shape)

    def scores(ks):
        k = k_ref[pl.ds(ks, t), :]
        return lax.dot_general(q_ref[...], k, (((1,), (1,)), ((), ())), preferred_element_type=F32)

    def block(part, ks, width, masked):
        rows = slice(part * hr, (part + 1) * hr)
        v = v_ref[pl.ds(ks, width), :]
        s = s_ref[rows, :width]
        cq = cq_ref[rows, :]
        crow = crow_ref[0, :, pl.ds(ks, width)]
        nj = width // LANES
        sj = [s[:, j * LANES:(j + 1) * LANES] + cq - crow[:, j * LANES:(j + 1) * LANES] for j in range(nj)]
        if masked:
            row = lax.broadcasted_iota(I32, (hr, LANES), 0) + part * hr
            col = lax.broadcasted_iota(I32, (hr, LANES), 1)
            sj = [jnp.where(col + j * LANES <= row, sj[j], -jnp.inf) for j in range(nj)]
        mx = sj[0]
        for j in range(1, nj):
            mx = jnp.maximum(mx, sj[j])
        m_prev = m_ref[rows, :]
        m_next = jnp.maximum(m_prev, jnp.max(mx, axis=-1, keepdims=True))
        alpha = jnp.exp2(m_prev - m_next)
        pj = [jnp.exp2(sj[j] - m_next) for j in range(nj)]
        psum = pj[0]
        for j in range(1, nj):
            psum = psum + pj[j]
        p = jnp.concatenate([x.astype(BF16) for x in pj], axis=-1)
        l_ref[rows, :] = alpha * l_ref[rows, :] + psum
        acc_ref[rows, :] = alpha * acc_ref[rows, :] + jnp.dot(p, v, preferred_element_type=F32)
        m_ref[rows, :] = m_next

    s_ref[...] = scores(0)

    def body(kc, carry):
        ks = pl.multiple_of(kc * t, t)
        s_next = scores(pl.multiple_of(ks + t, t))
        for part in range(parts):
            block(part, ks, t, False)
        s_ref[...] = s_next
        return carry

    lax.fori_loop(0, qi, body, 0)
    kd = pl.multiple_of(qi * t, t)
    for part in range(parts):
        block(part, kd, (part + 1) * hr, True)
    l = jnp.sum(l_ref[...], axis=-1, keepdims=True)
    o_ref[...] = (acc_ref[...] / l).astype(o_ref.dtype)


def _fox(c, proj, crow, ccol, qcol0):
    n = c.B * c.S
    t = c.tq
    nq = c.S // t
    h = c.FH
    return pl.pallas_call(
        functools.partial(_fox_kernel, t=t, parts=c.fox_parts),
        grid=(c.B, h, nq),
        in_specs=[
            pl.BlockSpec((t, LANES), lambda b, hh, qi: (b * nq + qi, qcol0 + hh)),
            pl.BlockSpec((c.S, LANES), lambda b, hh, qi: (b, qcol0 + h + hh)),
            pl.BlockSpec((c.S, LANES), lambda b, hh, qi: (b, qcol0 + 2 * h + hh)),
            pl.BlockSpec((1, 1, c.S), lambda b, hh, qi: (b * h + hh, 0, 0)),
            pl.BlockSpec((1, t, 1), lambda b, hh, qi: (b * h + hh, qi, 0)),
        ],
        out_specs=pl.BlockSpec((t, LANES), lambda b, hh, qi: (b * nq + qi, hh)),
        out_shape=jax.ShapeDtypeStruct((n, h * LANES), BF16),
        scratch_shapes=[pltpu.VMEM((t, LANES), F32)] * 4 + [pltpu.VMEM((t, t), F32)],
        compiler_params=_cparams(("parallel", "parallel", "arbitrary")),
        name="fox",
    )(proj, proj, proj, crow, ccol)


def _mix_kernel(x_ref, a_ref, u_ref, v_ref, ga_ref, gb_ref, lng_ref, lnb_ref, ws_ref, bs_ref,
                wa_ref, wb_ref, wo_ref, o_ref, sg_ref, *, sgc, groups):
    tm = x_ref.shape[0]
    v = v_ref[...].astype(F32)
    mu = jnp.mean(v, axis=-1, keepdims=True)
    vc = v - mu
    var = jnp.mean(vc * vc, axis=-1, keepdims=True)
    vn = (vc * lax.rsqrt(var + LN_EPS) * lng_ref[...] + lnb_ref[...]).astype(BF16)
    row = lax.broadcasted_iota(I32, (sgc, sgc), 0)
    col = lax.broadcasted_iota(I32, (sgc, sgc), 1)
    for g in range(groups):
        w = jnp.where(col <= row, ws_ref[g], 0.0).astype(BF16)
        bias = bs_ref[g]
        for ci in range(tm // sgc):
            rs = slice(ci * sgc, (ci + 1) * sgc)
            cs = slice(g * LANES, (g + 1) * LANES)
            mixed = jnp.dot(w, vn[rs, cs], preferred_element_type=F32) + bias
            sg_ref[rs, cs] = (u_ref[rs, cs].astype(F32) * mixed).astype(BF16)
    ya = jnp.dot(a_ref[...], wa_ref[...], preferred_element_type=F32)
    yb = jnp.dot(sg_ref[...], wb_ref[...], preferred_element_type=F32)
    merged = (ga_ref[...].astype(F32) * ya + gb_ref[...].astype(F32) * yb).astype(BF16)
    o_ref[...] = x_ref[...] + jnp.dot(merged, wo_ref[...], preferred_element_type=F32)


def _const_spec(shape):
    nd = len(shape)
    return pl.BlockSpec(shape, lambda i: (0,) * nd, pipeline_mode=pl.Buffered(1))


def _mix(c, x2d, attn, proj, ln_g, ln_b, w_s, b_s_full, w_a, w_b, w_o):
    n, d = x2d.shape
    tm = c.tm_mix
    fw = c.FH * LANES
    sw = c.SG * LANES
    ucol = 2 * d // sw
    return pl.pallas_call(
        functools.partial(_mix_kernel, sgc=c.SGC, groups=c.SG),
        grid=(n // tm,),
        in_specs=[
            pl.BlockSpec((tm, d), lambda i: (i, 0)),
            pl.BlockSpec((tm, fw), lambda i: (i, 0)),
            pl.BlockSpec((tm, sw), lambda i: (i, ucol)),
            pl.BlockSpec((tm, sw), lambda i: (i, ucol + 1)),
            pl.BlockSpec((tm, d), lambda i: (i, 0)),
            pl.BlockSpec((tm, d), lambda i: (i, 1)),
            _const_spec((1, sw)),
            _const_spec((1, sw)),
            _const_spec((c.SG, c.SGC, c.SGC)),
            _const_spec((c.SG, c.SGC, LANES)),
            _const_spec((fw, d)),
            _const_spec((sw, d)),
            _const_spec((d, d)),
        ],
        out_specs=pl.BlockSpec((tm, d), lambda i: (i, 0)),
        out_shape=jax.ShapeDtypeStruct((n, d), F32),
        scratch_shapes=[pltpu.VMEM((tm, sw), BF16)],
        compiler_params=_cparams(("parallel",)),
        name="mix",
    )(x2d, attn, proj, proj, proj, proj, ln_g, ln_b, w_s, b_s_full, w_a, w_b, w_o)


def _mem_kv_kernel(m_ref, g_ref, w_ref, o_ref):
    hm = _rms(m_ref[...], g_ref[...]).astype(BF16)
    o_ref[...] = jnp.dot(hm, w_ref[...], preferred_element_type=F32).astype(o_ref.dtype)


def _mem_kv(c, mem2d, g, w_xkv):
    n, d = mem2d.shape
    nc = w_xkv.shape[1]
    tm = c.MEM
    return pl.pallas_call(
        _mem_kv_kernel,
        grid=(n // tm,),
        in_specs=[pl.BlockSpec((tm, d), lambda i: (i, 0)), _const_spec((1, d)), _const_spec((d, nc))],
        out_specs=pl.BlockSpec((tm, nc), lambda i: (i, 0)),
        out_shape=jax.ShapeDtypeStruct((n, nc), BF16),
        compiler_params=_cparams(("parallel",)),
        name="mem_kv",
    )(mem2d, g, w_xkv)


def _xattn_kernel(x1_ref, gx_ref, wq_ref, kv_ref, wo_ref, gf_ref, wr_ref, br_ref,
                  x2_ref, hfp_ref, idx_ref, gate_ref, cnt_ref, carry_ref,
                  *, heads, n_exp, topk, qscale):
    i = pl.program_id(0)
    tm, d = x1_ref.shape
    xw = heads * LANES
    x1 = x1_ref[...]
    hx = _rms(x1, gx_ref[...]).astype(BF16)
    q = (jnp.dot(hx, wq_ref[...], preferred_element_type=F32) * qscale).astype(BF16)
    outs = []
    for h in range(heads):
        k = kv_ref[:, h * LANES:(h + 1) * LANES]
        v = kv_ref[:, xw + h * LANES:xw + (h + 1) * LANES]
        s = lax.dot_general(q[:, h * LANES:(h + 1) * LANES], k, (((1,), (1,)), ((), ())),
                            preferred_element_type=F32)
        s = s - jnp.max(s, axis=-1, keepdims=True)
        p = jnp.exp(s)
        p = p / jnp.sum(p, axis=-1, keepdims=True)
        outs.append(jnp.dot(p.astype(BF16), v, preferred_element_type=F32).astype(BF16))
    o = jnp.concatenate(outs, axis=-1)
    x2 = x1 + jnp.dot(o, wo_ref[...], preferred_element_type=F32)
    x2_ref[...] = x2

    hf = _rms(x2, gf_ref[...])
    hfp_ref[...] = _pack_bf16_pair(hf[:, :d // 2], hf[:, d // 2:])

    h_hi = hf.astype(BF16)
    h_lo = (hf - h_hi.astype(F32)).astype(BF16)
    l_hi = jnp.dot(h_hi, wr_ref[...], preferred_element_type=F32)
    l_lo = jnp.dot(h_lo, wr_ref[:, :LANES], preferred_element_type=F32)
    logits = l_hi[:, :LANES] + l_hi[:, LANES:] + l_lo + br_ref[...]
    lane = lax.broadcasted_iota(I32, (tm, LANES), 1)
    lg = jnp.where(lane < n_exp, logits, -jnp.inf)
    vals, idxs = [], []
    for _ in range(topk):
        m = jnp.max(lg, axis=-1, keepdims=True)
        ix = jnp.min(jnp.where(lg == m, lane, LANES), axis=-1, keepdims=True)
        vals.append(m)
        idxs.append(ix)
        lg = jnp.where(lane == ix, -jnp.inf, lg)
    es = [jnp.exp(vv - vals[0]) for vv in vals]
    denom = es[0]
    for e in es[1:]:
        denom = denom + e

    @pl.when(i == 0)
    def _():
        carry_ref[...] = jnp.zeros(carry_ref.shape, F32)

    onehot = jnp.zeros((tm, LANES), F32)
    for ix in idxs:
        onehot = onehot + (lane == ix).astype(F32)
    idx_out = jnp.zeros((tm, LANES), I32)
    gate_out = jnp.zeros((tm, LANES), F32)
    for kk in range(topk):
        idx_out = jnp.where(lane == kk, idxs[kk], idx_out)
        gate_out = jnp.where(lane == kk, es[kk] / denom, gate_out)
    idx_ref[...] = idx_out
    gate_ref[...] = gate_out
    carry_ref[...] = carry_ref[...] + jnp.sum(onehot, axis=0, keepdims=True)
    cnt_ref[...] = carry_ref[...]


def _xattn(c, x1, gx, w_xq, kv, w_xo, gf, w_r, b_r, qscale):
    n, d = x1.shape
    tm = c.tm_x
    xw = c.XH * LANES
    per_b = c.S // tm
    kern = functools.partial(_xattn_kernel, heads=c.XH, n_exp=c.E, topk=c.K, qscale=qscale)
    return pl.pallas_call(
        kern,
        grid=(n // tm,),
        in_specs=[
            pl.BlockSpec((tm, d), lambda i: (i, 0)),
            _const_spec((1, d)),
            _const_spec((d, xw)),
            pl.BlockSpec((c.MEM, 2 * xw), lambda i: (i // per_b, 0)),
            _const_spec((xw, d)),
            _const_spec((1, d)),
            _const_spec((d, 2 * LANES)),
            _const_spec((1, LANES)),
        ],
        out_specs=[
            pl.BlockSpec((tm, d), lambda i: (i, 0)),
            pl.BlockSpec((tm, d // 2), lambda i: (i, 0)),
            pl.BlockSpec((tm, LANES), lambda i: (i, 0)),
            pl.BlockSpec((tm, LANES), lambda i: (i, 0)),
            pl.BlockSpec((1, LANES), lambda i: (0, 0)),
        ],
        out_shape=[
            jax.ShapeDtypeStruct((n, d), F32),
            jax.ShapeDtypeStruct((n, d // 2), U32),
            jax.ShapeDtypeStruct((n, LANES), I32),
            jax.ShapeDtypeStruct((n, LANES), F32),
            jax.ShapeDtypeStruct((1, LANES), F32),
        ],
        scratch_shapes=[pltpu.VMEM((1, LANES), F32)],
        compiler_params=_cparams(("arbitrary",)),
        name="xattn_router",
    )(x1, gx, w_xq, kv, w_xo, gf, w_r, b_r)


def _moe_kernel(te_ref, ns_ref, cpi_ref, pos_ref, rows_ref, ord_ref, hfp_ref, wgu_ref, wd_ref, bgu_ref, bd_ref,
                ykt_ref, wgub, wdb, stage, act, xin, ybuf, ids, wsem, xsem, ysem, isem,
                *, ts, tm, nck, fc, n_tok, n_asg):
    i = pl.program_id(0)
    n = pl.num_programs(0)
    nsub = ns_ref[i]
    e = te_ref[i]
    cpi = cpi_ref[i]
    rows = rows_ref[i]
    d, f2 = wgub.shape
    f = f2 // 2
    dh = d // 2
    ckr = d // nck
    ckd = f // nck
    nst = stage.shape[0]
    idw = ids.shape[0] // 2
    nq = idw // ID_CHUNK
    id_base = (i % 2) * idw + (pos_ref[i] & (ID_CHUNK - 1))

    def ids_copy(item, q, slot):
        c0 = lax.shift_right_logical(pos_ref[item], ID_CHUNK.bit_length() - 1)
        src = ord_ref.at[pl.ds(pl.multiple_of((c0 + q) * ID_CHUNK, ID_CHUNK), ID_CHUNK)]
        dst = ids.at[pl.ds(pl.multiple_of(slot * idw + q * ID_CHUNK, ID_CHUNK), ID_CHUNK)]
        return pltpu.make_async_copy(src, dst, isem.at[slot])

    def token_of(a):
        return a & (n_tok - 1) if n_tok & (n_tok - 1) == 0 else lax.rem(a, n_tok)

    def gather(s, slot, r0=0, r1=ts, base=None):
        base = id_base if base is None else base
        for r in range(r0, r1):
            tok = token_of(ids[base + s * ts + r])
            pltpu.make_async_copy(hfp_ref.at[pl.ds(tok, 1)], xin.at[slot, pl.ds(r, 1)], xsem.at[slot]).start()

    def gather_wait(slot):
        for r in range(ts):
            pltpu.make_async_copy(hfp_ref.at[pl.ds(0, 1)], xin.at[slot, pl.ds(0, 1)], xsem.at[slot]).wait()

    def scatter(s, slot, r0=0, r1=ts):
        for r in range(r0, r1):
            g = s * ts + r
            dst = jnp.where(g < rows, ids[id_base + g], n_asg + slot * ts + r)
            pltpu.make_async_copy(ybuf.at[slot, pl.ds(r, 1)], ykt_ref.at[pl.ds(dst, 1)], ysem.at[slot]).start()

    def scatter_wait(slot):
        for r in range(ts):
            pltpu.make_async_copy(ybuf.at[slot, pl.ds(0, 1)], ykt_ref.at[pl.ds(0, 1)], ysem.at[slot]).wait()

    def gu_copy(ee, c, slot):
        return pltpu.make_async_copy(wgu_ref.at[ee, pl.ds(pl.multiple_of(c * ckr, ckr), ckr), :],
                                     stage.at[slot], wsem.at[slot])

    def d_copy(ee, c, slot):
        return pltpu.make_async_copy(wd_ref.at[ee, pl.ds(pl.multiple_of(c * ckd, ckd), ckd), :],
                                     stage.at[slot, pl.ds(0, ckd), pl.ds(0, d)], wsem.at[slot])

    def gu_step(ee, c):
        slot = c % nst
        gu_copy(ee, c, slot).wait()
        wgub[pl.ds(pl.multiple_of(c * ckr, ckr), ckr), :] = stage[slot].astype(BF16)

        @pl.when(c + nst < nck)
        def _():
            gu_copy(ee, c + nst, slot).start(priority=WEIGHT_DMA_PRIORITY)

    def d_step(ee, c):
        slot = c % nst
        d_copy(ee, c, slot).wait()
        wdb[pl.ds(pl.multiple_of(c * ckd, ckd), ckd), :] = stage[slot, :ckd, :d].astype(BF16)

        @pl.when(c + nst < nck)
        def _():
            d_copy(ee, c + nst, slot).start(priority=WEIGHT_DMA_PRIORITY)

    def prime(copy_fn, ee):
        for q in range(min(nst, nck)):
            copy_fn(ee, q, q).start(priority=WEIGHT_DMA_PRIORITY)

    def steps_after(s, step_fn, ee):
        def one(jj, carry):
            c = s * cpi + jj

            @pl.when(c < nck)
            def _():
                step_fn(ee, c)
            return carry
        lax.fori_loop(0, cpi, one, 0)

    @pl.when(nsub > 0)
    def _():
        nxt = jnp.minimum(i + 1, n - 1)
        has_next = (i + 1 < n) & (ns_ref[nxt] > 0)

        @pl.when(i == 0)
        def _():
            for q in range(nq):
                ids_copy(0, q, 0).start()
            prime(gu_copy, e)

            def first(c, carry):
                gu_step(e, c)
                return carry
            lax.fori_loop(0, nck, first, 0)
            for q in range(nq):
                ids_copy(0, q, 0).wait()
            gather(0, 0)

        nslot = 1 - i % 2
        for q in range(nq):
            ids_copy(nxt, q, nslot).start()

        prime(d_copy, e)
        bgu = bgu_ref[e]

        def phase_a(s, carry):
            slot = s % 2
            gather_wait(slot)
            p = xin[slot]
            xb = jnp.concatenate([_unpack_lo(p).astype(BF16), _unpack_hi(p).astype(BF16)], axis=-1)
            gather(s + 1, 1 - slot)
            for cc in range(f // fc):
                gate = jnp.dot(xb, wgub[:, cc * fc:(cc + 1) * fc], preferred_element_type=F32)
                gate = gate + bgu[:, cc * fc:(cc + 1) * fc]
                up = jnp.dot(xb, wgub[:, f + cc * fc:f + (cc + 1) * fc], preferred_element_type=F32)
                up = up + bgu[:, f + cc * fc:f + (cc + 1) * fc]
                gate = jnp.minimum(gate, SWIGLU_LIMIT)
                up = jnp.clip(up, -SWIGLU_LIMIT, SWIGLU_LIMIT)
                a = (up + 1.0) * gate * jax.nn.sigmoid(SWIGLU_ALPHA * gate)
                act[s, :, cc * fc:(cc + 1) * fc] = a.astype(BF16)
            steps_after(s, d_step, e)
            return carry

        lax.fori_loop(0, nsub, phase_a, 0)
        gather_wait(nsub % 2)

        e_next = te_ref[nxt]
        load_next = has_next & (e_next != e)

        @pl.when(load_next)
        def _():
            prime(gu_copy, e_next)

        bd = bd_ref[e]

        def down(s, slot, scatter_prev):
            a = act[s]
            if scatter_prev:
                scatter(s - 1, 1 - slot)
            fcb = min(fc, dh // 2)
            for cc in range(dh // fcb):
                lo = slice(cc * fcb, (cc + 1) * fcb)
                hi = slice(dh + cc * fcb, dh + (cc + 1) * fcb)
                y_lo = jnp.dot(a, wdb[:, lo], preferred_element_type=F32) + bd[:, lo]
                y_hi = jnp.dot(a, wdb[:, hi], preferred_element_type=F32) + bd[:, hi]
                ybuf[slot, :, lo] = _pack_bf16_pair(y_lo, y_hi)

        def next_weights(s):
            @pl.when(load_next)
            def _():
                steps_after(s, gu_step, e_next)

        for q in range(nq):
            ids_copy(nxt, q, nslot).wait()
        gather(0, 0, base=nslot * idw + (pos_ref[nxt] & (ID_CHUNK - 1)))
        down(0, 0, False)
        next_weights(0)

        def phase_b(s, carry):
            slot = s % 2

            @pl.when(s >= 2)
            def _():
                scatter_wait(slot)

            down(s, slot, True)
            next_weights(s)
            return carry

        lax.fori_loop(1, nsub, phase_b, 0)
        scatter(nsub - 1, (nsub - 1) % 2)
        scatter_wait((nsub - 1) % 2)

        @pl.when(nsub >= 2)
        def _():
            scatter_wait(nsub % 2)

        @pl.when(jnp.logical_not(has_next))
        def _():
            gather_wait(0)


def _id_window(c):
    return ((ID_CHUNK - 1 + c.tm_e + c.ts_e) // ID_CHUNK + 1) * ID_CHUNK


def _moe(c, sched, order_pad, hfp, w_gu, b_gu3, w_d, b_d3):
    n_tok, dh = hfp.shape
    d = 2 * dh
    f = c.DFF
    tm, ts, nck = c.tm_e, c.ts_e, c.nck_e
    n_asg = n_tok * c.K
    grid_spec = pltpu.PrefetchScalarGridSpec(
        num_scalar_prefetch=5,
        grid=(_n_tiles(c),),
        in_specs=[
            pl.BlockSpec(memory_space=pl.ANY),
            pl.BlockSpec(memory_space=pl.ANY),
            pl.BlockSpec(memory_space=pl.ANY),
            pl.BlockSpec(memory_space=pl.ANY),
            pl.BlockSpec((c.E, 1, 2 * f), lambda i, *_: (0, 0, 0), pipeline_mode=pl.Buffered(1)),
            pl.BlockSpec((c.E, 1, d), lambda i, *_: (0, 0, 0), pipeline_mode=pl.Buffered(1)),
        ],
        out_specs=pl.BlockSpec(memory_space=pl.ANY),
        scratch_shapes=[
            pltpu.VMEM((d, 2 * f), BF16),
            pltpu.VMEM((f, d), BF16),
            pltpu.VMEM((c.nst_e, d // nck, 2 * f), F32),
            pltpu.VMEM((tm // ts, ts, f), BF16),
            pltpu.VMEM((2, ts, dh), U32),
            pltpu.VMEM((2, ts, dh), U32),
            pltpu.SMEM((2 * _id_window(c),), I32),
            pltpu.SemaphoreType.DMA((c.nst_e,)),
            pltpu.SemaphoreType.DMA((2,)),
            pltpu.SemaphoreType.DMA((2,)),
            pltpu.SemaphoreType.DMA((2,)),
        ],
    )
    kern = functools.partial(_moe_kernel, ts=ts, tm=tm, nck=nck, fc=c.fc_e, n_tok=n_tok, n_asg=n_asg)
    return pl.pallas_call(
        kern,
        grid_spec=grid_spec,
        out_shape=jax.ShapeDtypeStruct((n_asg + 2 * ts, dh), U32),
        compiler_params=_cparams(("arbitrary",)),
        name="experts",
    )(*sched, order_pad, hfp, w_gu, w_d, b_gu3, b_d3)


def _combine_kernel(x2_ref, gate_ref, g_ref, *refs, topk):
    y_refs, o_ref = refs[:topk], refs[topk]
    d = x2_ref.shape[1]
    dh = d // 2
    lo = x2_ref[:, :dh]
    hi = x2_ref[:, dh:]
    gates = gate_ref[...]
    for kk in range(topk):
        wk = gates[:, kk:kk + 1]
        p = y_refs[kk][...]
        lo = lo + wk * _unpack_lo(p)
        hi = hi + wk * _unpack_hi(p)
    ms = (jnp.sum(lo * lo, axis=-1, keepdims=True) + jnp.sum(hi * hi, axis=-1, keepdims=True)) / d
    inv = lax.rsqrt(ms + RMS_EPS)
    o_ref[:, :dh] = lo * inv * g_ref[:, :dh]
    o_ref[:, dh:] = hi * inv * g_ref[:, dh:]


def _combine(c, x2, gates, g, ykt):
    n, d = x2.shape
    tm = c.tm_c
    nb = n // tm
    y_specs = [pl.BlockSpec((tm, d // 2), functools.partial(lambda i, kk: (kk * nb + i, 0), kk=kk))
               for kk in range(c.K)]
    return pl.pallas_call(
        functools.partial(_combine_kernel, topk=c.K),
        grid=(nb,),
        in_specs=[
            pl.BlockSpec((tm, d), lambda i: (i, 0)),
            pl.BlockSpec((tm, LANES), lambda i: (i, 0)),
            pl.BlockSpec((1, d), lambda i: (0, 0)),
        ] + y_specs,
        out_specs=pl.BlockSpec((tm, d), lambda i: (i, 0)),
        out_shape=jax.ShapeDtypeStruct((n, d), F32),
        compiler_params=_cparams(("parallel",)),
        name="combine",
    )(x2, gates, g, *([ykt] * c.K))


def _plan(c, idx, counts):
    tm, ts = c.tm_e, c.ts_e
    nk = idx.shape[0] * c.K
    pad = _id_window(c) + (-nk) % ID_CHUNK
    keys = jnp.concatenate([idx.T.reshape(-1), jnp.full((pad,), c.E, I32)])
    order_pad = jnp.argsort(keys).astype(I32)
    counts = counts.astype(I32)
    start = jnp.cumsum(counts) - counts
    tiles_per_e = (counts + tm - 1) // tm
    tile_end = jnp.cumsum(tiles_per_e)
    tile_start = tile_end - tiles_per_e
    t = jnp.arange(_n_tiles(c), dtype=I32)
    n_used = tile_end[-1]
    tc = jnp.minimum(t, n_used - 1)
    te = jnp.minimum(jnp.searchsorted(tile_end, tc, side="right"), c.E - 1).astype(I32)
    j = tc - tile_start[te]
    used = t < n_used
    rows = jnp.where(used, jnp.clip(counts[te] - j * tm, 0, tm), 0).astype(I32)
    nsub = (rows + ts - 1) // ts
    pos = jnp.where(used, start[te] + j * tm, 0).astype(I32)
    cpi = (c.nck_e + jnp.maximum(nsub, 1) - 1) // jnp.maximum(nsub, 1)
    return order_pad, (te, nsub.astype(I32), cpi.astype(I32), pos, rows)


def _n_tiles(c):
    return -(-(c.B * c.S * c.K) // c.tm_e) + c.E


def _forward(c, x, mem, norm_mix_g, w_in, b_forget, sg_ln_g, sg_ln_b, w_spatial, b_spatial,
             w_branch_a, w_branch_b, w_out, norm_x_g, norm_mem_g, w_xq, w_xkv, w_xo,
             norm_ffn_g, w_router, b_router, w_gate_up, b_gate_up, w_down, b_down, norm_final_g):
    B, S, D = x.shape
    n = B * S
    fw = c.FH * LANES
    sw = c.SG * LANES
    x2d = x.reshape(n, D)

    o_f = 3 * fw
    o_z = o_f + c.FH
    o_g = o_z + 2 * sw
    w_main = jnp.concatenate([w_in[:, o_g:], w_in[:, o_z:o_g], w_in[:, :o_f]], axis=1).astype(BF16)
    w_f = jnp.pad(w_in[:, o_f:o_z], ((0, 0), (0, LANES - c.FH))).astype(BF16)
    b_f = jnp.pad(b_forget.astype(F32), (0, LANES - c.FH)).reshape(1, LANES)
    proj, logf = _in_proj(c, x2d, norm_mix_g.reshape(1, D), w_main, w_f, b_f, LOG2E * LANES ** -0.5)

    logf_bhs = logf[:, :c.FH].reshape(B, S, c.FH).transpose(0, 2, 1)
    csum = _cumsum(c, logf_bhs)
    crow = csum.reshape(B * c.FH, 1, S)
    ccol = csum.reshape(B * c.FH, S, 1)
    qcol0 = (2 * D + 2 * sw) // LANES
    attn = _fox(c, proj, crow, ccol, qcol0)

    b_s_full = jnp.broadcast_to(b_spatial.astype(F32)[:, :, None], (c.SG, c.SGC, LANES))
    x1 = _mix(c, x2d, attn, proj, sg_ln_g.reshape(1, sw), sg_ln_b.reshape(1, sw), w_spatial, b_s_full,
              w_branch_a.astype(BF16), w_branch_b.astype(BF16), w_out.astype(BF16))

    kv = _mem_kv(c, mem.reshape(B * c.MEM, D), norm_mem_g.reshape(1, D), w_xkv.astype(BF16))
    w_r32 = jnp.pad(w_router.astype(F32), ((0, 0), (0, LANES - c.E)))
    w_r_hi = w_r32.astype(BF16)
    w_r = jnp.concatenate([w_r_hi, (w_r32 - w_r_hi.astype(F32)).astype(BF16)], axis=1)
    b_r = jnp.pad(b_router.astype(F32), (0, LANES - c.E)).reshape(1, LANES)
    x2, hfp, idx, gates, counts = _xattn(
        c, x1, norm_x_g.reshape(1, D), w_xq.astype(BF16), kv, w_xo.astype(BF16),
        norm_ffn_g.reshape(1, D), w_r, b_r, LANES ** -0.5)

    order_pad, sched = _plan(c, idx[:, :c.K], counts[0, :c.E])
    ykt = _moe(c, sched, order_pad, hfp, w_gate_up, b_gate_up.reshape(c.E, 1, 2 * c.DFF),
               w_down, b_down.reshape(c.E, 1, D))
    out = _combine(c, x2, gates, norm_final_g.reshape(1, D), ykt)
    return out.reshape(B, S, D)


_CFG = Cfg(B=4, S=4096, D=2048, MEM=256, FH=8, SG=8, SGC=128, XH=4, E=32, K=4, DFF=2048,
           tm_in=1024, tn_in=1024, tq=1024, tm_mix=256, tm_x=512, tm_e=2560, ts_e=256, fc_e=512, nck_e=16, nst_e=4,
           tm_c=256, tc_cs=512, fox_parts=2)


@jax.jit
def kernel(x, mem, norm_mix_g, w_in, b_forget, sg_ln_g, sg_ln_b, w_spatial, b_spatial, w_branch_a, w_branch_b,
           w_out, norm_x_g, norm_mem_g, w_xq, w_xkv, w_xo, norm_ffn_g, w_router, b_router, w_gate_up,
           b_gate_up, w_down, b_down, norm_final_g):
    return _forward(_CFG, x, mem, norm_mix_g, w_in, b_forget, sg_ln_g, sg_ln_b, w_spatial, b_spatial,
                    w_branch_a, w_branch_b, w_out, norm_x_g, norm_mem_g, w_xq, w_xkv, w_xo,
                    norm_ffn_g, w_router, b_router, w_gate_up, b_gate_up, w_down, b_down, norm_final_g)
```

```python
import functools
from typing import NamedTuple

import jax
import jax.numpy as jnp
from jax import lax
from jax.experimental import pallas as pl
from jax.experimental.pallas import tpu as pltpu

F32 = jnp.float32
BF16 = jnp.bfloat16
U32 = jnp.uint32
I32 = jnp.int32

LANES = 128
VMEM_LIMIT = 56 * 1024 * 1024

RMS_EPS = 1e-6
LN_EPS = 1e-5
SWIGLU_LIMIT = 7.0
SWIGLU_ALPHA = 1.702
GELU_C = 0.7978845608028654
LOG2E = 1.4426950408889634
ID_CHUNK = 1024
WEIGHT_DMA_PRIORITY = 1


class Cfg(NamedTuple):
    B: int
    S: int
    D: int
    MEM: int
    FH: int
    SG: int
    SGC: int
    XH: int
    E: int
    K: int
    DFF: int
    tm_in: int
    tn_in: int
    tq: int
    tm_mix: int
    tm_x: int
    tm_e: int
    ts_e: int
    fc_e: int
    nck_e: int
    nst_e: int
    tm_c: int
    tc_cs: int
    fox_parts: int


def _cparams(sem):
    return pltpu.CompilerParams(dimension_semantics=sem, vmem_limit_bytes=VMEM_LIMIT)


def _rms(x, g):
    ms = jnp.mean(x * x, axis=-1, keepdims=True)
    return x * lax.rsqrt(ms + RMS_EPS) * g


def _pack_bf16_pair(a, b):
    def rne(v):
        bits = lax.bitcast_convert_type(v, U32)
        return bits + jnp.uint32(0x7FFF) + ((bits >> 16) & jnp.uint32(1))
    return (rne(a) >> 16) | (rne(b) & jnp.uint32(0xFFFF0000))


def _unpack_lo(p):
    return lax.bitcast_convert_type(p << 16, F32)


def _unpack_hi(p):
    return lax.bitcast_convert_type(p & jnp.uint32(0xFFFF0000), F32)


def _in_proj_kernel(x_ref, g_ref, w_ref, wf_ref, bf_ref, o_ref, f_ref, h_ref, *, nj_gate, nj_z, nj_q, qscale):
    j = pl.program_id(1)

    @pl.when(j == 0)
    def _():
        hb = _rms(x_ref[...], g_ref[...]).astype(BF16)
        h_ref[...] = hb
        f = jnp.dot(hb, wf_ref[...], preferred_element_type=F32) + bf_ref[...]
        f_ref[...] = jnp.minimum(f, 0.0) - jnp.log1p(jnp.exp(-jnp.abs(f)))

    acc = jnp.dot(h_ref[...], w_ref[...], preferred_element_type=F32)

    is_gate = j < nj_gate
    is_z = (j >= nj_gate) & (j < nj_gate + nj_z)
    is_q = (j >= nj_gate + nj_z) & (j < nj_gate + nj_z + nj_q)
    a1 = jnp.where(is_gate, 0.5, jnp.where(is_z, GELU_C, 0.0)).astype(F32)
    a3 = jnp.where(is_z, GELU_C * 0.044715, 0.0).astype(F32)
    b0 = jnp.where(is_gate, 0.5, 0.0).astype(F32)
    b1 = jnp.where(is_gate, 0.0, jnp.where(is_z, 0.5, jnp.where(is_q, qscale, 1.0))).astype(F32)
    th = jnp.tanh(acc * (a1 + a3 * (acc * acc)))
    o_ref[...] = ((b0 + b1 * acc) * (1.0 + th)).astype(o_ref.dtype)


def _in_proj(c, x2d, g, w_main, w_f, b_f, qscale):
    n, d = x2d.shape
    nc = w_main.shape[1]
    tm, tn = c.tm_in, c.tn_in
    fw = c.FH * LANES
    kern = functools.partial(_in_proj_kernel, nj_gate=2 * d // tn, nj_z=2 * c.SG * LANES // tn,
                             nj_q=fw // tn, qscale=qscale)
    return pl.pallas_call(
        kern,
        grid=(n // tm, nc // tn),
        in_specs=[
            pl.BlockSpec((tm, d), lambda i, j: (i, 0)),
            pl.BlockSpec((1, d), lambda i, j: (0, 0)),
            pl.BlockSpec((d, tn), lambda i, j: (0, j)),
            pl.BlockSpec((d, LANES), lambda i, j: (0, 0)),
            pl.BlockSpec((1, LANES), lambda i, j: (0, 0)),
        ],
        out_specs=[
            pl.BlockSpec((tm, tn), lambda i, j: (i, j)),
            pl.BlockSpec((tm, LANES), lambda i, j: (i, 0)),
        ],
        out_shape=[jax.ShapeDtypeStruct((n, nc), BF16), jax.ShapeDtypeStruct((n, LANES), F32)],
        scratch_shapes=[pltpu.VMEM((tm, d), BF16)],
        compiler_params=_cparams(("parallel", "arbitrary")),
        name="in_proj",
    )(x2d, g, w_main, w_f, b_f)


def _cumsum_kernel(f_ref, o_ref, *, tc):
    rows, s = f_ref.shape[1], f_ref.shape[2]
    r = lax.broadcasted_iota(I32, (tc, tc), 0)
    col = lax.broadcasted_iota(I32, (tc, tc), 1)
    upper = (r <= col).astype(F32)
    carry = jnp.zeros((rows, 1), F32)
    for i in range(s // tc):
        blk = f_ref[0, :, i * tc:(i + 1) * tc]
        cs = jnp.dot(blk, upper, preferred_element_type=F32, precision=lax.Precision.HIGHEST) + carry
        o_ref[0, :, i * tc:(i + 1) * tc] = cs * LOG2E
        carry = cs[:, tc - 1:tc]


def _cumsum(c, logf_bhs):
    b, h, s = logf_bhs.shape
    return pl.pallas_call(
        functools.partial(_cumsum_kernel, tc=c.tc_cs),
        grid=(b,),
        in_specs=[pl.BlockSpec((1, h, s), lambda i: (i, 0, 0))],
        out_specs=pl.BlockSpec((1, h, s), lambda i: (i, 0, 0)),
        out_shape=jax.ShapeDtypeStruct((b, h, s), F32),
        compiler_params=_cparams(("parallel",)),
        name="cumsum",
    )(logf_bhs)


def _fox_kernel(q_ref, k_ref, v_ref, crow_ref, ccol_ref, o_ref, m_ref, l_ref, acc_ref, cq_ref, s_ref,
                *, t, parts):
    qi = pl.program_id(2)
    hr = t // parts
    m_ref[...] = jnp.full(m_ref.shape, -jnp.inf, F32)
    l_ref[...] = jnp.zeros(l_ref.shape, F32)
    acc_ref[...] = jnp.zeros(acc_ref.shape, F32)
    cq_ref[...] = jnp.broadcast_to(ccol_ref[0], cq_ref.shape)

    def scores(ks):
        k = k_ref[pl.ds(ks, t), :]
        return lax.dot_general(q_ref[...], k, (((1,), (1,)), ((), ())), preferred_element_type=F32)

    def block(part, ks, width, masked):
        rows = slice(part * hr, (part + 1) * hr)
        v = v_ref[pl.ds(ks, width), :]
        s = s_ref[rows, :width]
        cq = cq_ref[rows, :]
        crow = crow_ref[0, :, pl.ds(ks, width)]
        nj = width // LANES
        sj = [s[:, j * LANES:(j + 1) * LANES] + cq - crow[:, j * LANES:(j + 1) * LANES] for j in range(nj)]
        if masked:
            row = lax.broadcasted_iota(I32, (hr, LANES), 0) + part * hr
            col = lax.broadcasted_iota(I32, (hr, LANES), 1)
            sj = [jnp.where(col + j * LANES <= row, sj[j], -jnp.inf) for j in range(nj)]
        mx = sj[0]
        for j in range(1, nj):
            mx = jnp.maximum(mx, sj[j])
        m_prev = m_ref[rows, :]
        m_next = jnp.maximum(m_prev, jnp.max(mx, axis=-1, keepdims=True))
        alpha = jnp.exp2(m_prev - m_next)
        pj = [jnp.exp2(sj[j] - m_next) for j in range(nj)]
        psum = pj[0]
        for j in range(1, nj):
            psum = psum + pj[j]
        p = jnp.concatenate([x.astype(BF16) for x in pj], axis=-1)
        l_ref[rows, :] = alpha * l_ref[rows, :] + psum
        acc_ref[rows, :] = alpha * acc_ref[rows, :] + jnp.dot(p, v, preferred_element_type=F32)
        m_ref[rows, :] = m_next

    s_ref[...] = scores(0)

    def body(kc, carry):
        ks = pl.multiple_of(kc * t, t)
        s_next = scores(pl.multiple_of(ks + t, t))
        for part in range(parts):
            block(part, ks, t, False)
        s_ref[...] = s_next
        return carry

    lax.fori_loop(0, qi, body, 0)
    kd = pl.multiple_of(qi * t, t)
    for part in range(parts):
        block(part, kd, (part + 1) * hr, True)
    l = jnp.sum(l_ref[...], axis=-1, keepdims=True)
    o_ref[...] = (acc_ref[...] / l).astype(o_ref.dtype)


def _fox(c, proj, crow, ccol, qcol0):
    n = c.B * c.S
    t = c.tq
    nq = c.S // t
    h = c.FH
    return pl.pallas_call(
        functools.partial(_fox_kernel, t=t, parts=c.fox_parts),
        grid=(c.B, h, nq),
        in_specs=[
            pl.BlockSpec((t, LANES), lambda b, hh, qi: (b * nq + qi, qcol0 + hh)),
            pl.BlockSpec((c.S, LANES), lambda b, hh, qi: (b, qcol0 + h + hh)),
            pl.BlockSpec((c.S, LANES), lambda b, hh, qi: (b, qcol0 + 2 * h + hh)),
            pl.BlockSpec((1, 1, c.S), lambda b, hh, qi: (b * h + hh, 0, 0)),
            pl.BlockSpec((1, t, 1), lambda b, hh, qi: (b * h + hh, qi, 0)),
        ],
        out_specs=pl.BlockSpec((t, LANES), lambda b, hh, qi: (b * nq + qi, hh)),
        out_shape=jax.ShapeDtypeStruct((n, h * LANES), BF16),
        scratch_shapes=[pltpu.VMEM((t, LANES), F32)] * 4 + [pltpu.VMEM((t, t), F32)],
        compiler_params=_cparams(("parallel", "parallel", "arbitrary")),
        name="fox",
    )(proj, proj, proj, crow, ccol)


def _mix_kernel(x_ref, a_ref, u_ref, v_ref, ga_ref, gb_ref, lng_ref, lnb_ref, ws_ref, bs_ref,
                wa_ref, wb_ref, wo_ref, o_ref, sg_ref, *, sgc, groups):
    tm = x_ref.shape[0]
    v = v_ref[...].astype(F32)
    mu = jnp.mean(v, axis=-1, keepdims=True)
    vc = v - mu
    var = jnp.mean(vc * vc, axis=-1, keepdims=True)
    vn = (vc * lax.rsqrt(var + LN_EPS) * lng_ref[...] + lnb_ref[...]).astype(BF16)
    row = lax.broadcasted_iota(I32, (sgc, sgc), 0)
    col = lax.broadcasted_iota(I32, (sgc, sgc), 1)
    for g in range(groups):
        w = jnp.where(col <= row, ws_ref[g], 0.0).astype(BF16)
        bias = bs_ref[g]
        for ci in range(tm // sgc):
            rs = slice(ci * sgc, (ci + 1) * sgc)
            cs = slice(g * LANES, (g + 1) * LANES)
            mixed = jnp.dot(w, vn[rs, cs], preferred_element_type=F32) + bias
            sg_ref[rs, cs] = (u_ref[rs, cs].astype(F32) * mixed).astype(BF16)
    ya = jnp.dot(a_ref[...], wa_ref[...], preferred_element_type=F32)
    yb = jnp.dot(sg_ref[...], wb_ref[...], preferred_element_type=F32)
    merged = (ga_ref[...].astype(F32) * ya + gb_ref[...].astype(F32) * yb).astype(BF16)
    o_ref[...] = x_ref[...] + jnp.dot(merged, wo_ref[...], preferred_element_type=F32)


def _const_spec(shape):
    nd = len(shape)
    return pl.BlockSpec(shape, lambda i: (0,) * nd, pipeline_mode=pl.Buffered(1))


def _mix(c, x2d, attn, proj, ln_g, ln_b, w_s, b_s_full, w_a, w_b, w_o):
    n, d = x2d.shape
    tm = c.tm_mix
    fw = c.FH * LANES
    sw = c.SG * LANES
    ucol = 2 * d // sw
    return pl.pallas_call(
        functools.partial(_mix_kernel, sgc=c.SGC, groups=c.SG),
        grid=(n // tm,),
        in_specs=[
            pl.BlockSpec((tm, d), lambda i: (i, 0)),
            pl.BlockSpec((tm, fw), lambda i: (i, 0)),
            pl.BlockSpec((tm, sw), lambda i: (i, ucol)),
            pl.BlockSpec((tm, sw), lambda i: (i, ucol + 1)),
            pl.BlockSpec((tm, d), lambda i: (i, 0)),
            pl.BlockSpec((tm, d), lambda i: (i, 1)),
            _const_spec((1, sw)),
            _const_spec((1, sw)),
            _const_spec((c.SG, c.SGC, c.SGC)),
            _const_spec((c.SG, c.SGC, LANES)),
            _const_spec((fw, d)),
            _const_spec((sw, d)),
            _const_spec((d, d)),
        ],
        out_specs=pl.BlockSpec((tm, d), lambda i: (i, 0)),
        out_shape=jax.ShapeDtypeStruct((n, d), F32),
        scratch_shapes=[pltpu.VMEM((tm, sw), BF16)],
        compiler_params=_cparams(("parallel",)),
        name="mix",
    )(x2d, attn, proj, proj, proj, proj, ln_g, ln_b, w_s, b_s_full, w_a, w_b, w_o)


def _mem_kv_kernel(m_ref, g_ref, w_ref, o_ref):
    hm = _rms(m_ref[...], g_ref[...]).astype(BF16)
    o_ref[...] = jnp.dot(hm, w_ref[...], preferred_element_type=F32).astype(o_ref.dtype)


def _mem_kv(c, mem2d, g, w_xkv):
    n, d = mem2d.shape
    nc = w_xkv.shape[1]
    tm = c.MEM
    return pl.pallas_call(
        _mem_kv_kernel,
        grid=(n // tm,),
        in_specs=[pl.BlockSpec((tm, d), lambda i: (i, 0)), _const_spec((1, d)), _const_spec((d, nc))],
        out_specs=pl.BlockSpec((tm, nc), lambda i: (i, 0)),
        out_shape=jax.ShapeDtypeStruct((n, nc), BF16),
        compiler_params=_cparams(("parallel",)),
        name="mem_kv",
    )(mem2d, g, w_xkv)


def _xattn_kernel(x1_ref, gx_ref, wq_ref, kv_ref, wo_ref, gf_ref, wr_ref, br_ref,
                  x2_ref, hfp_ref, idx_ref, gate_ref, cnt_ref, carry_ref,
                  *, heads, n_exp, topk, qscale):
    i = pl.program_id(0)
    tm, d = x1_ref.shape
    xw = heads * LANES
    x1 = x1_ref[...]
    hx = _rms(x1, gx_ref[...]).astype(BF16)
    q = (jnp.dot(hx, wq_ref[...], preferred_element_type=F32) * qscale).astype(BF16)
    outs = []
    for h in range(heads):
        k = kv_ref[:, h * LANES:(h + 1) * LANES]
        v = kv_ref[:, xw + h * LANES:xw + (h + 1) * LANES]
        s = lax.dot_general(q[:, h * LANES:(h + 1) * LANES], k, (((1,), (1,)), ((), ())),
                            preferred_element_type=F32)
        s = s - jnp.max(s, axis=-1, keepdims=True)
        p = jnp.exp(s)
        p = p / jnp.sum(p, axis=-1, keepdims=True)
        outs.append(jnp.dot(p.astype(BF16), v, preferred_element_type=F32).astype(BF16))
    o = jnp.concatenate(outs, axis=-1)
    x2 = x1 + jnp.dot(o, wo_ref[...], preferred_element_type=F32)
    x2_ref[...] = x2

    hf = _rms(x2, gf_ref[...])
    hfp_ref[...] = _pack_bf16_pair(hf[:, :d // 2], hf[:, d // 2:])

    h_hi = hf.astype(BF16)
    h_lo = (hf - h_hi.astype(F32)).astype(BF16)
    l_hi = jnp.dot(h_hi, wr_ref[...], preferred_element_type=F32)
    l_lo = jnp.dot(h_lo, wr_ref[:, :LANES], preferred_element_type=F32)
    logits = l_hi[:, :LANES] + l_hi[:, LANES:] + l_lo + br_ref[...]
    lane = lax.broadcasted_iota(I32, (tm, LANES), 1)
    lg = jnp.where(lane < n_exp, logits, -jnp.inf)
    vals, idxs = [], []
    for _ in range(topk):
        m = jnp.max(lg, axis=-1, keepdims=True)
        ix = jnp.min(jnp.where(lg == m, lane, LANES), axis=-1, keepdims=True)
        vals.append(m)
        idxs.append(ix)
        lg = jnp.where(lane == ix, -jnp.inf, lg)
    es = [jnp.exp(vv - vals[0]) for vv in vals]
    denom = es[0]
    for e in es[1:]:
        denom = denom + e

    @pl.when(i == 0)
    def _():
        carry_ref[...] = jnp.zeros(carry_ref.shape, F32)

    onehot = jnp.zeros((tm, LANES), F32)
    for ix in idxs:
        onehot = onehot + (lane == ix).astype(F32)
    idx_out = jnp.zeros((tm, LANES), I32)
    gate_out = jnp.zeros((tm, LANES), F32)
    for kk in range(topk):
        idx_out = jnp.where(lane == kk, idxs[kk], idx_out)
        gate_out = jnp.where(lane == kk, es[kk] / denom, gate_out)
    idx_ref[...] = idx_out
    gate_ref[...] = gate_out
    carry_ref[...] = carry_ref[...] + jnp.sum(onehot, axis=0, keepdims=True)
    cnt_ref[...] = carry_ref[...]


def _xattn(c, x1, gx, w_xq, kv, w_xo, gf, w_r, b_r, qscale):
    n, d = x1.shape
    tm = c.tm_x
    xw = c.XH * LANES
    per_b = c.S // tm
    kern = functools.partial(_xattn_kernel, heads=c.XH, n_exp=c.E, topk=c.K, qscale=qscale)
    return pl.pallas_call(
        kern,
        grid=(n // tm,),
        in_specs=[
            pl.BlockSpec((tm, d), lambda i: (i, 0)),
            _const_spec((1, d)),
            _const_spec((d, xw)),
            pl.BlockSpec((c.MEM, 2 * xw), lambda i: (i // per_b, 0)),
            _const_spec((xw, d)),
            _const_spec((1, d)),
            _const_spec((d, 2 * LANES)),
            _const_spec((1, LANES)),
        ],
        out_specs=[
            pl.BlockSpec((tm, d), lambda i: (i, 0)),
            pl.BlockSpec((tm, d // 2), lambda i: (i, 0)),
            pl.BlockSpec((tm, LANES), lambda i: (i, 0)),
            pl.BlockSpec((tm, LANES), lambda i: (i, 0)),
            pl.BlockSpec((1, LANES), lambda i: (0, 0)),
        ],
        out_shape=[
            jax.ShapeDtypeStruct((n, d), F32),
            jax.ShapeDtypeStruct((n, d // 2), U32),
            jax.ShapeDtypeStruct((n, LANES), I32),
            jax.ShapeDtypeStruct((n, LANES), F32),
            jax.ShapeDtypeStruct((1, LANES), F32),
        ],
        scratch_shapes=[pltpu.VMEM((1, LANES), F32)],
        compiler_params=_cparams(("arbitrary",)),
        name="xattn_router",
    )(x1, gx, w_xq, kv, w_xo, gf, w_r, b_r)


def _moe_kernel(te_ref, ns_ref, cpi_ref, pos_ref, rows_ref, ord_ref, hfp_ref, wgu_ref, wd_ref, bgu_ref, bd_ref,
                ykt_ref, wgub, wdb, stage, act, xin, ybuf, ids, wsem, xsem, ysem, isem,
                *, ts, tm, nck, fc, n_tok, n_asg):
    i = pl.program_id(0)
    n = pl.num_programs(0)
    nsub = ns_ref[i]
    e = te_ref[i]
    cpi = cpi_ref[i]
    rows = rows_ref[i]
    d, f2 = wgub.shape
    f = f2 // 2
    dh = d // 2
    ckr = d // nck
    ckd = f // nck
    nst = stage.shape[0]
    idw = ids.shape[0] // 2
    nq = idw // ID_CHUNK
    id_base = (i % 2) * idw + (pos_ref[i] & (ID_CHUNK - 1))

    def ids_copy(item, q, slot):
        c0 = lax.shift_right_logical(pos_ref[item], ID_CHUNK.bit_length() - 1)
        src = ord_ref.at[pl.ds(pl.multiple_of((c0 + q) * ID_CHUNK, ID_CHUNK), ID_CHUNK)]
        dst = ids.at[pl.ds(pl.multiple_of(slot * idw + q * ID_CHUNK, ID_CHUNK), ID_CHUNK)]
        return pltpu.make_async_copy(src, dst, isem.at[slot])

    def token_of(a):
        return a & (n_tok - 1) if n_tok & (n_tok - 1) == 0 else lax.rem(a, n_tok)

    def gather(s, slot, r0=0, r1=ts, base=None):
        base = id_base if base is None else base
        for r in range(r0, r1):
            tok = token_of(ids[base + s * ts + r])
            pltpu.make_async_copy(hfp_ref.at[pl.ds(tok, 1)], xin.at[slot, pl.ds(r, 1)], xsem.at[slot]).start()

    def gather_wait(slot):
        for r in range(ts):
            pltpu.make_async_copy(hfp_ref.at[pl.ds(0, 1)], xin.at[slot, pl.ds(0, 1)], xsem.at[slot]).wait()

    def scatter(s, slot, r0=0, r1=ts):
        for r in range(r0, r1):
            g = s * ts + r
            dst = jnp.where(g < rows, ids[id_base + g], n_asg + slot * ts + r)
            pltpu.make_async_copy(ybuf.at[slot, pl.ds(r, 1)], ykt_ref.at[pl.ds(dst, 1)], ysem.at[slot]).start()

    def scatter_wait(slot):
        for r in range(ts):
            pltpu.make_async_copy(ybuf.at[slot, pl.ds(0, 1)], ykt_ref.at[pl.ds(0, 1)], ysem.at[slot]).wait()

    def gu_copy(ee, c, slot):
        return pltpu.make_async_copy(wgu_ref.at[ee, pl.ds(pl.multiple_of(c * ckr, ckr), ckr), :],
                                     stage.at[slot], wsem.at[slot])

    def d_copy(ee, c, slot):
        return pltpu.make_async_copy(wd_ref.at[ee, pl.ds(pl.multiple_of(c * ckd, ckd), ckd), :],
                                     stage.at[slot, pl.ds(0, ckd), pl.ds(0, d)], wsem.at[slot])

    def gu_step(ee, c):
        slot = c % nst
        gu_copy(ee, c, slot).wait()
        wgub[pl.ds(pl.multiple_of(c * ckr, ckr), ckr), :] = stage[slot].astype(BF16)

        @pl.when(c + nst < nck)
        def _():
            gu_copy(ee, c + nst, slot).start(priority=WEIGHT_DMA_PRIORITY)

    def d_step(ee, c):
        slot = c % nst
        d_copy(ee, c, slot).wait()
        wdb[pl.ds(pl.multiple_of(c * ckd, ckd), ckd), :] = stage[slot, :ckd, :d].astype(BF16)

        @pl.when(c + nst < nck)
        def _():
            d_copy(ee, c + nst, slot).start(priority=WEIGHT_DMA_PRIORITY)

    def prime(copy_fn, ee):
        for q in range(min(nst, nck)):
            copy_fn(ee, q, q).start(priority=WEIGHT_DMA_PRIORITY)

    def steps_after(s, step_fn, ee):
        def one(jj, carry):
            c = s * cpi + jj

            @pl.when(c < nck)
            def _():
                step_fn(ee, c)
            return carry
        lax.fori_loop(0, cpi, one, 0)

    @pl.when(nsub > 0)
    def _():
        nxt = jnp.minimum(i + 1, n - 1)
        has_next = (i + 1 < n) & (ns_ref[nxt] > 0)

        @pl.when(i == 0)
        def _():
            for q in range(nq):
                ids_copy(0, q, 0).start()
            prime(gu_copy, e)

            def first(c, carry):
                gu_step(e, c)
                return carry
            lax.fori_loop(0, nck, first, 0)
            for q in range(nq):
                ids_copy(0, q, 0).wait()
            gather(0, 0)

        nslot = 1 - i % 2
        for q in range(nq):
            ids_copy(nxt, q, nslot).start()

        prime(d_copy, e)
        bgu = bgu_ref[e]

        def phase_a(s, carry):
            slot = s % 2
            gather_wait(slot)
            p = xin[slot]
            xb = jnp.concatenate([_unpack_lo(p).astype(BF16), _unpack_hi(p).astype(BF16)], axis=-1)
            nc = f // fc
            for cc in range(nc):
                gather(s + 1, 1 - slot, cc * ts // nc, (cc + 1) * ts // nc)
                gate = jnp.dot(xb, wgub[:, cc * fc:(cc + 1) * fc], preferred_element_type=F32)
                gate = gate + bgu[:, cc * fc:(cc + 1) * fc]
                up = jnp.dot(xb, wgub[:, f + cc * fc:f + (cc + 1) * fc], preferred_element_type=F32)
                up = up + bgu[:, f + cc * fc:f + (cc + 1) * fc]
                gate = jnp.minimum(gate, SWIGLU_LIMIT)
                up = jnp.clip(up, -SWIGLU_LIMIT, SWIGLU_LIMIT)
                a = (up + 1.0) * gate * jax.nn.sigmoid(SWIGLU_ALPHA * gate)
                act[s, :, cc * fc:(cc + 1) * fc] = a.astype(BF16)
            steps_after(s, d_step, e)
            return carry

        lax.fori_loop(0, nsub, phase_a, 0)
        gather_wait(nsub % 2)

        e_next = te_ref[nxt]
        load_next = has_next & (e_next != e)

        @pl.when(load_next)
        def _():
            prime(gu_copy, e_next)

        bd = bd_ref[e]

        def down(s, slot, scatter_prev):
            a = act[s]
            if scatter_prev:
                scatter(s - 1, 1 - slot)
            fcb = min(fc, dh // 2)
            for cc in range(dh // fcb):
                lo = slice(cc * fcb, (cc + 1) * fcb)
                hi = slice(dh + cc * fcb, dh + (cc + 1) * fcb)
                y_lo = jnp.dot(a, wdb[:, lo], preferred_element_type=F32) + bd[:, lo]
                y_hi = jnp.dot(a, wdb[:, hi], preferred_element_type=F32) + bd[:, hi]
                ybuf[slot, :, lo] = _pack_bf16_pair(y_lo, y_hi)

        def next_weights(s):
            @pl.when(load_next)
            def _():
                steps_after(s, gu_step, e_next)

        for q in range(nq):
            ids_copy(nxt, q, nslot).wait()
        gather(0, 0, base=nslot * idw + (pos_ref[nxt] & (ID_CHUNK - 1)))
        down(0, 0, False)
        next_weights(0)

        def phase_b(s, carry):
            slot = s % 2

            @pl.when(s >= 2)
            def _():
                scatter_wait(slot)

            down(s, slot, True)
            next_weights(s)
            return carry

        lax.fori_loop(1, nsub, phase_b, 0)
        scatter(nsub - 1, (nsub - 1) % 2)
        scatter_wait((nsub - 1) % 2)

        @pl.when(nsub >= 2)
        def _():
            scatter_wait(nsub % 2)

        @pl.when(jnp.logical_not(has_next))
        def _():
            gather_wait(0)


def _id_window(c):
    return ((ID_CHUNK - 1 + c.tm_e + c.ts_e) // ID_CHUNK + 1) * ID_CHUNK


def _moe(c, sched, order_pad, hfp, w_gu, b_gu3, w_d, b_d3):
    n_tok, dh = hfp.shape
    d = 2 * dh
    f = c.DFF
    tm, ts, nck = c.tm_e, c.ts_e, c.nck_e
    n_asg = n_tok * c.K
    grid_spec = pltpu.PrefetchScalarGridSpec(
        num_scalar_prefetch=5,
        grid=(_n_tiles(c),),
        in_specs=[
            pl.BlockSpec(memory_space=pl.ANY),
            pl.BlockSpec(memory_space=pl.ANY),
            pl.BlockSpec(memory_space=pl.ANY),
            pl.BlockSpec(memory_space=pl.ANY),
            pl.BlockSpec((c.E, 1, 2 * f), lambda i, *_: (0, 0, 0), pipeline_mode=pl.Buffered(1)),
            pl.BlockSpec((c.E, 1, d), lambda i, *_: (0, 0, 0), pipeline_mode=pl.Buffered(1)),
        ],
        out_specs=pl.BlockSpec(memory_space=pl.ANY),
        scratch_shapes=[
            pltpu.VMEM((d, 2 * f), BF16),
            pltpu.VMEM((f, d), BF16),
            pltpu.VMEM((c.nst_e, d // nck, 2 * f), F32),
            pltpu.VMEM((tm // ts, ts, f), BF16),
            pltpu.VMEM((2, ts, dh), U32),
            pltpu.VMEM((2, ts, dh), U32),
            pltpu.SMEM((2 * _id_window(c),), I32),
            pltpu.SemaphoreType.DMA((c.nst_e,)),
            pltpu.SemaphoreType.DMA((2,)),
            pltpu.SemaphoreType.DMA((2,)),
            pltpu.SemaphoreType.DMA((2,)),
        ],
    )
    kern = functools.partial(_moe_kernel, ts=ts, tm=tm, nck=nck, fc=c.fc_e, n_tok=n_tok, n_asg=n_asg)
    return pl.pallas_call(
        kern,
        grid_spec=grid_spec,
        out_shape=jax.ShapeDtypeStruct((n_asg + 2 * ts, dh), U32),
        compiler_params=_cparams(("arbitrary",)),
        name="experts",
    )(*sched, order_pad, hfp, w_gu, w_d, b_gu3, b_d3)


def _combine_kernel(x2_ref, gate_ref, g_ref, *refs, topk):
    y_refs, o_ref = refs[:topk], refs[topk]
    d = x2_ref.shape[1]
    dh = d // 2
    lo = x2_ref[:, :dh]
    hi = x2_ref[:, dh:]
    gates = gate_ref[...]
    for kk in range(topk):
        wk = gates[:, kk:kk + 1]
        p = y_refs[kk][...]
        lo = lo + wk * _unpack_lo(p)
        hi = hi + wk * _unpack_hi(p)
    ms = (jnp.sum(lo * lo, axis=-1, keepdims=True) + jnp.sum(hi * hi, axis=-1, keepdims=True)) / d
    inv = lax.rsqrt(ms + RMS_EPS)
    o_ref[:, :dh] = lo * inv * g_ref[:, :dh]
    o_ref[:, dh:] = hi * inv * g_ref[:, dh:]


def _combine(c, x2, gates, g, ykt):
    n, d = x2.shape
    tm = c.tm_c
    nb = n // tm
    y_specs = [pl.BlockSpec((tm, d // 2), functools.partial(lambda i, kk: (kk * nb + i, 0), kk=kk))
               for kk in range(c.K)]
    return pl.pallas_call(
        functools.partial(_combine_kernel, topk=c.K),
        grid=(nb,),
        in_specs=[
            pl.BlockSpec((tm, d), lambda i: (i, 0)),
            pl.BlockSpec((tm, LANES), lambda i: (i, 0)),
            pl.BlockSpec((1, d), lambda i: (0, 0)),
        ] + y_specs,
        out_specs=pl.BlockSpec((tm, d), lambda i: (i, 0)),
        out_shape=jax.ShapeDtypeStruct((n, d), F32),
        compiler_params=_cparams(("parallel",)),
        name="combine",
    )(x2, gates, g, *([ykt] * c.K))


def _plan(c, idx, counts):
    tm, ts = c.tm_e, c.ts_e
    nk = idx.shape[0] * c.K
    order = jnp.argsort(idx.T.reshape(-1)).astype(I32)
    pad = _id_window(c) + (-nk) % ID_CHUNK
    order_pad = jnp.concatenate([order, jnp.zeros((pad,), I32)])
    counts = counts.astype(I32)
    start = jnp.cumsum(counts) - counts
    tiles_per_e = (counts + tm - 1) // tm
    tile_end = jnp.cumsum(tiles_per_e)
    tile_start = tile_end - tiles_per_e
    t = jnp.arange(_n_tiles(c), dtype=I32)
    n_used = tile_end[-1]
    tc = jnp.minimum(t, n_used - 1)
    te = jnp.minimum(jnp.searchsorted(tile_end, tc, side="right"), c.E - 1).astype(I32)
    j = tc - tile_start[te]
    used = t < n_used
    rows = jnp.where(used, jnp.clip(counts[te] - j * tm, 0, tm), 0).astype(I32)
    nsub = (rows + ts - 1) // ts
    pos = jnp.where(used, start[te] + j * tm, 0).astype(I32)
    cpi = (c.nck_e + jnp.maximum(nsub, 1) - 1) // jnp.maximum(nsub, 1)
    return order_pad, (te, nsub.astype(I32), cpi.astype(I32), pos, rows)


def _n_tiles(c):
    return -(-(c.B * c.S * c.K) // c.tm_e) + c.E


def _forward(c, x, mem, norm_mix_g, w_in, b_forget, sg_ln_g, sg_ln_b, w_spatial, b_spatial,
             w_branch_a, w_branch_b, w_out, norm_x_g, norm_mem_g, w_xq, w_xkv, w_xo,
             norm_ffn_g, w_router, b_router, w_gate_up, b_gate_up, w_down, b_down, norm_final_g):
    B, S, D = x.shape
    n = B * S
    fw = c.FH * LANES
    sw = c.SG * LANES
    x2d = x.reshape(n, D)

    o_f = 3 * fw
    o_z = o_f + c.FH
    o_g = o_z + 2 * sw
    w_main = jnp.concatenate([w_in[:, o_g:], w_in[:, o_z:o_g], w_in[:, :o_f]], axis=1).astype(BF16)
    w_f = jnp.pad(w_in[:, o_f:o_z], ((0, 0), (0, LANES - c.FH))).astype(BF16)
    b_f = jnp.pad(b_forget.astype(F32), (0, LANES - c.FH)).reshape(1, LANES)
    proj, logf = _in_proj(c, x2d, norm_mix_g.reshape(1, D), w_main, w_f, b_f, LOG2E * LANES ** -0.5)

    logf_bhs = logf[:, :c.FH].reshape(B, S, c.FH).transpose(0, 2, 1)
    csum = _cumsum(c, logf_bhs)
    crow = csum.reshape(B * c.FH, 1, S)
    ccol = csum.reshape(B * c.FH, S, 1)
    qcol0 = (2 * D + 2 * sw) // LANES
    attn = _fox(c, proj, crow, ccol, qcol0)

    b_s_full = jnp.broadcast_to(b_spatial.astype(F32)[:, :, None], (c.SG, c.SGC, LANES))
    x1 = _mix(c, x2d, attn, proj, sg_ln_g.reshape(1, sw), sg_ln_b.reshape(1, sw), w_spatial, b_s_full,
              w_branch_a.astype(BF16), w_branch_b.astype(BF16), w_out.astype(BF16))

    kv = _mem_kv(c, mem.reshape(B * c.MEM, D), norm_mem_g.reshape(1, D), w_xkv.astype(BF16))
    w_r32 = jnp.pad(w_router.astype(F32), ((0, 0), (0, LANES - c.E)))
    w_r_hi = w_r32.astype(BF16)
    w_r = jnp.concatenate([w_r_hi, (w_r32 - w_r_hi.astype(F32)).astype(BF16)], axis=1)
    b_r = jnp.pad(b_router.astype(F32), (0, LANES - c.E)).reshape(1, LANES)
    x2, hfp, idx, gates, counts = _xattn(
        c, x1, norm_x_g.reshape(1, D), w_xq.astype(BF16), kv, w_xo.astype(BF16),
        norm_ffn_g.reshape(1, D), w_r, b_r, LANES ** -0.5)

    order_pad, sched = _plan(c, idx[:, :c.K], counts[0, :c.E])
    ykt = _moe(c, sched, order_pad, hfp, w_gate_up, b_gate_up.reshape(c.E, 1, 2 * c.DFF),
               w_down, b_down.reshape(c.E, 1, D))
    out = _combine(c, x2, gates, norm_final_g.reshape(1, D), ykt)
    return out.reshape(B, S, D)


_CFG = Cfg(B=4, S=4096, D=2048, MEM=256, FH=8, SG=8, SGC=128, XH=4, E=32, K=4, DFF=2048,
           tm_in=1024, tn_in=1024, tq=1024, tm_mix=256, tm_x=512, tm_e=2560, ts_e=256, fc_e=256, nck_e=16, nst_e=4,
           tm_c=256, tc_cs=512, fox_parts=2)


@jax.jit
def kernel(x, mem, norm_mix_g, w_in, b_forget, sg_ln_g, sg_ln_b, w_spatial, b_spatial, w_branch_a, w_branch_b,
           w_out, norm_x_g, norm_mem_g, w_xq, w_xkv, w_xo, norm_ffn_g, w_router, b_router, w_gate_up,
           b_gate_up, w_down, b_down, norm_final_g):
    return _forward(_CFG, x, mem, norm_mix_g, w_in, b_forget, sg_ln_g, sg_ln_b, w_spatial, b_spatial,
                    w_branch_a, w_branch_b, w_out, norm_x_g, norm_mem_g, w_xq, w_xkv, w_xo,
                    norm_ffn_g, w_router, b_router, w_gate_up, b_gate_up, w_down, b_down, norm_final_g)
```

```python
import functools
from typing import NamedTuple

import jax
import jax.numpy as jnp
from jax import lax
from jax.experimental import pallas as pl
from jax.experimental.pallas import tpu as pltpu

F32 = jnp.float32
BF16 = jnp.bfloat16
U32 = jnp.uint32
I32 = jnp.int32

LANES = 128
VMEM_LIMIT = 56 * 1024 * 1024

RMS_EPS = 1e-6
LN_EPS = 1e-5
SWIGLU_LIMIT = 7.0
SWIGLU_ALPHA = 1.702
GELU_C = 0.7978845608028654
LOG2E = 1.4426950408889634
ID_CHUNK = 1024
WEIGHT_DMA_PRIORITY = 1


class Cfg(NamedTuple):
    B: int
    S: int
    D: int
    MEM: int
    FH: int
    SG: int
    SGC: int
    XH: int
    E: int
    K: int
    DFF: int
    tm_in: int
    tn_in: int
    tq: int
    tm_mix: int
    tm_x: int
    tm_e: int
    ts_e: int
    fc_e: int
    nck_e: int
    nst_e: int
    tm_c: int
    tc_cs: int
    fox_parts: int


def _cparams(sem):
    return pltpu.CompilerParams(dimension_semantics=sem, vmem_limit_bytes=VMEM_LIMIT)


def _rms(x, g):
    ms = jnp.mean(x * x, axis=-1, keepdims=True)
    return x * lax.rsqrt(ms + RMS_EPS) * g


def _pack_bf16_pair(a, b):
    def rne(v):
        bits = lax.bitcast_convert_type(v, U32)
        return bits + jnp.uint32(0x7FFF) + ((bits >> 16) & jnp.uint32(1))
    return (rne(a) >> 16) | (rne(b) & jnp.uint32(0xFFFF0000))


def _unpack_lo(p):
    return lax.bitcast_convert_type(p << 16, F32)


def _unpack_hi(p):
    return lax.bitcast_convert_type(p & jnp.uint32(0xFFFF0000), F32)


def _in_proj_kernel(x_ref, g_ref, w_ref, wf_ref, bf_ref, o_ref, f_ref, h_ref, *, nj_gate, nj_z, nj_q, qscale):
    j = pl.program_id(1)

    @pl.when(j == 0)
    def _():
        hb = _rms(x_ref[...], g_ref[...]).astype(BF16)
        h_ref[...] = hb
        f = jnp.dot(hb, wf_ref[...], preferred_element_type=F32) + bf_ref[...]
        f_ref[...] = jnp.minimum(f, 0.0) - jnp.log1p(jnp.exp(-jnp.abs(f)))

    acc = jnp.dot(h_ref[...], w_ref[...], preferred_element_type=F32)

    is_gate = j < nj_gate
    is_z = (j >= nj_gate) & (j < nj_gate + nj_z)
    is_q = (j >= nj_gate + nj_z) & (j < nj_gate + nj_z + nj_q)
    a1 = jnp.where(is_gate, 0.5, jnp.where(is_z, GELU_C, 0.0)).astype(F32)
    a3 = jnp.where(is_z, GELU_C * 0.044715, 0.0).astype(F32)
    b0 = jnp.where(is_gate, 0.5, 0.0).astype(F32)
    b1 = jnp.where(is_gate, 0.0, jnp.where(is_z, 0.5, jnp.where(is_q, qscale, 1.0))).astype(F32)
    th = jnp.tanh(acc * (a1 + a3 * (acc * acc)))
    o_ref[...] = ((b0 + b1 * acc) * (1.0 + th)).astype(o_ref.dtype)


def _in_proj(c, x2d, g, w_main, w_f, b_f, qscale):
    n, d = x2d.shape
    nc = w_main.shape[1]
    tm, tn = c.tm_in, c.tn_in
    fw = c.FH * LANES
    kern = functools.partial(_in_proj_kernel, nj_gate=2 * d // tn, nj_z=2 * c.SG * LANES // tn,
                             nj_q=fw // tn, qscale=qscale)
    return pl.pallas_call(
        kern,
        grid=(n // tm, nc // tn),
        in_specs=[
            pl.BlockSpec((tm, d), lambda i, j: (i, 0)),
            pl.BlockSpec((1, d), lambda i, j: (0, 0)),
            pl.BlockSpec((d, tn), lambda i, j: (0, j)),
            pl.BlockSpec((d, LANES), lambda i, j: (0, 0)),
            pl.BlockSpec((1, LANES), lambda i, j: (0, 0)),
        ],
        out_specs=[
            pl.BlockSpec((tm, tn), lambda i, j: (i, j)),
            pl.BlockSpec((tm, LANES), lambda i, j: (i, 0)),
        ],
        out_shape=[jax.ShapeDtypeStruct((n, nc), BF16), jax.ShapeDtypeStruct((n, LANES), F32)],
        scratch_shapes=[pltpu.VMEM((tm, d), BF16)],
        compiler_params=_cparams(("parallel", "arbitrary")),
        name="in_proj",
    )(x2d, g, w_main, w_f, b_f)


def _cumsum_kernel(f_ref, o_ref, *, tc):
    rows, s = f_ref.shape[1], f_ref.shape[2]
    r = lax.broadcasted_iota(I32, (tc, tc), 0)
    col = lax.broadcasted_iota(I32, (tc, tc), 1)
    upper = (r <= col).astype(F32)
    carry = jnp.zeros((rows, 1), F32)
    for i in range(s // tc):
        blk = f_ref[0, :, i * tc:(i + 1) * tc]
        cs = jnp.dot(blk, upper, preferred_element_type=F32, precision=lax.Precision.HIGHEST) + carry
        o_ref[0, :, i * tc:(i + 1) * tc] = cs * LOG2E
        carry = cs[:, tc - 1:tc]


def _cumsum(c, logf_bhs):
    b, h, s = logf_bhs.shape
    return pl.pallas_call(
        functools.partial(_cumsum_kernel, tc=c.tc_cs),
        grid=(b,),
        in_specs=[pl.BlockSpec((1, h, s), lambda i: (i, 0, 0))],
        out_specs=pl.BlockSpec((1, h, s), lambda i: (i, 0, 0)),
        out_shape=jax.ShapeDtypeStruct((b, h, s), F32),
        compiler_params=_cparams(("parallel",)),
        name="cumsum",
    )(logf_bhs)


def _fox_kernel(q_ref, k_ref, v_ref, crow_ref, ccol_ref, o_ref, m_ref, l_ref, acc_ref, cq_ref, s_ref,
                *, t, parts):
    qi = pl.program_id(2)
    hr = t // parts
    m_ref[...] = jnp.full(m_ref.shape, -jnp.inf, F32)
    l_ref[...] = jnp.zeros(l_ref.shape, F32)
    acc_ref[...] = jnp.zeros(acc_ref.shape, F32)
    cq_ref[...] = jnp.broadcast_to(ccol_ref[0], cq_ref.shape)

    def scores(ks):
        k = k_ref[pl.ds(ks, t), :]
        return lax.dot_general(q_ref[...], k, (((1,), (1,)), ((), ())), preferred_element_type=F32)

    def block(part, ks, width, masked):
        rows = slice(part * hr, (part + 1) * hr)
        v = v_ref[pl.ds(ks, width), :]
        s = s_ref[rows, :width]
        cq = cq_ref[rows, :]
        crow = crow_ref[0, :, pl.ds(ks, width)]
        nj = width // LANES
        sj = [s[:, j * LANES:(j + 1) * LANES] + cq - crow[:, j * LANES:(j + 1) * LANES] for j in range(nj)]
        if masked:
            row = lax.broadcasted_iota(I32, (hr, LANES), 0) + part * hr
            col = lax.broadcasted_iota(I32, (hr, LANES), 1)
            sj = [jnp.where(col + j * LANES <= row, sj[j], -jnp.inf) for j in range(nj)]
        mx = sj[0]
        for j in range(1, nj):
            mx = jnp.maximum(mx, sj[j])
        m_prev = m_ref[rows, :]
        m_next = jnp.maximum(m_prev, jnp.max(mx, axis=-1, keepdims=True))
        alpha = jnp.exp2(m_prev - m_next)
        pj = [jnp.exp2(sj[j] - m_next) for j in range(nj)]
        psum = pj[0]
        for j in range(1, nj):
            psum = psum + pj[j]
        p = jnp.concatenate([x.astype(BF16) for x in pj], axis=-1)
        l_ref[rows, :] = alpha * l_ref[rows, :] + psum
        acc_ref[rows, :] = alpha * acc_ref[rows, :] + jnp.dot(p, v, preferred_element_type=F32)
        m_ref[rows, :] = m_next

    s_ref[...] = scores(0)

    def body(kc, carry):
        ks = pl.multiple_of(kc * t, t)
        s_next = scores(pl.multiple_of(ks + t, t))
        for part in range(parts):
            block(part, ks, t, False)
        s_ref[...] = s_next
        return carry

    lax.fori_loop(0, qi, body, 0)
    kd = pl.multiple_of(qi * t, t)
    for part in range(parts):
        block(part, kd, (part + 1) * hr, True)
    l = jnp.sum(l_ref[...], axis=-1, keepdims=True)
    o_ref[...] = (acc_ref[...] / l).astype(o_ref.dtype)


def _fox(c, proj, crow, ccol, qcol0):
    n = c.B * c.S
    t = c.tq
    nq = c.S // t
    h = c.FH
    return pl.pallas_call(
        functools.partial(_fox_kernel, t=t, parts=c.fox_parts),
        grid=(c.B, h, nq),
        in_specs=[
            pl.BlockSpec((t, LANES), lambda b, hh, qi: (b * nq + qi, qcol0 + hh)),
            pl.BlockSpec((c.S, LANES), lambda b, hh, qi: (b, qcol0 + h + hh)),
            pl.BlockSpec((c.S, LANES), lambda b, hh, qi: (b, qcol0 + 2 * h + hh)),
            pl.BlockSpec((1, 1, c.S), lambda b, hh, qi: (b * h + hh, 0, 0)),
            pl.BlockSpec((1, t, 1), lambda b, hh, qi: (b * h + hh, qi, 0)),
        ],
        out_specs=pl.BlockSpec((t, LANES), lambda b, hh, qi: (b * nq + qi, hh)),
        out_shape=jax.ShapeDtypeStruct((n, h * LANES), BF16),
        scratch_shapes=[pltpu.VMEM((t, LANES), F32)] * 4 + [pltpu.VMEM((t, t), F32)],
        compiler_params=_cparams(("parallel", "parallel", "arbitrary")),
        name="fox",
    )(proj, proj, proj, crow, ccol)


def _mix_kernel(x_ref, a_ref, u_ref, v_ref, ga_ref, gb_ref, lng_ref, lnb_ref, ws_ref, bs_ref,
                wa_ref, wb_ref, wo_ref, o_ref, sg_ref, *, sgc, groups):
    tm = x_ref.shape[0]
    v = v_ref[...].astype(F32)
    mu = jnp.mean(v, axis=-1, keepdims=True)
    vc = v - mu
    var = jnp.mean(vc * vc, axis=-1, keepdims=True)
    vn = (vc * lax.rsqrt(var + LN_EPS) * lng_ref[...] + lnb_ref[...]).astype(BF16)
    row = lax.broadcasted_iota(I32, (sgc, sgc), 0)
    col = lax.broadcasted_iota(I32, (sgc, sgc), 1)
    for g in range(groups):
        w = jnp.where(col <= row, ws_ref[g], 0.0).astype(BF16)
        bias = bs_ref[g]
        for ci in range(tm // sgc):
            rs = slice(ci * sgc, (ci + 1) * sgc)
            cs = slice(g * LANES, (g + 1) * LANES)
            mixed = jnp.dot(w, vn[rs, cs], preferred_element_type=F32) + bias
            sg_ref[rs, cs] = (u_ref[rs, cs].astype(F32) * mixed).astype(BF16)
    ya = jnp.dot(a_ref[...], wa_ref[...], preferred_element_type=F32)
    yb = jnp.dot(sg_ref[...], wb_ref[...], preferred_element_type=F32)
    merged = (ga_ref[...].astype(F32) * ya + gb_ref[...].astype(F32) * yb).astype(BF16)
    o_ref[...] = x_ref[...] + jnp.dot(merged, wo_ref[...], preferred_element_type=F32)


def _const_spec(shape):
    nd = len(shape)
    return pl.BlockSpec(shape, lambda i: (0,) * nd, pipeline_mode=pl.Buffered(1))


def _mix(c, x2d, attn, proj, ln_g, ln_b, w_s, b_s_full, w_a, w_b, w_o):
    n, d = x2d.shape
    tm = c.tm_mix
    fw = c.FH * LANES
    sw = c.SG * LANES
    ucol = 2 * d // sw
    return pl.pallas_call(
        functools.partial(_mix_kernel, sgc=c.SGC, groups=c.SG),
        grid=(n // tm,),
        in_specs=[
            pl.BlockSpec((tm, d), lambda i: (i, 0)),
            pl.BlockSpec((tm, fw), lambda i: (i, 0)),
            pl.BlockSpec((tm, sw), lambda i: (i, ucol)),
            pl.BlockSpec((tm, sw), lambda i: (i, ucol + 1)),
            pl.BlockSpec((tm, d), lambda i: (i, 0)),
            pl.BlockSpec((tm, d), lambda i: (i, 1)),
            _const_spec((1, sw)),
            _const_spec((1, sw)),
            _const_spec((c.SG, c.SGC, c.SGC)),
            _const_spec((c.SG, c.SGC, LANES)),
            _const_spec((fw, d)),
            _const_spec((sw, d)),
            _const_spec((d, d)),
        ],
        out_specs=pl.BlockSpec((tm, d), lambda i: (i, 0)),
        out_shape=jax.ShapeDtypeStruct((n, d), F32),
        scratch_shapes=[pltpu.VMEM((tm, sw), BF16)],
        compiler_params=_cparams(("parallel",)),
        name="mix",
    )(x2d, attn, proj, proj, proj, proj, ln_g, ln_b, w_s, b_s_full, w_a, w_b, w_o)


def _mem_kv_kernel(m_ref, g_ref, w_ref, o_ref):
    hm = _rms(m_ref[...], g_ref[...]).astype(BF16)
    o_ref[...] = jnp.dot(hm, w_ref[...], preferred_element_type=F32).astype(o_ref.dtype)


def _mem_kv(c, mem2d, g, w_xkv):
    n, d = mem2d.shape
    nc = w_xkv.shape[1]
    tm = c.MEM
    return pl.pallas_call(
        _mem_kv_kernel,
        grid=(n // tm,),
        in_specs=[pl.BlockSpec((tm, d), lambda i: (i, 0)), _const_spec((1, d)), _const_spec((d, nc))],
        out_specs=pl.BlockSpec((tm, nc), lambda i: (i, 0)),
        out_shape=jax.ShapeDtypeStruct((n, nc), BF16),
        compiler_params=_cparams(("parallel",)),
        name="mem_kv",
    )(mem2d, g, w_xkv)


def _xattn_kernel(x1_ref, gx_ref, wq_ref, kv_ref, wo_ref, gf_ref, wr_ref, br_ref,
                  x2_ref, hfp_ref, idx_ref, gate_ref, cnt_ref, carry_ref,
                  *, heads, n_exp, topk, qscale):
    i = pl.program_id(0)
    tm, d = x1_ref.shape
    xw = heads * LANES
    x1 = x1_ref[...]
    hx = _rms(x1, gx_ref[...]).astype(BF16)
    q = (jnp.dot(hx, wq_ref[...], preferred_element_type=F32) * qscale).astype(BF16)
    outs = []
    for h in range(heads):
        k = kv_ref[:, h * LANES:(h + 1) * LANES]
        v = kv_ref[:, xw + h * LANES:xw + (h + 1) * LANES]
        s = lax.dot_general(q[:, h * LANES:(h + 1) * LANES], k, (((1,), (1,)), ((), ())),
                            preferred_element_type=F32)
        s = s - jnp.max(s, axis=-1, keepdims=True)
        p = jnp.exp(s)
        p = p / jnp.sum(p, axis=-1, keepdims=True)
        outs.append(jnp.dot(p.astype(BF16), v, preferred_element_type=F32).astype(BF16))
    o = jnp.concatenate(outs, axis=-1)
    x2 = x1 + jnp.dot(o, wo_ref[...], preferred_element_type=F32)
    x2_ref[...] = x2

    hf = _rms(x2, gf_ref[...])
    hfp_ref[...] = _pack_bf16_pair(hf[:, :d // 2], hf[:, d // 2:])

    h_hi = hf.astype(BF16)
    h_lo = (hf - h_hi.astype(F32)).astype(BF16)
    l_hi = jnp.dot(h_hi, wr_ref[...], preferred_element_type=F32)
    l_lo = jnp.dot(h_lo, wr_ref[:, :LANES], preferred_element_type=F32)
    logits = l_hi[:, :LANES] + l_hi[:, LANES:] + l_lo + br_ref[...]
    lane = lax.broadcasted_iota(I32, (tm, LANES), 1)
    lg = jnp.where(lane < n_exp, logits, -jnp.inf)
    vals, idxs = [], []
    for _ in range(topk):
        m = jnp.max(lg, axis=-1, keepdims=True)
        ix = jnp.min(jnp.where(lg == m, lane, LANES), axis=-1, keepdims=True)
        vals.append(m)
        idxs.append(ix)
        lg = jnp.where(lane == ix, -jnp.inf, lg)
    es = [jnp.exp(vv - vals[0]) for vv in vals]
    denom = es[0]
    for e in es[1:]:
        denom = denom + e

    @pl.when(i == 0)
    def _():
        carry_ref[...] = jnp.zeros(carry_ref.shape, F32)

    onehot = jnp.zeros((tm, LANES), F32)
    for ix in idxs:
        onehot = onehot + (lane == ix).astype(F32)
    idx_out = jnp.zeros((tm, LANES), I32)
    gate_out = jnp.zeros((tm, LANES), F32)
    for kk in range(topk):
        idx_out = jnp.where(lane == kk, idxs[kk], idx_out)
        gate_out = jnp.where(lane == kk, es[kk] / denom, gate_out)
    idx_ref[...] = idx_out
    gate_ref[...] = gate_out
    carry_ref[...] = carry_ref[...] + jnp.sum(onehot, axis=0, keepdims=True)
    cnt_ref[...] = carry_ref[...]


def _xattn(c, x1, gx, w_xq, kv, w_xo, gf, w_r, b_r, qscale):
    n, d = x1.shape
    tm = c.tm_x
    xw = c.XH * LANES
    per_b = c.S // tm
    kern = functools.partial(_xattn_kernel, heads=c.XH, n_exp=c.E, topk=c.K, qscale=qscale)
    return pl.pallas_call(
        kern,
        grid=(n // tm,),
        in_specs=[
            pl.BlockSpec((tm, d), lambda i: (i, 0)),
            _const_spec((1, d)),
            _const_spec((d, xw)),
            pl.BlockSpec((c.MEM, 2 * xw), lambda i: (i // per_b, 0)),
            _const_spec((xw, d)),
            _const_spec((1, d)),
            _const_spec((d, 2 * LANES)),
            _const_spec((1, LANES)),
        ],
        out_specs=[
            pl.BlockSpec((tm, d), lambda i: (i, 0)),
            pl.BlockSpec((tm, d // 2), lambda i: (i, 0)),
            pl.BlockSpec((tm, LANES), lambda i: (i, 0)),
            pl.BlockSpec((tm, LANES), lambda i: (i, 0)),
            pl.BlockSpec((1, LANES), lambda i: (0, 0)),
        ],
        out_shape=[
            jax.ShapeDtypeStruct((n, d), F32),
            jax.ShapeDtypeStruct((n, d // 2), U32),
            jax.ShapeDtypeStruct((n, LANES), I32),
            jax.ShapeDtypeStruct((n, LANES), F32),
            jax.ShapeDtypeStruct((1, LANES), F32),
        ],
        scratch_shapes=[pltpu.VMEM((1, LANES), F32)],
        compiler_params=_cparams(("arbitrary",)),
        name="xattn_router",
    )(x1, gx, w_xq, kv, w_xo, gf, w_r, b_r)


def _moe_kernel(te_ref, ns_ref, cpi_ref, pos_ref, rows_ref, ord_ref, hfp_ref, wgu_ref, wd_ref, bgu_ref, bd_ref,
                ykt_ref, wgub, wdb, stage, act, xin, ybuf, ids, wsem, xsem, ysem, isem,
                *, ts, tm, nck, fc, n_tok, n_asg):
    i = pl.program_id(0)
    n = pl.num_programs(0)
    nsub = ns_ref[i]
    e = te_ref[i]
    cpi = cpi_ref[i]
    rows = rows_ref[i]
    d, f2 = wgub.shape
    f = f2 // 2
    dh = d // 2
    ckr = d // nck
    ckd = f // nck
    nst = stage.shape[0]
    idw = ids.shape[0] // 2
    nq = idw // ID_CHUNK
    id_base = (i % 2) * idw + (pos_ref[i] & (ID_CHUNK - 1))

    def ids_copy(item, q, slot):
        c0 = lax.shift_right_logical(pos_ref[item], ID_CHUNK.bit_length() - 1)
        src = ord_ref.at[pl.ds(pl.multiple_of((c0 + q) * ID_CHUNK, ID_CHUNK), ID_CHUNK)]
        dst = ids.at[pl.ds(pl.multiple_of(slot * idw + q * ID_CHUNK, ID_CHUNK), ID_CHUNK)]
        return pltpu.make_async_copy(src, dst, isem.at[slot])

    def token_of(a):
        return a & (n_tok - 1) if n_tok & (n_tok - 1) == 0 else lax.rem(a, n_tok)

    def gather(s, slot, r0=0, r1=ts, base=None):
        base = id_base if base is None else base
        for r in range(r0, r1):
            tok = token_of(ids[base + s * ts + r])
            pltpu.make_async_copy(hfp_ref.at[pl.ds(tok, 1)], xin.at[slot, pl.ds(r, 1)], xsem.at[slot]).start()

    def gather_wait(slot):
        for r in range(ts):
            pltpu.make_async_copy(hfp_ref.at[pl.ds(0, 1)], xin.at[slot, pl.ds(0, 1)], xsem.at[slot]).wait()

    def scatter(s, slot, r0=0, r1=ts, base=None, nrows=None):
        base = id_base if base is None else base
        nrows = rows if nrows is None else nrows
        for r in range(r0, r1):
            g = s * ts + r
            dst = jnp.where(g < nrows, ids[base + g], n_asg + slot * ts + r)
            pltpu.make_async_copy(ybuf.at[slot, pl.ds(r, 1)], ykt_ref.at[pl.ds(dst, 1)], ysem.at[slot]).start()

    def scatter_wait(slot):
        for r in range(ts):
            pltpu.make_async_copy(ybuf.at[slot, pl.ds(0, 1)], ykt_ref.at[pl.ds(0, 1)], ysem.at[slot]).wait()

    def gu_copy(ee, c, slot):
        return pltpu.make_async_copy(wgu_ref.at[ee, pl.ds(pl.multiple_of(c * ckr, ckr), ckr), :],
                                     stage.at[slot], wsem.at[slot])

    def d_copy(ee, c, slot):
        return pltpu.make_async_copy(wd_ref.at[ee, pl.ds(pl.multiple_of(c * ckd, ckd), ckd), :],
                                     stage.at[slot, pl.ds(0, ckd), pl.ds(0, d)], wsem.at[slot])

    def gu_step(ee, c):
        slot = c % nst
        gu_copy(ee, c, slot).wait()
        wgub[pl.ds(pl.multiple_of(c * ckr, ckr), ckr), :] = stage[slot].astype(BF16)

        @pl.when(c + nst < nck)
        def _():
            gu_copy(ee, c + nst, slot).start(priority=WEIGHT_DMA_PRIORITY)

    def d_step(ee, c):
        slot = c % nst
        d_copy(ee, c, slot).wait()
        wdb[pl.ds(pl.multiple_of(c * ckd, ckd), ckd), :] = stage[slot, :ckd, :d].astype(BF16)

        @pl.when(c + nst < nck)
        def _():
            d_copy(ee, c + nst, slot).start(priority=WEIGHT_DMA_PRIORITY)

    def prime(copy_fn, ee):
        for q in range(min(nst, nck)):
            copy_fn(ee, q, q).start(priority=WEIGHT_DMA_PRIORITY)

    def steps_after(s, step_fn, ee):
        def one(jj, carry):
            c = s * cpi + jj

            @pl.when(c < nck)
            def _():
                step_fn(ee, c)
            return carry
        lax.fori_loop(0, cpi, one, 0)

    @pl.when(nsub > 0)
    def _():
        nxt = jnp.minimum(i + 1, n - 1)
        has_next = (i + 1 < n) & (ns_ref[nxt] > 0)

        @pl.when(i == 0)
        def _():
            for q in range(nq):
                ids_copy(0, q, 0).start()
            prime(gu_copy, e)

            def first(c, carry):
                gu_step(e, c)
                return carry
            lax.fori_loop(0, nck, first, 0)
            for q in range(nq):
                ids_copy(0, q, 0).wait()
            gather(0, 0)
            ybuf[...] = jnp.zeros(ybuf.shape, ybuf.dtype)

        prv = jnp.maximum(i - 1, 0)
        nsub_p = jnp.where(i > 0, ns_ref[prv], 1)
        rows_p = jnp.where(i > 0, rows_ref[prv], 0)
        nslot = 1 - i % 2
        base_p = jnp.where(i > 0, nslot * idw + (pos_ref[prv] & (ID_CHUNK - 1)), id_base)
        lslot_p = (nsub_p - 1) % 2

        prime(d_copy, e)
        bgu = bgu_ref[e]

        def gate_up(s, scatter_prev):
            slot = s % 2
            gather_wait(slot)
            p = xin[slot]
            xb = jnp.concatenate([_unpack_lo(p).astype(BF16), _unpack_hi(p).astype(BF16)], axis=-1)
            nc = f // fc
            for cc in range(nc):
                gather(s + 1, 1 - slot, cc * ts // nc, (cc + 1) * ts // nc)
                if scatter_prev:
                    scatter(nsub_p - 1, lslot_p, cc * ts // nc, (cc + 1) * ts // nc, base=base_p, nrows=rows_p)
                gate = jnp.dot(xb, wgub[:, cc * fc:(cc + 1) * fc], preferred_element_type=F32)
                gate = gate + bgu[:, cc * fc:(cc + 1) * fc]
                up = jnp.dot(xb, wgub[:, f + cc * fc:f + (cc + 1) * fc], preferred_element_type=F32)
                up = up + bgu[:, f + cc * fc:f + (cc + 1) * fc]
                gate = jnp.minimum(gate, SWIGLU_LIMIT)
                up = jnp.clip(up, -SWIGLU_LIMIT, SWIGLU_LIMIT)
                a = (up + 1.0) * gate * jax.nn.sigmoid(SWIGLU_ALPHA * gate)
                act[s, :, cc * fc:(cc + 1) * fc] = a.astype(BF16)
            steps_after(s, d_step, e)

        gate_up(0, True)

        for q in range(nq):
            ids_copy(nxt, q, nslot).start()

        def phase_a(s, carry):
            gate_up(s, False)
            return carry

        lax.fori_loop(1, nsub, phase_a, 0)
        gather_wait(nsub % 2)

        e_next = te_ref[nxt]
        load_next = has_next & (e_next != e)

        @pl.when(load_next)
        def _():
            prime(gu_copy, e_next)

        bd = bd_ref[e]

        def down(s, slot, scatter_prev):
            a = act[s]
            if scatter_prev:
                scatter(s - 1, 1 - slot)
            fcb = min(fc, dh // 2)
            for cc in range(dh // fcb):
                lo = slice(cc * fcb, (cc + 1) * fcb)
                hi = slice(dh + cc * fcb, dh + (cc + 1) * fcb)
                y_lo = jnp.dot(a, wdb[:, lo], preferred_element_type=F32) + bd[:, lo]
                y_hi = jnp.dot(a, wdb[:, hi], preferred_element_type=F32) + bd[:, hi]
                ybuf[slot, :, lo] = _pack_bf16_pair(y_lo, y_hi)

        def next_weights(s):
            @pl.when(load_next)
            def _():
                steps_after(s, gu_step, e_next)

        scatter_wait(lslot_p)

        @pl.when(nsub_p >= 2)
        def _():
            scatter_wait(1 - lslot_p)

        for q in range(nq):
            ids_copy(nxt, q, nslot).wait()
        gather(0, 0, base=nslot * idw + (pos_ref[nxt] & (ID_CHUNK - 1)))
        down(0, 0, False)
        next_weights(0)

        def phase_b(s, carry):
            slot = s % 2

            @pl.when(s >= 2)
            def _():
                scatter_wait(slot)

            down(s, slot, True)
            next_weights(s)
            return carry

        lax.fori_loop(1, nsub, phase_b, 0)

        @pl.when(jnp.logical_not(has_next))
        def _():
            scatter(nsub - 1, (nsub - 1) % 2)
            scatter_wait((nsub - 1) % 2)

            @pl.when(nsub >= 2)
            def _():
                scatter_wait(nsub % 2)

            gather_wait(0)


def _id_window(c):
    return ((ID_CHUNK - 1 + c.tm_e + c.ts_e) // ID_CHUNK + 1) * ID_CHUNK


def _moe(c, sched, order_pad, hfp, w_gu, b_gu3, w_d, b_d3):
    n_tok, dh = hfp.shape
    d = 2 * dh
    f = c.DFF
    tm, ts, nck = c.tm_e, c.ts_e, c.nck_e
    n_asg = n_tok * c.K
    grid_spec = pltpu.PrefetchScalarGridSpec(
        num_scalar_prefetch=5,
        grid=(_n_tiles(c),),
        in_specs=[
            pl.BlockSpec(memory_space=pl.ANY),
            pl.BlockSpec(memory_space=pl.ANY),
            pl.BlockSpec(memory_space=pl.ANY),
            pl.BlockSpec(memory_space=pl.ANY),
            pl.BlockSpec((c.E, 1, 2 * f), lambda i, *_: (0, 0, 0), pipeline_mode=pl.Buffered(1)),
            pl.BlockSpec((c.E, 1, d), lambda i, *_: (0, 0, 0), pipeline_mode=pl.Buffered(1)),
        ],
        out_specs=pl.BlockSpec(memory_space=pl.ANY),
        scratch_shapes=[
            pltpu.VMEM((d, 2 * f), BF16),
            pltpu.VMEM((f, d), BF16),
            pltpu.VMEM((c.nst_e, d // nck, 2 * f), F32),
            pltpu.VMEM((tm // ts, ts, f), BF16),
            pltpu.VMEM((2, ts, dh), U32),
            pltpu.VMEM((2, ts, dh), U32),
            pltpu.SMEM((2 * _id_window(c),), I32),
            pltpu.SemaphoreType.DMA((c.nst_e,)),
            pltpu.SemaphoreType.DMA((2,)),
            pltpu.SemaphoreType.DMA((2,)),
            pltpu.SemaphoreType.DMA((2,)),
        ],
    )
    kern = functools.partial(_moe_kernel, ts=ts, tm=tm, nck=nck, fc=c.fc_e, n_tok=n_tok, n_asg=n_asg)
    return pl.pallas_call(
        kern,
        grid_spec=grid_spec,
        out_shape=jax.ShapeDtypeStruct((n_asg + 2 * ts, dh), U32),
        compiler_params=_cparams(("arbitrary",)),
        name="experts",
    )(*sched, order_pad, hfp, w_gu, w_d, b_gu3, b_d3)


def _combine_kernel(x2_ref, gate_ref, g_ref, *refs, topk):
    y_refs, o_ref = refs[:topk], refs[topk]
    d = x2_ref.shape[1]
    dh = d // 2
    lo = x2_ref[:, :dh]
    hi = x2_ref[:, dh:]
    gates = gate_ref[...]
    for kk in range(topk):
        wk = gates[:, kk:kk + 1]
        p = y_refs[kk][...]
        lo = lo + wk * _unpack_lo(p)
        hi = hi + wk * _unpack_hi(p)
    ms = (jnp.sum(lo * lo, axis=-1, keepdims=True) + jnp.sum(hi * hi, axis=-1, keepdims=True)) / d
    inv = lax.rsqrt(ms + RMS_EPS)
    o_ref[:, :dh] = lo * inv * g_ref[:, :dh]
    o_ref[:, dh:] = hi * inv * g_ref[:, dh:]


def _combine(c, x2, gates, g, ykt):
    n, d = x2.shape
    tm = c.tm_c
    nb = n // tm
    y_specs = [pl.BlockSpec((tm, d // 2), functools.partial(lambda i, kk: (kk * nb + i, 0), kk=kk))
               for kk in range(c.K)]
    return pl.pallas_call(
        functools.partial(_combine_kernel, topk=c.K),
        grid=(nb,),
        in_specs=[
            pl.BlockSpec((tm, d), lambda i: (i, 0)),
            pl.BlockSpec((tm, LANES), lambda i: (i, 0)),
            pl.BlockSpec((1, d), lambda i: (0, 0)),
        ] + y_specs,
        out_specs=pl.BlockSpec((tm, d), lambda i: (i, 0)),
        out_shape=jax.ShapeDtypeStruct((n, d), F32),
        compiler_params=_cparams(("parallel",)),
        name="combine",
    )(x2, gates, g, *([ykt] * c.K))


def _plan(c, idx, counts):
    tm, ts = c.tm_e, c.ts_e
    nk = idx.shape[0] * c.K
    order = jnp.argsort(idx.T.reshape(-1)).astype(I32)
    pad = _id_window(c) + (-nk) % ID_CHUNK
    order_pad = jnp.concatenate([order, jnp.zeros((pad,), I32)])
    counts = counts.astype(I32)
    start = jnp.cumsum(counts) - counts
    tiles_per_e = (counts + tm - 1) // tm
    tile_end = jnp.cumsum(tiles_per_e)
    tile_start = tile_end - tiles_per_e
    t = jnp.arange(_n_tiles(c), dtype=I32)
    n_used = tile_end[-1]
    tc = jnp.minimum(t, n_used - 1)
    te = jnp.minimum(jnp.searchsorted(tile_end, tc, side="right"), c.E - 1).astype(I32)
    j = tc - tile_start[te]
    used = t < n_used
    rows = jnp.where(used, jnp.clip(counts[te] - j * tm, 0, tm), 0).astype(I32)
    nsub = (rows + ts - 1) // ts
    pos = jnp.where(used, start[te] + j * tm, 0).astype(I32)
    cpi = (c.nck_e + jnp.maximum(nsub, 1) - 1) // jnp.maximum(nsub, 1)
    return order_pad, (te, nsub.astype(I32), cpi.astype(I32), pos, rows)


def _n_tiles(c):
    return -(-(c.B * c.S * c.K) // c.tm_e) + c.E


def _forward(c, x, mem, norm_mix_g, w_in, b_forget, sg_ln_g, sg_ln_b, w_spatial, b_spatial,
             w_branch_a, w_branch_b, w_out, norm_x_g, norm_mem_g, w_xq, w_xkv, w_xo,
             norm_ffn_g, w_router, b_router, w_gate_up, b_gate_up, w_down, b_down, norm_final_g):
    B, S, D = x.shape
    n = B * S
    fw = c.FH * LANES
    sw = c.SG * LANES
    x2d = x.reshape(n, D)

    o_f = 3 * fw
    o_z = o_f + c.FH
    o_g = o_z + 2 * sw
    w_main = jnp.concatenate([w_in[:, o_g:], w_in[:, o_z:o_g], w_in[:, :o_f]], axis=1).astype(BF16)
    w_f = jnp.pad(w_in[:, o_f:o_z], ((0, 0), (0, LANES - c.FH))).astype(BF16)
    b_f = jnp.pad(b_forget.astype(F32), (0, LANES - c.FH)).reshape(1, LANES)
    proj, logf = _in_proj(c, x2d, norm_mix_g.reshape(1, D), w_main, w_f, b_f, LOG2E * LANES ** -0.5)

    logf_bhs = logf[:, :c.FH].reshape(B, S, c.FH).transpose(0, 2, 1)
    csum = _cumsum(c, logf_bhs)
    crow = csum.reshape(B * c.FH, 1, S)
    ccol = csum.reshape(B * c.FH, S, 1)
    qcol0 = (2 * D + 2 * sw) // LANES
    attn = _fox(c, proj, crow, ccol, qcol0)

    b_s_full = jnp.broadcast_to(b_spatial.astype(F32)[:, :, None], (c.SG, c.SGC, LANES))
    x1 = _mix(c, x2d, attn, proj, sg_ln_g.reshape(1, sw), sg_ln_b.reshape(1, sw), w_spatial, b_s_full,
              w_branch_a.astype(BF16), w_branch_b.astype(BF16), w_out.astype(BF16))

    kv = _mem_kv(c, mem.reshape(B * c.MEM, D), norm_mem_g.reshape(1, D), w_xkv.astype(BF16))
    w_r32 = jnp.pad(w_router.astype(F32), ((0, 0), (0, LANES - c.E)))
    w_r_hi = w_r32.astype(BF16)
    w_r = jnp.concatenate([w_r_hi, (w_r32 - w_r_hi.astype(F32)).astype(BF16)], axis=1)
    b_r = jnp.pad(b_router.astype(F32), (0, LANES - c.E)).reshape(1, LANES)
    x2, hfp, idx, gates, counts = _xattn(
        c, x1, norm_x_g.reshape(1, D), w_xq.astype(BF16), kv, w_xo.astype(BF16),
        norm_ffn_g.reshape(1, D), w_r, b_r, LANES ** -0.5)

    order_pad, sched = _plan(c, idx[:, :c.K], counts[0, :c.E])
    ykt = _moe(c, sched, order_pad, hfp, w_gate_up, b_gate_up.reshape(c.E, 1, 2 * c.DFF),
               w_down, b_down.reshape(c.E, 1, D))
    out = _combine(c, x2, gates, norm_final_g.reshape(1, D), ykt)
    return out.reshape(B, S, D)


_CFG = Cfg(B=4, S=4096, D=2048, MEM=256, FH=8, SG=8, SGC=128, XH=4, E=32, K=4, DFF=2048,
           tm_in=1024, tn_in=1024, tq=1024, tm_mix=256, tm_x=512, tm_e=2560, ts_e=256, fc_e=512, nck_e=16, nst_e=4,
           tm_c=256, tc_cs=512, fox_parts=2)


@jax.jit
def kernel(x, mem, norm_mix_g, w_in, b_forget, sg_ln_g, sg_ln_b, w_spatial, b_spatial, w_branch_a, w_branch_b,
           w_out, norm_x_g, norm_mem_g, w_xq, w_xkv, w_xo, norm_ffn_g, w_router, b_router, w_gate_up,
           b_gate_up, w_down, b_down, norm_final_g):
    return _forward(_CFG, x, mem, norm_mix_g, w_in, b_forget, sg_ln_g, sg_ln_b, w_spatial, b_spatial,
                    w_branch_a, w_branch_b, w_out, norm_x_g, norm_mem_g, w_xq, w_xkv, w_xo,
                    norm_ffn_g, w_router, b_router, w_gate_up, b_gate_up, w_down, b_down, norm_final_g)
```

```python
import functools
from typing import NamedTuple

import jax
import jax.numpy as jnp
from jax import lax
from jax.experimental import pallas as pl
from jax.experimental.pallas import tpu as pltpu

F32 = jnp.float32
BF16 = jnp.bfloat16
U32 = jnp.uint32
I32 = jnp.int32

LANES = 128
VMEM_LIMIT = 56 * 1024 * 1024

RMS_EPS = 1e-6
LN_EPS = 1e-5
SWIGLU_LIMIT = 7.0
SWIGLU_ALPHA = 1.702
GELU_C = 0.7978845608028654
LOG2E = 1.4426950408889634
ID_CHUNK = 1024
WEIGHT_DMA_PRIORITY = 1


class Cfg(NamedTuple):
    B: int
    S: int
    D: int
    MEM: int
    FH: int
    SG: int
    SGC: int
    XH: int
    E: int
    K: int
    DFF: int
    tm_in: int
    tn_in: int
    tq: int
    tm_mix: int
    tm_x: int
    tm_e: int
    ts_e: int
    fc_e: int
    nck_e: int
    nst_e: int
    tm_c: int
    tc_cs: int
    fox_parts: int


def _cparams(sem):
    return pltpu.CompilerParams(dimension_semantics=sem, vmem_limit_bytes=VMEM_LIMIT)


def _rms(x, g):
    ms = jnp.mean(x * x, axis=-1, keepdims=True)
    return x * lax.rsqrt(ms + RMS_EPS) * g


def _pack_bf16_pair(a, b):
    def rne(v):
        bits = lax.bitcast_convert_type(v, U32)
        return bits + jnp.uint32(0x7FFF) + ((bits >> 16) & jnp.uint32(1))
    return (rne(a) >> 16) | (rne(b) & jnp.uint32(0xFFFF0000))


def _unpack_lo(p):
    return lax.bitcast_convert_type(p << 16, F32)


def _unpack_hi(p):
    return lax.bitcast_convert_type(p & jnp.uint32(0xFFFF0000), F32)


def _in_proj_kernel(x_ref, g_ref, w_ref, wf_ref, bf_ref, o_ref, f_ref, h_ref, *, nj_gate, nj_z, nj_q, qscale):
    j = pl.program_id(1)

    @pl.when(j == 0)
    def _():
        hb = _rms(x_ref[...], g_ref[...]).astype(BF16)
        h_ref[...] = hb
        f = jnp.dot(hb, wf_ref[...], preferred_element_type=F32) + bf_ref[...]
        f_ref[...] = jnp.minimum(f, 0.0) - jnp.log1p(jnp.exp(-jnp.abs(f)))

    acc = jnp.dot(h_ref[...], w_ref[...], preferred_element_type=F32)

    is_gate = j < nj_gate
    is_z = (j >= nj_gate) & (j < nj_gate + nj_z)
    is_q = (j >= nj_gate + nj_z) & (j < nj_gate + nj_z + nj_q)
    a1 = jnp.where(is_gate, 0.5, jnp.where(is_z, GELU_C, 0.0)).astype(F32)
    a3 = jnp.where(is_z, GELU_C * 0.044715, 0.0).astype(F32)
    b0 = jnp.where(is_gate, 0.5, 0.0).astype(F32)
    b1 = jnp.where(is_gate, 0.0, jnp.where(is_z, 0.5, jnp.where(is_q, qscale, 1.0))).astype(F32)
    th = jnp.tanh(acc * (a1 + a3 * (acc * acc)))
    o_ref[...] = ((b0 + b1 * acc) * (1.0 + th)).astype(o_ref.dtype)


def _in_proj(c, x2d, g, w_main, w_f, b_f, qscale):
    n, d = x2d.shape
    nc = w_main.shape[1]
    tm, tn = c.tm_in, c.tn_in
    fw = c.FH * LANES
    kern = functools.partial(_in_proj_kernel, nj_gate=2 * d // tn, nj_z=2 * c.SG * LANES // tn,
                             nj_q=fw // tn, qscale=qscale)
    return pl.pallas_call(
        kern,
        grid=(n // tm, nc // tn),
        in_specs=[
            pl.BlockSpec((tm, d), lambda i, j: (i, 0)),
            pl.BlockSpec((1, d), lambda i, j: (0, 0)),
            pl.BlockSpec((d, tn), lambda i, j: (0, j)),
            pl.BlockSpec((d, LANES), lambda i, j: (0, 0)),
            pl.BlockSpec((1, LANES), lambda i, j: (0, 0)),
        ],
        out_specs=[
            pl.BlockSpec((tm, tn), lambda i, j: (i, j)),
            pl.BlockSpec((tm, LANES), lambda i, j: (i, 0)),
        ],
        out_shape=[jax.ShapeDtypeStruct((n, nc), BF16), jax.ShapeDtypeStruct((n, LANES), F32)],
        scratch_shapes=[pltpu.VMEM((tm, d), BF16)],
        compiler_params=_cparams(("parallel", "arbitrary")),
        name="in_proj",
    )(x2d, g, w_main, w_f, b_f)


def _cumsum_kernel(f_ref, o_ref, *, tc):
    rows, s = f_ref.shape[1], f_ref.shape[2]
    r = lax.broadcasted_iota(I32, (tc, tc), 0)
    col = lax.broadcasted_iota(I32, (tc, tc), 1)
    upper = (r <= col).astype(F32)
    carry = jnp.zeros((rows, 1), F32)
    for i in range(s // tc):
        blk = f_ref[0, :, i * tc:(i + 1) * tc]
        cs = jnp.dot(blk, upper, preferred_element_type=F32, precision=lax.Precision.HIGHEST) + carry
        o_ref[0, :, i * tc:(i + 1) * tc] = cs * LOG2E
        carry = cs[:, tc - 1:tc]


def _cumsum(c, logf_bhs):
    b, h, s = logf_bhs.shape
    return pl.pallas_call(
        functools.partial(_cumsum_kernel, tc=c.tc_cs),
        grid=(b,),
        in_specs=[pl.BlockSpec((1, h, s), lambda i: (i, 0, 0))],
        out_specs=pl.BlockSpec((1, h, s), lambda i: (i, 0, 0)),
        out_shape=jax.ShapeDtypeStruct((b, h, s), F32),
        compiler_params=_cparams(("parallel",)),
        name="cumsum",
    )(logf_bhs)


def _fox_kernel(q_ref, k_ref, v_ref, crow_ref, ccol_ref, o_ref, m_ref, l_ref, acc_ref, cq_ref, s_ref,
                *, t, parts):
    qi = pl.program_id(2)
    hr = t // parts
    m_ref[...] = jnp.full(m_ref.shape, -jnp.inf, F32)
    l_ref[...] = jnp.zeros(l_ref.shape, F32)
    acc_ref[...] = jnp.zeros(acc_ref.shape, F32)
    cq_ref[...] = jnp.broadcast_to(ccol_ref[0], cq_ref.shape)

    def scores(ks):
        k = k_ref[pl.ds(ks, t), :]
        return lax.dot_general(q_ref[...], k, (((1,), (1,)), ((), ())), preferred_element_type=F32)

    def block(part, ks, width, masked):
        rows = slice(part * hr, (part + 1) * hr)
        v = v_ref[pl.ds(ks, width), :]
        s = s_ref[rows, :width]
        cq = cq_ref[rows, :]
        crow = crow_ref[0, :, pl.ds(ks, width)]
        nj = width // LANES
        sj = [s[:, j * LANES:(j + 1) * LANES] + cq - crow[:, j * LANES:(j + 1) * LANES] for j in range(nj)]
        if masked:
            row = lax.broadcasted_iota(I32, (hr, LANES), 0) + part * hr
            col = lax.broadcasted_iota(I32, (hr, LANES), 1)
            sj = [jnp.where(col + j * LANES <= row, sj[j], -jnp.inf) for j in range(nj)]
        mx = sj[0]
        for j in range(1, nj):
            mx = jnp.maximum(mx, sj[j])
        m_prev = m_ref[rows, :]
        m_next = jnp.maximum(m_prev, jnp.max(mx, axis=-1, keepdims=True))
        alpha = jnp.exp2(m_prev - m_next)
        pj = [jnp.exp2(sj[j] - m_next) for j in range(nj)]
        psum = pj[0]
        for j in range(1, nj):
            psum = psum + pj[j]
        p = jnp.concatenate([x.astype(BF16) for x in pj], axis=-1)
        l_ref[rows, :] = alpha * l_ref[rows, :] + psum
        acc_ref[rows, :] = alpha * acc_ref[rows, :] + jnp.dot(p, v, preferred_element_type=F32)
        m_ref[rows, :] = m_next

    s_ref[...] = scores(0)

    def body(kc, carry):
        ks = pl.multiple_of(kc * t, t)
        s_next = scores(pl.multiple_of(ks + t, t))
        for part in range(parts):
            block(part, ks, t, False)
        s_ref[...] = s_next
        return carry

    lax.fori_loop(0, qi, body, 0)
    kd = pl.multiple_of(qi * t, t)
    for part in range(parts):
        block(part, kd, (part + 1) * hr, True)
    l = jnp.sum(l_ref[...], axis=-1, keepdims=True)
    o_ref[...] = (acc_ref[...] / l).astype(o_ref.dtype)


def _fox(c, proj, crow, ccol, qcol0):
    n = c.B * c.S
    t = c.tq
    nq = c.S // t
    h = c.FH
    return pl.pallas_call(
        functools.partial(_fox_kernel, t=t, parts=c.fox_parts),
        grid=(c.B, h, nq),
        in_specs=[
            pl.BlockSpec((t, LANES), lambda b, hh, qi: (b * nq + qi, qcol0 + hh)),
            pl.BlockSpec((c.S, LANES), lambda b, hh, qi: (b, qcol0 + h + hh)),
            pl.BlockSpec((c.S, LANES), lambda b, hh, qi: (b, qcol0 + 2 * h + hh)),
            pl.BlockSpec((1, 1, c.S), lambda b, hh, qi: (b * h + hh, 0, 0)),
            pl.BlockSpec((1, t, 1), lambda b, hh, qi: (b * h + hh, qi, 0)),
        ],
        out_specs=pl.BlockSpec((t, LANES), lambda b, hh, qi: (b * nq + qi, hh)),
        out_shape=jax.ShapeDtypeStruct((n, h * LANES), BF16),
        scratch_shapes=[pltpu.VMEM((t, LANES), F32)] * 4 + [pltpu.VMEM((t, t), F32)],
        compiler_params=_cparams(("parallel", "parallel", "arbitrary")),
        name="fox",
    )(proj, proj, proj, crow, ccol)


def _mix_kernel(x_ref, a_ref, u_ref, v_ref, ga_ref, gb_ref, lng_ref, lnb_ref, ws_ref, bs_ref,
                wa_ref, wb_ref, wo_ref, o_ref, sg_ref, *, sgc, groups):
    tm = x_ref.shape[0]
    v = v_ref[...].astype(F32)
    mu = jnp.mean(v, axis=-1, keepdims=True)
    vc = v - mu
    var = jnp.mean(vc * vc, axis=-1, keepdims=True)
    vn = (vc * lax.rsqrt(var + LN_EPS) * lng_ref[...] + lnb_ref[...]).astype(BF16)
    row = lax.broadcasted_iota(I32, (sgc, sgc), 0)
    col = lax.broadcasted_iota(I32, (sgc, sgc), 1)
    for g in range(groups):
        w = jnp.where(col <= row, ws_ref[g], 0.0).astype(BF16)
        bias = bs_ref[g]
        for ci in range(tm // sgc):
            rs = slice(ci * sgc, (ci + 1) * sgc)
            cs = slice(g * LANES, (g + 1) * LANES)
            mixed = jnp.dot(w, vn[rs, cs], preferred_element_type=F32) + bias
            sg_ref[rs, cs] = (u_ref[rs, cs].astype(F32) * mixed).astype(BF16)
    ya = jnp.dot(a_ref[...], wa_ref[...], preferred_element_type=F32)
    yb = jnp.dot(sg_ref[...], wb_ref[...], preferred_element_type=F32)
    merged = (ga_ref[...].astype(F32) * ya + gb_ref[...].astype(F32) * yb).astype(BF16)
    o_ref[...] = x_ref[...] + jnp.dot(merged, wo_ref[...], preferred_element_type=F32)


def _const_spec(shape):
    nd = len(shape)
    return pl.BlockSpec(shape, lambda i: (0,) * nd, pipeline_mode=pl.Buffered(1))


def _mix(c, x2d, attn, proj, ln_g, ln_b, w_s, b_s_full, w_a, w_b, w_o):
    n, d = x2d.shape
    tm = c.tm_mix
    fw = c.FH * LANES
    sw = c.SG * LANES
    ucol = 2 * d // sw
    return pl.pallas_call(
        functools.partial(_mix_kernel, sgc=c.SGC, groups=c.SG),
        grid=(n // tm,),
        in_specs=[
            pl.BlockSpec((tm, d), lambda i: (i, 0)),
            pl.BlockSpec((tm, fw), lambda i: (i, 0)),
            pl.BlockSpec((tm, sw), lambda i: (i, ucol)),
            pl.BlockSpec((tm, sw), lambda i: (i, ucol + 1)),
            pl.BlockSpec((tm, d), lambda i: (i, 0)),
            pl.BlockSpec((tm, d), lambda i: (i, 1)),
            _const_spec((1, sw)),
            _const_spec((1, sw)),
            _const_spec((c.SG, c.SGC, c.SGC)),
            _const_spec((c.SG, c.SGC, LANES)),
            _const_spec((fw, d)),
            _const_spec((sw, d)),
            _const_spec((d, d)),
        ],
        out_specs=pl.BlockSpec((tm, d), lambda i: (i, 0)),
        out_shape=jax.ShapeDtypeStruct((n, d), F32),
        scratch_shapes=[pltpu.VMEM((tm, sw), BF16)],
        compiler_params=_cparams(("parallel",)),
        name="mix",
    )(x2d, attn, proj, proj, proj, proj, ln_g, ln_b, w_s, b_s_full, w_a, w_b, w_o)


def _mem_kv_kernel(m_ref, g_ref, w_ref, o_ref):
    hm = _rms(m_ref[...], g_ref[...]).astype(BF16)
    o_ref[...] = jnp.dot(hm, w_ref[...], preferred_element_type=F32).astype(o_ref.dtype)


def _mem_kv(c, mem2d, g, w_xkv):
    n, d = mem2d.shape
    nc = w_xkv.shape[1]
    tm = c.MEM
    return pl.pallas_call(
        _mem_kv_kernel,
        grid=(n // tm,),
        in_specs=[pl.BlockSpec((tm, d), lambda i: (i, 0)), _const_spec((1, d)), _const_spec((d, nc))],
        out_specs=pl.BlockSpec((tm, nc), lambda i: (i, 0)),
        out_shape=jax.ShapeDtypeStruct((n, nc), BF16),
        compiler_params=_cparams(("parallel",)),
        name="mem_kv",
    )(mem2d, g, w_xkv)


def _xattn_kernel(x1_ref, gx_ref, wq_ref, kv_ref, wo_ref, gf_ref, wr_ref, br_ref,
                  x2_ref, hfp_ref, idx_ref, gate_ref, cnt_ref, carry_ref,
                  *, heads, n_exp, topk, qscale):
    i = pl.program_id(0)
    tm, d = x1_ref.shape
    xw = heads * LANES
    x1 = x1_ref[...]
    hx = _rms(x1, gx_ref[...]).astype(BF16)
    q = (jnp.dot(hx, wq_ref[...], preferred_element_type=F32) * qscale).astype(BF16)
    outs = []
    for h in range(heads):
        k = kv_ref[:, h * LANES:(h + 1) * LANES]
        v = kv_ref[:, xw + h * LANES:xw + (h + 1) * LANES]
        s = lax.dot_general(q[:, h * LANES:(h + 1) * LANES], k, (((1,), (1,)), ((), ())),
                            preferred_element_type=F32)
        s = s - jnp.max(s, axis=-1, keepdims=True)
        p = jnp.exp(s)
        p = p / jnp.sum(p, axis=-1, keepdims=True)
        outs.append(jnp.dot(p.astype(BF16), v, preferred_element_type=F32).astype(BF16))
    o = jnp.concatenate(outs, axis=-1)
    x2 = x1 + jnp.dot(o, wo_ref[...], preferred_element_type=F32)
    x2_ref[...] = x2

    hf = _rms(x2, gf_ref[...])
    hfp_ref[...] = _pack_bf16_pair(hf[:, :d // 2], hf[:, d // 2:])

    h_hi = hf.astype(BF16)
    h_lo = (hf - h_hi.astype(F32)).astype(BF16)
    l_hi = jnp.dot(h_hi, wr_ref[...], preferred_element_type=F32)
    l_lo = jnp.dot(h_lo, wr_ref[:, :LANES], preferred_element_type=F32)
    logits = l_hi[:, :LANES] + l_hi[:, LANES:] + l_lo + br_ref[...]
    lane = lax.broadcasted_iota(I32, (tm, LANES), 1)
    lg = jnp.where(lane < n_exp, logits, -jnp.inf)
    vals, idxs = [], []
    for _ in range(topk):
        m = jnp.max(lg, axis=-1, keepdims=True)
        ix = jnp.min(jnp.where(lg == m, lane, LANES), axis=-1, keepdims=True)
        vals.append(m)
        idxs.append(ix)
        lg = jnp.where(lane == ix, -jnp.inf, lg)
    es = [jnp.exp(vv - vals[0]) for vv in vals]
    denom = es[0]
    for e in es[1:]:
        denom = denom + e

    @pl.when(i == 0)
    def _():
        carry_ref[...] = jnp.zeros(carry_ref.shape, F32)

    onehot = jnp.zeros((tm, LANES), F32)
    for ix in idxs:
        onehot = onehot + (lane == ix).astype(F32)
    idx_out = jnp.zeros((tm, LANES), I32)
    gate_out = jnp.zeros((tm, LANES), F32)
    for kk in range(topk):
        idx_out = jnp.where(lane == kk, idxs[kk], idx_out)
        gate_out = jnp.where(lane == kk, es[kk] / denom, gate_out)
    idx_ref[...] = idx_out
    gate_ref[...] = gate_out
    carry_ref[...] = carry_ref[...] + jnp.sum(onehot, axis=0, keepdims=True)
    cnt_ref[...] = carry_ref[...]


def _xattn(c, x1, gx, w_xq, kv, w_xo, gf, w_r, b_r, qscale):
    n, d = x1.shape
    tm = c.tm_x
    xw = c.XH * LANES
    per_b = c.S // tm
    kern = functools.partial(_xattn_kernel, heads=c.XH, n_exp=c.E, topk=c.K, qscale=qscale)
    return pl.pallas_call(
        kern,
        grid=(n // tm,),
        in_specs=[
            pl.BlockSpec((tm, d), lambda i: (i, 0)),
            _const_spec((1, d)),
            _const_spec((d, xw)),
            pl.BlockSpec((c.MEM, 2 * xw), lambda i: (i // per_b, 0)),
            _const_spec((xw, d)),
            _const_spec((1, d)),
            _const_spec((d, 2 * LANES)),
            _const_spec((1, LANES)),
        ],
        out_specs=[
            pl.BlockSpec((tm, d), lambda i: (i, 0)),
            pl.BlockSpec((tm, d // 2), lambda i: (i, 0)),
            pl.BlockSpec((tm, LANES), lambda i: (i, 0)),
            pl.BlockSpec((tm, LANES), lambda i: (i, 0)),
            pl.BlockSpec((1, LANES), lambda i: (0, 0)),
        ],
        out_shape=[
            jax.ShapeDtypeStruct((n, d), F32),
            jax.ShapeDtypeStruct((n, d // 2), U32),
            jax.ShapeDtypeStruct((n, LANES), I32),
            jax.ShapeDtypeStruct((n, LANES), F32),
            jax.ShapeDtypeStruct((1, LANES), F32),
        ],
        scratch_shapes=[pltpu.VMEM((1, LANES), F32)],
        compiler_params=_cparams(("arbitrary",)),
        name="xattn_router",
    )(x1, gx, w_xq, kv, w_xo, gf, w_r, b_r)


def _moe_kernel(te_ref, ns_ref, cpi_ref, pos_ref, rows_ref, ord_ref, hfp_ref, wgu_ref, wd_ref, bgu_ref, bd_ref,
                ykt_ref, wgub, wdb, stage, act, xin, ybuf, ids, wsem, xsem, ysem, isem,
                *, ts, nck, fc, n_tok, n_asg):
    i = pl.program_id(0)
    n = pl.num_programs(0)
    nsub = ns_ref[i]
    e = te_ref[i]
    cpi = cpi_ref[i]
    rows = rows_ref[i]
    d, f2 = wgub.shape
    f = f2 // 2
    dh = d // 2
    ckr = d // nck
    ckd = f // nck
    nst = stage.shape[0]
    idw = ids.shape[0] // 2
    nq = idw // ID_CHUNK
    id_base = (i % 2) * idw + (pos_ref[i] & (ID_CHUNK - 1))

    def ids_copy(item, q, slot):
        c0 = lax.shift_right_logical(pos_ref[item], ID_CHUNK.bit_length() - 1)
        src = ord_ref.at[pl.ds(pl.multiple_of((c0 + q) * ID_CHUNK, ID_CHUNK), ID_CHUNK)]
        dst = ids.at[pl.ds(pl.multiple_of(slot * idw + q * ID_CHUNK, ID_CHUNK), ID_CHUNK)]
        return pltpu.make_async_copy(src, dst, isem.at[slot])

    def token_of(a):
        return a & (n_tok - 1) if n_tok & (n_tok - 1) == 0 else lax.rem(a, n_tok)

    def gather(s, slot, r0=0, r1=ts, base=None):
        base = id_base if base is None else base
        for r in range(r0, r1):
            tok = token_of(ids[base + s * ts + r])
            pltpu.make_async_copy(hfp_ref.at[pl.ds(tok, 1)], xin.at[slot, pl.ds(r, 1)], xsem.at[slot]).start()

    def gather_wait(slot):
        for r in range(ts):
            pltpu.make_async_copy(hfp_ref.at[pl.ds(0, 1)], xin.at[slot, pl.ds(0, 1)], xsem.at[slot]).wait()

    def scatter(s, slot, r0=0, r1=ts, base=None, nrows=None):
        base = id_base if base is None else base
        nrows = rows if nrows is None else nrows
        for r in range(r0, r1):
            g = s * ts + r
            dst = jnp.where(g < nrows, ids[base + g], n_asg + slot * ts + r)
            pltpu.make_async_copy(ybuf.at[slot, pl.ds(r, 1)], ykt_ref.at[pl.ds(dst, 1)], ysem.at[slot]).start()

    def scatter_wait(slot):
        for r in range(ts):
            pltpu.make_async_copy(ybuf.at[slot, pl.ds(0, 1)], ykt_ref.at[pl.ds(0, 1)], ysem.at[slot]).wait()

    def gu_copy(ee, c, slot):
        return pltpu.make_async_copy(wgu_ref.at[ee, pl.ds(pl.multiple_of(c * ckr, ckr), ckr), :],
                                     stage.at[slot], wsem.at[slot])

    def d_copy(ee, c, slot):
        return pltpu.make_async_copy(wd_ref.at[ee, pl.ds(pl.multiple_of(c * ckd, ckd), ckd), :],
                                     stage.at[slot, pl.ds(0, ckd), pl.ds(0, d)], wsem.at[slot])

    def gu_step(ee, c):
        slot = c % nst
        gu_copy(ee, c, slot).wait()
        wgub[pl.ds(pl.multiple_of(c * ckr, ckr), ckr), :] = stage[slot].astype(BF16)

        @pl.when(c + nst < nck)
        def _():
            gu_copy(ee, c + nst, slot).start(priority=WEIGHT_DMA_PRIORITY)

    def d_step(ee, c):
        slot = c % nst
        d_copy(ee, c, slot).wait()
        wdb[pl.ds(pl.multiple_of(c * ckd, ckd), ckd), :] = stage[slot, :ckd, :d].astype(BF16)

        @pl.when(c + nst < nck)
        def _():
            d_copy(ee, c + nst, slot).start(priority=WEIGHT_DMA_PRIORITY)

    def prime(copy_fn, ee):
        for q in range(min(nst, nck)):
            copy_fn(ee, q, q).start(priority=WEIGHT_DMA_PRIORITY)

    def steps_after(s, step_fn, ee):
        def one(jj, carry):
            c = s * cpi + jj

            @pl.when(c < nck)
            def _():
                step_fn(ee, c)
            return carry
        lax.fori_loop(0, cpi, one, 0)

    @pl.when(nsub > 0)
    def _():
        nxt = jnp.minimum(i + 1, n - 1)
        has_next = (i + 1 < n) & (ns_ref[nxt] > 0)

        @pl.when(i == 0)
        def _():
            for q in range(nq):
                ids_copy(0, q, 0).start()
            prime(gu_copy, e)

            def first(c, carry):
                gu_step(e, c)
                return carry
            lax.fori_loop(0, nck, first, 0)
            for q in range(nq):
                ids_copy(0, q, 0).wait()
            gather(0, 0)
            ybuf[...] = jnp.zeros(ybuf.shape, ybuf.dtype)

        prv = jnp.maximum(i - 1, 0)
        nsub_p = jnp.where(i > 0, ns_ref[prv], 1)
        rows_p = jnp.where(i > 0, rows_ref[prv], 0)
        nslot = 1 - i % 2
        base_p = jnp.where(i > 0, nslot * idw + (pos_ref[prv] & (ID_CHUNK - 1)), id_base)
        lslot_p = (nsub_p - 1) % 2

        prime(d_copy, e)
        bgu = bgu_ref[e]

        def gate_up(s, scatter_prev):
            slot = s % 2
            gather_wait(slot)
            p = xin[slot]
            xb = jnp.concatenate([_unpack_lo(p).astype(BF16), _unpack_hi(p).astype(BF16)], axis=-1)
            nc = f // fc
            for cc in range(nc):
                gather(s + 1, 1 - slot, cc * ts // nc, (cc + 1) * ts // nc)
                if scatter_prev:
                    scatter(nsub_p - 1, lslot_p, cc * ts // nc, (cc + 1) * ts // nc, base=base_p, nrows=rows_p)
                gate = jnp.dot(xb, wgub[:, cc * fc:(cc + 1) * fc], preferred_element_type=F32)
                gate = gate + bgu[:, cc * fc:(cc + 1) * fc]
                up = jnp.dot(xb, wgub[:, f + cc * fc:f + (cc + 1) * fc], preferred_element_type=F32)
                up = up + bgu[:, f + cc * fc:f + (cc + 1) * fc]
                gate = jnp.minimum(gate, SWIGLU_LIMIT)
                up = jnp.clip(up, -SWIGLU_LIMIT, SWIGLU_LIMIT)
                a = (up + 1.0) * gate * jax.nn.sigmoid(SWIGLU_ALPHA * gate)
                act[s, :, cc * fc:(cc + 1) * fc] = a.astype(BF16)
            steps_after(s, d_step, e)

        gate_up(0, True)

        for q in range(nq):
            ids_copy(nxt, q, nslot).start()

        def phase_a(s, carry):
            gate_up(s, False)
            return carry

        lax.fori_loop(1, nsub, phase_a, 0)
        gather_wait(nsub % 2)

        e_next = te_ref[nxt]
        load_next = has_next & (e_next != e)

        @pl.when(load_next)
        def _():
            prime(gu_copy, e_next)

        bd = bd_ref[e]

        def down(s, slot, scatter_prev):
            a = act[s]
            if scatter_prev:
                scatter(s - 1, 1 - slot)
            fcb = min(fc, dh // 2)
            for cc in range(dh // fcb):
                lo = slice(cc * fcb, (cc + 1) * fcb)
                hi = slice(dh + cc * fcb, dh + (cc + 1) * fcb)
                y_lo = jnp.dot(a, wdb[:, lo], preferred_element_type=F32) + bd[:, lo]
                y_hi = jnp.dot(a, wdb[:, hi], preferred_element_type=F32) + bd[:, hi]
                ybuf[slot, :, lo] = _pack_bf16_pair(y_lo, y_hi)

        def next_weights(s):
            @pl.when(load_next)
            def _():
                steps_after(s, gu_step, e_next)

        scatter_wait(lslot_p)

        @pl.when(nsub_p >= 2)
        def _():
            scatter_wait(1 - lslot_p)

        for q in range(nq):
            ids_copy(nxt, q, nslot).wait()
        gather(0, 0, base=nslot * idw + (pos_ref[nxt] & (ID_CHUNK - 1)))
        down(0, 0, False)
        next_weights(0)

        def phase_b(s, carry):
            slot = s % 2

            @pl.when(s >= 2)
            def _():
                scatter_wait(slot)

            down(s, slot, True)
            next_weights(s)
            return carry

        lax.fori_loop(1, nsub, phase_b, 0)

        @pl.when(jnp.logical_not(has_next))
        def _():
            scatter(nsub - 1, (nsub - 1) % 2)
            scatter_wait((nsub - 1) % 2)

            @pl.when(nsub >= 2)
            def _():
                scatter_wait(nsub % 2)

            gather_wait(0)


def _id_window(c):
    return ((ID_CHUNK - 1 + c.tm_e + c.ts_e) // ID_CHUNK + 1) * ID_CHUNK


def _moe(c, sched, order_pad, hfp, w_gu, b_gu3, w_d, b_d3):
    n_tok, dh = hfp.shape
    d = 2 * dh
    f = c.DFF
    tm, ts, nck = c.tm_e, c.ts_e, c.nck_e
    n_asg = n_tok * c.K
    grid_spec = pltpu.PrefetchScalarGridSpec(
        num_scalar_prefetch=5,
        grid=(_n_tiles(c),),
        in_specs=[
            pl.BlockSpec(memory_space=pl.ANY),
            pl.BlockSpec(memory_space=pl.ANY),
            pl.BlockSpec(memory_space=pl.ANY),
            pl.BlockSpec(memory_space=pl.ANY),
            pl.BlockSpec((c.E, 1, 2 * f), lambda i, *_: (0, 0, 0), pipeline_mode=pl.Buffered(1)),
            pl.BlockSpec((c.E, 1, d), lambda i, *_: (0, 0, 0), pipeline_mode=pl.Buffered(1)),
        ],
        out_specs=pl.BlockSpec(memory_space=pl.ANY),
        scratch_shapes=[
            pltpu.VMEM((d, 2 * f), BF16),
            pltpu.VMEM((f, d), BF16),
            pltpu.VMEM((c.nst_e, d // nck, 2 * f), F32),
            pltpu.VMEM((tm // ts, ts, f), BF16),
            pltpu.VMEM((2, ts, dh), U32),
            pltpu.VMEM((2, ts, dh), U32),
            pltpu.SMEM((2 * _id_window(c),), I32),
            pltpu.SemaphoreType.DMA((c.nst_e,)),
            pltpu.SemaphoreType.DMA((2,)),
            pltpu.SemaphoreType.DMA((2,)),
            pltpu.SemaphoreType.DMA((2,)),
        ],
    )
    kern = functools.partial(_moe_kernel, ts=ts, nck=nck, fc=c.fc_e, n_tok=n_tok, n_asg=n_asg)
    return pl.pallas_call(
        kern,
        grid_spec=grid_spec,
        out_shape=jax.ShapeDtypeStruct((n_asg + 2 * ts, dh), U32),
        compiler_params=_cparams(("arbitrary",)),
        name="experts",
    )(*sched, order_pad, hfp, w_gu, w_d, b_gu3, b_d3)


def _combine_kernel(x2_ref, gate_ref, g_ref, *refs, topk):
    y_refs, o_ref = refs[:topk], refs[topk]
    d = x2_ref.shape[1]
    dh = d // 2
    lo = x2_ref[:, :dh]
    hi = x2_ref[:, dh:]
    gates = gate_ref[...]
    for kk in range(topk):
        wk = gates[:, kk:kk + 1]
        p = y_refs[kk][...]
        lo = lo + wk * _unpack_lo(p)
        hi = hi + wk * _unpack_hi(p)
    ms = (jnp.sum(lo * lo, axis=-1, keepdims=True) + jnp.sum(hi * hi, axis=-1, keepdims=True)) / d
    inv = lax.rsqrt(ms + RMS_EPS)
    o_ref[:, :dh] = lo * inv * g_ref[:, :dh]
    o_ref[:, dh:] = hi * inv * g_ref[:, dh:]


def _combine(c, x2, gates, g, ykt):
    n, d = x2.shape
    tm = c.tm_c
    nb = n // tm
    y_specs = [pl.BlockSpec((tm, d // 2), functools.partial(lambda i, kk: (kk * nb + i, 0), kk=kk))
               for kk in range(c.K)]
    return pl.pallas_call(
        functools.partial(_combine_kernel, topk=c.K),
        grid=(nb,),
        in_specs=[
            pl.BlockSpec((tm, d), lambda i: (i, 0)),
            pl.BlockSpec((tm, LANES), lambda i: (i, 0)),
            pl.BlockSpec((1, d), lambda i: (0, 0)),
        ] + y_specs,
        out_specs=pl.BlockSpec((tm, d), lambda i: (i, 0)),
        out_shape=jax.ShapeDtypeStruct((n, d), F32),
        compiler_params=_cparams(("parallel",)),
        name="combine",
    )(x2, gates, g, *([ykt] * c.K))


def _plan(c, idx, counts):
    tm, ts = c.tm_e, c.ts_e
    nk = idx.shape[0] * c.K
    order = jnp.argsort(idx.T.reshape(-1)).astype(I32)
    pad = _id_window(c) + (-nk) % ID_CHUNK
    order_pad = jnp.concatenate([order, jnp.zeros((pad,), I32)])
    counts = counts.astype(I32)
    start = jnp.cumsum(counts) - counts
    tiles_per_e = (counts + tm - 1) // tm
    tile_end = jnp.cumsum(tiles_per_e)
    tile_start = tile_end - tiles_per_e
    t = jnp.arange(_n_tiles(c), dtype=I32)
    n_used = tile_end[-1]
    tc = jnp.minimum(t, n_used - 1)
    te = jnp.minimum(jnp.searchsorted(tile_end, tc, side="right"), c.E - 1).astype(I32)
    j = tc - tile_start[te]
    used = t < n_used
    rows = jnp.where(used, jnp.clip(counts[te] - j * tm, 0, tm), 0).astype(I32)
    nsub = (rows + ts - 1) // ts
    pos = jnp.where(used, start[te] + j * tm, 0).astype(I32)
    cpi = (c.nck_e + jnp.maximum(nsub, 1) - 1) // jnp.maximum(nsub, 1)
    return order_pad, (te, nsub.astype(I32), cpi.astype(I32), pos, rows)


def _n_tiles(c):
    return -(-(c.B * c.S * c.K) // c.tm_e) + c.E


def _forward(c, x, mem, norm_mix_g, w_in, b_forget, sg_ln_g, sg_ln_b, w_spatial, b_spatial,
             w_branch_a, w_branch_b, w_out, norm_x_g, norm_mem_g, w_xq, w_xkv, w_xo,
             norm_ffn_g, w_router, b_router, w_gate_up, b_gate_up, w_down, b_down, norm_final_g):
    B, S, D = x.shape
    n = B * S
    fw = c.FH * LANES
    sw = c.SG * LANES
    x2d = x.reshape(n, D)

    o_f = 3 * fw
    o_z = o_f + c.FH
    o_g = o_z + 2 * sw
    w_main = jnp.concatenate([w_in[:, o_g:], w_in[:, o_z:o_g], w_in[:, :o_f]], axis=1).astype(BF16)
    w_f = jnp.pad(w_in[:, o_f:o_z], ((0, 0), (0, LANES - c.FH))).astype(BF16)
    b_f = jnp.pad(b_forget.astype(F32), (0, LANES - c.FH)).reshape(1, LANES)
    proj, logf = _in_proj(c, x2d, norm_mix_g.reshape(1, D), w_main, w_f, b_f, LOG2E * LANES ** -0.5)

    logf_bhs = logf[:, :c.FH].reshape(B, S, c.FH).transpose(0, 2, 1)
    csum = _cumsum(c, logf_bhs)
    crow = csum.reshape(B * c.FH, 1, S)
    ccol = csum.reshape(B * c.FH, S, 1)
    qcol0 = (2 * D + 2 * sw) // LANES
    attn = _fox(c, proj, crow, ccol, qcol0)

    b_s_full = jnp.broadcast_to(b_spatial.astype(F32)[:, :, None], (c.SG, c.SGC, LANES))
    x1 = _mix(c, x2d, attn, proj, sg_ln_g.reshape(1, sw), sg_ln_b.reshape(1, sw), w_spatial, b_s_full,
              w_branch_a.astype(BF16), w_branch_b.astype(BF16), w_out.astype(BF16))

    kv = _mem_kv(c, mem.reshape(B * c.MEM, D), norm_mem_g.reshape(1, D), w_xkv.astype(BF16))
    w_r32 = jnp.pad(w_router.astype(F32), ((0, 0), (0, LANES - c.E)))
    w_r_hi = w_r32.astype(BF16)
    w_r = jnp.concatenate([w_r_hi, (w_r32 - w_r_hi.astype(F32)).astype(BF16)], axis=1)
    b_r = jnp.pad(b_router.astype(F32), (0, LANES - c.E)).reshape(1, LANES)
    x2, hfp, idx, gates, counts = _xattn(
        c, x1, norm_x_g.reshape(1, D), w_xq.astype(BF16), kv, w_xo.astype(BF16),
        norm_ffn_g.reshape(1, D), w_r, b_r, LANES ** -0.5)

    order_pad, sched = _plan(c, idx[:, :c.K], counts[0, :c.E])
    ykt = _moe(c, sched, order_pad, hfp, w_gate_up, b_gate_up.reshape(c.E, 1, 2 * c.DFF),
               w_down, b_down.reshape(c.E, 1, D))
    out = _combine(c, x2, gates, norm_final_g.reshape(1, D), ykt)
    return out.reshape(B, S, D)


_CFG = Cfg(B=4, S=4096, D=2048, MEM=256, FH=8, SG=8, SGC=128, XH=4, E=32, K=4, DFF=2048,
           tm_in=1024, tn_in=1024, tq=1024, tm_mix=256, tm_x=512, tm_e=2560, ts_e=256, fc_e=512, nck_e=16, nst_e=4,
           tm_c=256, tc_cs=512, fox_parts=2)


@jax.jit
def kernel(x, mem, norm_mix_g, w_in, b_forget, sg_ln_g, sg_ln_b, w_spatial, b_spatial, w_branch_a, w_branch_b,
           w_out, norm_x_g, norm_mem_g, w_xq, w_xkv, w_xo, norm_ffn_g, w_router, b_router, w_gate_up,
           b_gate_up, w_down, b_down, norm_final_g):
    return _forward(_CFG, x, mem, norm_mix_g, w_in, b_forget, sg_ln_g, sg_ln_b, w_spatial, b_spatial,
                    w_branch_a, w_branch_b, w_out, norm_x_g, norm_mem_g, w_xq, w_xkv, w_xo,
                    norm_ffn_g, w_router, b_router, w_gate_up, b_gate_up, w_down, b_down, norm_final_g)
```

```python
import functools
from typing import NamedTuple

import jax
import jax.numpy as jnp
from jax import lax
from jax.experimental import pallas as pl
from jax.experimental.pallas import tpu as pltpu

F32 = jnp.float32
BF16 = jnp.bfloat16
U32 = jnp.uint32
I32 = jnp.int32

LANES = 128
VMEM_LIMIT = 56 * 1024 * 1024

RMS_EPS = 1e-6
LN_EPS = 1e-5
SWIGLU_LIMIT = 7.0
SWIGLU_ALPHA = 1.702
GELU_C = 0.7978845608028654
LOG2E = 1.4426950408889634
ID_CHUNK = 1024
WEIGHT_DMA_PRIORITY = 1


class Cfg(NamedTuple):
    B: int
    S: int
    D: int
    MEM: int
    FH: int
    SG: int
    SGC: int
    XH: int
    E: int
    K: int
    DFF: int
    tm_in: int
    tn_in: int
    tq: int
    tm_mix: int
    tm_x: int
    tm_e: int
    ts_e: int
    fc_e: int
    nck_e: int
    nst_e: int
    tm_c: int
    tc_cs: int
    fox_parts: int


def _cparams(sem):
    return pltpu.CompilerParams(dimension_semantics=sem, vmem_limit_bytes=VMEM_LIMIT)


def _rms(x, g):
    ms = jnp.mean(x * x, axis=-1, keepdims=True)
    return x * lax.rsqrt(ms + RMS_EPS) * g


def _pack_bf16_pair(a, b):
    def rne(v):
        bits = lax.bitcast_convert_type(v, U32)
        return bits + jnp.uint32(0x7FFF) + ((bits >> 16) & jnp.uint32(1))
    return (rne(a) >> 16) | (rne(b) & jnp.uint32(0xFFFF0000))


def _unpack_lo(p):
    return lax.bitcast_convert_type(p << 16, F32)


def _unpack_hi(p):
    return lax.bitcast_convert_type(p & jnp.uint32(0xFFFF0000), F32)


def _in_proj_kernel(x_ref, g_ref, w_ref, wf_ref, bf_ref, o_ref, f_ref, h_ref, *, nj_gate, nj_z, nj_q, qscale):
    j = pl.program_id(1)

    @pl.when(j == 0)
    def _():
        hb = _rms(x_ref[...], g_ref[...]).astype(BF16)
        h_ref[...] = hb
        f = jnp.dot(hb, wf_ref[...], preferred_element_type=F32) + bf_ref[...]
        f_ref[...] = jnp.minimum(f, 0.0) - jnp.log1p(jnp.exp(-jnp.abs(f)))

    acc = jnp.dot(h_ref[...], w_ref[...], preferred_element_type=F32)

    is_gate = j < nj_gate
    is_z = (j >= nj_gate) & (j < nj_gate + nj_z)
    is_q = (j >= nj_gate + nj_z) & (j < nj_gate + nj_z + nj_q)
    a1 = jnp.where(is_gate, 0.5, jnp.where(is_z, GELU_C, 0.0)).astype(F32)
    a3 = jnp.where(is_z, GELU_C * 0.044715, 0.0).astype(F32)
    b0 = jnp.where(is_gate, 0.5, 0.0).astype(F32)
    b1 = jnp.where(is_gate, 0.0, jnp.where(is_z, 0.5, jnp.where(is_q, qscale, 1.0))).astype(F32)
    th = jnp.tanh(acc * (a1 + a3 * (acc * acc)))
    o_ref[...] = ((b0 + b1 * acc) * (1.0 + th)).astype(o_ref.dtype)


def _in_proj(c, x2d, g, w_main, w_f, b_f, qscale):
    n, d = x2d.shape
    nc = w_main.shape[1]
    tm, tn = c.tm_in, c.tn_in
    fw = c.FH * LANES
    kern = functools.partial(_in_proj_kernel, nj_gate=2 * d // tn, nj_z=2 * c.SG * LANES // tn,
                             nj_q=fw // tn, qscale=qscale)
    return pl.pallas_call(
        kern,
        grid=(n // tm, nc // tn),
        in_specs=[
            pl.BlockSpec((tm, d), lambda i, j: (i, 0)),
            pl.BlockSpec((1, d), lambda i, j: (0, 0)),
            pl.BlockSpec((d, tn), lambda i, j: (0, j)),
            pl.BlockSpec((d, LANES), lambda i, j: (0, 0)),
            pl.BlockSpec((1, LANES), lambda i, j: (0, 0)),
        ],
        out_specs=[
            pl.BlockSpec((tm, tn), lambda i, j: (i, j)),
            pl.BlockSpec((tm, LANES), lambda i, j: (i, 0)),
        ],
        out_shape=[jax.ShapeDtypeStruct((n, nc), BF16), jax.ShapeDtypeStruct((n, LANES), F32)],
        scratch_shapes=[pltpu.VMEM((tm, d), BF16)],
        compiler_params=_cparams(("parallel", "arbitrary")),
        name="in_proj",
    )(x2d, g, w_main, w_f, b_f)


def _cumsum_kernel(f_ref, o_ref, *, tc):
    rows, s = f_ref.shape[1], f_ref.shape[2]
    r = lax.broadcasted_iota(I32, (tc, tc), 0)
    col = lax.broadcasted_iota(I32, (tc, tc), 1)
    upper = (r <= col).astype(F32)
    carry = jnp.zeros((rows, 1), F32)
    for i in range(s // tc):
        blk = f_ref[0, :, i * tc:(i + 1) * tc]
        cs = jnp.dot(blk, upper, preferred_element_type=F32, precision=lax.Precision.HIGHEST) + carry
        o_ref[0, :, i * tc:(i + 1) * tc] = cs * LOG2E
        carry = cs[:, tc - 1:tc]


def _cumsum(c, logf_bhs):
    b, h, s = logf_bhs.shape
    return pl.pallas_call(
        functools.partial(_cumsum_kernel, tc=c.tc_cs),
        grid=(b,),
        in_specs=[pl.BlockSpec((1, h, s), lambda i: (i, 0, 0))],
        out_specs=pl.BlockSpec((1, h, s), lambda i: (i, 0, 0)),
        out_shape=jax.ShapeDtypeStruct((b, h, s), F32),
        compiler_params=_cparams(("parallel",)),
        name="cumsum",
    )(logf_bhs)


def _fox_kernel(q_ref, k_ref, v_ref, crow_ref, ccol_ref, o_ref, m_ref, l_ref, acc_ref, cq_ref, s_ref,
                *, t, parts):
    qi = pl.program_id(2)
    hr = t // parts
    m_ref[...] = jnp.full(m_ref.shape, -jnp.inf, F32)
    l_ref[...] = jnp.zeros(l_ref.shape, F32)
    acc_ref[...] = jnp.zeros(acc_ref.shape, F32)
    cq_ref[...] = jnp.broadcast_to(ccol_ref[0], cq_ref.shape)

    def scores(ks):
        k = k_ref[pl.ds(ks, t), :]
        return lax.dot_general(q_ref[...], k, (((1,), (1,)), ((), ())), preferred_element_type=F32)

    def block(part, ks, width, masked):
        rows = slice(part * hr, (part + 1) * hr)
        v = v_ref[pl.ds(ks, width), :]
        s = s_ref[rows, :width]
        cq = cq_ref[rows, :]
        crow = crow_ref[0, :, pl.ds(ks, width)]
        nj = width // LANES
        sj = [s[:, j * LANES:(j + 1) * LANES] + cq - crow[:, j * LANES:(j + 1) * LANES] for j in range(nj)]
        if masked:
            row = lax.broadcasted_iota(I32, (hr, LANES), 0) + part * hr
            col = lax.broadcasted_iota(I32, (hr, LANES), 1)
            sj = [jnp.where(col + j * LANES <= row, sj[j], -jnp.inf) for j in range(nj)]
        mx = sj[0]
        for j in range(1, nj):
            mx = jnp.maximum(mx, sj[j])
        m_prev = m_ref[rows, :]
        m_next = jnp.maximum(m_prev, jnp.max(mx, axis=-1, keepdims=True))
        alpha = jnp.exp2(m_prev - m_next)
        pj = [jnp.exp2(sj[j] - m_next) for j in range(nj)]
        psum = pj[0]
        for j in range(1, nj):
            psum = psum + pj[j]
        p = jnp.concatenate([x.astype(BF16) for x in pj], axis=-1)
        l_ref[rows, :] = alpha * l_ref[rows, :] + psum
        acc_ref[rows, :] = alpha * acc_ref[rows, :] + jnp.dot(p, v, preferred_element_type=F32)
        m_ref[rows, :] = m_next

    s_ref[...] = scores(0)

    def body(kc, carry):
        ks = pl.multiple_of(kc * t, t)
        s_next = scores(pl.multiple_of(ks + t, t))
        for part in range(parts):
            block(part, ks, t, False)
        s_ref[...] = s_next
        return carry

    lax.fori_loop(0, qi, body, 0)
    kd = pl.multiple_of(qi * t, t)
    for part in range(parts):
        block(part, kd, (part + 1) * hr, True)
    l = jnp.sum(l_ref[...], axis=-1, keepdims=True)
    o_ref[...] = (acc_ref[...] / l).astype(o_ref.dtype)


def _fox(c, proj, crow, ccol, qcol0):
    n = c.B * c.S
    t = c.tq
    nq = c.S // t
    h = c.FH
    return pl.pallas_call(
        functools.partial(_fox_kernel, t=t, parts=c.fox_parts),
        grid=(c.B, h, nq),
        in_specs=[
            pl.BlockSpec((t, LANES), lambda b, hh, qi: (b * nq + qi, qcol0 + hh)),
            pl.BlockSpec((c.S, LANES), lambda b, hh, qi: (b, qcol0 + h + hh)),
            pl.BlockSpec((c.S, LANES), lambda b, hh, qi: (b, qcol0 + 2 * h + hh)),
            pl.BlockSpec((1, 1, c.S), lambda b, hh, qi: (b * h + hh, 0, 0)),
            pl.BlockSpec((1, t, 1), lambda b, hh, qi: (b * h + hh, qi, 0)),
        ],
        out_specs=pl.BlockSpec((t, LANES), lambda b, hh, qi: (b * nq + qi, hh)),
        out_shape=jax.ShapeDtypeStruct((n, h * LANES), BF16),
        scratch_shapes=[pltpu.VMEM((t, LANES), F32)] * 4 + [pltpu.VMEM((t, t), F32)],
        compiler_params=_cparams(("parallel", "parallel", "arbitrary")),
        name="fox",
    )(proj, proj, proj, crow, ccol)


def _mix_kernel(x_ref, a_ref, u_ref, v_ref, ga_ref, gb_ref, lng_ref, lnb_ref, ws_ref, bs_ref,
                wa_ref, wb_ref, wo_ref, o_ref, sg_ref, *, sgc, groups):
    tm = x_ref.shape[0]
    v = v_ref[...].astype(F32)
    mu = jnp.mean(v, axis=-1, keepdims=True)
    vc = v - mu
    var = jnp.mean(vc * vc, axis=-1, keepdims=True)
    vn = (vc * lax.rsqrt(var + LN_EPS) * lng_ref[...] + lnb_ref[...]).astype(BF16)
    row = lax.broadcasted_iota(I32, (sgc, sgc), 0)
    col = lax.broadcasted_iota(I32, (sgc, sgc), 1)
    for g in range(groups):
        w = jnp.where(col <= row, ws_ref[g], 0.0).astype(BF16)
        bias = bs_ref[g]
        for ci in range(tm // sgc):
            rs = slice(ci * sgc, (ci + 1) * sgc)
            cs = slice(g * LANES, (g + 1) * LANES)
            mixed = jnp.dot(w, vn[rs, cs], preferred_element_type=F32) + bias
            sg_ref[rs, cs] = (u_ref[rs, cs].astype(F32) * mixed).astype(BF16)
    ya = jnp.dot(a_ref[...], wa_ref[...], preferred_element_type=F32)
    yb = jnp.dot(sg_ref[...], wb_ref[...], preferred_element_type=F32)
    merged = (ga_ref[...].astype(F32) * ya + gb_ref[...].astype(F32) * yb).astype(BF16)
    o_ref[...] = x_ref[...] + jnp.dot(merged, wo_ref[...], preferred_element_type=F32)


def _const_spec(shape):
    nd = len(shape)
    return pl.BlockSpec(shape, lambda i: (0,) * nd, pipeline_mode=pl.Buffered(1))


def _mix(c, x2d, attn, proj, ln_g, ln_b, w_s, b_s_full, w_a, w_b, w_o):
    n, d = x2d.shape
    tm = c.tm_mix
    fw = c.FH * LANES
    sw = c.SG * LANES
    ucol = 2 * d // sw
    return pl.pallas_call(
        functools.partial(_mix_kernel, sgc=c.SGC, groups=c.SG),
        grid=(n // tm,),
        in_specs=[
            pl.BlockSpec((tm, d), lambda i: (i, 0)),
            pl.BlockSpec((tm, fw), lambda i: (i, 0)),
            pl.BlockSpec((tm, sw), lambda i: (i, ucol)),
            pl.BlockSpec((tm, sw), lambda i: (i, ucol + 1)),
            pl.BlockSpec((tm, d), lambda i: (i, 0)),
            pl.BlockSpec((tm, d), lambda i: (i, 1)),
            _const_spec((1, sw)),
            _const_spec((1, sw)),
            _const_spec((c.SG, c.SGC, c.SGC)),
            _const_spec((c.SG, c.SGC, LANES)),
            _const_spec((fw, d)),
            _const_spec((sw, d)),
            _const_spec((d, d)),
        ],
        out_specs=pl.BlockSpec((tm, d), lambda i: (i, 0)),
        out_shape=jax.ShapeDtypeStruct((n, d), F32),
        scratch_shapes=[pltpu.VMEM((tm, sw), BF16)],
        compiler_params=_cparams(("parallel",)),
        name="mix",
    )(x2d, attn, proj, proj, proj, proj, ln_g, ln_b, w_s, b_s_full, w_a, w_b, w_o)


def _mem_kv_kernel(m_ref, g_ref, w_ref, o_ref):
    hm = _rms(m_ref[...], g_ref[...]).astype(BF16)
    o_ref[...] = jnp.dot(hm, w_ref[...], preferred_element_type=F32).astype(o_ref.dtype)


def _mem_kv(c, mem2d, g, w_xkv):
    n, d = mem2d.shape
    nc = w_xkv.shape[1]
    tm = c.MEM
    return pl.pallas_call(
        _mem_kv_kernel,
        grid=(n // tm,),
        in_specs=[pl.BlockSpec((tm, d), lambda i: (i, 0)), _const_spec((1, d)), _const_spec((d, nc))],
        out_specs=pl.BlockSpec((tm, nc), lambda i: (i, 0)),
        out_shape=jax.ShapeDtypeStruct((n, nc), BF16),
        compiler_params=_cparams(("parallel",)),
        name="mem_kv",
    )(mem2d, g, w_xkv)


def _xattn_kernel(x1_ref, gx_ref, wq_ref, kv_ref, wo_ref, gf_ref, wr_ref, br_ref,
                  x2_ref, hfp_ref, idx_ref, gate_ref, cnt_ref, carry_ref,
                  *, heads, n_exp, topk, qscale):
    i = pl.program_id(0)
    tm, d = x1_ref.shape
    xw = heads * LANES
    x1 = x1_ref[...]
    hx = _rms(x1, gx_ref[...]).astype(BF16)
    q = (jnp.dot(hx, wq_ref[...], preferred_element_type=F32) * qscale).astype(BF16)
    outs = []
    for h in range(heads):
        k = kv_ref[:, h * LANES:(h + 1) * LANES]
        v = kv_ref[:, xw + h * LANES:xw + (h + 1) * LANES]
        s = lax.dot_general(q[:, h * LANES:(h + 1) * LANES], k, (((1,), (1,)), ((), ())),
                            preferred_element_type=F32)
        s = s - jnp.max(s, axis=-1, keepdims=True)
        p = jnp.exp(s)
        p = p / jnp.sum(p, axis=-1, keepdims=True)
        outs.append(jnp.dot(p.astype(BF16), v, preferred_element_type=F32).astype(BF16))
    o = jnp.concatenate(outs, axis=-1)
    x2 = x1 + jnp.dot(o, wo_ref[...], preferred_element_type=F32)
    x2_ref[...] = x2

    hf = _rms(x2, gf_ref[...])
    hfp_ref[...] = _pack_bf16_pair(hf[:, :d // 2], hf[:, d // 2:])

    h_hi = hf.astype(BF16)
    h_lo = (hf - h_hi.astype(F32)).astype(BF16)
    l_hi = jnp.dot(h_hi, wr_ref[...], preferred_element_type=F32)
    l_lo = jnp.dot(h_lo, wr_ref[:, :LANES], preferred_element_type=F32)
    logits = l_hi[:, :LANES] + l_hi[:, LANES:] + l_lo + br_ref[...]
    lane = lax.broadcasted_iota(I32, (tm, LANES), 1)
    lg = jnp.where(lane < n_exp, logits, -jnp.inf)
    vals, idxs = [], []
    for _ in range(topk):
        m = jnp.max(lg, axis=-1, keepdims=True)
        ix = jnp.min(jnp.where(lg == m, lane, LANES), axis=-1, keepdims=True)
        vals.append(m)
        idxs.append(ix)
        lg = jnp.where(lane == ix, -jnp.inf, lg)
    es = [jnp.exp(vv - vals[0]) for vv in vals]
    denom = es[0]
    for e in es[1:]:
        denom = denom + e

    @pl.when(i == 0)
    def _():
        carry_ref[...] = jnp.zeros(carry_ref.shape, F32)

    onehot = jnp.zeros((tm, LANES), F32)
    for ix in idxs:
        onehot = onehot + (lane == ix).astype(F32)
    idx_out = jnp.zeros((tm, LANES), I32)
    gate_out = jnp.zeros((tm, LANES), F32)
    for kk in range(topk):
        idx_out = jnp.where(lane == kk, idxs[kk], idx_out)
        gate_out = jnp.where(lane == kk, es[kk] / denom, gate_out)
    idx_ref[...] = idx_out
    gate_ref[...] = gate_out
    carry_ref[...] = carry_ref[...] + jnp.sum(onehot, axis=0, keepdims=True)
    cnt_ref[...] = carry_ref[...]


def _xattn(c, x1, gx, w_xq, kv, w_xo, gf, w_r, b_r, qscale):
    n, d = x1.shape
    tm = c.tm_x
    xw = c.XH * LANES
    per_b = c.S // tm
    kern = functools.partial(_xattn_kernel, heads=c.XH, n_exp=c.E, topk=c.K, qscale=qscale)
    return pl.pallas_call(
        kern,
        grid=(n // tm,),
        in_specs=[
            pl.BlockSpec((tm, d), lambda i: (i, 0)),
            _const_spec((1, d)),
            _const_spec((d, xw)),
            pl.BlockSpec((c.MEM, 2 * xw), lambda i: (i // per_b, 0)),
            _const_spec((xw, d)),
            _const_spec((1, d)),
            _const_spec((d, 2 * LANES)),
            _const_spec((1, LANES)),
        ],
        out_specs=[
            pl.BlockSpec((tm, d), lambda i: (i, 0)),
            pl.BlockSpec((tm, d // 2), lambda i: (i, 0)),
            pl.BlockSpec((tm, LANES), lambda i: (i, 0)),
            pl.BlockSpec((tm, LANES), lambda i: (i, 0)),
            pl.BlockSpec((1, LANES), lambda i: (0, 0)),
        ],
        out_shape=[
            jax.ShapeDtypeStruct((n, d), F32),
            jax.ShapeDtypeStruct((n, d // 2), U32),
            jax.ShapeDtypeStruct((n, LANES), I32),
            jax.ShapeDtypeStruct((n, LANES), F32),
            jax.ShapeDtypeStruct((1, LANES), F32),
        ],
        scratch_shapes=[pltpu.VMEM((1, LANES), F32)],
        compiler_params=_cparams(("arbitrary",)),
        name="xattn_router",
    )(x1, gx, w_xq, kv, w_xo, gf, w_r, b_r)


def _moe_kernel(te_ref, ns_ref, cpi_ref, pos_ref, rows_ref, ord_ref, hfp_ref, wgu_ref, wd_ref, bgu_ref, bd_ref,
                ykt_ref, wgub, wdb, stage, act, xin, ybuf, ids, wsem, xsem, ysem, isem,
                *, ts, nck, fc, n_tok, n_asg):
    i = pl.program_id(0)
    n = pl.num_programs(0)
    nsub = ns_ref[i]
    e = te_ref[i]
    cpi = cpi_ref[i]
    rows = rows_ref[i]
    d, f2 = wgub.shape
    f = f2 // 2
    dh = d // 2
    ckr = d // nck
    ckd = f // nck
    nst = stage.shape[0]
    idw = ids.shape[0] // 2
    nq = idw // ID_CHUNK
    id_base = (i % 2) * idw + (pos_ref[i] & (ID_CHUNK - 1))

    def ids_copy(item, q, slot):
        c0 = lax.shift_right_logical(pos_ref[item], ID_CHUNK.bit_length() - 1)
        src = ord_ref.at[pl.ds(pl.multiple_of((c0 + q) * ID_CHUNK, ID_CHUNK), ID_CHUNK)]
        dst = ids.at[pl.ds(pl.multiple_of(slot * idw + q * ID_CHUNK, ID_CHUNK), ID_CHUNK)]
        return pltpu.make_async_copy(src, dst, isem.at[slot])

    def token_of(a):
        return a & (n_tok - 1) if n_tok & (n_tok - 1) == 0 else lax.rem(a, n_tok)

    def gather(s, slot, r0=0, r1=ts, base=None):
        base = id_base if base is None else base
        for r in range(r0, r1):
            tok = token_of(ids[base + s * ts + r])
            pltpu.make_async_copy(hfp_ref.at[pl.ds(tok, 1)], xin.at[slot, pl.ds(r, 1)], xsem.at[slot]).start()

    def gather_wait(slot):
        for r in range(ts):
            pltpu.make_async_copy(hfp_ref.at[pl.ds(0, 1)], xin.at[slot, pl.ds(0, 1)], xsem.at[slot]).wait()

    def scatter(s, slot, r0=0, r1=ts, base=None, nrows=None):
        base = id_base if base is None else base
        nrows = rows if nrows is None else nrows
        for r in range(r0, r1):
            g = s * ts + r
            dst = jnp.where(g < nrows, ids[base + g], n_asg + slot * ts + r)
            pltpu.make_async_copy(ybuf.at[slot, pl.ds(r, 1)], ykt_ref.at[pl.ds(dst, 1)], ysem.at[slot]).start()

    def scatter_wait(slot):
        for r in range(ts):
            pltpu.make_async_copy(ybuf.at[slot, pl.ds(0, 1)], ykt_ref.at[pl.ds(0, 1)], ysem.at[slot]).wait()

    def gu_copy(ee, c, slot):
        return pltpu.make_async_copy(wgu_ref.at[ee, pl.ds(pl.multiple_of(c * ckr, ckr), ckr), :],
                                     stage.at[slot], wsem.at[slot])

    def d_copy(ee, c, slot):
        return pltpu.make_async_copy(wd_ref.at[ee, pl.ds(pl.multiple_of(c * ckd, ckd), ckd), :],
                                     stage.at[slot, pl.ds(0, ckd), pl.ds(0, d)], wsem.at[slot])

    def gu_step(ee, c):
        slot = c % nst
        gu_copy(ee, c, slot).wait()
        wgub[pl.ds(pl.multiple_of(c * ckr, ckr), ckr), :] = stage[slot].astype(BF16)

        @pl.when(c + nst < nck)
        def _():
            gu_copy(ee, c + nst, slot).start(priority=WEIGHT_DMA_PRIORITY)

    def d_step(ee, c):
        slot = c % nst
        d_copy(ee, c, slot).wait()
        wdb[pl.ds(pl.multiple_of(c * ckd, ckd), ckd), :] = stage[slot, :ckd, :d].astype(BF16)

        @pl.when(c + nst < nck)
        def _():
            d_copy(ee, c + nst, slot).start(priority=WEIGHT_DMA_PRIORITY)

    def prime(copy_fn, ee):
        for q in range(min(nst, nck)):
            copy_fn(ee, q, q).start(priority=WEIGHT_DMA_PRIORITY)

    def steps_after(s, step_fn, ee):
        def one(jj, carry):
            c = s * cpi + jj

            @pl.when(c < nck)
            def _():
                step_fn(ee, c)
            return carry
        lax.fori_loop(0, cpi, one, 0)

    @pl.when(nsub > 0)
    def _():
        nxt = jnp.minimum(i + 1, n - 1)
        has_next = (i + 1 < n) & (ns_ref[nxt] > 0)

        @pl.when(i == 0)
        def _():
            for q in range(nq):
                ids_copy(0, q, 0).start()
            prime(gu_copy, e)

            def first(c, carry):
                gu_step(e, c)
                return carry
            lax.fori_loop(0, nck, first, 0)
            for q in range(nq):
                ids_copy(0, q, 0).wait()
            gather(0, 0)
            ybuf[...] = jnp.zeros(ybuf.shape, ybuf.dtype)

        prv = jnp.maximum(i - 1, 0)
        nsub_p = jnp.where(i > 0, ns_ref[prv], 1)
        rows_p = jnp.where(i > 0, rows_ref[prv], 0)
        nslot = 1 - i % 2
        base_p = jnp.where(i > 0, nslot * idw + (pos_ref[prv] & (ID_CHUNK - 1)), id_base)
        lslot_p = (nsub_p - 1) % 2

        prime(d_copy, e)
        bgu = bgu_ref[e]

        def gate_up(s, scatter_prev):
            slot = s % 2
            gather_wait(slot)
            p = xin[slot]
            xb = jnp.concatenate([_unpack_lo(p).astype(BF16), _unpack_hi(p).astype(BF16)], axis=-1)
            nc = f // fc
            for cc in range(nc):
                gather(s + 1, 1 - slot, cc * ts // nc, (cc + 1) * ts // nc)
                if scatter_prev:
                    scatter(nsub_p - 1, lslot_p, cc * ts // nc, (cc + 1) * ts // nc, base=base_p, nrows=rows_p)
                gate = jnp.dot(xb, wgub[:, cc * fc:(cc + 1) * fc], preferred_element_type=F32)
                gate = gate + bgu[:, cc * fc:(cc + 1) * fc]
                up = jnp.dot(xb, wgub[:, f + cc * fc:f + (cc + 1) * fc], preferred_element_type=F32)
                up = up + bgu[:, f + cc * fc:f + (cc + 1) * fc]
                gate = jnp.minimum(gate, SWIGLU_LIMIT)
                up = jnp.clip(up, -SWIGLU_LIMIT, SWIGLU_LIMIT)
                a = (up + 1.0) * gate * jax.nn.sigmoid(SWIGLU_ALPHA * gate)
                act[s, :, cc * fc:(cc + 1) * fc] = a.astype(BF16)
            steps_after(s, d_step, e)

        gate_up(0, True)

        for q in range(nq):
            ids_copy(nxt, q, nslot).start()

        def phase_a(s, carry):
            gate_up(s, False)
            return carry

        lax.fori_loop(1, nsub, phase_a, 0)
        gather_wait(nsub % 2)

        e_next = te_ref[nxt]
        load_next = has_next & (e_next != e)

        @pl.when(load_next)
        def _():
            prime(gu_copy, e_next)

        bd = bd_ref[e]

        def down(s, slot, scatter_prev):
            a = act[s]
            if scatter_prev:
                scatter(s - 1, 1 - slot)
            fcb = min(fc, dh // 2)
            for cc in range(dh // fcb):
                lo = slice(cc * fcb, (cc + 1) * fcb)
                hi = slice(dh + cc * fcb, dh + (cc + 1) * fcb)
                y_lo = jnp.dot(a, wdb[:, lo], preferred_element_type=F32) + bd[:, lo]
                y_hi = jnp.dot(a, wdb[:, hi], preferred_element_type=F32) + bd[:, hi]
                ybuf[slot, :, lo] = _pack_bf16_pair(y_lo, y_hi)

        def next_weights(s):
            @pl.when(load_next)
            def _():
                steps_after(s, gu_step, e_next)

        scatter_wait(lslot_p)

        @pl.when(nsub_p >= 2)
        def _():
            scatter_wait(1 - lslot_p)

        for q in range(nq):
            ids_copy(nxt, q, nslot).wait()
        gather(0, 0, base=nslot * idw + (pos_ref[nxt] & (ID_CHUNK - 1)))
        down(0, 0, False)
        next_weights(0)

        def phase_b(s, carry):
            slot = s % 2

            @pl.when(s >= 2)
            def _():
                scatter_wait(slot)

            down(s, slot, True)
            next_weights(s)
            return carry

        lax.fori_loop(1, nsub, phase_b, 0)

        @pl.when(jnp.logical_not(has_next))
        def _():
            scatter(nsub - 1, (nsub - 1) % 2)
            scatter_wait((nsub - 1) % 2)

            @pl.when(nsub >= 2)
            def _():
                scatter_wait(nsub % 2)

            gather_wait(0)


def _id_window(c):
    return ((ID_CHUNK - 1 + c.tm_e + c.ts_e) // ID_CHUNK + 1) * ID_CHUNK


def _moe(c, sched, order_pad, hfp, w_gu, b_gu3, w_d, b_d3):
    n_tok, dh = hfp.shape
    d = 2 * dh
    f = c.DFF
    tm, ts, nck = c.tm_e, c.ts_e, c.nck_e
    n_asg = n_tok * c.K
    grid_spec = pltpu.PrefetchScalarGridSpec(
        num_scalar_prefetch=5,
        grid=(_n_tiles(c),),
        in_specs=[
            pl.BlockSpec(memory_space=pl.ANY),
            pl.BlockSpec(memory_space=pl.ANY),
            pl.BlockSpec(memory_space=pl.ANY),
            pl.BlockSpec(memory_space=pl.ANY),
            pl.BlockSpec((c.E, 1, 2 * f), lambda i, *_: (0, 0, 0), pipeline_mode=pl.Buffered(1)),
            pl.BlockSpec((c.E, 1, d), lambda i, *_: (0, 0, 0), pipeline_mode=pl.Buffered(1)),
        ],
        out_specs=pl.BlockSpec(memory_space=pl.ANY),
        scratch_shapes=[
            pltpu.VMEM((d, 2 * f), BF16),
            pltpu.VMEM((f, d), BF16),
            pltpu.VMEM((c.nst_e, d // nck, 2 * f), F32),
            pltpu.VMEM((tm // ts, ts, f), BF16),
            pltpu.VMEM((2, ts, dh), U32),
            pltpu.VMEM((2, ts, dh), U32),
            pltpu.SMEM((2 * _id_window(c),), I32),
            pltpu.SemaphoreType.DMA((c.nst_e,)),
            pltpu.SemaphoreType.DMA((2,)),
            pltpu.SemaphoreType.DMA((2,)),
            pltpu.SemaphoreType.DMA((2,)),
        ],
    )
    kern = functools.partial(_moe_kernel, ts=ts, nck=nck, fc=c.fc_e, n_tok=n_tok, n_asg=n_asg)
    return pl.pallas_call(
        kern,
        grid_spec=grid_spec,
        out_shape=jax.ShapeDtypeStruct((n_asg + 2 * ts, dh), U32),
        compiler_params=_cparams(("arbitrary",)),
        name="experts",
    )(*sched, order_pad, hfp, w_gu, w_d, b_gu3, b_d3)


def _combine_kernel(x2_ref, gate_ref, g_ref, *refs, topk):
    y_refs, o_ref = refs[:topk], refs[topk]
    d = x2_ref.shape[1]
    dh = d // 2
    lo = x2_ref[:, :dh]
    hi = x2_ref[:, dh:]
    gates = gate_ref[...]
    for kk in range(topk):
        wk = gates[:, kk:kk + 1]
        p = y_refs[kk][...]
        lo = lo + wk * _unpack_lo(p)
        hi = hi + wk * _unpack_hi(p)
    ms = (jnp.sum(lo * lo, axis=-1, keepdims=True) + jnp.sum(hi * hi, axis=-1, keepdims=True)) / d
    inv = lax.rsqrt(ms + RMS_EPS)
    o_ref[:, :dh] = lo * inv * g_ref[:, :dh]
    o_ref[:, dh:] = hi * inv * g_ref[:, dh:]


def _combine(c, x2, gates, g, ykt):
    n, d = x2.shape
    tm = c.tm_c
    nb = n // tm
    y_specs = [pl.BlockSpec((tm, d // 2), functools.partial(lambda i, kk: (kk * nb + i, 0), kk=kk))
               for kk in range(c.K)]
    return pl.pallas_call(
        functools.partial(_combine_kernel, topk=c.K),
        grid=(nb,),
        in_specs=[
            pl.BlockSpec((tm, d), lambda i: (i, 0)),
            pl.BlockSpec((tm, LANES), lambda i: (i, 0)),
            pl.BlockSpec((1, d), lambda i: (0, 0)),
        ] + y_specs,
        out_specs=pl.BlockSpec((tm, d), lambda i: (i, 0)),
        out_shape=jax.ShapeDtypeStruct((n, d), F32),
        compiler_params=_cparams(("parallel",)),
        name="combine",
    )(x2, gates, g, *([ykt] * c.K))


def _plan(c, idx, counts):
    tm, ts = c.tm_e, c.ts_e
    nk = idx.shape[0] * c.K
    order = jnp.argsort(idx.T.reshape(-1)).astype(I32)
    pad = _id_window(c) + (-nk) % ID_CHUNK
    order_pad = jnp.concatenate([order, jnp.zeros((pad,), I32)])
    counts = counts.astype(I32)
    start = jnp.cumsum(counts) - counts
    tiles_per_e = (counts + tm - 1) // tm
    tile_end = jnp.cumsum(tiles_per_e)
    tile_start = tile_end - tiles_per_e
    t = jnp.arange(_n_tiles(c), dtype=I32)
    n_used = tile_end[-1]
    tc = jnp.minimum(t, n_used - 1)
    te = jnp.minimum(jnp.searchsorted(tile_end, tc, side="right"), c.E - 1).astype(I32)
    j = tc - tile_start[te]
    used = t < n_used
    rows = jnp.where(used, jnp.clip(counts[te] - j * tm, 0, tm), 0).astype(I32)
    nsub = (rows + ts - 1) // ts
    pos = jnp.where(used, start[te] + j * tm, 0).astype(I32)
    cpi = (c.nck_e + jnp.maximum(nsub, 1) - 1) // jnp.maximum(nsub, 1)
    return order_pad, (te, nsub.astype(I32), cpi.astype(I32), pos, rows)


def _n_tiles(c):
    return -(-(c.B * c.S * c.K) // c.tm_e) + c.E


def _forward(c, x, mem, norm_mix_g, w_in, b_forget, sg_ln_g, sg_ln_b, w_spatial, b_spatial,
             w_branch_a, w_branch_b, w_out, norm_x_g, norm_mem_g, w_xq, w_xkv, w_xo,
             norm_ffn_g, w_router, b_router, w_gate_up, b_gate_up, w_down, b_down, norm_final_g):
    B, S, D = x.shape
    n = B * S
    fw = c.FH * LANES
    sw = c.SG * LANES
    x2d = x.reshape(n, D)

    o_f = 3 * fw
    o_z = o_f + c.FH
    o_g = o_z + 2 * sw
    w_main = jnp.concatenate([w_in[:, o_g:], w_in[:, o_z:o_g], w_in[:, :o_f]], axis=1).astype(BF16)
    w_f = jnp.pad(w_in[:, o_f:o_z], ((0, 0), (0, LANES - c.FH))).astype(BF16)
    b_f = jnp.pad(b_forget.astype(F32), (0, LANES - c.FH)).reshape(1, LANES)
    proj, logf = _in_proj(c, x2d, norm_mix_g.reshape(1, D), w_main, w_f, b_f, LOG2E * LANES ** -0.5)

    logf_bhs = logf[:, :c.FH].reshape(B, S, c.FH).transpose(0, 2, 1)
    csum = _cumsum(c, logf_bhs)
    crow = csum.reshape(B * c.FH, 1, S)
    ccol = csum.reshape(B * c.FH, S, 1)
    qcol0 = (2 * D + 2 * sw) // LANES
    attn = _fox(c, proj, crow, ccol, qcol0)

    b_s_full = jnp.broadcast_to(b_spatial.astype(F32)[:, :, None], (c.SG, c.SGC, LANES))
    x1 = _mix(c, x2d, attn, proj, sg_ln_g.reshape(1, sw), sg_ln_b.reshape(1, sw), w_spatial, b_s_full,
              w_branch_a.astype(BF16), w_branch_b.astype(BF16), w_out.astype(BF16))

    kv = _mem_kv(c, mem.reshape(B * c.MEM, D), norm_mem_g.reshape(1, D), w_xkv.astype(BF16))
    w_r32 = jnp.pad(w_router.astype(F32), ((0, 0), (0, LANES - c.E)))
    w_r_hi = w_r32.astype(BF16)
    w_r = jnp.concatenate([w_r_hi, (w_r32 - w_r_hi.astype(F32)).astype(BF16)], axis=1)
    b_r = jnp.pad(b_router.astype(F32), (0, LANES - c.E)).reshape(1, LANES)
    x2, hfp, idx, gates, counts = _xattn(
        c, x1, norm_x_g.reshape(1, D), w_xq.astype(BF16), kv, w_xo.astype(BF16),
        norm_ffn_g.reshape(1, D), w_r, b_r, LANES ** -0.5)

    order_pad, sched = _plan(c, idx[:, :c.K], counts[0, :c.E])
    ykt = _moe(c, sched, order_pad, hfp, w_gate_up, b_gate_up.reshape(c.E, 1, 2 * c.DFF),
               w_down, b_down.reshape(c.E, 1, D))
    out = _combine(c, x2, gates, norm_final_g.reshape(1, D), ykt)
    return out.reshape(B, S, D)


_CFG = Cfg(B=4, S=4096, D=2048, MEM=256, FH=8, SG=8, SGC=128, XH=4, E=32, K=4, DFF=2048,
           tm_in=1024, tn_in=1024, tq=2048, tm_mix=256, tm_x=512, tm_e=2560, ts_e=256, fc_e=512, nck_e=16, nst_e=4,
           tm_c=256, tc_cs=512, fox_parts=4)


@jax.jit
def kernel(x, mem, norm_mix_g, w_in, b_forget, sg_ln_g, sg_ln_b, w_spatial, b_spatial, w_branch_a, w_branch_b,
           w_out, norm_x_g, norm_mem_g, w_xq, w_xkv, w_xo, norm_ffn_g, w_router, b_router, w_gate_up,
           b_gate_up, w_down, b_down, norm_final_g):
    return _forward(_CFG, x, mem, norm_mix_g, w_in, b_forget, sg_ln_g, sg_ln_b, w_spatial, b_spatial,
                    w_branch_a, w_branch_b, w_out, norm_x_g, norm_mem_g, w_xq, w_xkv, w_xo,
                    norm_ffn_g, w_router, b_router, w_gate_up, b_gate_up, w_down, b_down, norm_final_g)
```

```python
import functools
from typing import NamedTuple

import jax
import jax.numpy as jnp
from jax import lax
from jax.experimental import pallas as pl
from jax.experimental.pallas import tpu as pltpu

F32 = jnp.float32
BF16 = jnp.bfloat16
U32 = jnp.uint32
I32 = jnp.int32

LANES = 128
VMEM_LIMIT = 56 * 1024 * 1024

RMS_EPS = 1e-6
LN_EPS = 1e-5
SWIGLU_LIMIT = 7.0
SWIGLU_ALPHA = 1.702
GELU_C = 0.7978845608028654
LOG2E = 1.4426950408889634
ID_CHUNK = 1024
WEIGHT_DMA_PRIORITY = 1


class Cfg(NamedTuple):
    B: int
    S: int
    D: int
    MEM: int
    FH: int
    SG: int
    SGC: int
    XH: int
    E: int
    K: int
    DFF: int
    tm_in: int
    tn_in: int
    tq: int
    tm_mix: int
    tm_x: int
    tm_e: int
    ts_e: int
    fc_e: int
    nck_e: int
    nst_e: int
    tm_c: int
    tc_cs: int
    fox_parts: int


def _cparams(sem):
    return pltpu.CompilerParams(dimension_semantics=sem, vmem_limit_bytes=VMEM_LIMIT)


def _rms(x, g):
    ms = jnp.mean(x * x, axis=-1, keepdims=True)
    return x * lax.rsqrt(ms + RMS_EPS) * g


def _pack_bf16_pair(a, b):
    def rne(v):
        bits = lax.bitcast_convert_type(v, U32)
        return bits + jnp.uint32(0x7FFF) + ((bits >> 16) & jnp.uint32(1))
    return (rne(a) >> 16) | (rne(b) & jnp.uint32(0xFFFF0000))


def _unpack_lo(p):
    return lax.bitcast_convert_type(p << 16, F32)


def _unpack_hi(p):
    return lax.bitcast_convert_type(p & jnp.uint32(0xFFFF0000), F32)


def _in_proj_kernel(x_ref, g_ref, w_ref, wf_ref, bf_ref, o_ref, f_ref, h_ref, *, nj_gate, nj_z, nj_q, qscale):
    j = pl.program_id(1)

    @pl.when(j == 0)
    def _():
        hb = _rms(x_ref[...], g_ref[...]).astype(BF16)
        h_ref[...] = hb
        f = jnp.dot(hb, wf_ref[...], preferred_element_type=F32) + bf_ref[...]
        f_ref[...] = jnp.minimum(f, 0.0) - jnp.log1p(jnp.exp(-jnp.abs(f)))

    acc = jnp.dot(h_ref[...], w_ref[...], preferred_element_type=F32)

    is_gate = j < nj_gate
    is_z = (j >= nj_gate) & (j < nj_gate + nj_z)
    is_q = (j >= nj_gate + nj_z) & (j < nj_gate + nj_z + nj_q)
    a1 = jnp.where(is_gate, 0.5, jnp.where(is_z, GELU_C, 0.0)).astype(F32)
    a3 = jnp.where(is_z, GELU_C * 0.044715, 0.0).astype(F32)
    b0 = jnp.where(is_gate, 0.5, 0.0).astype(F32)
    b1 = jnp.where(is_gate, 0.0, jnp.where(is_z, 0.5, jnp.where(is_q, qscale, 1.0))).astype(F32)
    th = jnp.tanh(acc * (a1 + a3 * (acc * acc)))
    o_ref[...] = ((b0 + b1 * acc) * (1.0 + th)).astype(o_ref.dtype)


def _in_proj(c, x2d, g, w_main, w_f, b_f, qscale):
    n, d = x2d.shape
    nc = w_main.shape[1]
    tm, tn = c.tm_in, c.tn_in
    fw = c.FH * LANES
    kern = functools.partial(_in_proj_kernel, nj_gate=2 * d // tn, nj_z=2 * c.SG * LANES // tn,
                             nj_q=fw // tn, qscale=qscale)
    return pl.pallas_call(
        kern,
        grid=(n // tm, nc // tn),
        in_specs=[
            pl.BlockSpec((tm, d), lambda i, j: (i, 0)),
            pl.BlockSpec((1, d), lambda i, j: (0, 0)),
            pl.BlockSpec((d, tn), lambda i, j: (0, j)),
            pl.BlockSpec((d, LANES), lambda i, j: (0, 0)),
            pl.BlockSpec((1, LANES), lambda i, j: (0, 0)),
        ],
        out_specs=[
            pl.BlockSpec((tm, tn), lambda i, j: (i, j)),
            pl.BlockSpec((tm, LANES), lambda i, j: (i, 0)),
        ],
        out_shape=[jax.ShapeDtypeStruct((n, nc), BF16), jax.ShapeDtypeStruct((n, LANES), F32)],
        scratch_shapes=[pltpu.VMEM((tm, d), BF16)],
        compiler_params=_cparams(("parallel", "arbitrary")),
        name="in_proj",
    )(x2d, g, w_main, w_f, b_f)


def _cumsum_kernel(f_ref, o_ref, *, tc):
    rows, s = f_ref.shape[1], f_ref.shape[2]
    r = lax.broadcasted_iota(I32, (tc, tc), 0)
    col = lax.broadcasted_iota(I32, (tc, tc), 1)
    upper = (r <= col).astype(F32)
    carry = jnp.zeros((rows, 1), F32)
    for i in range(s // tc):
        blk = f_ref[0, :, i * tc:(i + 1) * tc]
        cs = jnp.dot(blk, upper, preferred_element_type=F32, precision=lax.Precision.HIGHEST) + carry
        o_ref[0, :, i * tc:(i + 1) * tc] = cs * LOG2E
        carry = cs[:, tc - 1:tc]


def _cumsum(c, logf_bhs):
    b, h, s = logf_bhs.shape
    return pl.pallas_call(
        functools.partial(_cumsum_kernel, tc=c.tc_cs),
        grid=(b,),
        in_specs=[pl.BlockSpec((1, h, s), lambda i: (i, 0, 0))],
        out_specs=pl.BlockSpec((1, h, s), lambda i: (i, 0, 0)),
        out_shape=jax.ShapeDtypeStruct((b, h, s), F32),
        compiler_params=_cparams(("parallel",)),
        name="cumsum",
    )(logf_bhs)


def _fox_kernel(q_ref, k_ref, v_ref, crow_ref, ccol_ref, o_ref, m_ref, l_ref, acc_ref, cq_ref, s_ref,
                *, t, parts):
    qi = pl.program_id(2)
    hr = t // parts
    m_ref[...] = jnp.full(m_ref.shape, -jnp.inf, F32)
    l_ref[...] = jnp.zeros(l_ref.shape, F32)
    acc_ref[...] = jnp.zeros(acc_ref.shape, F32)
    cq_ref[...] = jnp.broadcast_to(ccol_ref[0], cq_ref.shape)

    def scores(ks):
        k = k_ref[pl.ds(ks, t), :]
        return lax.dot_general(q_ref[...], k, (((1,), (1,)), ((), ())), preferred_element_type=F32)

    def block(part, ks, width, masked):
        rows = slice(part * hr, (part + 1) * hr)
        v = v_ref[pl.ds(ks, width), :]
        s = s_ref[rows, :width]
        cq = cq_ref[rows, :]
        crow = crow_ref[0, :, pl.ds(ks, width)]
        nj = width // LANES
        sj = [s[:, j * LANES:(j + 1) * LANES] - crow[:, j * LANES:(j + 1) * LANES] for j in range(nj)]
        if masked:
            row = lax.broadcasted_iota(I32, (hr, LANES), 0) + part * hr
            col = lax.broadcasted_iota(I32, (hr, LANES), 1)
            sj = [jnp.where(col + j * LANES <= row, sj[j], -jnp.inf) for j in range(nj)]
        mx = sj[0]
        for j in range(1, nj):
            mx = jnp.maximum(mx, sj[j])
        m_prev = m_ref[rows, :]
        m_next = jnp.maximum(m_prev, jnp.max(mx, axis=-1, keepdims=True) + cq)
        alpha = jnp.exp2(m_prev - m_next)
        shift = m_next - cq
        pj = [jnp.exp2(sj[j] - shift) for j in range(nj)]
        psum = pj[0]
        for j in range(1, nj):
            psum = psum + pj[j]
        p = jnp.concatenate([x.astype(BF16) for x in pj], axis=-1)
        l_ref[rows, :] = alpha * l_ref[rows, :] + psum
        acc_ref[rows, :] = alpha * acc_ref[rows, :] + jnp.dot(p, v, preferred_element_type=F32)
        m_ref[rows, :] = m_next

    s_ref[...] = scores(0)

    def body(kc, carry):
        ks = pl.multiple_of(kc * t, t)
        s_next = scores(pl.multiple_of(ks + t, t))
        for part in range(parts):
            block(part, ks, t, False)
        s_ref[...] = s_next
        return carry

    lax.fori_loop(0, qi, body, 0)
    kd = pl.multiple_of(qi * t, t)
    for part in range(parts):
        block(part, kd, (part + 1) * hr, True)
    l = jnp.sum(l_ref[...], axis=-1, keepdims=True)
    o_ref[...] = (acc_ref[...] / l).astype(o_ref.dtype)


def _fox(c, proj, crow, ccol, qcol0):
    n = c.B * c.S
    t = c.tq
    nq = c.S // t
    h = c.FH
    return pl.pallas_call(
        functools.partial(_fox_kernel, t=t, parts=c.fox_parts),
        grid=(c.B, h, nq),
        in_specs=[
            pl.BlockSpec((t, LANES), lambda b, hh, qi: (b * nq + qi, qcol0 + hh)),
            pl.BlockSpec((c.S, LANES), lambda b, hh, qi: (b, qcol0 + h + hh)),
            pl.BlockSpec((c.S, LANES), lambda b, hh, qi: (b, qcol0 + 2 * h + hh)),
            pl.BlockSpec((1, 1, c.S), lambda b, hh, qi: (b * h + hh, 0, 0)),
            pl.BlockSpec((1, t, 1), lambda b, hh, qi: (b * h + hh, qi, 0)),
        ],
        out_specs=pl.BlockSpec((t, LANES), lambda b, hh, qi: (b * nq + qi, hh)),
        out_shape=jax.ShapeDtypeStruct((n, h * LANES), BF16),
        scratch_shapes=[pltpu.VMEM((t, LANES), F32)] * 4 + [pltpu.VMEM((t, t), F32)],
        compiler_params=_cparams(("parallel", "parallel", "arbitrary")),
        name="fox",
    )(proj, proj, proj, crow, ccol)


def _mix_kernel(x_ref, a_ref, u_ref, v_ref, ga_ref, gb_ref, lng_ref, lnb_ref, ws_ref, bs_ref,
                wa_ref, wb_ref, wo_ref, o_ref, sg_ref, *, sgc, groups):
    tm = x_ref.shape[0]
    v = v_ref[...].astype(F32)
    mu = jnp.mean(v, axis=-1, keepdims=True)
    vc = v - mu
    var = jnp.mean(vc * vc, axis=-1, keepdims=True)
    vn = (vc * lax.rsqrt(var + LN_EPS) * lng_ref[...] + lnb_ref[...]).astype(BF16)
    row = lax.broadcasted_iota(I32, (sgc, sgc), 0)
    col = lax.broadcasted_iota(I32, (sgc, sgc), 1)
    for g in range(groups):
        w = jnp.where(col <= row, ws_ref[g], 0.0).astype(BF16)
        bias = bs_ref[g]
        for ci in range(tm // sgc):
            rs = slice(ci * sgc, (ci + 1) * sgc)
            cs = slice(g * LANES, (g + 1) * LANES)
            mixed = jnp.dot(w, vn[rs, cs], preferred_element_type=F32) + bias
            sg_ref[rs, cs] = (u_ref[rs, cs].astype(F32) * mixed).astype(BF16)
    ya = jnp.dot(a_ref[...], wa_ref[...], preferred_element_type=F32)
    yb = jnp.dot(sg_ref[...], wb_ref[...], preferred_element_type=F32)
    merged = (ga_ref[...].astype(F32) * ya + gb_ref[...].astype(F32) * yb).astype(BF16)
    o_ref[...] = x_ref[...] + jnp.dot(merged, wo_ref[...], preferred_element_type=F32)


def _const_spec(shape):
    nd = len(shape)
    return pl.BlockSpec(shape, lambda i: (0,) * nd, pipeline_mode=pl.Buffered(1))


def _mix(c, x2d, attn, proj, ln_g, ln_b, w_s, b_s_full, w_a, w_b, w_o):
    n, d = x2d.shape
    tm = c.tm_mix
    fw = c.FH * LANES
    sw = c.SG * LANES
    ucol = 2 * d // sw
    return pl.pallas_call(
        functools.partial(_mix_kernel, sgc=c.SGC, groups=c.SG),
        grid=(n // tm,),
        in_specs=[
            pl.BlockSpec((tm, d), lambda i: (i, 0)),
            pl.BlockSpec((tm, fw), lambda i: (i, 0)),
            pl.BlockSpec((tm, sw), lambda i: (i, ucol)),
            pl.BlockSpec((tm, sw), lambda i: (i, ucol + 1)),
            pl.BlockSpec((tm, d), lambda i: (i, 0)),
            pl.BlockSpec((tm, d), lambda i: (i, 1)),
            _const_spec((1, sw)),
            _const_spec((1, sw)),
            _const_spec((c.SG, c.SGC, c.SGC)),
            _const_spec((c.SG, c.SGC, LANES)),
            _const_spec((fw, d)),
            _const_spec((sw, d)),
            _const_spec((d, d)),
        ],
        out_specs=pl.BlockSpec((tm, d), lambda i: (i, 0)),
        out_shape=jax.ShapeDtypeStruct((n, d), F32),
        scratch_shapes=[pltpu.VMEM((tm, sw), BF16)],
        compiler_params=_cparams(("parallel",)),
        name="mix",
    )(x2d, attn, proj, proj, proj, proj, ln_g, ln_b, w_s, b_s_full, w_a, w_b, w_o)


def _mem_kv_kernel(m_ref, g_ref, w_ref, o_ref):
    hm = _rms(m_ref[...], g_ref[...]).astype(BF16)
    o_ref[...] = jnp.dot(hm, w_ref[...], preferred_element_type=F32).astype(o_ref.dtype)


def _mem_kv(c, mem2d, g, w_xkv):
    n, d = mem2d.shape
    nc = w_xkv.shape[1]
    tm = c.MEM
    return pl.pallas_call(
        _mem_kv_kernel,
        grid=(n // tm,),
        in_specs=[pl.BlockSpec((tm, d), lambda i: (i, 0)), _const_spec((1, d)), _const_spec((d, nc))],
        out_specs=pl.BlockSpec((tm, nc), lambda i: (i, 0)),
        out_shape=jax.ShapeDtypeStruct((n, nc), BF16),
        compiler_params=_cparams(("parallel",)),
        name="mem_kv",
    )(mem2d, g, w_xkv)


def _xattn_kernel(x1_ref, gx_ref, wq_ref, kv_ref, wo_ref, gf_ref, wr_ref, br_ref,
                  x2_ref, hfp_ref, idx_ref, gate_ref, cnt_ref, carry_ref,
                  *, heads, n_exp, topk, qscale):
    i = pl.program_id(0)
    tm, d = x1_ref.shape
    xw = heads * LANES
    x1 = x1_ref[...]
    hx = _rms(x1, gx_ref[...]).astype(BF16)
    q = (jnp.dot(hx, wq_ref[...], preferred_element_type=F32) * qscale).astype(BF16)
    outs = []
    for h in range(heads):
        k = kv_ref[:, h * LANES:(h + 1) * LANES]
        v = kv_ref[:, xw + h * LANES:xw + (h + 1) * LANES]
        s = lax.dot_general(q[:, h * LANES:(h + 1) * LANES], k, (((1,), (1,)), ((), ())),
                            preferred_element_type=F32)
        s = s - jnp.max(s, axis=-1, keepdims=True)
        p = jnp.exp(s)
        p = p / jnp.sum(p, axis=-1, keepdims=True)
        outs.append(jnp.dot(p.astype(BF16), v, preferred_element_type=F32).astype(BF16))
    o = jnp.concatenate(outs, axis=-1)
    x2 = x1 + jnp.dot(o, wo_ref[...], preferred_element_type=F32)
    x2_ref[...] = x2

    hf = _rms(x2, gf_ref[...])
    hfp_ref[...] = _pack_bf16_pair(hf[:, :d // 2], hf[:, d // 2:])

    h_hi = hf.astype(BF16)
    h_lo = (hf - h_hi.astype(F32)).astype(BF16)
    l_hi = jnp.dot(h_hi, wr_ref[...], preferred_element_type=F32)
    l_lo = jnp.dot(h_lo, wr_ref[:, :LANES], preferred_element_type=F32)
    logits = l_hi[:, :LANES] + l_hi[:, LANES:] + l_lo + br_ref[...]
    lane = lax.broadcasted_iota(I32, (tm, LANES), 1)
    lg = jnp.where(lane < n_exp, logits, -jnp.inf)
    vals, idxs = [], []
    for _ in range(topk):
        m = jnp.max(lg, axis=-1, keepdims=True)
        ix = jnp.min(jnp.where(lg == m, lane, LANES), axis=-1, keepdims=True)
        vals.append(m)
        idxs.append(ix)
        lg = jnp.where(lane == ix, -jnp.inf, lg)
    es = [jnp.exp(vv - vals[0]) for vv in vals]
    denom = es[0]
    for e in es[1:]:
        denom = denom + e

    @pl.when(i == 0)
    def _():
        carry_ref[...] = jnp.zeros(carry_ref.shape, F32)

    onehot = jnp.zeros((tm, LANES), F32)
    for ix in idxs:
        onehot = onehot + (lane == ix).astype(F32)
    idx_out = jnp.zeros((tm, LANES), I32)
    gate_out = jnp.zeros((tm, LANES), F32)
    for kk in range(topk):
        idx_out = jnp.where(lane == kk, idxs[kk], idx_out)
        gate_out = jnp.where(lane == kk, es[kk] / denom, gate_out)
    idx_ref[...] = idx_out
    gate_ref[...] = gate_out
    carry_ref[...] = carry_ref[...] + jnp.sum(onehot, axis=0, keepdims=True)
    cnt_ref[...] = carry_ref[...]


def _xattn(c, x1, gx, w_xq, kv, w_xo, gf, w_r, b_r, qscale):
    n, d = x1.shape
    tm = c.tm_x
    xw = c.XH * LANES
    per_b = c.S // tm
    kern = functools.partial(_xattn_kernel, heads=c.XH, n_exp=c.E, topk=c.K, qscale=qscale)
    return pl.pallas_call(
        kern,
        grid=(n // tm,),
        in_specs=[
            pl.BlockSpec((tm, d), lambda i: (i, 0)),
            _const_spec((1, d)),
            _const_spec((d, xw)),
            pl.BlockSpec((c.MEM, 2 * xw), lambda i: (i // per_b, 0)),
            _const_spec((xw, d)),
            _const_spec((1, d)),
            _const_spec((d, 2 * LANES)),
            _const_spec((1, LANES)),
        ],
        out_specs=[
            pl.BlockSpec((tm, d), lambda i: (i, 0)),
            pl.BlockSpec((tm, d // 2), lambda i: (i, 0)),
            pl.BlockSpec((tm, LANES), lambda i: (i, 0)),
            pl.BlockSpec((tm, LANES), lambda i: (i, 0)),
            pl.BlockSpec((1, LANES), lambda i: (0, 0)),
        ],
        out_shape=[
            jax.ShapeDtypeStruct((n, d), F32),
            jax.ShapeDtypeStruct((n, d // 2), U32),
            jax.ShapeDtypeStruct((n, LANES), I32),
            jax.ShapeDtypeStruct((n, LANES), F32),
            jax.ShapeDtypeStruct((1, LANES), F32),
        ],
        scratch_shapes=[pltpu.VMEM((1, LANES), F32)],
        compiler_params=_cparams(("arbitrary",)),
        name="xattn_router",
    )(x1, gx, w_xq, kv, w_xo, gf, w_r, b_r)


def _moe_kernel(te_ref, ns_ref, cpi_ref, pos_ref, rows_ref, ord_ref, hfp_ref, wgu_ref, wd_ref, bgu_ref, bd_ref,
                ykt_ref, wgub, wdb, stage, act, xin, ybuf, ids, wsem, xsem, ysem, isem,
                *, ts, nck, fc, n_tok, n_asg):
    i = pl.program_id(0)
    n = pl.num_programs(0)
    nsub = ns_ref[i]
    e = te_ref[i]
    cpi = cpi_ref[i]
    rows = rows_ref[i]
    d, f2 = wgub.shape
    f = f2 // 2
    dh = d // 2
    ckr = d // nck
    ckd = f // nck
    nst = stage.shape[0]
    idw = ids.shape[0] // 2
    nq = idw // ID_CHUNK
    id_base = (i % 2) * idw + (pos_ref[i] & (ID_CHUNK - 1))

    def ids_copy(item, q, slot):
        c0 = lax.shift_right_logical(pos_ref[item], ID_CHUNK.bit_length() - 1)
        src = ord_ref.at[pl.ds(pl.multiple_of((c0 + q) * ID_CHUNK, ID_CHUNK), ID_CHUNK)]
        dst = ids.at[pl.ds(pl.multiple_of(slot * idw + q * ID_CHUNK, ID_CHUNK), ID_CHUNK)]
        return pltpu.make_async_copy(src, dst, isem.at[slot])

    def token_of(a):
        return a & (n_tok - 1) if n_tok & (n_tok - 1) == 0 else lax.rem(a, n_tok)

    def gather(s, slot, r0=0, r1=ts, base=None):
        base = id_base if base is None else base
        for r in range(r0, r1):
            tok = token_of(ids[base + s * ts + r])
            pltpu.make_async_copy(hfp_ref.at[pl.ds(tok, 1)], xin.at[slot, pl.ds(r, 1)], xsem.at[slot]).start()

    def gather_wait(slot):
        for r in range(ts):
            pltpu.make_async_copy(hfp_ref.at[pl.ds(0, 1)], xin.at[slot, pl.ds(0, 1)], xsem.at[slot]).wait()

    def scatter(s, slot, r0=0, r1=ts, base=None, nrows=None):
        base = id_base if base is None else base
        nrows = rows if nrows is None else nrows
        for r in range(r0, r1):
            g = s * ts + r
            dst = jnp.where(g < nrows, ids[base + g], n_asg + slot * ts + r)
            pltpu.make_async_copy(ybuf.at[slot, pl.ds(r, 1)], ykt_ref.at[pl.ds(dst, 1)], ysem.at[slot]).start()

    def scatter_wait(slot):
        for r in range(ts):
            pltpu.make_async_copy(ybuf.at[slot, pl.ds(0, 1)], ykt_ref.at[pl.ds(0, 1)], ysem.at[slot]).wait()

    def gu_copy(ee, c, slot):
        return pltpu.make_async_copy(wgu_ref.at[ee, pl.ds(pl.multiple_of(c * ckr, ckr), ckr), :],
                                     stage.at[slot], wsem.at[slot])

    def d_copy(ee, c, slot):
        return pltpu.make_async_copy(wd_ref.at[ee, pl.ds(pl.multiple_of(c * ckd, ckd), ckd), :],
                                     stage.at[slot, pl.ds(0, ckd), pl.ds(0, d)], wsem.at[slot])

    def gu_step(ee, c):
        slot = c % nst
        gu_copy(ee, c, slot).wait()
        wgub[pl.ds(pl.multiple_of(c * ckr, ckr), ckr), :] = stage[slot].astype(BF16)

        @pl.when(c + nst < nck)
        def _():
            gu_copy(ee, c + nst, slot).start(priority=WEIGHT_DMA_PRIORITY)

    def d_step(ee, c):
        slot = c % nst
        d_copy(ee, c, slot).wait()
        wdb[pl.ds(pl.multiple_of(c * ckd, ckd), ckd), :] = stage[slot, :ckd, :d].astype(BF16)

        @pl.when(c + nst < nck)
        def _():
            d_copy(ee, c + nst, slot).start(priority=WEIGHT_DMA_PRIORITY)

    def prime(copy_fn, ee):
        for q in range(min(nst, nck)):
            copy_fn(ee, q, q).start(priority=WEIGHT_DMA_PRIORITY)

    def steps_after(s, step_fn, ee):
        def one(jj, carry):
            c = s * cpi + jj

            @pl.when(c < nck)
            def _():
                step_fn(ee, c)
            return carry
        lax.fori_loop(0, cpi, one, 0)

    @pl.when(nsub > 0)
    def _():
        nxt = jnp.minimum(i + 1, n - 1)
        has_next = (i + 1 < n) & (ns_ref[nxt] > 0)

        @pl.when(i == 0)
        def _():
            for q in range(nq):
                ids_copy(0, q, 0).start()
            prime(gu_copy, e)

            def first(c, carry):
                gu_step(e, c)
                return carry
            lax.fori_loop(0, nck, first, 0)
            for q in range(nq):
                ids_copy(0, q, 0).wait()
            gather(0, 0)
            ybuf[...] = jnp.zeros(ybuf.shape, ybuf.dtype)

        prv = jnp.maximum(i - 1, 0)
        nsub_p = jnp.where(i > 0, ns_ref[prv], 1)
        rows_p = jnp.where(i > 0, rows_ref[prv], 0)
        nslot = 1 - i % 2
        base_p = jnp.where(i > 0, nslot * idw + (pos_ref[prv] & (ID_CHUNK - 1)), id_base)
        lslot_p = (nsub_p - 1) % 2

        prime(d_copy, e)
        bgu = bgu_ref[e]

        def gate_up(s, scatter_prev):
            slot = s % 2
            gather_wait(slot)
            p = xin[slot]
            xb = jnp.concatenate([_unpack_lo(p).astype(BF16), _unpack_hi(p).astype(BF16)], axis=-1)
            nc = f // fc
            for cc in range(nc):
                gather(s + 1, 1 - slot, cc * ts // nc, (cc + 1) * ts // nc)
                if scatter_prev:
                    scatter(nsub_p - 1, lslot_p, cc * ts // nc, (cc + 1) * ts // nc, base=base_p, nrows=rows_p)
                gate = jnp.dot(xb, wgub[:, cc * fc:(cc + 1) * fc], preferred_element_type=F32)
                gate = gate + bgu[:, cc * fc:(cc + 1) * fc]
                up = jnp.dot(xb, wgub[:, f + cc * fc:f + (cc + 1) * fc], preferred_element_type=F32)
                up = up + bgu[:, f + cc * fc:f + (cc + 1) * fc]
                gate = jnp.minimum(gate, SWIGLU_LIMIT)
                up = jnp.clip(up, -SWIGLU_LIMIT, SWIGLU_LIMIT)
                a = (up + 1.0) * gate * jax.nn.sigmoid(SWIGLU_ALPHA * gate)
                act[s, :, cc * fc:(cc + 1) * fc] = a.astype(BF16)
            steps_after(s, d_step, e)

        gate_up(0, True)

        for q in range(nq):
            ids_copy(nxt, q, nslot).start()

        def phase_a(s, carry):
            gate_up(s, False)
            return carry

        lax.fori_loop(1, nsub, phase_a, 0)
        gather_wait(nsub % 2)

        e_next = te_ref[nxt]
        load_next = has_next & (e_next != e)

        @pl.when(load_next)
        def _():
            prime(gu_copy, e_next)

        bd = bd_ref[e]

        def down(s, slot, scatter_prev):
            a = act[s]
            if scatter_prev:
                scatter(s - 1, 1 - slot)
            fcb = min(fc, dh // 2)
            for cc in range(dh // fcb):
                lo = slice(cc * fcb, (cc + 1) * fcb)
                hi = slice(dh + cc * fcb, dh + (cc + 1) * fcb)
                y_lo = jnp.dot(a, wdb[:, lo], preferred_element_type=F32) + bd[:, lo]
                y_hi = jnp.dot(a, wdb[:, hi], preferred_element_type=F32) + bd[:, hi]
                ybuf[slot, :, lo] = _pack_bf16_pair(y_lo, y_hi)

        def next_weights(s):
            @pl.when(load_next)
            def _():
                steps_after(s, gu_step, e_next)

        scatter_wait(lslot_p)

        @pl.when(nsub_p >= 2)
        def _():
            scatter_wait(1 - lslot_p)

        for q in range(nq):
            ids_copy(nxt, q, nslot).wait()
        gather(0, 0, base=nslot * idw + (pos_ref[nxt] & (ID_CHUNK - 1)))
        down(0, 0, False)
        next_weights(0)

        def phase_b(s, carry):
            slot = s % 2

            @pl.when(s >= 2)
            def _():
                scatter_wait(slot)

            down(s, slot, True)
            next_weights(s)
            return carry

        lax.fori_loop(1, nsub, phase_b, 0)

        @pl.when(jnp.logical_not(has_next))
        def _():
            scatter(nsub - 1, (nsub - 1) % 2)
            scatter_wait((nsub - 1) % 2)

            @pl.when(nsub >= 2)
            def _():
                scatter_wait(nsub % 2)

            gather_wait(0)


def _id_window(c):
    return ((ID_CHUNK - 1 + c.tm_e + c.ts_e) // ID_CHUNK + 1) * ID_CHUNK


def _moe(c, sched, order_pad, hfp, w_gu, b_gu3, w_d, b_d3):
    n_tok, dh = hfp.shape
    d = 2 * dh
    f = c.DFF
    tm, ts, nck = c.tm_e, c.ts_e, c.nck_e
    n_asg = n_tok * c.K
    grid_spec = pltpu.PrefetchScalarGridSpec(
        num_scalar_prefetch=5,
        grid=(_n_tiles(c),),
        in_specs=[
            pl.BlockSpec(memory_space=pl.ANY),
            pl.BlockSpec(memory_space=pl.ANY),
            pl.BlockSpec(memory_space=pl.ANY),
            pl.BlockSpec(memory_space=pl.ANY),
            pl.BlockSpec((c.E, 1, 2 * f), lambda i, *_: (0, 0, 0), pipeline_mode=pl.Buffered(1)),
            pl.BlockSpec((c.E, 1, d), lambda i, *_: (0, 0, 0), pipeline_mode=pl.Buffered(1)),
        ],
        out_specs=pl.BlockSpec(memory_space=pl.ANY),
        scratch_shapes=[
            pltpu.VMEM((d, 2 * f), BF16),
            pltpu.VMEM((f, d), BF16),
            pltpu.VMEM((c.nst_e, d // nck, 2 * f), F32),
            pltpu.VMEM((tm // ts, ts, f), BF16),
            pltpu.VMEM((2, ts, dh), U32),
            pltpu.VMEM((2, ts, dh), U32),
            pltpu.SMEM((2 * _id_window(c),), I32),
            pltpu.SemaphoreType.DMA((c.nst_e,)),
            pltpu.SemaphoreType.DMA((2,)),
            pltpu.SemaphoreType.DMA((2,)),
            pltpu.SemaphoreType.DMA((2,)),
        ],
    )
    kern = functools.partial(_moe_kernel, ts=ts, nck=nck, fc=c.fc_e, n_tok=n_tok, n_asg=n_asg)
    return pl.pallas_call(
        kern,
        grid_spec=grid_spec,
        out_shape=jax.ShapeDtypeStruct((n_asg + 2 * ts, dh), U32),
        compiler_params=_cparams(("arbitrary",)),
        name="experts",
    )(*sched, order_pad, hfp, w_gu, w_d, b_gu3, b_d3)


def _combine_kernel(x2_ref, gate_ref, g_ref, *refs, topk):
    y_refs, o_ref = refs[:topk], refs[topk]
    d = x2_ref.shape[1]
    dh = d // 2
    lo = x2_ref[:, :dh]
    hi = x2_ref[:, dh:]
    gates = gate_ref[...]
    for kk in range(topk):
        wk = gates[:, kk:kk + 1]
        p = y_refs[kk][...]
        lo = lo + wk * _unpack_lo(p)
        hi = hi + wk * _unpack_hi(p)
    ms = (jnp.sum(lo * lo, axis=-1, keepdims=True) + jnp.sum(hi * hi, axis=-1, keepdims=True)) / d
    inv = lax.rsqrt(ms + RMS_EPS)
    o_ref[:, :dh] = lo * inv * g_ref[:, :dh]
    o_ref[:, dh:] = hi * inv * g_ref[:, dh:]


def _combine(c, x2, gates, g, ykt):
    n, d = x2.shape
    tm = c.tm_c
    nb = n // tm
    y_specs = [pl.BlockSpec((tm, d // 2), functools.partial(lambda i, kk: (kk * nb + i, 0), kk=kk))
               for kk in range(c.K)]
    return pl.pallas_call(
        functools.partial(_combine_kernel, topk=c.K),
        grid=(nb,),
        in_specs=[
            pl.BlockSpec((tm, d), lambda i: (i, 0)),
            pl.BlockSpec((tm, LANES), lambda i: (i, 0)),
            pl.BlockSpec((1, d), lambda i: (0, 0)),
        ] + y_specs,
        out_specs=pl.BlockSpec((tm, d), lambda i: (i, 0)),
        out_shape=jax.ShapeDtypeStruct((n, d), F32),
        compiler_params=_cparams(("parallel",)),
        name="combine",
    )(x2, gates, g, *([ykt] * c.K))


def _plan(c, idx, counts):
    tm, ts = c.tm_e, c.ts_e
    nk = idx.shape[0] * c.K
    order = jnp.argsort(idx.T.reshape(-1)).astype(I32)
    pad = _id_window(c) + (-nk) % ID_CHUNK
    order_pad = jnp.concatenate([order, jnp.zeros((pad,), I32)])
    counts = counts.astype(I32)
    start = jnp.cumsum(counts) - counts
    tiles_per_e = (counts + tm - 1) // tm
    tile_end = jnp.cumsum(tiles_per_e)
    tile_start = tile_end - tiles_per_e
    t = jnp.arange(_n_tiles(c), dtype=I32)
    n_used = tile_end[-1]
    tc = jnp.minimum(t, n_used - 1)
    te = jnp.minimum(jnp.searchsorted(tile_end, tc, side="right"), c.E - 1).astype(I32)
    j = tc - tile_start[te]
    used = t < n_used
    rows = jnp.where(used, jnp.clip(counts[te] - j * tm, 0, tm), 0).astype(I32)
    nsub = (rows + ts - 1) // ts
    pos = jnp.where(used, start[te] + j * tm, 0).astype(I32)
    cpi = (c.nck_e + jnp.maximum(nsub, 1) - 1) // jnp.maximum(nsub, 1)
    return order_pad, (te, nsub.astype(I32), cpi.astype(I32), pos, rows)


def _n_tiles(c):
    return -(-(c.B * c.S * c.K) // c.tm_e) + c.E


def _forward(c, x, mem, norm_mix_g, w_in, b_forget, sg_ln_g, sg_ln_b, w_spatial, b_spatial,
             w_branch_a, w_branch_b, w_out, norm_x_g, norm_mem_g, w_xq, w_xkv, w_xo,
             norm_ffn_g, w_router, b_router, w_gate_up, b_gate_up, w_down, b_down, norm_final_g):
    B, S, D = x.shape
    n = B * S
    fw = c.FH * LANES
    sw = c.SG * LANES
    x2d = x.reshape(n, D)

    o_f = 3 * fw
    o_z = o_f + c.FH
    o_g = o_z + 2 * sw
    w_main = jnp.concatenate([w_in[:, o_g:], w_in[:, o_z:o_g], w_in[:, :o_f]], axis=1).astype(BF16)
    w_f = jnp.pad(w_in[:, o_f:o_z], ((0, 0), (0, LANES - c.FH))).astype(BF16)
    b_f = jnp.pad(b_forget.astype(F32), (0, LANES - c.FH)).reshape(1, LANES)
    proj, logf = _in_proj(c, x2d, norm_mix_g.reshape(1, D), w_main, w_f, b_f, LOG2E * LANES ** -0.5)

    logf_bhs = logf[:, :c.FH].reshape(B, S, c.FH).transpose(0, 2, 1)
    csum = _cumsum(c, logf_bhs)
    crow = csum.reshape(B * c.FH, 1, S)
    ccol = csum.reshape(B * c.FH, S, 1)
    qcol0 = (2 * D + 2 * sw) // LANES
    attn = _fox(c, proj, crow, ccol, qcol0)

    b_s_full = jnp.broadcast_to(b_spatial.astype(F32)[:, :, None], (c.SG, c.SGC, LANES))
    x1 = _mix(c, x2d, attn, proj, sg_ln_g.reshape(1, sw), sg_ln_b.reshape(1, sw), w_spatial, b_s_full,
              w_branch_a.astype(BF16), w_branch_b.astype(BF16), w_out.astype(BF16))

    kv = _mem_kv(c, mem.reshape(B * c.MEM, D), norm_mem_g.reshape(1, D), w_xkv.astype(BF16))
    w_r32 = jnp.pad(w_router.astype(F32), ((0, 0), (0, LANES - c.E)))
    w_r_hi = w_r32.astype(BF16)
    w_r = jnp.concatenate([w_r_hi, (w_r32 - w_r_hi.astype(F32)).astype(BF16)], axis=1)
    b_r = jnp.pad(b_router.astype(F32), (0, LANES - c.E)).reshape(1, LANES)
    x2, hfp, idx, gates, counts = _xattn(
        c, x1, norm_x_g.reshape(1, D), w_xq.astype(BF16), kv, w_xo.astype(BF16),
        norm_ffn_g.reshape(1, D), w_r, b_r, LANES ** -0.5)

    order_pad, sched = _plan(c, idx[:, :c.K], counts[0, :c.E])
    ykt = _moe(c, sched, order_pad, hfp, w_gate_up, b_gate_up.reshape(c.E, 1, 2 * c.DFF),
               w_down, b_down.reshape(c.E, 1, D))
    out = _combine(c, x2, gates, norm_final_g.reshape(1, D), ykt)
    return out.reshape(B, S, D)


_CFG = Cfg(B=4, S=4096, D=2048, MEM=256, FH=8, SG=8, SGC=128, XH=4, E=32, K=4, DFF=2048,
           tm_in=1024, tn_in=1024, tq=1024, tm_mix=256, tm_x=512, tm_e=2560, ts_e=256, fc_e=512, nck_e=16, nst_e=4,
           tm_c=256, tc_cs=512, fox_parts=2)


@jax.jit
def kernel(x, mem, norm_mix_g, w_in, b_forget, sg_ln_g, sg_ln_b, w_spatial, b_spatial, w_branch_a, w_branch_b,
           w_out, norm_x_g, norm_mem_g, w_xq, w_xkv, w_xo, norm_ffn_g, w_router, b_router, w_gate_up,
           b_gate_up, w_down, b_down, norm_final_g):
    return _forward(_CFG, x, mem, norm_mix_g, w_in, b_forget, sg_ln_g, sg_ln_b, w_spatial, b_spatial,
                    w_branch_a, w_branch_b, w_out, norm_x_g, norm_mem_g, w_xq, w_xkv, w_xo,
                    norm_ffn_g, w_router, b_router, w_gate_up, b_gate_up, w_down, b_down, norm_final_g)
```

```python
import functools
from typing import NamedTuple

import jax
import jax.numpy as jnp
from jax import lax
from jax.experimental import pallas as pl
from jax.experimental.pallas import tpu as pltpu

F32 = jnp.float32
BF16 = jnp.bfloat16
U32 = jnp.uint32
I32 = jnp.int32

LANES = 128
VMEM_LIMIT = 56 * 1024 * 1024

RMS_EPS = 1e-6
LN_EPS = 1e-5
SWIGLU_LIMIT = 7.0
SWIGLU_ALPHA = 1.702
GELU_C = 0.7978845608028654
LOG2E = 1.4426950408889634
ID_CHUNK = 1024
WEIGHT_DMA_PRIORITY = 1


class Cfg(NamedTuple):
    B: int
    S: int
    D: int
    MEM: int
    FH: int
    SG: int
    SGC: int
    XH: int
    E: int
    K: int
    DFF: int
    tm_in: int
    tn_in: int
    tq: int
    tm_mix: int
    tm_x: int
    tm_e: int
    ts_e: int
    fc_e: int
    nck_e: int
    nst_e: int
    tm_c: int
    tc_cs: int
    fox_parts: int


def _cparams(sem):
    return pltpu.CompilerParams(dimension_semantics=sem, vmem_limit_bytes=VMEM_LIMIT)


def _rms(x, g):
    ms = jnp.mean(x * x, axis=-1, keepdims=True)
    return x * lax.rsqrt(ms + RMS_EPS) * g


def _pack_bf16_pair(a, b):
    def rne(v):
        bits = lax.bitcast_convert_type(v, U32)
        return bits + jnp.uint32(0x7FFF) + ((bits >> 16) & jnp.uint32(1))
    return (rne(a) >> 16) | (rne(b) & jnp.uint32(0xFFFF0000))


def _unpack_lo(p):
    return lax.bitcast_convert_type(p << 16, F32)


def _unpack_hi(p):
    return lax.bitcast_convert_type(p & jnp.uint32(0xFFFF0000), F32)


def _in_proj_kernel(x_ref, g_ref, w_ref, wf_ref, bf_ref, o_ref, f_ref, h_ref, *, nj_gate, nj_z, nj_q, qscale):
    j = pl.program_id(1)

    @pl.when(j == 0)
    def _():
        hb = _rms(x_ref[...], g_ref[...]).astype(BF16)
        h_ref[...] = hb
        f = jnp.dot(hb, wf_ref[...], preferred_element_type=F32) + bf_ref[...]
        f_ref[...] = jnp.minimum(f, 0.0) - jnp.log1p(jnp.exp(-jnp.abs(f)))

    acc = jnp.dot(h_ref[...], w_ref[...], preferred_element_type=F32)

    is_gate = j < nj_gate
    is_z = (j >= nj_gate) & (j < nj_gate + nj_z)
    is_q = (j >= nj_gate + nj_z) & (j < nj_gate + nj_z + nj_q)
    a1 = jnp.where(is_gate, 0.5, jnp.where(is_z, GELU_C, 0.0)).astype(F32)
    a3 = jnp.where(is_z, GELU_C * 0.044715, 0.0).astype(F32)
    b0 = jnp.where(is_gate, 0.5, 0.0).astype(F32)
    b1 = jnp.where(is_gate, 0.0, jnp.where(is_z, 0.5, jnp.where(is_q, qscale, 1.0))).astype(F32)
    th = jnp.tanh(acc * (a1 + a3 * (acc * acc)))
    o_ref[...] = ((b0 + b1 * acc) * (1.0 + th)).astype(o_ref.dtype)


def _in_proj(c, x2d, g, w_main, w_f, b_f, qscale):
    n, d = x2d.shape
    nc = w_main.shape[1]
    tm, tn = c.tm_in, c.tn_in
    fw = c.FH * LANES
    kern = functools.partial(_in_proj_kernel, nj_gate=2 * d // tn, nj_z=2 * c.SG * LANES // tn,
                             nj_q=fw // tn, qscale=qscale)
    return pl.pallas_call(
        kern,
        grid=(n // tm, nc // tn),
        in_specs=[
            pl.BlockSpec((tm, d), lambda i, j: (i, 0)),
            pl.BlockSpec((1, d), lambda i, j: (0, 0)),
            pl.BlockSpec((d, tn), lambda i, j: (0, j)),
            pl.BlockSpec((d, LANES), lambda i, j: (0, 0)),
            pl.BlockSpec((1, LANES), lambda i, j: (0, 0)),
        ],
        out_specs=[
            pl.BlockSpec((tm, tn), lambda i, j: (i, j)),
            pl.BlockSpec((tm, LANES), lambda i, j: (i, 0)),
        ],
        out_shape=[jax.ShapeDtypeStruct((n, nc), BF16), jax.ShapeDtypeStruct((n, LANES), F32)],
        scratch_shapes=[pltpu.VMEM((tm, d), BF16)],
        compiler_params=_cparams(("parallel", "arbitrary")),
        name="in_proj",
    )(x2d, g, w_main, w_f, b_f)


def _cumsum_kernel(f_ref, o_ref, *, tc):
    rows, s = f_ref.shape[1], f_ref.shape[2]
    r = lax.broadcasted_iota(I32, (tc, tc), 0)
    col = lax.broadcasted_iota(I32, (tc, tc), 1)
    upper = (r <= col).astype(F32)
    carry = jnp.zeros((rows, 1), F32)
    for i in range(s // tc):
        blk = f_ref[0, :, i * tc:(i + 1) * tc]
        cs = jnp.dot(blk, upper, preferred_element_type=F32, precision=lax.Precision.HIGHEST) + carry
        o_ref[0, :, i * tc:(i + 1) * tc] = cs * LOG2E
        carry = cs[:, tc - 1:tc]


def _cumsum(c, logf_bhs):
    b, h, s = logf_bhs.shape
    return pl.pallas_call(
        functools.partial(_cumsum_kernel, tc=c.tc_cs),
        grid=(b,),
        in_specs=[pl.BlockSpec((1, h, s), lambda i: (i, 0, 0))],
        out_specs=pl.BlockSpec((1, h, s), lambda i: (i, 0, 0)),
        out_shape=jax.ShapeDtypeStruct((b, h, s), F32),
        compiler_params=_cparams(("parallel",)),
        name="cumsum",
    )(logf_bhs)


def _fox_kernel(q_ref, k_ref, v_ref, crow_ref, ccol_ref, o_ref, m_ref, l_ref, acc_ref, cq_ref, s_ref,
                *, t, parts):
    qi = pl.program_id(2)
    hr = t // parts
    m_ref[...] = jnp.full(m_ref.shape, -jnp.inf, F32)
    l_ref[...] = jnp.zeros(l_ref.shape, F32)
    acc_ref[...] = jnp.zeros(acc_ref.shape, F32)
    cq_ref[...] = jnp.broadcast_to(ccol_ref[0], cq_ref.shape)

    def scores(ks):
        k = k_ref[pl.ds(ks, t), :]
        return lax.dot_general(q_ref[...], k, (((1,), (1,)), ((), ())), preferred_element_type=F32)

    def block(part, ks, width, masked):
        rows = slice(part * hr, (part + 1) * hr)
        v = v_ref[pl.ds(ks, width), :]
        s = s_ref[rows, :width]
        cq = cq_ref[rows, :]
        crow = crow_ref[0, :, pl.ds(ks, width)]
        nj = width // LANES
        sj = [s[:, j * LANES:(j + 1) * LANES] + cq - crow[:, j * LANES:(j + 1) * LANES] for j in range(nj)]
        if masked:
            row = lax.broadcasted_iota(I32, (hr, LANES), 0) + part * hr
            col = lax.broadcasted_iota(I32, (hr, LANES), 1)
            sj = [jnp.where(col + j * LANES <= row, sj[j], -jnp.inf) for j in range(nj)]
        mx = sj[0]
        for j in range(1, nj):
            mx = jnp.maximum(mx, sj[j])
        m_prev = m_ref[rows, :]
        m_next = jnp.maximum(m_prev, jnp.max(mx, axis=-1, keepdims=True))
        alpha = jnp.exp2(m_prev - m_next)
        pj = [jnp.exp2(sj[j] - m_next) for j in range(nj)]
        psum = pj[0]
        for j in range(1, nj):
            psum = psum + pj[j]
        p = jnp.concatenate([x.astype(BF16) for x in pj], axis=-1)
        l_ref[rows, :] = alpha * l_ref[rows, :] + psum
        acc_ref[rows, :] = alpha * acc_ref[rows, :] + jnp.dot(p, v, preferred_element_type=F32)
        m_ref[rows, :] = m_next

    s_ref[...] = scores(0)

    def body(kc, carry):
        ks = pl.multiple_of(kc * t, t)
        s_next = scores(pl.multiple_of(ks + t, t))
        for part in range(parts):
            block(part, ks, t, False)
        s_ref[...] = s_next
        return carry

    lax.fori_loop(0, qi, body, 0)
    kd = pl.multiple_of(qi * t, t)
    for part in range(parts):
        block(part, kd, (part + 1) * hr, True)
    l = jnp.sum(l_ref[...], axis=-1, keepdims=True)
    o_ref[...] = (acc_ref[...] / l).astype(o_ref.dtype)


def _fox(c, proj, crow, ccol, qcol0):
    n = c.B * c.S
    t = c.tq
    nq = c.S // t
    h = c.FH
    return pl.pallas_call(
        functools.partial(_fox_kernel, t=t, parts=c.fox_parts),
        grid=(c.B, h, nq),
        in_specs=[
            pl.BlockSpec((t, LANES), lambda b, hh, qi: (b * nq + qi, qcol0 + hh)),
            pl.BlockSpec((c.S, LANES), lambda b, hh, qi: (b, qcol0 + h + hh)),
            pl.BlockSpec((c.S, LANES), lambda b, hh, qi: (b, qcol0 + 2 * h + hh)),
            pl.BlockSpec((1, 1, c.S), lambda b, hh, qi: (b * h + hh, 0, 0)),
            pl.BlockSpec((1, t, 1), lambda b, hh, qi: (b * h + hh, qi, 0)),
        ],
        out_specs=pl.BlockSpec((t, LANES), lambda b, hh, qi: (b * nq + qi, hh)),
        out_shape=jax.ShapeDtypeStruct((n, h * LANES), BF16),
        scratch_shapes=[pltpu.VMEM((t, LANES), F32)] * 4 + [pltpu.VMEM((t, t), F32)],
        compiler_params=_cparams(("parallel", "parallel", "arbitrary")),
        name="fox",
    )(proj, proj, proj, crow, ccol)


def _mix_kernel(x_ref, a_ref, u_ref, v_ref, ga_ref, gb_ref, lng_ref, lnb_ref, ws_ref, bs_ref,
                wa_ref, wb_ref, wo_ref, o_ref, sg_ref, *, sgc, groups):
    tm = x_ref.shape[0]
    v = v_ref[...].astype(F32)
    mu = jnp.mean(v, axis=-1, keepdims=True)
    vc = v - mu
    var = jnp.mean(vc * vc, axis=-1, keepdims=True)
    vn = (vc * lax.rsqrt(var + LN_EPS) * lng_ref[...] + lnb_ref[...]).astype(BF16)
    row = lax.broadcasted_iota(I32, (sgc, sgc), 0)
    col = lax.broadcasted_iota(I32, (sgc, sgc), 1)
    for g in range(groups):
        w = jnp.where(col <= row, ws_ref[g], 0.0).astype(BF16)
        bias = bs_ref[g]
        for ci in range(tm // sgc):
            rs = slice(ci * sgc, (ci + 1) * sgc)
            cs = slice(g * LANES, (g + 1) * LANES)
            mixed = jnp.dot(w, vn[rs, cs], preferred_element_type=F32) + bias
            sg_ref[rs, cs] = (u_ref[rs, cs].astype(F32) * mixed).astype(BF16)
    ya = jnp.dot(a_ref[...], wa_ref[...], preferred_element_type=F32)
    yb = jnp.dot(sg_ref[...], wb_ref[...], preferred_element_type=F32)
    merged = (ga_ref[...].astype(F32) * ya + gb_ref[...].astype(F32) * yb).astype(BF16)
    o_ref[...] = x_ref[...] + jnp.dot(merged, wo_ref[...], preferred_element_type=F32)


def _const_spec(shape):
    nd = len(shape)
    return pl.BlockSpec(shape, lambda i: (0,) * nd, pipeline_mode=pl.Buffered(1))


def _mix(c, x2d, attn, proj, ln_g, ln_b, w_s, b_s_full, w_a, w_b, w_o):
    n, d = x2d.shape
    tm = c.tm_mix
    fw = c.FH * LANES
    sw = c.SG * LANES
    ucol = 2 * d // sw
    return pl.pallas_call(
        functools.partial(_mix_kernel, sgc=c.SGC, groups=c.SG),
        grid=(n // tm,),
        in_specs=[
            pl.BlockSpec((tm, d), lambda i: (i, 0)),
            pl.BlockSpec((tm, fw), lambda i: (i, 0)),
            pl.BlockSpec((tm, sw), lambda i: (i, ucol)),
            pl.BlockSpec((tm, sw), lambda i: (i, ucol + 1)),
            pl.BlockSpec((tm, d), lambda i: (i, 0)),
            pl.BlockSpec((tm, d), lambda i: (i, 1)),
            _const_spec((1, sw)),
            _const_spec((1, sw)),
            _const_spec((c.SG, c.SGC, c.SGC)),
            _const_spec((c.SG, c.SGC, LANES)),
            _const_spec((fw, d)),
            _const_spec((sw, d)),
            _const_spec((d, d)),
        ],
        out_specs=pl.BlockSpec((tm, d), lambda i: (i, 0)),
        out_shape=jax.ShapeDtypeStruct((n, d), F32),
        scratch_shapes=[pltpu.VMEM((tm, sw), BF16)],
        compiler_params=_cparams(("parallel",)),
        name="mix",
    )(x2d, attn, proj, proj, proj, proj, ln_g, ln_b, w_s, b_s_full, w_a, w_b, w_o)


def _mem_kv_kernel(m_ref, g_ref, w_ref, o_ref):
    hm = _rms(m_ref[...], g_ref[...]).astype(BF16)
    o_ref[...] = jnp.dot(hm, w_ref[...], preferred_element_type=F32).astype(o_ref.dtype)


def _mem_kv(c, mem2d, g, w_xkv):
    n, d = mem2d.shape
    nc = w_xkv.shape[1]
    tm = c.MEM
    return pl.pallas_call(
        _mem_kv_kernel,
        grid=(n // tm,),
        in_specs=[pl.BlockSpec((tm, d), lambda i: (i, 0)), _const_spec((1, d)), _const_spec((d, nc))],
        out_specs=pl.BlockSpec((tm, nc), lambda i: (i, 0)),
        out_shape=jax.ShapeDtypeStruct((n, nc), BF16),
        compiler_params=_cparams(("parallel",)),
        name="mem_kv",
    )(mem2d, g, w_xkv)


def _xattn_kernel(x1_ref, gx_ref, wq_ref, kv_ref, wo_ref, gf_ref, wr_ref, br_ref,
                  x2_ref, hfp_ref, idx_ref, gate_ref, cnt_ref, carry_ref,
                  *, heads, n_exp, topk, qscale):
    i = pl.program_id(0)
    tm, d = x1_ref.shape
    xw = heads * LANES
    x1 = x1_ref[...]
    hx = _rms(x1, gx_ref[...]).astype(BF16)
    q = (jnp.dot(hx, wq_ref[...], preferred_element_type=F32) * qscale).astype(BF16)
    outs = []
    for h in range(heads):
        k = kv_ref[:, h * LANES:(h + 1) * LANES]
        v = kv_ref[:, xw + h * LANES:xw + (h + 1) * LANES]
        s = lax.dot_general(q[:, h * LANES:(h + 1) * LANES], k, (((1,), (1,)), ((), ())),
                            preferred_element_type=F32)
        s = s - jnp.max(s, axis=-1, keepdims=True)
        p = jnp.exp(s)
        p = p / jnp.sum(p, axis=-1, keepdims=True)
        outs.append(jnp.dot(p.astype(BF16), v, preferred_element_type=F32).astype(BF16))
    o = jnp.concatenate(outs, axis=-1)
    x2 = x1 + jnp.dot(o, wo_ref[...], preferred_element_type=F32)
    x2_ref[...] = x2

    hf = _rms(x2, gf_ref[...])
    hfp_ref[...] = _pack_bf16_pair(hf[:, :d // 2], hf[:, d // 2:])

    h_hi = hf.astype(BF16)
    h_lo = (hf - h_hi.astype(F32)).astype(BF16)
    l_hi = jnp.dot(h_hi, wr_ref[...], preferred_element_type=F32)
    l_lo = jnp.dot(h_lo, wr_ref[:, :LANES], preferred_element_type=F32)
    logits = l_hi[:, :LANES] + l_hi[:, LANES:] + l_lo + br_ref[...]
    lane = lax.broadcasted_iota(I32, (tm, LANES), 1)
    lg = jnp.where(lane < n_exp, logits, -jnp.inf)
    vals, idxs = [], []
    for _ in range(topk):
        m = jnp.max(lg, axis=-1, keepdims=True)
        ix = jnp.min(jnp.where(lg == m, lane, LANES), axis=-1, keepdims=True)
        vals.append(m)
        idxs.append(ix)
        lg = jnp.where(lane == ix, -jnp.inf, lg)
    es = [jnp.exp(vv - vals[0]) for vv in vals]
    denom = es[0]
    for e in es[1:]:
        denom = denom + e

    @pl.when(i == 0)
    def _():
        carry_ref[...] = jnp.zeros(carry_ref.shape, F32)

    onehot = jnp.zeros((tm, LANES), F32)
    for ix in idxs:
        onehot = onehot + (lane == ix).astype(F32)
    idx_out = jnp.zeros((tm, LANES), I32)
    gate_out = jnp.zeros((tm, LANES), F32)
    for kk in range(topk):
        idx_out = jnp.where(lane == kk, idxs[kk], idx_out)
        gate_out = jnp.where(lane == kk, es[kk] / denom, gate_out)
    idx_ref[...] = idx_out
    gate_ref[...] = gate_out
    carry_ref[...] = carry_ref[...] + jnp.sum(onehot, axis=0, keepdims=True)
    cnt_ref[...] = carry_ref[...]


def _xattn(c, x1, gx, w_xq, kv, w_xo, gf, w_r, b_r, qscale):
    n, d = x1.shape
    tm = c.tm_x
    xw = c.XH * LANES
    per_b = c.S // tm
    kern = functools.partial(_xattn_kernel, heads=c.XH, n_exp=c.E, topk=c.K, qscale=qscale)
    return pl.pallas_call(
        kern,
        grid=(n // tm,),
        in_specs=[
            pl.BlockSpec((tm, d), lambda i: (i, 0)),
            _const_spec((1, d)),
            _const_spec((d, xw)),
            pl.BlockSpec((c.MEM, 2 * xw), lambda i: (i // per_b, 0)),
            _const_spec((xw, d)),
            _const_spec((1, d)),
            _const_spec((d, 2 * LANES)),
            _const_spec((1, LANES)),
        ],
        out_specs=[
            pl.BlockSpec((tm, d), lambda i: (i, 0)),
            pl.BlockSpec((tm, d // 2), lambda i: (i, 0)),
            pl.BlockSpec((tm, LANES), lambda i: (i, 0)),
            pl.BlockSpec((tm, LANES), lambda i: (i, 0)),
            pl.BlockSpec((1, LANES), lambda i: (0, 0)),
        ],
        out_shape=[
            jax.ShapeDtypeStruct((n, d), F32),
            jax.ShapeDtypeStruct((n, d // 2), U32),
            jax.ShapeDtypeStruct((n, LANES), I32),
            jax.ShapeDtypeStruct((n, LANES), F32),
            jax.ShapeDtypeStruct((1, LANES), F32),
        ],
        scratch_shapes=[pltpu.VMEM((1, LANES), F32)],
        compiler_params=_cparams(("arbitrary",)),
        name="xattn_router",
    )(x1, gx, w_xq, kv, w_xo, gf, w_r, b_r)


def _moe_kernel(te_ref, ns_ref, cpi_ref, pos_ref, rows_ref, ord_ref, hfp_ref, wgu_ref, wd_ref, bgu_ref, bd_ref,
                ykt_ref, wgub, wdb, stage, act, xin, ybuf, ids, wsem, xsem, ysem, isem,
                *, ts, nck, fc, n_tok, n_asg):
    i = pl.program_id(0)
    n = pl.num_programs(0)
    nsub = ns_ref[i]
    e = te_ref[i]
    cpi = cpi_ref[i]
    rows = rows_ref[i]
    d, f2 = wgub.shape
    f = f2 // 2
    dh = d // 2
    ckr = d // nck
    ckd = f // nck
    nst = stage.shape[0]
    idw = ids.shape[0] // 2
    nq = idw // ID_CHUNK
    id_base = (i % 2) * idw + (pos_ref[i] & (ID_CHUNK - 1))

    def ids_copy(item, q, slot):
        c0 = lax.shift_right_logical(pos_ref[item], ID_CHUNK.bit_length() - 1)
        src = ord_ref.at[pl.ds(pl.multiple_of((c0 + q) * ID_CHUNK, ID_CHUNK), ID_CHUNK)]
        dst = ids.at[pl.ds(pl.multiple_of(slot * idw + q * ID_CHUNK, ID_CHUNK), ID_CHUNK)]
        return pltpu.make_async_copy(src, dst, isem.at[slot])

    def token_of(a):
        return a & (n_tok - 1) if n_tok & (n_tok - 1) == 0 else lax.rem(a, n_tok)

    def gather(s, slot, r0=0, r1=ts, base=None):
        base = id_base if base is None else base
        for r in range(r0, r1):
            tok = token_of(ids[base + s * ts + r])
            pltpu.make_async_copy(hfp_ref.at[pl.ds(tok, 1)], xin.at[slot, pl.ds(r, 1)], xsem.at[slot]).start()

    def gather_wait(slot):
        for r in range(ts):
            pltpu.make_async_copy(hfp_ref.at[pl.ds(0, 1)], xin.at[slot, pl.ds(0, 1)], xsem.at[slot]).wait()

    def scatter(s, slot, r0=0, r1=ts, base=None, nrows=None):
        base = id_base if base is None else base
        nrows = rows if nrows is None else nrows
        for r in range(r0, r1):
            g = s * ts + r
            dst = jnp.where(g < nrows, ids[base + g], n_asg + slot * ts + r)
            pltpu.make_async_copy(ybuf.at[slot, pl.ds(r, 1)], ykt_ref.at[pl.ds(dst, 1)], ysem.at[slot]).start()

    def scatter_wait(slot):
        for r in range(ts):
            pltpu.make_async_copy(ybuf.at[slot, pl.ds(0, 1)], ykt_ref.at[pl.ds(0, 1)], ysem.at[slot]).wait()

    def gu_copy(ee, c, slot):
        return pltpu.make_async_copy(wgu_ref.at[ee, pl.ds(pl.multiple_of(c * ckr, ckr), ckr), :],
                                     stage.at[slot], wsem.at[slot])

    def d_copy(ee, c, slot):
        return pltpu.make_async_copy(wd_ref.at[ee, pl.ds(pl.multiple_of(c * ckd, ckd), ckd), :],
                                     stage.at[slot, pl.ds(0, ckd), pl.ds(0, d)], wsem.at[slot])

    def gu_step(ee, c):
        slot = c % nst
        gu_copy(ee, c, slot).wait()
        wgub[pl.ds(pl.multiple_of(c * ckr, ckr), ckr), :] = stage[slot].astype(BF16)

        @pl.when(c + nst < nck)
        def _():
            gu_copy(ee, c + nst, slot).start(priority=WEIGHT_DMA_PRIORITY)

    def d_step(ee, c):
        slot = c % nst
        d_copy(ee, c, slot).wait()
        wdb[pl.ds(pl.multiple_of(c * ckd, ckd), ckd), :] = stage[slot, :ckd, :d].astype(BF16)

        @pl.when(c + nst < nck)
        def _():
            d_copy(ee, c + nst, slot).start(priority=WEIGHT_DMA_PRIORITY)

    def prime(copy_fn, ee):
        for q in range(min(nst, nck)):
            copy_fn(ee, q, q).start(priority=WEIGHT_DMA_PRIORITY)

    def steps_after(s, step_fn, ee):
        def one(jj, carry):
            c = s * cpi + jj

            @pl.when(c < nck)
            def _():
                step_fn(ee, c)
            return carry
        lax.fori_loop(0, cpi, one, 0)

    @pl.when(nsub > 0)
    def _():
        nxt = jnp.minimum(i + 1, n - 1)
        has_next = (i + 1 < n) & (ns_ref[nxt] > 0)

        @pl.when(i == 0)
        def _():
            for q in range(nq):
                ids_copy(0, q, 0).start()
            prime(gu_copy, e)

            def first(c, carry):
                gu_step(e, c)
                return carry
            lax.fori_loop(0, nck, first, 0)
            for q in range(nq):
                ids_copy(0, q, 0).wait()
            gather(0, 0)
            ybuf[...] = jnp.zeros(ybuf.shape, ybuf.dtype)

        prv = jnp.maximum(i - 1, 0)
        nsub_p = jnp.where(i > 0, ns_ref[prv], 1)
        rows_p = jnp.where(i > 0, rows_ref[prv], 0)
        nslot = 1 - i % 2
        base_p = jnp.where(i > 0, nslot * idw + (pos_ref[prv] & (ID_CHUNK - 1)), id_base)
        lslot_p = (nsub_p - 1) % 2

        prime(d_copy, e)
        bgu = bgu_ref[e]

        def gate_up(s, scatter_prev):
            slot = s % 2
            gather_wait(slot)
            p = xin[slot]
            xb = jnp.concatenate([_unpack_lo(p).astype(BF16), _unpack_hi(p).astype(BF16)], axis=-1)
            nc = f // fc
            for cc in range(nc):
                gather(s + 1, 1 - slot, cc * ts // nc, (cc + 1) * ts // nc)
                if scatter_prev:
                    scatter(nsub_p - 1, lslot_p, cc * ts // nc, (cc + 1) * ts // nc, base=base_p, nrows=rows_p)
                gate = jnp.dot(xb, wgub[:, cc * fc:(cc + 1) * fc], preferred_element_type=F32)
                gate = gate + bgu[:, cc * fc:(cc + 1) * fc]
                up = jnp.dot(xb, wgub[:, f + cc * fc:f + (cc + 1) * fc], preferred_element_type=F32)
                up = up + bgu[:, f + cc * fc:f + (cc + 1) * fc]
                gate = jnp.minimum(gate, SWIGLU_LIMIT)
                up = jnp.clip(up, -SWIGLU_LIMIT, SWIGLU_LIMIT)
                a = (up + 1.0) * gate * jax.nn.sigmoid(SWIGLU_ALPHA * gate)
                act[s, :, cc * fc:(cc + 1) * fc] = a.astype(BF16)
            steps_after(s, d_step, e)

        gate_up(0, True)

        for q in range(nq):
            ids_copy(nxt, q, nslot).start()

        def phase_a(s, carry):
            gate_up(s, False)
            return carry

        lax.fori_loop(1, nsub, phase_a, 0)
        gather_wait(nsub % 2)

        e_next = te_ref[nxt]
        load_next = has_next & (e_next != e)

        @pl.when(load_next)
        def _():
            prime(gu_copy, e_next)

        bd = bd_ref[e]

        def down(s, slot, scatter_prev):
            a = act[s]
            if scatter_prev:
                scatter(s - 1, 1 - slot)
            fcb = min(fc, dh // 2)
            for cc in range(dh // fcb):
                lo = slice(cc * fcb, (cc + 1) * fcb)
                hi = slice(dh + cc * fcb, dh + (cc + 1) * fcb)
                y_lo = jnp.dot(a, wdb[:, lo], preferred_element_type=F32) + bd[:, lo]
                y_hi = jnp.dot(a, wdb[:, hi], preferred_element_type=F32) + bd[:, hi]
                ybuf[slot, :, lo] = _pack_bf16_pair(y_lo, y_hi)

        def next_weights(s):
            @pl.when(load_next)
            def _():
                steps_after(s, gu_step, e_next)

        scatter_wait(lslot_p)

        @pl.when(nsub_p >= 2)
        def _():
            scatter_wait(1 - lslot_p)

        for q in range(nq):
            ids_copy(nxt, q, nslot).wait()
        gather(0, 0, base=nslot * idw + (pos_ref[nxt] & (ID_CHUNK - 1)))
        down(0, 0, False)
        next_weights(0)

        def phase_b(s, carry):
            slot = s % 2

            @pl.when(s >= 2)
            def _():
                scatter_wait(slot)

            down(s, slot, True)
            next_weights(s)
            return carry

        lax.fori_loop(1, nsub, phase_b, 0)

        @pl.when(jnp.logical_not(has_next))
        def _():
            scatter(nsub - 1, (nsub - 1) % 2)
            scatter_wait((nsub - 1) % 2)

            @pl.when(nsub >= 2)
            def _():
                scatter_wait(nsub % 2)

            gather_wait(0)


def _id_window(c):
    return ((ID_CHUNK - 1 + c.tm_e + c.ts_e) // ID_CHUNK + 1) * ID_CHUNK


def _moe(c, sched, order_pad, hfp, w_gu, b_gu3, w_d, b_d3):
    n_tok, dh = hfp.shape
    d = 2 * dh
    f = c.DFF
    tm, ts, nck = c.tm_e, c.ts_e, c.nck_e
    n_asg = n_tok * c.K
    grid_spec = pltpu.PrefetchScalarGridSpec(
        num_scalar_prefetch=5,
        grid=(_n_tiles(c),),
        in_specs=[
            pl.BlockSpec(memory_space=pl.ANY),
            pl.BlockSpec(memory_space=pl.ANY),
            pl.BlockSpec(memory_space=pl.ANY),
            pl.BlockSpec(memory_space=pl.ANY),
            pl.BlockSpec((c.E, 1, 2 * f), lambda i, *_: (0, 0, 0), pipeline_mode=pl.Buffered(1)),
            pl.BlockSpec((c.E, 1, d), lambda i, *_: (0, 0, 0), pipeline_mode=pl.Buffered(1)),
        ],
        out_specs=pl.BlockSpec(memory_space=pl.ANY),
        scratch_shapes=[
            pltpu.VMEM((d, 2 * f), BF16),
            pltpu.VMEM((f, d), BF16),
            pltpu.VMEM((c.nst_e, d // nck, 2 * f), F32),
            pltpu.VMEM((tm // ts, ts, f), BF16),
            pltpu.VMEM((2, ts, dh), U32),
            pltpu.VMEM((2, ts, dh), U32),
            pltpu.SMEM((2 * _id_window(c),), I32),
            pltpu.SemaphoreType.DMA((c.nst_e,)),
            pltpu.SemaphoreType.DMA((2,)),
            pltpu.SemaphoreType.DMA((2,)),
            pltpu.SemaphoreType.DMA((2,)),
        ],
    )
    kern = functools.partial(_moe_kernel, ts=ts, nck=nck, fc=c.fc_e, n_tok=n_tok, n_asg=n_asg)
    return pl.pallas_call(
        kern,
        grid_spec=grid_spec,
        out_shape=jax.ShapeDtypeStruct((n_asg + 2 * ts, dh), U32),
        compiler_params=_cparams(("arbitrary",)),
        name="experts",
    )(*sched, order_pad, hfp, w_gu, w_d, b_gu3, b_d3)


def _combine_kernel(x2_ref, gate_ref, g_ref, *refs, topk):
    y_refs, o_ref = refs[:topk], refs[topk]
    d = x2_ref.shape[1]
    dh = d // 2
    lo = x2_ref[:, :dh]
    hi = x2_ref[:, dh:]
    gates = gate_ref[...]
    for kk in range(topk):
        wk = gates[:, kk:kk + 1]
        p = y_refs[kk][...]
        lo = lo + wk * _unpack_lo(p)
        hi = hi + wk * _unpack_hi(p)
    ms = (jnp.sum(lo * lo, axis=-1, keepdims=True) + jnp.sum(hi * hi, axis=-1, keepdims=True)) / d
    inv = lax.rsqrt(ms + RMS_EPS)
    o_ref[:, :dh] = lo * inv * g_ref[:, :dh]
    o_ref[:, dh:] = hi * inv * g_ref[:, dh:]


def _combine(c, x2, gates, g, ykt):
    n, d = x2.shape
    tm = c.tm_c
    nb = n // tm
    y_specs = [pl.BlockSpec((tm, d // 2), functools.partial(lambda i, kk: (kk * nb + i, 0), kk=kk))
               for kk in range(c.K)]
    return pl.pallas_call(
        functools.partial(_combine_kernel, topk=c.K),
        grid=(nb,),
        in_specs=[
            pl.BlockSpec((tm, d), lambda i: (i, 0)),
            pl.BlockSpec((tm, LANES), lambda i: (i, 0)),
            pl.BlockSpec((1, d), lambda i: (0, 0)),
        ] + y_specs,
        out_specs=pl.BlockSpec((tm, d), lambda i: (i, 0)),
        out_shape=jax.ShapeDtypeStruct((n, d), F32),
        compiler_params=_cparams(("parallel",)),
        name="combine",
    )(x2, gates, g, *([ykt] * c.K))


def _plan(c, idx, counts):
    tm, ts = c.tm_e, c.ts_e
    nk = idx.shape[0] * c.K
    order = jnp.argsort(idx.T.reshape(-1)).astype(I32)
    pad = _id_window(c) + (-nk) % ID_CHUNK
    order_pad = jnp.concatenate([order, jnp.zeros((pad,), I32)])
    counts = counts.astype(I32)
    start = jnp.cumsum(counts) - counts
    tiles_per_e = (counts + tm - 1) // tm
    tile_end = jnp.cumsum(tiles_per_e)
    tile_start = tile_end - tiles_per_e
    t = jnp.arange(_n_tiles(c), dtype=I32)
    n_used = tile_end[-1]
    tc = jnp.minimum(t, n_used - 1)
    te = jnp.minimum(jnp.searchsorted(tile_end, tc, side="right"), c.E - 1).astype(I32)
    j = tc - tile_start[te]
    used = t < n_used
    rows = jnp.where(used, jnp.clip(counts[te] - j * tm, 0, tm), 0).astype(I32)
    nsub = (rows + ts - 1) // ts
    pos = jnp.where(used, start[te] + j * tm, 0).astype(I32)
    cpi = (c.nck_e + jnp.maximum(nsub, 1) - 1) // jnp.maximum(nsub, 1)
    return order_pad, (te, nsub.astype(I32), cpi.astype(I32), pos, rows)


def _n_tiles(c):
    return -(-(c.B * c.S * c.K) // c.tm_e) + c.E


def _forward(c, x, mem, norm_mix_g, w_in, b_forget, sg_ln_g, sg_ln_b, w_spatial, b_spatial,
             w_branch_a, w_branch_b, w_out, norm_x_g, norm_mem_g, w_xq, w_xkv, w_xo,
             norm_ffn_g, w_router, b_router, w_gate_up, b_gate_up, w_down, b_down, norm_final_g):
    B, S, D = x.shape
    n = B * S
    fw = c.FH * LANES
    sw = c.SG * LANES
    x2d = x.reshape(n, D)

    o_f = 3 * fw
    o_z = o_f + c.FH
    o_g = o_z + 2 * sw
    w_main = jnp.concatenate([w_in[:, o_g:], w_in[:, o_z:o_g], w_in[:, :o_f]], axis=1).astype(BF16)
    w_f = jnp.pad(w_in[:, o_f:o_z], ((0, 0), (0, LANES - c.FH))).astype(BF16)
    b_f = jnp.pad(b_forget.astype(F32), (0, LANES - c.FH)).reshape(1, LANES)
    proj, logf = _in_proj(c, x2d, norm_mix_g.reshape(1, D), w_main, w_f, b_f, LOG2E * LANES ** -0.5)

    logf_bhs = logf[:, :c.FH].reshape(B, S, c.FH).transpose(0, 2, 1)
    csum = _cumsum(c, logf_bhs)
    crow = csum.reshape(B * c.FH, 1, S)
    ccol = csum.reshape(B * c.FH, S, 1)
    qcol0 = (2 * D + 2 * sw) // LANES
    attn = _fox(c, proj, crow, ccol, qcol0)

    b_s_full = jnp.broadcast_to(b_spatial.astype(F32)[:, :, None], (c.SG, c.SGC, LANES))
    x1 = _mix(c, x2d, attn, proj, sg_ln_g.reshape(1, sw), sg_ln_b.reshape(1, sw), w_spatial, b_s_full,
              w_branch_a.astype(BF16), w_branch_b.astype(BF16), w_out.astype(BF16))

    kv = _mem_kv(c, mem.reshape(B * c.MEM, D), norm_mem_g.reshape(1, D), w_xkv.astype(BF16))
    w_r32 = jnp.pad(w_router.astype(F32), ((0, 0), (0, LANES - c.E)))
    w_r_hi = w_r32.astype(BF16)
    w_r = jnp.concatenate([w_r_hi, (w_r32 - w_r_hi.astype(F32)).astype(BF16)], axis=1)
    b_r = jnp.pad(b_router.astype(F32), (0, LANES - c.E)).reshape(1, LANES)
    x2, hfp, idx, gates, counts = _xattn(
        c, x1, norm_x_g.reshape(1, D), w_xq.astype(BF16), kv, w_xo.astype(BF16),
        norm_ffn_g.reshape(1, D), w_r, b_r, LANES ** -0.5)

    order_pad, sched = _plan(c, idx[:, :c.K], counts[0, :c.E])
    ykt = _moe(c, sched, order_pad, hfp, w_gate_up, b_gate_up.reshape(c.E, 1, 2 * c.DFF),
               w_down, b_down.reshape(c.E, 1, D))
    out = _combine(c, x2, gates, norm_final_g.reshape(1, D), ykt)
    return out.reshape(B, S, D)


_CFG = Cfg(B=4, S=4096, D=2048, MEM=256, FH=8, SG=8, SGC=128, XH=4, E=32, K=4, DFF=2048,
           tm_in=1024, tn_in=1024, tq=1024, tm_mix=256, tm_x=512, tm_e=2560, ts_e=256, fc_e=512, nck_e=8, nst_e=2,
           tm_c=256, tc_cs=512, fox_parts=2)


@jax.jit
def kernel(x, mem, norm_mix_g, w_in, b_forget, sg_ln_g, sg_ln_b, w_spatial, b_spatial, w_branch_a, w_branch_b,
           w_out, norm_x_g, norm_mem_g, w_xq, w_xkv, w_xo, norm_ffn_g, w_router, b_router, w_gate_up,
           b_gate_up, w_down, b_down, norm_final_g):
    return _forward(_CFG, x, mem, norm_mix_g, w_in, b_forget, sg_ln_g, sg_ln_b, w_spatial, b_spatial,
                    w_branch_a, w_branch_b, w_out, norm_x_g, norm_mem_g, w_xq, w_xkv, w_xo,
                    norm_ffn_g, w_router, b_router, w_gate_up, b_gate_up, w_down, b_down, norm_final_g)
```

```python
import functools
from typing import NamedTuple

import jax
import jax.numpy as jnp
from jax import lax
from jax.experimental import pallas as pl
from jax.experimental.pallas import tpu as pltpu

F32 = jnp.float32
BF16 = jnp.bfloat16
U32 = jnp.uint32
I32 = jnp.int32

LANES = 128
VMEM_LIMIT = 56 * 1024 * 1024

RMS_EPS = 1e-6
LN_EPS = 1e-5
SWIGLU_LIMIT = 7.0
SWIGLU_ALPHA = 1.702
GELU_C = 0.7978845608028654
LOG2E = 1.4426950408889634
ID_CHUNK = 1024
WEIGHT_DMA_PRIORITY = 1


class Cfg(NamedTuple):
    B: int
    S: int
    D: int
    MEM: int
    FH: int
    SG: int
    SGC: int
    XH: int
    E: int
    K: int
    DFF: int
    tm_in: int
    tn_in: int
    tq: int
    tm_mix: int
    tm_x: int
    tm_e: int
    ts_e: int
    fc_e: int
    nck_e: int
    nst_e: int
    tm_c: int
    tc_cs: int
    fox_parts: int


def _cparams(sem):
    return pltpu.CompilerParams(dimension_semantics=sem, vmem_limit_bytes=VMEM_LIMIT)


def _rms(x, g):
    ms = jnp.mean(x * x, axis=-1, keepdims=True)
    return x * lax.rsqrt(ms + RMS_EPS) * g


def _pack_bf16_pair(a, b):
    def rne(v):
        bits = lax.bitcast_convert_type(v, U32)
        return bits + jnp.uint32(0x7FFF) + ((bits >> 16) & jnp.uint32(1))
    return (rne(a) >> 16) | (rne(b) & jnp.uint32(0xFFFF0000))


def _unpack_lo(p):
    return lax.bitcast_convert_type(p << 16, F32)


def _unpack_hi(p):
    return lax.bitcast_convert_type(p & jnp.uint32(0xFFFF0000), F32)


def _in_proj_kernel(x_ref, g_ref, w_ref, wf_ref, bf_ref, o_ref, f_ref, h_ref, *, nj_gate, nj_z, nj_q, qscale):
    j = pl.program_id(1)

    @pl.when(j == 0)
    def _():
        hb = _rms(x_ref[...], g_ref[...]).astype(BF16)
        h_ref[...] = hb
        f = jnp.dot(hb, wf_ref[...], preferred_element_type=F32) + bf_ref[...]
        f_ref[...] = jnp.minimum(f, 0.0) - jnp.log1p(jnp.exp(-jnp.abs(f)))

    acc = jnp.dot(h_ref[...], w_ref[...], preferred_element_type=F32)

    is_gate = j < nj_gate
    is_z = (j >= nj_gate) & (j < nj_gate + nj_z)
    is_q = (j >= nj_gate + nj_z) & (j < nj_gate + nj_z + nj_q)
    a1 = jnp.where(is_gate, 0.5, jnp.where(is_z, GELU_C, 0.0)).astype(F32)
    a3 = jnp.where(is_z, GELU_C * 0.044715, 0.0).astype(F32)
    b0 = jnp.where(is_gate, 0.5, 0.0).astype(F32)
    b1 = jnp.where(is_gate, 0.0, jnp.where(is_z, 0.5, jnp.where(is_q, qscale, 1.0))).astype(F32)
    th = jnp.tanh(acc * (a1 + a3 * (acc * acc)))
    o_ref[...] = ((b0 + b1 * acc) * (1.0 + th)).astype(o_ref.dtype)


def _in_proj(c, x2d, g, w_main, w_f, b_f, qscale):
    n, d = x2d.shape
    nc = w_main.shape[1]
    tm, tn = c.tm_in, c.tn_in
    fw = c.FH * LANES
    kern = functools.partial(_in_proj_kernel, nj_gate=2 * d // tn, nj_z=2 * c.SG * LANES // tn,
                             nj_q=fw // tn, qscale=qscale)
    return pl.pallas_call(
        kern,
        grid=(n // tm, nc // tn),
        in_specs=[
            pl.BlockSpec((tm, d), lambda i, j: (i, 0)),
            pl.BlockSpec((1, d), lambda i, j: (0, 0)),
            pl.BlockSpec((d, tn), lambda i, j: (0, j)),
            pl.BlockSpec((d, LANES), lambda i, j: (0, 0)),
            pl.BlockSpec((1, LANES), lambda i, j: (0, 0)),
        ],
        out_specs=[
            pl.BlockSpec((tm, tn), lambda i, j: (i, j)),
            pl.BlockSpec((tm, LANES), lambda i, j: (i, 0)),
        ],
        out_shape=[jax.ShapeDtypeStruct((n, nc), BF16), jax.ShapeDtypeStruct((n, LANES), F32)],
        scratch_shapes=[pltpu.VMEM((tm, d), BF16)],
        compiler_params=_cparams(("parallel", "arbitrary")),
        name="in_proj",
    )(x2d, g, w_main, w_f, b_f)


def _cumsum_kernel(f_ref, o_ref, *, tc):
    rows, s = f_ref.shape[1], f_ref.shape[2]
    r = lax.broadcasted_iota(I32, (tc, tc), 0)
    col = lax.broadcasted_iota(I32, (tc, tc), 1)
    upper = (r <= col).astype(F32)
    carry = jnp.zeros((rows, 1), F32)
    for i in range(s // tc):
        blk = f_ref[0, :, i * tc:(i + 1) * tc]
        cs = jnp.dot(blk, upper, preferred_element_type=F32, precision=lax.Precision.HIGHEST) + carry
        o_ref[0, :, i * tc:(i + 1) * tc] = cs * LOG2E
        carry = cs[:, tc - 1:tc]


def _cumsum(c, logf_bhs):
    b, h, s = logf_bhs.shape
    return pl.pallas_call(
        functools.partial(_cumsum_kernel, tc=c.tc_cs),
        grid=(b,),
        in_specs=[pl.BlockSpec((1, h, s), lambda i: (i, 0, 0))],
        out_specs=pl.BlockSpec((1, h, s), lambda i: (i, 0, 0)),
        out_shape=jax.ShapeDtypeStruct((b, h, s), F32),
        compiler_params=_cparams(("parallel",)),
        name="cumsum",
    )(logf_bhs)


def _fox_kernel(q_ref, k_ref, v_ref, crow_ref, ccol_ref, o_ref, m_ref, l_ref, acc_ref, cq_ref, s_ref,
                *, t, parts):
    qi = pl.program_id(2)
    hr = t // parts
    m_ref[...] = jnp.full(m_ref.shape, -jnp.inf, F32)
    l_ref[...] = jnp.zeros(l_ref.shape, F32)
    acc_ref[...] = jnp.zeros(acc_ref.shape, F32)
    cq_ref[...] = jnp.broadcast_to(ccol_ref[0], cq_ref.shape)

    def scores(ks):
        k = k_ref[pl.ds(ks, t), :]
        return lax.dot_general(q_ref[...], k, (((1,), (1,)), ((), ())), preferred_element_type=F32)

    def block(part, ks, width, masked):
        rows = slice(part * hr, (part + 1) * hr)
        v = v_ref[pl.ds(ks, width), :]
        s = s_ref[rows, :width]
        cq = cq_ref[rows, :]
        crow = crow_ref[0, :, pl.ds(ks, width)]
        nj = width // LANES
        sj = [s[:, j * LANES:(j + 1) * LANES] + cq - crow[:, j * LANES:(j + 1) * LANES] for j in range(nj)]
        if masked:
            row = lax.broadcasted_iota(I32, (hr, LANES), 0) + part * hr
            col = lax.broadcasted_iota(I32, (hr, LANES), 1)
            sj = [jnp.where(col + j * LANES <= row, sj[j], -jnp.inf) for j in range(nj)]
        mx = sj[0]
        for j in range(1, nj):
            mx = jnp.maximum(mx, sj[j])
        m_prev = m_ref[rows, :]
        m_next = jnp.maximum(m_prev, jnp.max(mx, axis=-1, keepdims=True))
        alpha = jnp.exp2(m_prev - m_next)
        pj = [jnp.exp2(sj[j] - m_next) for j in range(nj)]
        psum = pj[0]
        for j in range(1, nj):
            psum = psum + pj[j]
        p = jnp.concatenate([x.astype(BF16) for x in pj], axis=-1)
        l_ref[rows, :] = alpha * l_ref[rows, :] + psum
        acc_ref[rows, :] = alpha * acc_ref[rows, :] + jnp.dot(p, v, preferred_element_type=F32)
        m_ref[rows, :] = m_next

    s_ref[...] = scores(0)

    def body(kc, carry):
        ks = pl.multiple_of(kc * t, t)
        s_next = scores(pl.multiple_of(ks + t, t))
        for part in range(parts):
            block(part, ks, t, False)
        s_ref[...] = s_next
        return carry

    lax.fori_loop(0, qi, body, 0)
    kd = pl.multiple_of(qi * t, t)
    for part in range(parts):
        block(part, kd, (part + 1) * hr, True)
    l = jnp.sum(l_ref[...], axis=-1, keepdims=True)
    o_ref[...] = (acc_ref[...] / l).astype(o_ref.dtype)


def _fox(c, proj, crow, ccol, qcol0):
    n = c.B * c.S
    t = c.tq
    nq = c.S // t
    h = c.FH
    return pl.pallas_call(
        functools.partial(_fox_kernel, t=t, parts=c.fox_parts),
        grid=(c.B, h, nq),
        in_specs=[
            pl.BlockSpec((t, LANES), lambda b, hh, qi: (b * nq + qi, qcol0 + hh)),
            pl.BlockSpec((c.S, LANES), lambda b, hh, qi: (b, qcol0 + h + hh)),
            pl.BlockSpec((c.S, LANES), lambda b, hh, qi: (b, qcol0 + 2 * h + hh)),
            pl.BlockSpec((1, 1, c.S), lambda b, hh, qi: (b * h + hh, 0, 0)),
            pl.BlockSpec((1, t, 1), lambda b, hh, qi: (b * h + hh, qi, 0)),
        ],
        out_specs=pl.BlockSpec((t, LANES), lambda b, hh, qi: (b * nq + qi, hh)),
        out_shape=jax.ShapeDtypeStruct((n, h * LANES), BF16),
        scratch_shapes=[pltpu.VMEM((t, LANES), F32)] * 4 + [pltpu.VMEM((t, t), F32)],
        compiler_params=_cparams(("parallel", "parallel", "arbitrary")),
        name="fox",
    )(proj, proj, proj, crow, ccol)


def _mix_kernel(x_ref, a_ref, u_ref, v_ref, ga_ref, gb_ref, lng_ref, lnb_ref, ws_ref, bs_ref,
                wa_ref, wb_ref, wo_ref, o_ref, sg_ref, *, sgc, groups):
    tm = x_ref.shape[0]
    v = v_ref[...].astype(F32)
    mu = jnp.mean(v, axis=-1, keepdims=True)
    vc = v - mu
    var = jnp.mean(vc * vc, axis=-1, keepdims=True)
    vn = (vc * lax.rsqrt(var + LN_EPS) * lng_ref[...] + lnb_ref[...]).astype(BF16)
    row = lax.broadcasted_iota(I32, (sgc, sgc), 0)
    col = lax.broadcasted_iota(I32, (sgc, sgc), 1)
    for g in range(groups):
        w = jnp.where(col <= row, ws_ref[g], 0.0).astype(BF16)
        bias = bs_ref[g]
        for ci in range(tm // sgc):
            rs = slice(ci * sgc, (ci + 1) * sgc)
            cs = slice(g * LANES, (g + 1) * LANES)
            mixed = jnp.dot(w, vn[rs, cs], preferred_element_type=F32) + bias
            sg_ref[rs, cs] = (u_ref[rs, cs].astype(F32) * mixed).astype(BF16)
    ya = jnp.dot(a_ref[...], wa_ref[...], preferred_element_type=F32)
    yb = jnp.dot(sg_ref[...], wb_ref[...], preferred_element_type=F32)
    merged = (ga_ref[...].astype(F32) * ya + gb_ref[...].astype(F32) * yb).astype(BF16)
    o_ref[...] = x_ref[...] + jnp.dot(merged, wo_ref[...], preferred_element_type=F32)


def _const_spec(shape):
    nd = len(shape)
    return pl.BlockSpec(shape, lambda i: (0,) * nd, pipeline_mode=pl.Buffered(1))


def _mix(c, x2d, attn, proj, ln_g, ln_b, w_s, b_s_full, w_a, w_b, w_o):
    n, d = x2d.shape
    tm = c.tm_mix
    fw = c.FH * LANES
    sw = c.SG * LANES
    ucol = 2 * d // sw
    return pl.pallas_call(
        functools.partial(_mix_kernel, sgc=c.SGC, groups=c.SG),
        grid=(n // tm,),
        in_specs=[
            pl.BlockSpec((tm, d), lambda i: (i, 0)),
            pl.BlockSpec((tm, fw), lambda i: (i, 0)),
            pl.BlockSpec((tm, sw), lambda i: (i, ucol)),
            pl.BlockSpec((tm, sw), lambda i: (i, ucol + 1)),
            pl.BlockSpec((tm, d), lambda i: (i, 0)),
            pl.BlockSpec((tm, d), lambda i: (i, 1)),
            _const_spec((1, sw)),
            _const_spec((1, sw)),
            _const_spec((c.SG, c.SGC, c.SGC)),
            _const_spec((c.SG, c.SGC, LANES)),
            _const_spec((fw, d)),
            _const_spec((sw, d)),
            _const_spec((d, d)),
        ],
        out_specs=pl.BlockSpec((tm, d), lambda i: (i, 0)),
        out_shape=jax.ShapeDtypeStruct((n, d), F32),
        scratch_shapes=[pltpu.VMEM((tm, sw), BF16)],
        compiler_params=_cparams(("parallel",)),
        name="mix",
    )(x2d, attn, proj, proj, proj, proj, ln_g, ln_b, w_s, b_s_full, w_a, w_b, w_o)


def _xattn_kernel(x1_ref, gx_ref, wq_ref, mem_ref, gm_ref, wkv_ref, wo_ref, gf_ref, wr_ref, br_ref,
                  x2_ref, hfp_ref, idx_ref, gate_ref, cnt_ref, carry_ref, kv_ref,
                  *, heads, n_exp, topk, qscale, per_b):
    i = pl.program_id(0)
    tm, d = x1_ref.shape
    xw = heads * LANES

    @pl.when(i % per_b == 0)
    def _():
        hm = _rms(mem_ref[...], gm_ref[...]).astype(BF16)
        kv_ref[...] = jnp.dot(hm, wkv_ref[...], preferred_element_type=F32).astype(BF16)

    x1 = x1_ref[...]
    hx = _rms(x1, gx_ref[...]).astype(BF16)
    q = (jnp.dot(hx, wq_ref[...], preferred_element_type=F32) * qscale).astype(BF16)
    outs = []
    for h in range(heads):
        k = kv_ref[:, h * LANES:(h + 1) * LANES]
        v = kv_ref[:, xw + h * LANES:xw + (h + 1) * LANES]
        s = lax.dot_general(q[:, h * LANES:(h + 1) * LANES], k, (((1,), (1,)), ((), ())),
                            preferred_element_type=F32)
        s = s - jnp.max(s, axis=-1, keepdims=True)
        p = jnp.exp(s)
        p = p / jnp.sum(p, axis=-1, keepdims=True)
        outs.append(jnp.dot(p.astype(BF16), v, preferred_element_type=F32).astype(BF16))
    o = jnp.concatenate(outs, axis=-1)
    x2 = x1 + jnp.dot(o, wo_ref[...], preferred_element_type=F32)
    x2_ref[...] = x2

    hf = _rms(x2, gf_ref[...])
    hfp_ref[...] = _pack_bf16_pair(hf[:, :d // 2], hf[:, d // 2:])

    h_hi = hf.astype(BF16)
    h_lo = (hf - h_hi.astype(F32)).astype(BF16)
    l_hi = jnp.dot(h_hi, wr_ref[...], preferred_element_type=F32)
    l_lo = jnp.dot(h_lo, wr_ref[:, :LANES], preferred_element_type=F32)
    logits = l_hi[:, :LANES] + l_hi[:, LANES:] + l_lo + br_ref[...]
    lane = lax.broadcasted_iota(I32, (tm, LANES), 1)
    lg = jnp.where(lane < n_exp, logits, -jnp.inf)
    vals, idxs = [], []
    for _ in range(topk):
        m = jnp.max(lg, axis=-1, keepdims=True)
        ix = jnp.min(jnp.where(lg == m, lane, LANES), axis=-1, keepdims=True)
        vals.append(m)
        idxs.append(ix)
        lg = jnp.where(lane == ix, -jnp.inf, lg)
    es = [jnp.exp(vv - vals[0]) for vv in vals]
    denom = es[0]
    for e in es[1:]:
        denom = denom + e

    @pl.when(i == 0)
    def _():
        carry_ref[...] = jnp.zeros(carry_ref.shape, F32)

    onehot = jnp.zeros((tm, LANES), F32)
    for ix in idxs:
        onehot = onehot + (lane == ix).astype(F32)
    idx_out = jnp.zeros((tm, LANES), I32)
    gate_out = jnp.zeros((tm, LANES), F32)
    for kk in range(topk):
        idx_out = jnp.where(lane == kk, idxs[kk], idx_out)
        gate_out = jnp.where(lane == kk, es[kk] / denom, gate_out)
    idx_ref[...] = idx_out
    gate_ref[...] = gate_out
    carry_ref[...] = carry_ref[...] + jnp.sum(onehot, axis=0, keepdims=True)
    cnt_ref[...] = carry_ref[...]


def _xattn(c, x1, gx, w_xq, mem2d, gm, w_xkv, w_xo, gf, w_r, b_r, qscale):
    n, d = x1.shape
    tm = c.tm_x
    xw = c.XH * LANES
    per_b = c.S // tm
    kern = functools.partial(_xattn_kernel, heads=c.XH, n_exp=c.E, topk=c.K, qscale=qscale, per_b=per_b)
    return pl.pallas_call(
        kern,
        grid=(n // tm,),
        in_specs=[
            pl.BlockSpec((tm, d), lambda i: (i, 0)),
            _const_spec((1, d)),
            _const_spec((d, xw)),
            pl.BlockSpec((c.MEM, d), lambda i: (i // per_b, 0)),
            _const_spec((1, d)),
            _const_spec((d, 2 * xw)),
            _const_spec((xw, d)),
            _const_spec((1, d)),
            _const_spec((d, 2 * LANES)),
            _const_spec((1, LANES)),
        ],
        out_specs=[
            pl.BlockSpec((tm, d), lambda i: (i, 0)),
            pl.BlockSpec((tm, d // 2), lambda i: (i, 0)),
            pl.BlockSpec((tm, LANES), lambda i: (i, 0)),
            pl.BlockSpec((tm, LANES), lambda i: (i, 0)),
            pl.BlockSpec((1, LANES), lambda i: (0, 0)),
        ],
        out_shape=[
            jax.ShapeDtypeStruct((n, d), F32),
            jax.ShapeDtypeStruct((n, d // 2), U32),
            jax.ShapeDtypeStruct((n, LANES), I32),
            jax.ShapeDtypeStruct((n, LANES), F32),
            jax.ShapeDtypeStruct((1, LANES), F32),
        ],
        scratch_shapes=[pltpu.VMEM((1, LANES), F32), pltpu.VMEM((c.MEM, 2 * xw), BF16)],
        compiler_params=_cparams(("arbitrary",)),
        name="xattn_router",
    )(x1, gx, w_xq, mem2d, gm, w_xkv, w_xo, gf, w_r, b_r)


def _moe_kernel(te_ref, ns_ref, cpi_ref, pos_ref, rows_ref, ord_ref, hfp_ref, wgu_ref, wd_ref, bgu_ref, bd_ref,
                ykt_ref, wgub, wdb, stage, act, xin, ybuf, ids, wsem, xsem, ysem, isem,
                *, ts, nck, fc, n_tok, n_asg):
    i = pl.program_id(0)
    n = pl.num_programs(0)
    nsub = ns_ref[i]
    e = te_ref[i]
    cpi = cpi_ref[i]
    rows = rows_ref[i]
    d, f2 = wgub.shape
    f = f2 // 2
    dh = d // 2
    ckr = d // nck
    ckd = f // nck
    nst = stage.shape[0]
    idw = ids.shape[0] // 2
    nq = idw // ID_CHUNK
    id_base = (i % 2) * idw + (pos_ref[i] & (ID_CHUNK - 1))

    def ids_copy(item, q, slot):
        c0 = lax.shift_right_logical(pos_ref[item], ID_CHUNK.bit_length() - 1)
        src = ord_ref.at[pl.ds(pl.multiple_of((c0 + q) * ID_CHUNK, ID_CHUNK), ID_CHUNK)]
        dst = ids.at[pl.ds(pl.multiple_of(slot * idw + q * ID_CHUNK, ID_CHUNK), ID_CHUNK)]
        return pltpu.make_async_copy(src, dst, isem.at[slot])

    def token_of(a):
        return a & (n_tok - 1) if n_tok & (n_tok - 1) == 0 else lax.rem(a, n_tok)

    def gather(s, slot, r0=0, r1=ts, base=None):
        base = id_base if base is None else base
        for r in range(r0, r1):
            tok = token_of(ids[base + s * ts + r])
            pltpu.make_async_copy(hfp_ref.at[pl.ds(tok, 1)], xin.at[slot, pl.ds(r, 1)], xsem.at[slot]).start()

    def gather_wait(slot):
        for r in range(ts):
            pltpu.make_async_copy(hfp_ref.at[pl.ds(0, 1)], xin.at[slot, pl.ds(0, 1)], xsem.at[slot]).wait()

    def scatter(s, slot, r0=0, r1=ts, base=None, nrows=None):
        base = id_base if base is None else base
        nrows = rows if nrows is None else nrows
        for r in range(r0, r1):
            g = s * ts + r
            dst = jnp.where(g < nrows, ids[base + g], n_asg + slot * ts + r)
            pltpu.make_async_copy(ybuf.at[slot, pl.ds(r, 1)], ykt_ref.at[pl.ds(dst, 1)], ysem.at[slot]).start()

    def scatter_wait(slot):
        for r in range(ts):
            pltpu.make_async_copy(ybuf.at[slot, pl.ds(0, 1)], ykt_ref.at[pl.ds(0, 1)], ysem.at[slot]).wait()

    def gu_copy(ee, c, slot):
        return pltpu.make_async_copy(wgu_ref.at[ee, pl.ds(pl.multiple_of(c * ckr, ckr), ckr), :],
                                     stage.at[slot], wsem.at[slot])

    def d_copy(ee, c, slot):
        return pltpu.make_async_copy(wd_ref.at[ee, pl.ds(pl.multiple_of(c * ckd, ckd), ckd), :],
                                     stage.at[slot, pl.ds(0, ckd), pl.ds(0, d)], wsem.at[slot])

    def gu_step(ee, c):
        slot = c % nst
        gu_copy(ee, c, slot).wait()
        wgub[pl.ds(pl.multiple_of(c * ckr, ckr), ckr), :] = stage[slot].astype(BF16)

        @pl.when(c + nst < nck)
        def _():
            gu_copy(ee, c + nst, slot).start(priority=WEIGHT_DMA_PRIORITY)

    def d_step(ee, c):
        slot = c % nst
        d_copy(ee, c, slot).wait()
        wdb[pl.ds(pl.multiple_of(c * ckd, ckd), ckd), :] = stage[slot, :ckd, :d].astype(BF16)

        @pl.when(c + nst < nck)
        def _():
            d_copy(ee, c + nst, slot).start(priority=WEIGHT_DMA_PRIORITY)

    def prime(copy_fn, ee):
        for q in range(min(nst, nck)):
            copy_fn(ee, q, q).start(priority=WEIGHT_DMA_PRIORITY)

    def steps_after(s, step_fn, ee):
        def one(jj, carry):
            c = s * cpi + jj

            @pl.when(c < nck)
            def _():
                step_fn(ee, c)
            return carry
        lax.fori_loop(0, cpi, one, 0)

    @pl.when(nsub > 0)
    def _():
        nxt = jnp.minimum(i + 1, n - 1)
        has_next = (i + 1 < n) & (ns_ref[nxt] > 0)

        @pl.when(i == 0)
        def _():
            for q in range(nq):
                ids_copy(0, q, 0).start()
            prime(gu_copy, e)

            def first(c, carry):
                gu_step(e, c)
                return carry
            lax.fori_loop(0, nck, first, 0)
            for q in range(nq):
                ids_copy(0, q, 0).wait()
            gather(0, 0)
            ybuf[...] = jnp.zeros(ybuf.shape, ybuf.dtype)

        prv = jnp.maximum(i - 1, 0)
        nsub_p = jnp.where(i > 0, ns_ref[prv], 1)
        rows_p = jnp.where(i > 0, rows_ref[prv], 0)
        nslot = 1 - i % 2
        base_p = jnp.where(i > 0, nslot * idw + (pos_ref[prv] & (ID_CHUNK - 1)), id_base)
        lslot_p = (nsub_p - 1) % 2

        prime(d_copy, e)
        bgu = bgu_ref[e]

        def gate_up(s, scatter_prev):
            slot = s % 2
            gather_wait(slot)
            p = xin[slot]
            xb = jnp.concatenate([_unpack_lo(p).astype(BF16), _unpack_hi(p).astype(BF16)], axis=-1)
            nc = f // fc
            for cc in range(nc):
                gather(s + 1, 1 - slot, cc * ts // nc, (cc + 1) * ts // nc)
                if scatter_prev:
                    scatter(nsub_p - 1, lslot_p, cc * ts // nc, (cc + 1) * ts // nc, base=base_p, nrows=rows_p)
                gate = jnp.dot(xb, wgub[:, cc * fc:(cc + 1) * fc], preferred_element_type=F32)
                gate = gate + bgu[:, cc * fc:(cc + 1) * fc]
                up = jnp.dot(xb, wgub[:, f + cc * fc:f + (cc + 1) * fc], preferred_element_type=F32)
                up = up + bgu[:, f + cc * fc:f + (cc + 1) * fc]
                gate = jnp.minimum(gate, SWIGLU_LIMIT)
                up = jnp.clip(up, -SWIGLU_LIMIT, SWIGLU_LIMIT)
                a = (up + 1.0) * gate * jax.nn.sigmoid(SWIGLU_ALPHA * gate)
                act[s, :, cc * fc:(cc + 1) * fc] = a.astype(BF16)
            steps_after(s, d_step, e)

        gate_up(0, True)

        for q in range(nq):
            ids_copy(nxt, q, nslot).start()

        def phase_a(s, carry):
            gate_up(s, False)
            return carry

        lax.fori_loop(1, nsub, phase_a, 0)
        gather_wait(nsub % 2)

        e_next = te_ref[nxt]
        load_next = has_next & (e_next != e)

        @pl.when(load_next)
        def _():
            prime(gu_copy, e_next)

        bd = bd_ref[e]

        def down(s, slot, scatter_prev):
            a = act[s]
            if scatter_prev:
                scatter(s - 1, 1 - slot)
            fcb = min(fc, dh // 2)
            for cc in range(dh // fcb):
                lo = slice(cc * fcb, (cc + 1) * fcb)
                hi = slice(dh + cc * fcb, dh + (cc + 1) * fcb)
                y_lo = jnp.dot(a, wdb[:, lo], preferred_element_type=F32) + bd[:, lo]
                y_hi = jnp.dot(a, wdb[:, hi], preferred_element_type=F32) + bd[:, hi]
                ybuf[slot, :, lo] = _pack_bf16_pair(y_lo, y_hi)

        def next_weights(s):
            @pl.when(load_next)
            def _():
                steps_after(s, gu_step, e_next)

        scatter_wait(lslot_p)

        @pl.when(nsub_p >= 2)
        def _():
            scatter_wait(1 - lslot_p)

        for q in range(nq):
            ids_copy(nxt, q, nslot).wait()
        gather(0, 0, base=nslot * idw + (pos_ref[nxt] & (ID_CHUNK - 1)))
        down(0, 0, False)
        next_weights(0)

        def phase_b(s, carry):
            slot = s % 2

            @pl.when(s >= 2)
            def _():
                scatter_wait(slot)

            down(s, slot, True)
            next_weights(s)
            return carry

        lax.fori_loop(1, nsub, phase_b, 0)

        @pl.when(jnp.logical_not(has_next))
        def _():
            scatter(nsub - 1, (nsub - 1) % 2)
            scatter_wait((nsub - 1) % 2)

            @pl.when(nsub >= 2)
            def _():
                scatter_wait(nsub % 2)

            gather_wait(0)


def _id_window(c):
    return ((ID_CHUNK - 1 + c.tm_e + c.ts_e) // ID_CHUNK + 1) * ID_CHUNK


def _moe(c, sched, order_pad, hfp, w_gu, b_gu3, w_d, b_d3):
    n_tok, dh = hfp.shape
    d = 2 * dh
    f = c.DFF
    tm, ts, nck = c.tm_e, c.ts_e, c.nck_e
    n_asg = n_tok * c.K
    grid_spec = pltpu.PrefetchScalarGridSpec(
        num_scalar_prefetch=5,
        grid=(_n_tiles(c),),
        in_specs=[
            pl.BlockSpec(memory_space=pl.ANY),
            pl.BlockSpec(memory_space=pl.ANY),
            pl.BlockSpec(memory_space=pl.ANY),
            pl.BlockSpec(memory_space=pl.ANY),
            pl.BlockSpec((c.E, 1, 2 * f), lambda i, *_: (0, 0, 0), pipeline_mode=pl.Buffered(1)),
            pl.BlockSpec((c.E, 1, d), lambda i, *_: (0, 0, 0), pipeline_mode=pl.Buffered(1)),
        ],
        out_specs=pl.BlockSpec(memory_space=pl.ANY),
        scratch_shapes=[
            pltpu.VMEM((d, 2 * f), BF16),
            pltpu.VMEM((f, d), BF16),
            pltpu.VMEM((c.nst_e, d // nck, 2 * f), F32),
            pltpu.VMEM((tm // ts, ts, f), BF16),
            pltpu.VMEM((2, ts, dh), U32),
            pltpu.VMEM((2, ts, dh), U32),
            pltpu.SMEM((2 * _id_window(c),), I32),
            pltpu.SemaphoreType.DMA((c.nst_e,)),
            pltpu.SemaphoreType.DMA((2,)),
            pltpu.SemaphoreType.DMA((2,)),
            pltpu.SemaphoreType.DMA((2,)),
        ],
    )
    kern = functools.partial(_moe_kernel, ts=ts, nck=nck, fc=c.fc_e, n_tok=n_tok, n_asg=n_asg)
    return pl.pallas_call(
        kern,
        grid_spec=grid_spec,
        out_shape=jax.ShapeDtypeStruct((n_asg + 2 * ts, dh), U32),
        compiler_params=_cparams(("arbitrary",)),
        name="experts",
    )(*sched, order_pad, hfp, w_gu, w_d, b_gu3, b_d3)


def _combine_kernel(x2_ref, gate_ref, g_ref, *refs, topk):
    y_refs, o_ref = refs[:topk], refs[topk]
    d = x2_ref.shape[1]
    dh = d // 2
    lo = x2_ref[:, :dh]
    hi = x2_ref[:, dh:]
    gates = gate_ref[...]
    for kk in range(topk):
        wk = gates[:, kk:kk + 1]
        p = y_refs[kk][...]
        lo = lo + wk * _unpack_lo(p)
        hi = hi + wk * _unpack_hi(p)
    ms = (jnp.sum(lo * lo, axis=-1, keepdims=True) + jnp.sum(hi * hi, axis=-1, keepdims=True)) / d
    inv = lax.rsqrt(ms + RMS_EPS)
    o_ref[:, :dh] = lo * inv * g_ref[:, :dh]
    o_ref[:, dh:] = hi * inv * g_ref[:, dh:]


def _combine(c, x2, gates, g, ykt):
    n, d = x2.shape
    tm = c.tm_c
    nb = n // tm
    y_specs = [pl.BlockSpec((tm, d // 2), functools.partial(lambda i, kk: (kk * nb + i, 0), kk=kk))
               for kk in range(c.K)]
    return pl.pallas_call(
        functools.partial(_combine_kernel, topk=c.K),
        grid=(nb,),
        in_specs=[
            pl.BlockSpec((tm, d), lambda i: (i, 0)),
            pl.BlockSpec((tm, LANES), lambda i: (i, 0)),
            pl.BlockSpec((1, d), lambda i: (0, 0)),
        ] + y_specs,
        out_specs=pl.BlockSpec((tm, d), lambda i: (i, 0)),
        out_shape=jax.ShapeDtypeStruct((n, d), F32),
        compiler_params=_cparams(("parallel",)),
        name="combine",
    )(x2, gates, g, *([ykt] * c.K))


def _plan(c, idx, counts):
    tm, ts = c.tm_e, c.ts_e
    nk = idx.shape[0] * c.K
    order = jnp.argsort(idx.T.reshape(-1)).astype(I32)
    pad = _id_window(c) + (-nk) % ID_CHUNK
    order_pad = jnp.concatenate([order, jnp.zeros((pad,), I32)])
    counts = counts.astype(I32)
    start = jnp.cumsum(counts) - counts
    tiles_per_e = (counts + tm - 1) // tm
    tile_end = jnp.cumsum(tiles_per_e)
    tile_start = tile_end - tiles_per_e
    t = jnp.arange(_n_tiles(c), dtype=I32)
    n_used = tile_end[-1]
    tc = jnp.minimum(t, n_used - 1)
    te = jnp.minimum(jnp.searchsorted(tile_end, tc, side="right"), c.E - 1).astype(I32)
    j = tc - tile_start[te]
    used = t < n_used
    rows = jnp.where(used, jnp.clip(counts[te] - j * tm, 0, tm), 0).astype(I32)
    nsub = (rows + ts - 1) // ts
    pos = jnp.where(used, start[te] + j * tm, 0).astype(I32)
    cpi = (c.nck_e + jnp.maximum(nsub, 1) - 1) // jnp.maximum(nsub, 1)
    return order_pad, (te, nsub.astype(I32), cpi.astype(I32), pos, rows)


def _n_tiles(c):
    return -(-(c.B * c.S * c.K) // c.tm_e) + c.E


def _forward(c, x, mem, norm_mix_g, w_in, b_forget, sg_ln_g, sg_ln_b, w_spatial, b_spatial,
             w_branch_a, w_branch_b, w_out, norm_x_g, norm_mem_g, w_xq, w_xkv, w_xo,
             norm_ffn_g, w_router, b_router, w_gate_up, b_gate_up, w_down, b_down, norm_final_g):
    B, S, D = x.shape
    n = B * S
    fw = c.FH * LANES
    sw = c.SG * LANES
    x2d = x.reshape(n, D)

    o_f = 3 * fw
    o_z = o_f + c.FH
    o_g = o_z + 2 * sw
    w_main = jnp.concatenate([w_in[:, o_g:], w_in[:, o_z:o_g], w_in[:, :o_f]], axis=1).astype(BF16)
    w_f = jnp.pad(w_in[:, o_f:o_z], ((0, 0), (0, LANES - c.FH))).astype(BF16)
    b_f = jnp.pad(b_forget.astype(F32), (0, LANES - c.FH)).reshape(1, LANES)
    proj, logf = _in_proj(c, x2d, norm_mix_g.reshape(1, D), w_main, w_f, b_f, LOG2E * LANES ** -0.5)

    logf_bhs = logf[:, :c.FH].reshape(B, S, c.FH).transpose(0, 2, 1)
    csum = _cumsum(c, logf_bhs)
    crow = csum.reshape(B * c.FH, 1, S)
    ccol = csum.reshape(B * c.FH, S, 1)
    qcol0 = (2 * D + 2 * sw) // LANES
    attn = _fox(c, proj, crow, ccol, qcol0)

    b_s_full = jnp.broadcast_to(b_spatial.astype(F32)[:, :, None], (c.SG, c.SGC, LANES))
    x1 = _mix(c, x2d, attn, proj, sg_ln_g.reshape(1, sw), sg_ln_b.reshape(1, sw), w_spatial, b_s_full,
              w_branch_a.astype(BF16), w_branch_b.astype(BF16), w_out.astype(BF16))

    w_r32 = jnp.pad(w_router.astype(F32), ((0, 0), (0, LANES - c.E)))
    w_r_hi = w_r32.astype(BF16)
    w_r = jnp.concatenate([w_r_hi, (w_r32 - w_r_hi.astype(F32)).astype(BF16)], axis=1)
    b_r = jnp.pad(b_router.astype(F32), (0, LANES - c.E)).reshape(1, LANES)
    x2, hfp, idx, gates, counts = _xattn(
        c, x1, norm_x_g.reshape(1, D), w_xq.astype(BF16), mem.reshape(B * c.MEM, D), norm_mem_g.reshape(1, D),
        w_xkv.astype(BF16), w_xo.astype(BF16),
        norm_ffn_g.reshape(1, D), w_r, b_r, LANES ** -0.5)

    order_pad, sched = _plan(c, idx[:, :c.K], counts[0, :c.E])
    ykt = _moe(c, sched, order_pad, hfp, w_gate_up, b_gate_up.reshape(c.E, 1, 2 * c.DFF),
               w_down, b_down.reshape(c.E, 1, D))
    out = _combine(c, x2, gates, norm_final_g.reshape(1, D), ykt)
    return out.reshape(B, S, D)


_CFG = Cfg(B=4, S=4096, D=2048, MEM=256, FH=8, SG=8, SGC=128, XH=4, E=32, K=4, DFF=2048,
           tm_in=1024, tn_in=1024, tq=1024, tm_mix=256, tm_x=512, tm_e=2560, ts_e=256, fc_e=512, nck_e=16, nst_e=4,
           tm_c=256, tc_cs=512, fox_parts=2)


@jax.jit
def kernel(x, mem, norm_mix_g, w_in, b_forget, sg_ln_g, sg_ln_b, w_spatial, b_spatial, w_branch_a, w_branch_b,
           w_out, norm_x_g, norm_mem_g, w_xq, w_xkv, w_xo, norm_ffn_g, w_router, b_router, w_gate_up,
           b_gate_up, w_down, b_down, norm_final_g):
    return _forward(_CFG, x, mem, norm_mix_g, w_in, b_forget, sg_ln_g, sg_ln_b, w_spatial, b_spatial,
                    w_branch_a, w_branch_b, w_out, norm_x_g, norm_mem_g, w_xq, w_xkv, w_xo,
                    norm_ffn_g, w_router, b_router, w_gate_up, b_gate_up, w_down, b_down, norm_final_g)
```
